```python
import jax, jax.numpy as jnp
from jax import lax
import numpy as np

D_MODEL = 1024
BATCH = 32
SEQ = 2048
DEPTH = 1

HEAD_DIM = 64
N_Q_HEADS = 8
N_KV_HEADS = 2
Q_PER_KV = N_Q_HEADS // N_KV_HEADS
ATTN_WIDTH = N_Q_HEADS * HEAD_DIM
KV_WIDTH = N_KV_HEADS * HEAD_DIM
WINDOW = 128
BLOCK = WINDOW
ROPE_THETA = 10000.0
POOL_WINDOWS = (2, 4, 8, 16)
N_POOL_GROUPS = len(POOL_WINDOWS)
POOL_WIDTH = D_MODEL - ATTN_WIDTH
POOL_GROUP_DIM = POOL_WIDTH // N_POOL_GROUPS
MIX_WIDTH = ATTN_WIDTH + POOL_WIDTH
IN_WIDTH = ATTN_WIDTH + 2 * KV_WIDTH + POOL_WIDTH
D_FF = 4 * D_MODEL
EPS = 1e-6

kernel_name = "hymba_swa_sink_multiscale_pool_block"


def _rmsnorm(x, g):
    xf = x.astype(jnp.float32)
    y = xf * lax.rsqrt(jnp.mean(xf * xf, axis=-1, keepdims=True) + EPS)
    return (y * g.astype(jnp.float32)).astype(x.dtype)


def _rope(x, pos):
    half = HEAD_DIM // 2
    inv_freq = ROPE_THETA ** (-jnp.arange(half, dtype=jnp.float32) / half)
    ang = pos.astype(jnp.float32)[:, None] * inv_freq[None, :]
    cos = jnp.cos(ang)[None, :, None, :]
    sin = jnp.sin(ang)[None, :, None, :]
    xf = x.astype(jnp.float32)
    x1, x2 = xf[..., :half], xf[..., half:]
    out = jnp.concatenate([x1 * cos - x2 * sin, x2 * cos + x1 * sin], axis=-1)
    return out.astype(x.dtype)


def _swa_with_sinks(q, k, v, sinks):
    B, S = q.shape[0], q.shape[1]
    nb = S // BLOCK
    qb = q.reshape(B, nb, BLOCK, N_KV_HEADS, Q_PER_KV, HEAD_DIM)
    kb = k.reshape(B, nb, BLOCK, N_KV_HEADS, HEAD_DIM)
    vb = v.reshape(B, nb, BLOCK, N_KV_HEADS, HEAD_DIM)

    def with_prev(t):
        prev = jnp.pad(t[:, :-1], ((0, 0), (1, 0), (0, 0), (0, 0), (0, 0)))
        return jnp.concatenate([prev, t], axis=2)

    kw, vw = with_prev(kb), with_prev(vb)
    scale = HEAD_DIM ** -0.5
    logits = jnp.einsum('bnqkgd,bnskd->bnkgqs', qb, kw,
                        preferred_element_type=jnp.float32) * scale
    blk = jnp.arange(nb)[:, None, None]
    qi = jnp.arange(BLOCK)[None, :, None]
    kj = jnp.arange(2 * BLOCK)[None, None, :]
    rel = BLOCK + qi - kj
    kpos = (blk - 1) * BLOCK + kj
    mask = (rel >= 0) & (rel < WINDOW) & (kpos >= 0)
    logits = jnp.where(mask[None, :, None, None], logits, -jnp.inf)
    sink = sinks.astype(jnp.float32).reshape(N_KV_HEADS, Q_PER_KV)[None, None, :, :, None, None]
    m = jnp.maximum(jnp.max(logits, axis=-1, keepdims=True), sink)
    p = jnp.exp(logits - m)
    denom = jnp.sum(p, axis=-1, keepdims=True) + jnp.exp(sink - m)
    probs = (p / denom).astype(v.dtype)
    out = jnp.einsum('bnkgqs,bnskd->bnqkgd', probs, vw)
    return out.reshape(B, S, ATTN_WIDTH)


def _multiscale_pool(u, w_pool, pool_scale):
    B, S = u.shape[0], u.shape[1]
    ug = u.reshape(B, S, N_POOL_GROUPS, POOL_GROUP_DIM).astype(jnp.float32)
    c = jnp.pad(jnp.cumsum(ug, axis=1), ((0, 0), (1, 0), (0, 0), (0, 0)))
    t = jnp.arange(S)
    means = []
    for g, w in enumerate(POOL_WINDOWS):
        cg = c[:, :, g]
        lagged = jnp.pad(cg[:, :S + 1 - w], ((0, 0), (w - 1, 0), (0, 0)))
        cnt = jnp.minimum(t + 1, w).astype(jnp.float32)[None, :, None]
        means.append((cg[:, 1:] - lagged) / cnt)
    mean = jnp.stack(means, axis=2)
    d = (mean - ug).astype(u.dtype)
    y = jnp.einsum('bsgc,gcd->bsgd', d, w_pool)
    return y.reshape(B, S, POOL_WIDTH) * pool_scale


def _fwd_setup_inputs(seed: int = 0) -> dict:
    key = jax.random.key(seed)
    ks = jax.random.split(key, 12)
    f32 = jnp.float32
    x = jax.random.normal(ks[0], (BATCH, SEQ, D_MODEL), f32)
    attn_norm_g = 1.0 + 0.02 * jax.random.normal(ks[1], (DEPTH, D_MODEL), f32)
    w_in = jax.random.normal(ks[2], (DEPTH, D_MODEL, IN_WIDTH), f32) * D_MODEL ** -0.5
    attn_sinks = 0.5 * jax.random.normal(ks[3], (DEPTH, N_Q_HEADS), f32)
    w_pool = jax.random.normal(ks[4], (DEPTH, N_POOL_GROUPS, POOL_GROUP_DIM, POOL_GROUP_DIM), f32) * POOL_GROUP_DIM ** -0.5
    pool_scale = 1.0 + 0.1 * jax.random.normal(ks[5], (DEPTH, POOL_WIDTH), f32)
    w_out = jax.random.normal(ks[6], (DEPTH, MIX_WIDTH, D_MODEL), f32) * MIX_WIDTH ** -0.5
    mlp_norm_g = 1.0 + 0.02 * jax.random.normal(ks[7], (DEPTH, D_MODEL), f32)
    w_up = jax.random.normal(ks[8], (DEPTH, D_MODEL, D_FF), f32) * D_MODEL ** -0.5
    w_down = jax.random.normal(ks[9], (DEPTH, D_FF, D_MODEL), f32) * D_FF ** -0.5
    final_norm_g = 1.0 + 0.02 * jax.random.normal(ks[10], (D_MODEL,), f32)
    return {"x": x, "attn_norm_g": attn_norm_g, "w_in": w_in, "attn_sinks": attn_sinks,
            "w_pool": w_pool, "pool_scale": pool_scale, "w_out": w_out,
            "mlp_norm_g": mlp_norm_g, "w_up": w_up, "w_down": w_down,
            "final_norm_g": final_norm_g}


def _fwd_reference(x, attn_norm_g, w_in, attn_sinks, w_pool, pool_scale, w_out,
              mlp_norm_g, w_up, w_down, final_norm_g):
    B, S = x.shape[0], x.shape[1]
    pos = jnp.arange(S)
    for l in range(DEPTH):
        h = _rmsnorm(x, attn_norm_g[l])
        proj = h @ w_in[l]
        q = proj[..., :ATTN_WIDTH].reshape(B, S, N_Q_HEADS, HEAD_DIM)
        k = proj[..., ATTN_WIDTH:ATTN_WIDTH + KV_WIDTH].reshape(B, S, N_KV_HEADS, HEAD_DIM)
        v = proj[..., ATTN_WIDTH + KV_WIDTH:ATTN_WIDTH + 2 * KV_WIDTH].reshape(B, S, N_KV_HEADS, HEAD_DIM)
        u = proj[..., ATTN_WIDTH + 2 * KV_WIDTH:]
        q, k = _rope(q, pos), _rope(k, pos)
        attn = _swa_with_sinks(q, k, v, attn_sinks[l])
        pool = _multiscale_pool(u, w_pool[l], pool_scale[l])
        x = x + jnp.concatenate([attn, pool], axis=-1) @ w_out[l]
        h = _rmsnorm(x, mlp_norm_g[l])
        x = x + jnp.square(jax.nn.relu(h @ w_up[l])) @ w_down[l]
    return _rmsnorm(x, final_norm_g)


import jax as _jax
import jax.numpy as _jnp

TWIN_FORMAT = 'train_step'
FWD_PARAMS = ['x', 'attn_norm_g', 'w_in', 'attn_sinks', 'w_pool', 'pool_scale', 'w_out', 'mlp_norm_g', 'w_up', 'w_down', 'final_norm_g']
TWIN_WEIGHTS = ['attn_norm_g', 'w_in', 'attn_sinks', 'w_pool', 'pool_scale', 'w_out', 'mlp_norm_g', 'w_up', 'w_down', 'final_norm_g']
TWIN_DIFF_INPUT = 'x'
TWIN_INPUTS = ['x', 'attn_norm_g', 'w_in', 'attn_sinks', 'w_pool', 'pool_scale', 'w_out', 'mlp_norm_g', 'w_up', 'w_down', 'final_norm_g', 'loss_target', 'm_attn_norm_g', 'm_w_in', 'm_attn_sinks', 'm_w_pool', 'm_pool_scale', 'm_w_out', 'm_mlp_norm_g', 'm_w_up', 'm_w_down', 'm_final_norm_g', 'v_attn_norm_g', 'v_w_in', 'v_attn_sinks', 'v_w_pool', 'v_pool_scale', 'v_w_out', 'v_mlp_norm_g', 'v_w_up', 'v_w_down', 'v_final_norm_g']
TWIN_OUTPUTS = ['loss', 'grad_x', 'grad_attn_norm_g', 'grad_w_in', 'grad_attn_sinks', 'grad_w_pool', 'grad_pool_scale', 'grad_w_out', 'grad_mlp_norm_g', 'grad_w_up', 'grad_w_down', 'grad_final_norm_g', 'delta_attn_norm_g', 'delta_w_in', 'delta_attn_sinks', 'delta_w_pool', 'delta_pool_scale', 'delta_w_out', 'delta_mlp_norm_g', 'delta_w_up', 'delta_w_down', 'delta_final_norm_g', 'new_m_attn_norm_g', 'new_m_w_in', 'new_m_attn_sinks', 'new_m_w_pool', 'new_m_pool_scale', 'new_m_w_out', 'new_m_mlp_norm_g', 'new_m_w_up', 'new_m_w_down', 'new_m_final_norm_g', 'new_v_attn_norm_g', 'new_v_w_in', 'new_v_attn_sinks', 'new_v_w_pool', 'new_v_pool_scale', 'new_v_w_out', 'new_v_mlp_norm_g', 'new_v_w_up', 'new_v_w_down', 'new_v_final_norm_g']
TWIN_LEAF_KINDS = {'loss': 'loss', 'grad_x': 'grad_x', 'grad_attn_norm_g': 'grad_w', 'grad_w_in': 'grad_w', 'grad_attn_sinks': 'grad_w', 'grad_w_pool': 'grad_w', 'grad_pool_scale': 'grad_w', 'grad_w_out': 'grad_w', 'grad_mlp_norm_g': 'grad_w', 'grad_w_up': 'grad_w', 'grad_w_down': 'grad_w', 'grad_final_norm_g': 'grad_w', 'delta_attn_norm_g': 'delta_w', 'delta_w_in': 'delta_w', 'delta_attn_sinks': 'delta_w', 'delta_w_pool': 'delta_w', 'delta_pool_scale': 'delta_w', 'delta_w_out': 'delta_w', 'delta_mlp_norm_g': 'delta_w', 'delta_w_up': 'delta_w', 'delta_w_down': 'delta_w', 'delta_final_norm_g': 'delta_w', 'new_m_attn_norm_g': 'new_m', 'new_m_w_in': 'new_m', 'new_m_attn_sinks': 'new_m', 'new_m_w_pool': 'new_m', 'new_m_pool_scale': 'new_m', 'new_m_w_out': 'new_m', 'new_m_mlp_norm_g': 'new_m', 'new_m_w_up': 'new_m', 'new_m_w_down': 'new_m', 'new_m_final_norm_g': 'new_m', 'new_v_attn_norm_g': 'new_v', 'new_v_w_in': 'new_v', 'new_v_attn_sinks': 'new_v', 'new_v_w_pool': 'new_v', 'new_v_pool_scale': 'new_v', 'new_v_w_out': 'new_v', 'new_v_mlp_norm_g': 'new_v', 'new_v_w_up': 'new_v', 'new_v_w_down': 'new_v', 'new_v_final_norm_g': 'new_v'}


def _forward(args):
    return _fwd_reference(*[args[k] for k in FWD_PARAMS])


def _output_shape():
    out = _jax.eval_shape(lambda: _forward(_fwd_setup_inputs(0)))
    return out.shape, out.dtype

N_MICROBATCH = 1
ADAM_LR = 0.001
ADAM_B1 = 0.9
ADAM_B2 = 0.999
ADAM_EPS = 1e-08
ADAM_WD = 0.01
ADAM_STEP = 10
PER_EXAMPLE_BATCH_AXIS = {'x': 0, 'loss_target': 0}
SHARED_INPUTS = []
_WEIGHT_DTYPES = {'attn_norm_g': _jnp.float32, 'w_in': _jnp.float32, 'attn_sinks': _jnp.float32, 'w_pool': _jnp.float32, 'pool_scale': _jnp.float32, 'w_out': _jnp.float32, 'mlp_norm_g': _jnp.float32, 'w_up': _jnp.float32, 'w_down': _jnp.float32, 'final_norm_g': _jnp.float32}
MOMENT_SCALE = {'attn_norm_g': 1.483639e-01, 'w_in': 1.353648e-01, 'attn_sinks': 2.551691e-02, 'w_pool': 2.027955e-01, 'pool_scale': 2.032771e-01, 'w_out': 1.457861e-01, 'mlp_norm_g': 2.026141e-01, 'w_up': 1.045981e-01, 'w_down': 1.995577e-01, 'final_norm_g': 6.445681e+01}


def _to_microbatches(a, axis):
    t = _jnp.moveaxis(a, axis, 0)
    t = t.reshape((N_MICROBATCH, t.shape[0] // N_MICROBATCH) + t.shape[1:])
    return _jnp.moveaxis(t, 1, axis + 1)


def setup_inputs(seed: int = 0) -> dict:
    inp = _fwd_setup_inputs(seed)
    key = _jax.random.fold_in(_jax.random.key(seed), 7919)
    shape, _ = _output_shape()
    out = dict(inp)
    out["loss_target"] = _jax.random.normal(_jax.random.fold_in(key, 0), shape, _jnp.float32)
    for i, name in enumerate(TWIN_WEIGHTS):
        w = inp[name].astype(_jnp.float32)
        if MOMENT_SCALE is None:
            s = _jnp.sqrt(_jnp.mean(_jnp.square(w)) + 1e-30)
        else:
            s = MOMENT_SCALE[name]
        km, kv = _jax.random.split(_jax.random.fold_in(key, i + 1))
        out[name] = w
        out["m_" + name] = s * _jax.random.normal(km, w.shape, _jnp.float32)
        out["v_" + name] = (s * s) * _jax.random.uniform(kv, w.shape, _jnp.float32, 0.5, 1.5)
    if N_MICROBATCH > 1:
        for name, axis in PER_EXAMPLE_BATCH_AXIS.items():
            out[name] = _to_microbatches(out[name], axis)
    return {'x': out['x'], 'attn_norm_g': out['attn_norm_g'], 'w_in': out['w_in'], 'attn_sinks': out['attn_sinks'], 'w_pool': out['w_pool'], 'pool_scale': out['pool_scale'], 'w_out': out['w_out'], 'mlp_norm_g': out['mlp_norm_g'], 'w_up': out['w_up'], 'w_down': out['w_down'], 'final_norm_g': out['final_norm_g'], 'loss_target': out['loss_target'], 'm_attn_norm_g': out['m_attn_norm_g'], 'm_w_in': out['m_w_in'], 'm_attn_sinks': out['m_attn_sinks'], 'm_w_pool': out['m_w_pool'], 'm_pool_scale': out['m_pool_scale'], 'm_w_out': out['m_w_out'], 'm_mlp_norm_g': out['m_mlp_norm_g'], 'm_w_up': out['m_w_up'], 'm_w_down': out['m_w_down'], 'm_final_norm_g': out['m_final_norm_g'], 'v_attn_norm_g': out['v_attn_norm_g'], 'v_w_in': out['v_w_in'], 'v_attn_sinks': out['v_attn_sinks'], 'v_w_pool': out['v_w_pool'], 'v_pool_scale': out['v_pool_scale'], 'v_w_out': out['v_w_out'], 'v_mlp_norm_g': out['v_mlp_norm_g'], 'v_w_up': out['v_w_up'], 'v_w_down': out['v_w_down'], 'v_final_norm_g': out['v_final_norm_g']}


def _loss(weights, diff, rest, loss_target):
    with _jax.named_scope("forward"):
        args = {**rest, TWIN_DIFF_INPUT: diff, **{k: w.astype(_WEIGHT_DTYPES[k]) for k, w in weights.items()}}
        y = _forward(args)
    with _jax.named_scope("loss_head"):
        err = _jnp.square(y.astype(_jnp.float32) - loss_target)
        return 0.5 * _jnp.sum(_jnp.mean(err, axis=-1)) if err.ndim else 0.5 * err


def _adamw(w, g, m, v):
    m = ADAM_B1 * m + (1.0 - ADAM_B1) * g
    v = ADAM_B2 * v + (1.0 - ADAM_B2) * _jnp.square(g)
    m_hat = m / (1.0 - ADAM_B1 ** ADAM_STEP)
    v_hat = v / (1.0 - ADAM_B2 ** ADAM_STEP)
    delta = -ADAM_LR * (m_hat / (_jnp.sqrt(v_hat) + ADAM_EPS) + ADAM_WD * w)
    return delta, m, v


def reference(x, attn_norm_g, w_in, attn_sinks, w_pool, pool_scale, w_out, mlp_norm_g, w_up, w_down, final_norm_g, loss_target, m_attn_norm_g, m_w_in, m_attn_sinks, m_w_pool, m_pool_scale, m_w_out, m_mlp_norm_g, m_w_up, m_w_down, m_final_norm_g, v_attn_norm_g, v_w_in, v_attn_sinks, v_w_pool, v_pool_scale, v_w_out, v_mlp_norm_g, v_w_up, v_w_down, v_final_norm_g):
    given = dict(x=x, attn_norm_g=attn_norm_g, w_in=w_in, attn_sinks=attn_sinks, w_pool=w_pool, pool_scale=pool_scale, w_out=w_out, mlp_norm_g=mlp_norm_g, w_up=w_up, w_down=w_down, final_norm_g=final_norm_g, loss_target=loss_target, m_attn_norm_g=m_attn_norm_g, m_w_in=m_w_in, m_attn_sinks=m_attn_sinks, m_w_pool=m_w_pool, m_pool_scale=m_pool_scale, m_w_out=m_w_out, m_mlp_norm_g=m_mlp_norm_g, m_w_up=m_w_up, m_w_down=m_w_down, m_final_norm_g=m_final_norm_g, v_attn_norm_g=v_attn_norm_g, v_w_in=v_w_in, v_attn_sinks=v_attn_sinks, v_w_pool=v_w_pool, v_pool_scale=v_pool_scale, v_w_out=v_w_out, v_mlp_norm_g=v_mlp_norm_g, v_w_up=v_w_up, v_w_down=v_w_down, v_final_norm_g=v_final_norm_g)
    weights = {n: given[n] for n in TWIN_WEIGHTS}
    shared = {n: given[n] for n in SHARED_INPUTS}
    per_example = {n: given[n] for n in ['x']}
    grad_fn = _jax.value_and_grad(_loss, argnums=(0, 1))

    def one_microbatch(ex, loss_target):
        ex = dict(ex)
        diff = ex.pop(TWIN_DIFF_INPUT)
        return grad_fn(weights, diff, {**shared, **ex}, loss_target)

    if N_MICROBATCH == 1:
        loss, (grad_w, grad_x) = one_microbatch(per_example, given["loss_target"])
    else:
        def body(carry, xs):
            loss_sum, grad_sum = carry
            l_k, (gw_k, gx_k) = one_microbatch(xs[0], xs[1])
            with _jax.named_scope("update"):
                return (loss_sum + l_k, _jax.tree.map(_jnp.add, grad_sum, gw_k)), gx_k

        init = (_jnp.zeros((), _jnp.float32), _jax.tree.map(_jnp.zeros_like, weights))
        (loss, grad_w), grad_x = _jax.lax.scan(body, init, (per_example, given["loss_target"]))
    with _jax.named_scope("update"):
        delta_w, new_m, new_v = {}, {}, {}
        for n in TWIN_WEIGHTS:
            delta_w[n], new_m[n], new_v[n] = _adamw(weights[n], grad_w[n], given["m_" + n], given["v_" + n])
    return (loss, grad_x, *[grad_w[n] for n in TWIN_WEIGHTS], *[delta_w[n] for n in TWIN_WEIGHTS],
            *[new_m[n] for n in TWIN_WEIGHTS], *[new_v[n] for n in TWIN_WEIGHTS])
```

```python
import functools

import jax
import jax.numpy as jnp
from jax import lax
from jax.experimental import pallas as pl
from jax.experimental.pallas import tpu as pltpu

F32 = jnp.float32
BF16 = jnp.bfloat16

D_MODEL = 1024
HEAD_DIM = 64
N_Q_HEADS = 8
Q_PER_KV = 4
ATTN_WIDTH = 512
KV_WIDTH = 128
BLOCK = 128
ROPE_THETA = 10000.0
POOL_WINDOWS = (2, 4, 8, 16)
POOL_WIDTH = 512
POOL_GROUP_DIM = 128
IN_WIDTH = 1280
D_FF = 4096
EPS = 1e-6
N_DEV = 8
FF_BLOCK = D_FF // N_DEV
IN_BLOCK = IN_WIDTH // N_DEV
OUT_BLOCK = D_MODEL // N_DEV
ADAM_LR = 0.001
ADAM_B1 = 0.9
ADAM_B2 = 0.999
ADAM_EPS = 1e-08
ADAM_WD = 0.01
ADAM_STEP = 10
NEG = -1e30
LANES = 128
MIB = 1024 * 1024
MESH = pl.DeviceIdType.MESH

ROW_G1, ROW_G2, ROW_G3, ROW_PS, ROW_SINK, ROW_LOSS, ROW_WP, SMALL_ROWS = 0, 8, 16, 24, 32, 40, 48, 560


def _cparams(semantics, vmem_mib):
    return pltpu.CompilerParams(dimension_semantics=semantics, vmem_limit_bytes=vmem_mib * MIB)


def _dot(a, b):
    return jnp.dot(a, b, preferred_element_type=F32)


def _dot_nt(a, b):
    return lax.dot_general(a, b, (((1,), (1,)), ((), ())), preferred_element_type=F32)


def _dot_tn(a, b):
    return lax.dot_general(a, b, (((0,), (0,)), ((), ())), preferred_element_type=F32)


def _swap_halves(x):
    width = x.shape[1]
    lane = lax.broadcasted_iota(jnp.int32, x.shape, 1)
    ahead = pltpu.roll(x, width - HEAD_DIM // 2, 1)
    behind = pltpu.roll(x, HEAD_DIM // 2, 1)
    return jnp.where(lane % HEAD_DIM < HEAD_DIM // 2, ahead, behind)


def _rope(x, cos, sin):
    reps = x.shape[1] // LANES
    if reps > 1:
        cos = jnp.tile(cos, (1, reps))
        sin = jnp.tile(sin, (1, reps))
    return x * cos + _swap_halves(x) * sin


def _rope_tables(seq):
    half = HEAD_DIM // 2
    inv_freq = ROPE_THETA ** (-jnp.arange(half, dtype=F32) / half)
    ang = jnp.arange(seq).astype(F32)[:, None] * inv_freq[None, :]
    cos, sin = jnp.cos(ang), jnp.sin(ang)
    cos = jnp.tile(cos, (1, LANES // half))
    sin = jnp.tile(jnp.concatenate([-sin, sin], axis=1), (1, LANES // HEAD_DIM))
    return cos, sin


def _both_halves(x):
    lane = lax.broadcasted_iota(jnp.int32, x.shape, 1)
    other = pltpu.roll(x, HEAD_DIM, 1)
    low = lane < HEAD_DIM
    return jnp.where(low, x, other), jnp.where(low, other, x)


def _rms_backward(dh, xin, gain):
    r = lax.rsqrt(jnp.mean(xin * xin, axis=-1, keepdims=True) + EPS)
    xhat = xin * r
    dxhat = dh * gain
    dx = r * (dxhat - xhat * jnp.mean(dxhat * xhat, axis=-1, keepdims=True))
    return dx, jnp.sum(dh * xhat, axis=0, keepdims=True)


def _norm_inproj(x2d, gain, w_in, cos, sin, seq, tm):
    rows = x2d.shape[0]
    tiles_per_seq = seq // tm

    def body(x_ref, g_ref, w_ref, cos_ref, sin_ref, h_ref, q_ref, k_ref, v_ref, u_ref):
        x = x_ref[...]
        r = lax.rsqrt(jnp.mean(x * x, axis=-1, keepdims=True) + EPS)
        h = (x * r * g_ref[...]).astype(BF16)
        h_ref[...] = h
        proj = _dot(h, w_ref[...])
        cos_t, sin_t = cos_ref[...], sin_ref[...]
        q = _rope(proj[:, :ATTN_WIDTH], cos_t, sin_t) * (HEAD_DIM ** -0.5)
        q_ref[...] = q.astype(BF16)
        k = _rope(proj[:, ATTN_WIDTH:ATTN_WIDTH + KV_WIDTH], cos_t, sin_t)
        k0, k1 = _both_halves(k)
        k_ref[...] = jnp.concatenate([k0, k1], axis=1).astype(BF16)
        v0, v1 = _both_halves(proj[:, ATTN_WIDTH + KV_WIDTH:ATTN_WIDTH + 2 * KV_WIDTH])
        v_ref[...] = jnp.concatenate([v0, v1], axis=1).astype(BF16)
        u_ref[...] = proj[:, ATTN_WIDTH + 2 * KV_WIDTH:]

    row = lambda width: pl.BlockSpec((tm, width), lambda i: (i, 0))
    table = pl.BlockSpec((tm, LANES), lambda i: (i % tiles_per_seq, 0))
    return pl.pallas_call(
        body, name="norm_inproj", grid=(rows // tm,),
        in_specs=[row(D_MODEL), pl.BlockSpec((1, D_MODEL), lambda i: (0, 0)),
                  pl.BlockSpec((D_MODEL, IN_WIDTH), lambda i: (0, 0)), table, table],
        out_specs=[row(D_MODEL), row(ATTN_WIDTH), row(2 * KV_WIDTH), row(2 * KV_WIDTH), row(POOL_WIDTH)],
        out_shape=[jax.ShapeDtypeStruct((rows, D_MODEL), BF16), jax.ShapeDtypeStruct((rows, ATTN_WIDTH), BF16),
                   jax.ShapeDtypeStruct((rows, 2 * KV_WIDTH), BF16), jax.ShapeDtypeStruct((rows, 2 * KV_WIDTH), BF16),
                   jax.ShapeDtypeStruct((rows, POOL_WIDTH), F32)],
        compiler_params=_cparams(("parallel",), 40),
    )(x2d, gain, w_in, cos, sin)


def _lane_masks(dtype):
    lane = lax.broadcasted_iota(jnp.int32, (BLOCK, LANES), 1)
    return lane < HEAD_DIM, jnp.zeros((BLOCK, LANES), dtype)


def _head_operand(dup, head, low, zero):
    kv = head // Q_PER_KV
    block = dup[:, kv * LANES:(kv + 1) * LANES]
    return jnp.where(low, block, zero) if head % 2 == 0 else jnp.where(low, zero, block)


def _window_masks(n):
    qi = lax.broadcasted_iota(jnp.int32, (BLOCK, BLOCK), 0)
    kj = lax.broadcasted_iota(jnp.int32, (BLOCK, BLOCK), 1)
    return kj <= qi, jnp.logical_and(kj > qi, n > 0)


def _head_probs(q_pair, kc_h, kp_h, cur_mask, prev_mask, sink):
    sc = jnp.where(cur_mask, _dot_nt(q_pair, kc_h), NEG)
    sp = jnp.where(prev_mask, _dot_nt(q_pair, kp_h), NEG)
    m = jnp.maximum(jnp.maximum(jnp.max(sc, axis=-1, keepdims=True), jnp.max(sp, axis=-1, keepdims=True)), sink)
    pc, pp, ps = jnp.exp(sc - m), jnp.exp(sp - m), jnp.exp(sink - m)
    inv = 1.0 / (jnp.sum(pc, axis=-1, keepdims=True) + jnp.sum(pp, axis=-1, keepdims=True) + ps)
    return pc * inv, pp * inv, ps * inv


def _attention_forward(sinks, q, kd, vd, n_seq, seq):
    n_blocks = seq // BLOCK

    def body(sink_ref, q_ref, k_ref, v_ref, o_ref):
        low, zero = _lane_masks(BF16)

        def block(n, carry):
            r0 = pl.multiple_of(n * BLOCK, BLOCK)
            p0 = pl.multiple_of(jnp.maximum(n - 1, 0) * BLOCK, BLOCK)
            cur_mask, prev_mask = _window_masks(n)
            kc, kp = k_ref[pl.ds(r0, BLOCK), :], k_ref[pl.ds(p0, BLOCK), :]
            vc, vp = v_ref[pl.ds(r0, BLOCK), :], v_ref[pl.ds(p0, BLOCK), :]
            for pair in range(N_Q_HEADS // 2):
                q_pair = q_ref[pl.ds(r0, BLOCK), pair * LANES:(pair + 1) * LANES]
                out = jnp.zeros((BLOCK, LANES), F32)
                for head in (2 * pair, 2 * pair + 1):
                    pc, pp, _ = _head_probs(q_pair, _head_operand(kc, head, low, zero), _head_operand(kp, head, low, zero),
                                            cur_mask, prev_mask, sink_ref[0, head])
                    out += _dot(pc.astype(BF16), _head_operand(vc, head, low, zero))
                    out += _dot(pp.astype(BF16), _head_operand(vp, head, low, zero))
                o_ref[pl.ds(r0, BLOCK), pair * LANES:(pair + 1) * LANES] = out.astype(BF16)
            return carry

        lax.fori_loop(0, n_blocks, block, 0)

    seq_block = lambda width: pl.BlockSpec((seq, width), lambda b: (b, 0))
    return pl.pallas_call(
        body, name="attention_forward", grid=(n_seq,),
        in_specs=[pl.BlockSpec(memory_space=pltpu.SMEM), seq_block(ATTN_WIDTH), seq_block(2 * KV_WIDTH), seq_block(2 * KV_WIDTH)],
        out_specs=seq_block(ATTN_WIDTH),
        out_shape=jax.ShapeDtypeStruct((n_seq * seq, ATTN_WIDTH), BF16),
        compiler_params=_cparams(("parallel",), 40),
    )(sinks, q, kd, vd)


def _trailing(x, window, t, seq):
    k = 1
    while k < window:
        x = x + jnp.where(t >= k, pltpu.roll(x, k, 0), 0.0)
        k *= 2
    return x


def _leading(x, window, t, seq):
    k = 1
    while k < window:
        x = x + jnp.where(t < seq - k, pltpu.roll(x, seq - k, 0), 0.0)
        k *= 2
    return x


def _pool_features(u_g, window, t, seq):
    count = jnp.minimum(t + 1, window).astype(F32)
    return (_trailing(u_g, window, t, seq) / count - u_g).astype(BF16), count


def _pool_forward(u, w_pool, pool_scale, n_seq, seq):
    def body(u_ref, w_ref, s_ref, o_ref):
        t = lax.broadcasted_iota(jnp.int32, (seq, 1), 0)
        for g, window in enumerate(POOL_WINDOWS):
            cols = slice(g * POOL_GROUP_DIM, (g + 1) * POOL_GROUP_DIM)
            d, _ = _pool_features(u_ref[:, cols], window, t, seq)
            o_ref[:, cols] = (_dot(d, w_ref[g]) * s_ref[:, cols]).astype(BF16)

    seq_block = pl.BlockSpec((seq, POOL_WIDTH), lambda b: (b, 0))
    return pl.pallas_call(
        body, name="pool_forward", grid=(n_seq,),
        in_specs=[seq_block, pl.BlockSpec((len(POOL_WINDOWS), POOL_GROUP_DIM, POOL_GROUP_DIM), lambda b: (0, 0, 0)),
                  pl.BlockSpec((1, POOL_WIDTH), lambda b: (0, 0))],
        out_specs=seq_block,
        out_shape=jax.ShapeDtypeStruct((n_seq * seq, POOL_WIDTH), BF16),
        compiler_params=_cparams(("parallel",), 40),
    )(u, w_pool, pool_scale)


def _outproj_norm(x2d, attn, pool, w_out, gain, tm):
    rows = x2d.shape[0]

    def body(x_ref, a_ref, p_ref, w_ref, g_ref, x2_ref, h_ref):
        x2 = x_ref[...] + _dot(a_ref[...], w_ref[:ATTN_WIDTH, :]) + _dot(p_ref[...], w_ref[ATTN_WIDTH:, :])
        x2_ref[...] = x2
        r = lax.rsqrt(jnp.mean(x2 * x2, axis=-1, keepdims=True) + EPS)
        h_ref[...] = (x2 * r * g_ref[...]).astype(BF16)

    row = lambda width: pl.BlockSpec((tm, width), lambda i: (i, 0))
    return pl.pallas_call(
        body, name="outproj_norm", grid=(rows // tm,),
        in_specs=[row(D_MODEL), row(ATTN_WIDTH), row(POOL_WIDTH), pl.BlockSpec((D_MODEL, D_MODEL), lambda i: (0, 0)),
                  pl.BlockSpec((1, D_MODEL), lambda i: (0, 0))],
        out_specs=[row(D_MODEL), row(D_MODEL)],
        out_shape=[jax.ShapeDtypeStruct((rows, D_MODEL), F32), jax.ShapeDtypeStruct((rows, D_MODEL), BF16)],
        compiler_params=_cparams(("parallel",), 40),
    )(x2d, attn, pool, w_out, gain)


def _mlp_forward_loss(h2, x2, w_up, w_down, gain, target, tm):
    rows = h2.shape[0]
    last = N_DEV - 1

    def body(h_ref, x_ref, up_ref, down_ref, g_ref, t_ref, a_ref, dx_ref, dxb_ref, loss_ref, dg_ref, acc_ref):
        i, d = pl.program_id(0), pl.program_id(1)
        a = _dot(h_ref[...], up_ref[...])
        a_ref[...] = a
        r = jnp.maximum(a, 0.0)
        part = _dot((r * r).astype(BF16), down_ref[...])

        @pl.when(d == 0)
        def _():
            acc_ref[...] = x_ref[...] + part

        @pl.when(d > 0)
        def _():
            acc_ref[...] += part

        @pl.when(jnp.logical_and(i == 0, d == 0))
        def _():
            loss_ref[...] = jnp.zeros_like(loss_ref)
            dg_ref[...] = jnp.zeros_like(dg_ref)

        @pl.when(d == last)
        def _():
            x3 = acc_ref[...]
            rn = lax.rsqrt(jnp.mean(x3 * x3, axis=-1, keepdims=True) + EPS)
            err = x3 * rn * g_ref[...] - t_ref[...]
            loss_ref[...] += jnp.sum(err * err, axis=0, keepdims=True)
            dx, dg = _rms_backward(err / D_MODEL, x3, g_ref[...])
            dg_ref[...] += dg
            dx_ref[...] = dx
            dxb_ref[...] = dx.astype(BF16)

    row = lambda width: pl.BlockSpec((tm, width), lambda i, d: (i, 0))
    vec = pl.BlockSpec((1, D_MODEL), lambda i, d: (0, 0))
    return pl.pallas_call(
        body, name="mlp_forward_loss", grid=(rows // tm, N_DEV),
        in_specs=[row(D_MODEL), row(D_MODEL), pl.BlockSpec((None, D_MODEL, FF_BLOCK), lambda i, d: (d, 0, 0)),
                  pl.BlockSpec((FF_BLOCK, D_MODEL), lambda i, d: (d, 0)), vec, row(D_MODEL)],
        out_specs=[pl.BlockSpec((tm, FF_BLOCK), lambda i, d: (i, d)), row(D_MODEL), row(D_MODEL), vec, vec],
        out_shape=[jax.ShapeDtypeStruct((rows, D_FF), F32), jax.ShapeDtypeStruct((rows, D_MODEL), F32),
                   jax.ShapeDtypeStruct((rows, D_MODEL), BF16), jax.ShapeDtypeStruct((1, D_MODEL), F32),
                   jax.ShapeDtypeStruct((1, D_MODEL), F32)],
        scratch_shapes=[pltpu.VMEM((tm, D_MODEL), F32)],
        compiler_params=_cparams(("arbitrary", "arbitrary"), 48),
    )(h2, x2, w_up, w_down, gain, target)


def _mlp_backward_data(dx3b, a, w_down, w_up, dx3, x2, gain, w_out, attn, pool, tm):
    rows = dx3b.shape[0]
    last = N_DEV - 1

    def body(dxb_ref, a_ref, down_ref, up_ref, dx3_ref, x2_ref, g_ref, wo_ref, attn_ref, pool_ref,
             da_ref, f_ref, dx2_ref, dattn_ref, dpool_ref, dg_ref, dwo_ref, acc_ref):
        i, d = pl.program_id(0), pl.program_id(1)
        r = jnp.maximum(a_ref[...], 0.0)
        f_ref[...] = (r * r).astype(BF16)
        da = (_dot_nt(dxb_ref[...], down_ref[...]) * (2.0 * r)).astype(BF16)
        da_ref[...] = da
        part = _dot_nt(da, up_ref[...])

        @pl.when(d == 0)
        def _():
            acc_ref[...] = part

        @pl.when(d > 0)
        def _():
            acc_ref[...] += part

        @pl.when(jnp.logical_and(i == 0, d == 0))
        def _():
            dg_ref[...] = jnp.zeros_like(dg_ref)
            dwo_ref[...] = jnp.zeros_like(dwo_ref)

        @pl.when(d == last)
        def _():
            dnorm, dg = _rms_backward(acc_ref[...], x2_ref[...], g_ref[...])
            dg_ref[...] += dg
            dx2 = dx3_ref[...] + dnorm
            dx2_ref[...] = dx2
            dx2b = dx2.astype(BF16)
            dmix = _dot_nt(dx2b, wo_ref[...])
            dattn_ref[...] = dmix[:, :ATTN_WIDTH].astype(BF16)
            dpool_ref[...] = dmix[:, ATTN_WIDTH:]
            dwo_ref[:ATTN_WIDTH, :] += _dot_tn(attn_ref[...], dx2b)
            dwo_ref[ATTN_WIDTH:, :] += _dot_tn(pool_ref[...], dx2b)

    row = lambda width: pl.BlockSpec((tm, width), lambda i, d: (i, 0))
    hidden = pl.BlockSpec((tm, FF_BLOCK), lambda i, d: (i, d))
    vec = pl.BlockSpec((1, D_MODEL), lambda i, d: (0, 0))
    square = pl.BlockSpec((D_MODEL, D_MODEL), lambda i, d: (0, 0))
    return pl.pallas_call(
        body, name="mlp_backward_data", grid=(rows // tm, N_DEV),
        in_specs=[row(D_MODEL), hidden, pl.BlockSpec((FF_BLOCK, D_MODEL), lambda i, d: (d, 0)),
                  pl.BlockSpec((None, D_MODEL, FF_BLOCK), lambda i, d: (d, 0, 0)), row(D_MODEL), row(D_MODEL), vec, square,
                  row(ATTN_WIDTH), row(POOL_WIDTH)],
        out_specs=[hidden, hidden, row(D_MODEL), row(ATTN_WIDTH), row(POOL_WIDTH), vec, square],
        out_shape=[jax.ShapeDtypeStruct((rows, D_FF), BF16), jax.ShapeDtypeStruct((rows, D_FF), BF16),
                   jax.ShapeDtypeStruct((rows, D_MODEL), F32), jax.ShapeDtypeStruct((rows, ATTN_WIDTH), BF16),
                   jax.ShapeDtypeStruct((rows, POOL_WIDTH), F32), jax.ShapeDtypeStruct((1, D_MODEL), F32),
                   jax.ShapeDtypeStruct((D_MODEL, D_MODEL), F32)],
        scratch_shapes=[pltpu.VMEM((tm, D_MODEL), F32)],
        compiler_params=_cparams(("arbitrary", "arbitrary"), 56),
    )(dx3b, a, w_down, w_up, dx3, x2, gain, w_out, attn, pool)


def _mlp_backward_weights(f, da, dx3b, h2, tm):
    rows = f.shape[0]

    def body(f_ref, da_ref, dx_ref, h_ref, ddown_ref, dup_ref):
        @pl.when(pl.program_id(1) == 0)
        def _():
            ddown_ref[...] = jnp.zeros_like(ddown_ref)
            dup_ref[...] = jnp.zeros_like(dup_ref)

        ddown_ref[...] += _dot_tn(f_ref[...], dx_ref[...])
        dup_ref[...] += _dot_tn(h_ref[...], da_ref[...])

    hidden = pl.BlockSpec((tm, FF_BLOCK), lambda d, i: (i, d))
    row = pl.BlockSpec((tm, D_MODEL), lambda d, i: (i, 0))
    return pl.pallas_call(
        body, name="mlp_backward_weights", grid=(N_DEV, rows // tm),
        in_specs=[hidden, hidden, row, row],
        out_specs=[pl.BlockSpec((None, FF_BLOCK, D_MODEL), lambda d, i: (d, 0, 0)),
                   pl.BlockSpec((None, D_MODEL, FF_BLOCK), lambda d, i: (d, 0, 0))],
        out_shape=[jax.ShapeDtypeStruct((N_DEV, FF_BLOCK, D_MODEL), F32), jax.ShapeDtypeStruct((N_DEV, D_MODEL, FF_BLOCK), F32)],
        compiler_params=_cparams(("parallel", "arbitrary"), 40),
    )(f, da, dx3b, h2)


def _attention_backward(sinks, q, kd, vd, dout, cos, sin, n_seq, seq):
    n_blocks = seq // BLOCK

    def body(sink_ref, q_ref, k_ref, v_ref, do_ref, cos_ref, sin_ref, dq_ref, dk_ref, dv_ref, dsink_ref, dk_acc, dv_acc):
        low, zero = _lane_masks(BF16)
        lane_row = lax.broadcasted_iota(jnp.int32, (1, LANES), 1)

        @pl.when(pl.program_id(0) == 0)
        def _():
            dsink_ref[...] = jnp.zeros_like(dsink_ref)

        def fold(x):
            return x + pltpu.roll(x, HEAD_DIM, 1)

        def block(n, dsink):
            r0 = pl.multiple_of(n * BLOCK, BLOCK)
            p0 = pl.multiple_of(jnp.maximum(n - 1, 0) * BLOCK, BLOCK)
            cur_mask, prev_mask = _window_masks(n)
            kc, kp = k_ref[pl.ds(r0, BLOCK), :], k_ref[pl.ds(p0, BLOCK), :]
            vc, vp = v_ref[pl.ds(r0, BLOCK), :], v_ref[pl.ds(p0, BLOCK), :]
            dk_cur, dk_prev, dv_cur, dv_prev = [], [], [], []
            for kv in range(N_Q_HEADS // Q_PER_KV):
                acc = [jnp.zeros((BLOCK, LANES), F32) for _ in range(4)]
                for pair in range(kv * 2, kv * 2 + 2):
                    q_pair = q_ref[pl.ds(r0, BLOCK), pair * LANES:(pair + 1) * LANES]
                    do_pair = do_ref[pl.ds(r0, BLOCK), pair * LANES:(pair + 1) * LANES]
                    dq = jnp.zeros((BLOCK, LANES), F32)
                    picked = []
                    for head in (2 * pair, 2 * pair + 1):
                        kc_h, kp_h = _head_operand(kc, head, low, zero), _head_operand(kp, head, low, zero)
                        pc, pp, ps = _head_probs(q_pair, kc_h, kp_h, cur_mask, prev_mask, sink_ref[0, head])
                        dpc = _dot_nt(do_pair, _head_operand(vc, head, low, zero))
                        dpp = _dot_nt(do_pair, _head_operand(vp, head, low, zero))
                        delta = jnp.sum(pc * dpc, axis=-1, keepdims=True) + jnp.sum(pp * dpp, axis=-1, keepdims=True)
                        dsc, dsp = (pc * (dpc - delta)).astype(BF16), (pp * (dpp - delta)).astype(BF16)
                        dq += _dot(dsc, kc_h) + _dot(dsp, kp_h)
                        dsink = dsink + jnp.where(lane_row == head, -jnp.sum(ps * delta, axis=0, keepdims=True), 0.0)
                        picked.append((_dot_tn(dsc, q_pair), _dot_tn(dsp, q_pair),
                                       _dot_tn(pc.astype(BF16), do_pair), _dot_tn(pp.astype(BF16), do_pair)))
                    for j in range(4):
                        acc[j] += jnp.where(low, picked[0][j], picked[1][j])
                    dq = _rope(dq * (HEAD_DIM ** -0.5), cos_ref[pl.ds(r0, BLOCK), :], -sin_ref[pl.ds(r0, BLOCK), :])
                    dq_ref[pl.ds(r0, BLOCK), pair * LANES:(pair + 1) * LANES] = dq.astype(BF16)
                for j, out in enumerate((dk_cur, dk_prev, dv_cur, dv_prev)):
                    out.append(fold(acc[j]))
            merge = lambda parts: jnp.where(low, parts[0], parts[1])
            dk_acc[pl.ds(r0, BLOCK), :] = merge(dk_cur)
            dv_acc[pl.ds(r0, BLOCK), :] = merge(dv_cur)

            @pl.when(n > 0)
            def _():
                dk_acc[pl.ds(p0, BLOCK), :] += merge(dk_prev)
                dv_acc[pl.ds(p0, BLOCK), :] += merge(dv_prev)

            return dsink

        dsink_ref[...] += lax.fori_loop(0, n_blocks, block, jnp.zeros((1, LANES), F32))
        dk_ref[...] = _rope(dk_acc[...], cos_ref[...], -sin_ref[...]).astype(BF16)
        dv_ref[...] = dv_acc[...].astype(BF16)

    seq_block = lambda width: pl.BlockSpec((seq, width), lambda b: (b, 0))
    table = pl.BlockSpec((seq, LANES), lambda b: (0, 0))
    return pl.pallas_call(
        body, name="attention_backward", grid=(n_seq,),
        in_specs=[pl.BlockSpec(memory_space=pltpu.SMEM), seq_block(ATTN_WIDTH), seq_block(2 * KV_WIDTH), seq_block(2 * KV_WIDTH),
                  seq_block(ATTN_WIDTH), table, table],
        out_specs=[seq_block(ATTN_WIDTH), seq_block(KV_WIDTH), seq_block(KV_WIDTH), pl.BlockSpec((1, LANES), lambda b: (0, 0))],
        out_shape=[jax.ShapeDtypeStruct((n_seq * seq, ATTN_WIDTH), BF16), jax.ShapeDtypeStruct((n_seq * seq, KV_WIDTH), BF16),
                   jax.ShapeDtypeStruct((n_seq * seq, KV_WIDTH), BF16), jax.ShapeDtypeStruct((1, LANES), F32)],
        scratch_shapes=[pltpu.VMEM((seq, KV_WIDTH), F32), pltpu.VMEM((seq, KV_WIDTH), F32)],
        compiler_params=_cparams(("arbitrary",), 40),
    )(sinks, q, kd, vd, dout, cos, sin)


def _pool_backward(u, dpool, w_pool, pool_scale, n_seq, seq):
    groups = len(POOL_WINDOWS)

    def body(u_ref, dp_ref, w_ref, s_ref, du_ref, dw_ref, ds_ref):
        @pl.when(pl.program_id(0) == 0)
        def _():
            dw_ref[...] = jnp.zeros_like(dw_ref)
            ds_ref[...] = jnp.zeros_like(ds_ref)

        t = lax.broadcasted_iota(jnp.int32, (seq, 1), 0)
        for g, window in enumerate(POOL_WINDOWS):
            cols = slice(g * POOL_GROUP_DIM, (g + 1) * POOL_GROUP_DIM)
            d, count = _pool_features(u_ref[:, cols], window, t, seq)
            dpool_g = dp_ref[:, cols]
            ds_ref[:, cols] += jnp.sum(dpool_g * _dot(d, w_ref[g]), axis=0, keepdims=True)
            dy = (dpool_g * s_ref[:, cols]).astype(BF16)
            dw_ref[g] += _dot_tn(d, dy)
            dd = _dot_nt(dy, w_ref[g])
            du_ref[:, cols] = (_leading(dd / count, window, t, seq) - dd).astype(BF16)

    seq_block = pl.BlockSpec((seq, POOL_WIDTH), lambda b: (b, 0))
    weights = pl.BlockSpec((groups, POOL_GROUP_DIM, POOL_GROUP_DIM), lambda b: (0, 0, 0))
    scale = pl.BlockSpec((1, POOL_WIDTH), lambda b: (0, 0))
    return pl.pallas_call(
        body, name="pool_backward", grid=(n_seq,),
        in_specs=[seq_block, seq_block, weights, scale],
        out_specs=[seq_block, weights, scale],
        out_shape=[jax.ShapeDtypeStruct((n_seq * seq, POOL_WIDTH), BF16),
                   jax.ShapeDtypeStruct((groups, POOL_GROUP_DIM, POOL_GROUP_DIM), F32), jax.ShapeDtypeStruct((1, POOL_WIDTH), F32)],
        compiler_params=_cparams(("arbitrary",), 40),
    )(u, dpool, w_pool, pool_scale)


def _inproj_backward(dq, dk, dv, du, w_in, h1, x2d, dx2, gain, tm):
    rows = x2d.shape[0]

    def body(dq_ref, dk_ref, dv_ref, du_ref, w_ref, h_ref, x_ref, dx2_ref, g_ref, dx_ref, dw_ref, dg_ref):
        @pl.when(pl.program_id(0) == 0)
        def _():
            dw_ref[...] = jnp.zeros_like(dw_ref)
            dg_ref[...] = jnp.zeros_like(dg_ref)

        dproj = jnp.concatenate([dq_ref[...], dk_ref[...], dv_ref[...], du_ref[...]], axis=1)
        dw_ref[...] += _dot_tn(h_ref[...], dproj)
        dnorm, dg = _rms_backward(_dot_nt(dproj, w_ref[...]), x_ref[...], g_ref[...])
        dg_ref[...] += dg
        dx_ref[...] = dx2_ref[...] + dnorm

    row = lambda width: pl.BlockSpec((tm, width), lambda i: (i, 0))
    vec = pl.BlockSpec((1, D_MODEL), lambda i: (0, 0))
    weight = pl.BlockSpec((D_MODEL, IN_WIDTH), lambda i: (0, 0))
    return pl.pallas_call(
        body, name="inproj_backward", grid=(rows // tm,),
        in_specs=[row(ATTN_WIDTH), row(KV_WIDTH), row(KV_WIDTH), row(POOL_WIDTH), weight, row(D_MODEL), row(D_MODEL), row(D_MODEL), vec],
        out_specs=[row(D_MODEL), weight, vec],
        out_shape=[jax.ShapeDtypeStruct((rows, D_MODEL), F32), jax.ShapeDtypeStruct((D_MODEL, IN_WIDTH), F32),
                   jax.ShapeDtypeStruct((1, D_MODEL), F32)],
        compiler_params=_cparams(("arbitrary",), 48),
    )(dq, dk, dv, du, w_in, h1, x2d, dx2, gain)


def _place():
    return lax.axis_index("x"), lax.axis_index("y"), lax.axis_index("c")


def _peer(x, y, c, rel):
    return (1 - x if rel & 4 else x, 1 - y if rel & 2 else y, 1 - c if rel & 1 else c)


def _index(px, py, pc):
    return 4 * px + 2 * py + pc


def _gather_weights(shards):
    n = len(shards)

    def body(*refs):
        ins, outs = refs[:n], refs[n:2 * n]
        stage = refs[2 * n:3 * n]
        send_sems, recv_sems, local_sems = refs[3 * n:]
        x, y, c = _place()
        me, sibling = (x, y, c), (x, y, 1 - c)
        chips = [(1 - x, y), (x, 1 - y), (1 - x, 1 - y)]

        def copy(a, k, block, to, src=None):
            slot = outs[a].at[_index(*block)]
            return pltpu.make_async_remote_copy(src_ref=slot if src is None else src, dst_ref=slot, send_sem=send_sems.at[a, k],
                                                recv_sem=recv_sems.at[a, k], device_id=to, device_id_type=MESH)

        mine, first = [], []
        for a in range(n):
            stage[a][...] = ins[a][...].astype(BF16)
            mine.append(pltpu.make_async_copy(stage[a], outs[a].at[_index(*me)], local_sems.at[a]))
            mine[-1].start()
            first.append(copy(a, 0, me, sibling, src=stage[a]))
            first += [copy(a, 1 + j, me, (*chip, c), src=stage[a]) for j, chip in enumerate(chips)]
        for cp in first:
            cp.start()
        passed = []
        for j, chip in enumerate(chips):
            for a in range(n):
                copy(a, 1 + j, (*chip, c), me).wait_recv()
                passed.append(copy(a, 4 + j, (*chip, c), sibling))
                passed[-1].start()
        for a in range(n):
            copy(a, 0, sibling, me).wait_recv()
            for j, chip in enumerate(chips):
                copy(a, 4 + j, (*chip, 1 - c), me).wait_recv()
        for cp in first + passed:
            cp.wait_send()
        for cp in mine:
            cp.wait()

    return pl.pallas_call(
        body, name="gather_weights",
        in_specs=[pl.BlockSpec(memory_space=pltpu.VMEM)] * n,
        out_specs=[pl.BlockSpec(memory_space=pl.ANY)] * n,
        out_shape=[jax.ShapeDtypeStruct((N_DEV,) + s.shape, BF16) for s in shards],
        scratch_shapes=[pltpu.VMEM(s.shape, BF16) for s in shards]
        + [pltpu.SemaphoreType.DMA((n, 7)), pltpu.SemaphoreType.DMA((n, 7)), pltpu.SemaphoreType.DMA((n,))],
        compiler_params=pltpu.CompilerParams(vmem_limit_bytes=32 * MIB),
    )(*shards)


def _exchange_gradients(grads, small):
    n = len(grads)

    def body(*refs):
        ins, small_ref = refs[:n], refs[n]
        outs, total_ref = refs[n + 1:2 * n + 1], refs[2 * n + 1]
        gathered, send_sems, recv_sems, small_send, small_recv = refs[2 * n + 2:]
        x, y, c = _place()
        me = _index(x, y, c)
        gathered[0] = small_ref[...]
        copies = []
        for rel in range(1, N_DEV):
            to = _peer(x, y, c, rel)
            copies.append(pltpu.make_async_remote_copy(src_ref=small_ref, dst_ref=gathered.at[rel], send_sem=small_send.at[rel - 1],
                                                       recv_sem=small_recv.at[rel - 1], device_id=to, device_id_type=MESH))
            for a in range(n):
                copies.append(pltpu.make_async_remote_copy(src_ref=ins[a].at[_index(*to)], dst_ref=outs[a].at[rel - 1],
                                                           send_sem=send_sems.at[a, rel - 1], recv_sem=recv_sems.at[a, rel - 1],
                                                           device_id=to, device_id_type=MESH))
        for cp in copies:
            cp.start()
        for cp in copies:
            cp.wait()
        total = gathered[me]
        for source in range(1, N_DEV):
            total = total + gathered[jnp.bitwise_xor(me, source)]
        total_ref[...] = total

    return pl.pallas_call(
        body, name="exchange_gradients",
        in_specs=[pl.BlockSpec(memory_space=pl.ANY)] * n + [pl.BlockSpec(memory_space=pltpu.VMEM)],
        out_specs=[pl.BlockSpec(memory_space=pl.ANY)] * n + [pl.BlockSpec(memory_space=pltpu.VMEM)],
        out_shape=[jax.ShapeDtypeStruct((N_DEV - 1,) + g.shape[1:], F32) for g in grads] + [jax.ShapeDtypeStruct(small.shape, F32)],
        scratch_shapes=[pltpu.VMEM((N_DEV,) + small.shape, F32), pltpu.SemaphoreType.DMA((n, 7)), pltpu.SemaphoreType.DMA((n, 7)),
                        pltpu.SemaphoreType.DMA((7,)), pltpu.SemaphoreType.DMA((7,))],
        compiler_params=pltpu.CompilerParams(vmem_limit_bytes=32 * MIB),
    )(*grads, small)


def _adamw_math(w, g, m, v):
    m = ADAM_B1 * m + (1.0 - ADAM_B1) * g
    v = ADAM_B2 * v + (1.0 - ADAM_B2) * (g * g)
    m_hat = m / (1.0 - ADAM_B1 ** ADAM_STEP)
    v_hat = v / (1.0 - ADAM_B2 ** ADAM_STEP)
    return -ADAM_LR * (m_hat / (jnp.sqrt(v_hat) + ADAM_EPS) + ADAM_WD * w), m, v


def _adamw_sharded(me, own, received, w, m, v, tr):
    rows, cols = w.shape

    def body(me_ref, own_ref, rec_ref, w_ref, m_ref, v_ref, g_ref, d_ref, nm_ref, nv_ref):
        g = own_ref[...]
        for r in range(N_DEV - 1):
            g = g + rec_ref[r]
        g_ref[...] = g
        d_ref[...], nm_ref[...], nv_ref[...] = _adamw_math(w_ref[...], g, m_ref[...], v_ref[...])

    tile = pl.BlockSpec((tr, cols), lambda i, me_ref: (i, 0))
    shape = jax.ShapeDtypeStruct((rows, cols), F32)
    return pl.pallas_call(
        body, name="adamw_sharded",
        grid_spec=pltpu.PrefetchScalarGridSpec(
            num_scalar_prefetch=1, grid=(rows // tr,),
            in_specs=[pl.BlockSpec((None, tr, cols), lambda i, me_ref: (me_ref[0], i, 0)),
                      pl.BlockSpec((N_DEV - 1, tr, cols), lambda i, me_ref: (0, i, 0)), tile, tile, tile],
            out_specs=[tile, tile, tile, tile]),
        out_shape=[shape, shape, shape, shape],
        compiler_params=_cparams(("parallel",), 40),
    )(me, own, received, w, m, v)


def _adamw_small(w, g, m, v):
    def body(w_ref, g_ref, m_ref, v_ref, d_ref, nm_ref, nv_ref):
        d_ref[...], nm_ref[...], nv_ref[...] = _adamw_math(w_ref[...], g_ref[...], m_ref[...], v_ref[...])

    shape = jax.ShapeDtypeStruct(w.shape, F32)
    return pl.pallas_call(body, name="adamw_small", out_shape=[shape, shape, shape])(w, g, m, v)


def _pack_small(g1, g2, g3, ps, sink, loss, wp):
    def rows(a, n):
        a = a.reshape(-1, LANES)
        return jnp.pad(a, ((0, n - a.shape[0]), (0, 0)))

    sink = jnp.pad(sink.reshape(1, -1), ((0, 0), (0, LANES - sink.size)))
    return jnp.concatenate([rows(g1, 8), rows(g2, 8), rows(g3, 8), rows(ps, 8), rows(sink, 8), rows(loss, 8), rows(wp, 512)], axis=0)


def _unpack_small(p, like):
    g1, g2, g3, ps, sink, wp = like
    return (p[ROW_G1:ROW_G1 + 8].reshape(g1.shape), p[ROW_G2:ROW_G2 + 8].reshape(g2.shape), p[ROW_G3:ROW_G3 + 8].reshape(g3.shape),
            p[ROW_PS:ROW_PS + 4].reshape(ps.shape), p[ROW_SINK, :N_Q_HEADS].reshape(sink.shape), p[ROW_WP:ROW_WP + 512].reshape(wp.shape))


def _local_step(x, target, attn_norm_g, w_in_full, attn_sinks, w_pool, pool_scale, w_out_full, mlp_norm_g, w_up_blocks,
                w_down_full, final_norm_g):
    n_seq, seq, _ = x.shape
    rows = n_seq * seq
    tm = min(512, seq)
    x2d, t2d = x.reshape(rows, D_MODEL), target.reshape(rows, D_MODEL)
    g3 = final_norm_g.reshape(1, D_MODEL)
    cos, sin = _rope_tables(seq)
    wp_b = w_pool[0].astype(BF16)

    h1, q, kd, vd, u = _norm_inproj(x2d, attn_norm_g, w_in_full, cos, sin, seq, tm)
    attn = _attention_forward(attn_sinks, q, kd, vd, n_seq, seq)
    pool = _pool_forward(u, wp_b, pool_scale, n_seq, seq)
    x2, h2 = _outproj_norm(x2d, attn, pool, w_out_full, mlp_norm_g, tm)
    a, dx3, dx3b, loss_cols, dg3 = _mlp_forward_loss(h2, x2, w_up_blocks, w_down_full, g3, t2d, tm)

    da, f, dx2, dattn, dpool, dg2, d_w_out = _mlp_backward_data(dx3b, a, w_down_full, w_up_blocks, dx3, x2, mlp_norm_g, w_out_full,
                                                               attn, pool, tm)
    d_w_down, d_w_up = _mlp_backward_weights(f, da, dx3b, h2, min(1024, seq))
    dq, dk, dv, dsink = _attention_backward(attn_sinks, q, kd, vd, dattn, cos, sin, n_seq, seq)
    du, d_w_pool, d_pool_scale = _pool_backward(u, dpool, wp_b, pool_scale, n_seq, seq)
    grad_x, d_w_in, dg1 = _inproj_backward(dq, dk, dv, du, w_in_full, h1, x2d, dx2, attn_norm_g, tm)
    return (grad_x.reshape(x.shape), loss_cols, dg1, d_w_in, dsink, d_w_pool, d_pool_scale, d_w_out, dg2, d_w_up, d_w_down, dg3)


def kernel(x, attn_norm_g, w_in, attn_sinks, w_pool, pool_scale, w_out, mlp_norm_g, w_up, w_down, final_norm_g, loss_target, m_attn_norm_g, m_w_in, m_attn_sinks, m_w_pool, m_pool_scale, m_w_out, m_mlp_norm_g, m_w_up, m_w_down, m_final_norm_g, v_attn_norm_g, v_w_in, v_attn_sinks, v_w_pool, v_pool_scale, v_w_out, v_mlp_norm_g, v_w_up, v_w_down, v_final_norm_g):
    me = (4 * lax.axis_index("x") + 2 * lax.axis_index("y") + lax.axis_index("c")).astype(jnp.int32).reshape(1)

    win_g, wout_g, wup_g, wdown_g = _gather_weights([w_in[0], w_out[0], w_up[0], w_down[0]])
    w_in_full = jnp.transpose(win_g, (1, 0, 2)).reshape(D_MODEL, IN_WIDTH)
    (grad_x, loss_cols, dg1, d_w_in, dsink, d_w_pool, d_pool_scale, d_w_out, dg2, d_w_up, d_w_down, dg3) = _local_step(
        x, loss_target, attn_norm_g, w_in_full, attn_sinks, w_pool, pool_scale, wout_g.reshape(D_MODEL, D_MODEL), mlp_norm_g, wup_g,
        wdown_g.reshape(D_FF, D_MODEL), final_norm_g)

    big = [jnp.transpose(d_w_in.reshape(D_MODEL, N_DEV, IN_BLOCK), (1, 0, 2)), d_w_out.reshape(N_DEV, OUT_BLOCK, D_MODEL), d_w_up, d_w_down]
    small = _pack_small(dg1, dg2, dg3, d_pool_scale, dsink[0, :N_Q_HEADS], loss_cols, d_w_pool)
    *received, small_sum = _exchange_gradients(big, small)

    loss = 0.5 * jnp.sum(small_sum[ROW_LOSS:ROW_LOSS + 8]) / D_MODEL
    shards = [(w_in, m_w_in, v_w_in, 1024), (w_out, m_w_out, v_w_out, 128), (w_up, m_w_up, v_w_up, 256), (w_down, m_w_down, v_w_down, 256)]
    sharded = [_adamw_sharded(me, own, rec, w[0], m[0], v[0], tr) for own, rec, (w, m, v, tr) in zip(big, received, shards)]
    g_in, g_out, g_up, g_down = [[r[None] for r in res] for res in sharded]

    like = (attn_norm_g, mlp_norm_g, final_norm_g, pool_scale, attn_sinks, w_pool)
    zeros = jnp.zeros((1, D_MODEL), F32)
    pack = lambda t: _pack_small(t[0], t[1], t[2], t[3], t[4], zeros, t[5])
    small_delta, small_m, small_v = _adamw_small(
        pack(like), small_sum, pack((m_attn_norm_g, m_mlp_norm_g, m_final_norm_g, m_pool_scale, m_attn_sinks, m_w_pool)),
        pack((v_attn_norm_g, v_mlp_norm_g, v_final_norm_g, v_pool_scale, v_attn_sinks, v_w_pool)))
    s_grad, s_delta, s_m, s_v = [_unpack_small(p, like) for p in (small_sum, small_delta, small_m, small_v)]

    def ordered(small_parts, k):
        n1, n2, n3, ps, sink, wp = small_parts
        return [n1, g_in[k], sink, wp, ps, g_out[k], n2, g_up[k], g_down[k], n3]

    return (loss, grad_x, *ordered(s_grad, 0), *ordered(s_delta, 1), *ordered(s_m, 2), *ordered(s_v, 3))
```

```python
import functools

import jax
import jax.numpy as jnp
from jax import lax
from jax.experimental import pallas as pl
from jax.experimental.pallas import tpu as pltpu

F32 = jnp.float32
BF16 = jnp.bfloat16

D_MODEL = 1024
HEAD_DIM = 64
N_Q_HEADS = 8
Q_PER_KV = 4
ATTN_WIDTH = 512
KV_WIDTH = 128
BLOCK = 128
ROPE_THETA = 10000.0
POOL_WINDOWS = (2, 4, 8, 16)
POOL_WIDTH = 512
POOL_GROUP_DIM = 128
IN_WIDTH = 1280
D_FF = 4096
EPS = 1e-6
N_DEV = 8
FF_BLOCK = D_FF // N_DEV
IN_BLOCK = IN_WIDTH // N_DEV
OUT_BLOCK = D_MODEL // N_DEV
ADAM_LR = 0.001
ADAM_B1 = 0.9
ADAM_B2 = 0.999
ADAM_EPS = 1e-08
ADAM_WD = 0.01
ADAM_STEP = 10
NEG = -1e30
LANES = 128
MIB = 1024 * 1024
MESH = pl.DeviceIdType.MESH

ROW_G1, ROW_G2, ROW_G3, ROW_PS, ROW_SINK, ROW_LOSS, ROW_WP, SMALL_ROWS = 0, 8, 16, 24, 32, 40, 48, 560


def _cparams(semantics, vmem_mib):
    return pltpu.CompilerParams(dimension_semantics=semantics, vmem_limit_bytes=vmem_mib * MIB)


def _dot(a, b):
    return jnp.dot(a, b, preferred_element_type=F32)


def _dot_nt(a, b):
    return lax.dot_general(a, b, (((1,), (1,)), ((), ())), preferred_element_type=F32)


def _dot_tn(a, b):
    return lax.dot_general(a, b, (((0,), (0,)), ((), ())), preferred_element_type=F32)


def _swap_halves(x):
    width = x.shape[1]
    lane = lax.broadcasted_iota(jnp.int32, x.shape, 1)
    ahead = pltpu.roll(x, width - HEAD_DIM // 2, 1)
    behind = pltpu.roll(x, HEAD_DIM // 2, 1)
    return jnp.where(lane % HEAD_DIM < HEAD_DIM // 2, ahead, behind)


def _rope(x, cos, sin):
    reps = x.shape[1] // LANES
    if reps > 1:
        cos = jnp.tile(cos, (1, reps))
        sin = jnp.tile(sin, (1, reps))
    return x * cos + _swap_halves(x) * sin


def _rope_tables(seq):
    half = HEAD_DIM // 2
    inv_freq = ROPE_THETA ** (-jnp.arange(half, dtype=F32) / half)
    ang = jnp.arange(seq).astype(F32)[:, None] * inv_freq[None, :]
    cos, sin = jnp.cos(ang), jnp.sin(ang)
    cos = jnp.tile(cos, (1, LANES // half))
    sin = jnp.tile(jnp.concatenate([-sin, sin], axis=1), (1, LANES // HEAD_DIM))
    return cos, sin


def _both_halves(x):
    lane = lax.broadcasted_iota(jnp.int32, x.shape, 1)
    other = pltpu.roll(x, HEAD_DIM, 1)
    low = lane < HEAD_DIM
    return jnp.where(low, x, other), jnp.where(low, other, x)


def _rms_backward(dh, xin, gain):
    r = lax.rsqrt(jnp.mean(xin * xin, axis=-1, keepdims=True) + EPS)
    xhat = xin * r
    dxhat = dh * gain
    dx = r * (dxhat - xhat * jnp.mean(dxhat * xhat, axis=-1, keepdims=True))
    return dx, jnp.sum(dh * xhat, axis=0, keepdims=True)


def _norm_inproj(x2d, gain, w_in, cos, sin, token, seq, tm):
    rows = x2d.shape[0]
    tiles_per_seq = seq // tm

    def body(x_ref, g_ref, w_ref, cos_ref, sin_ref, token_ref, h_ref, q_ref, k_ref, v_ref, u_ref):
        x = x_ref[...]
        r = lax.rsqrt(jnp.mean(x * x, axis=-1, keepdims=True) + EPS)
        h = (x * r * g_ref[...]).astype(BF16)
        h_ref[...] = h
        proj = _dot(h, w_ref[...])
        cos_t, sin_t = cos_ref[...], sin_ref[...]
        q = _rope(proj[:, :ATTN_WIDTH], cos_t, sin_t) * (HEAD_DIM ** -0.5)
        q_ref[...] = q.astype(BF16)
        k = _rope(proj[:, ATTN_WIDTH:ATTN_WIDTH + KV_WIDTH], cos_t, sin_t)
        k0, k1 = _both_halves(k)
        k_ref[...] = jnp.concatenate([k0, k1], axis=1).astype(BF16)
        v0, v1 = _both_halves(proj[:, ATTN_WIDTH + KV_WIDTH:ATTN_WIDTH + 2 * KV_WIDTH])
        v_ref[...] = jnp.concatenate([v0, v1], axis=1).astype(BF16)
        u_ref[...] = proj[:, ATTN_WIDTH + 2 * KV_WIDTH:]

    row = lambda width: pl.BlockSpec((tm, width), lambda i: (i, 0))
    table = pl.BlockSpec((tm, LANES), lambda i: (i % tiles_per_seq, 0))
    return pl.pallas_call(
        body, name="norm_inproj", grid=(rows // tm,),
        in_specs=[row(D_MODEL), pl.BlockSpec((1, D_MODEL), lambda i: (0, 0)),
                  pl.BlockSpec((D_MODEL, IN_WIDTH), lambda i: (0, 0)), table, table, pl.BlockSpec((8, LANES), lambda i: (0, 0))],
        out_specs=[row(D_MODEL), row(ATTN_WIDTH), row(2 * KV_WIDTH), row(2 * KV_WIDTH), row(POOL_WIDTH)],
        out_shape=[jax.ShapeDtypeStruct((rows, D_MODEL), BF16), jax.ShapeDtypeStruct((rows, ATTN_WIDTH), BF16),
                   jax.ShapeDtypeStruct((rows, 2 * KV_WIDTH), BF16), jax.ShapeDtypeStruct((rows, 2 * KV_WIDTH), BF16),
                   jax.ShapeDtypeStruct((rows, POOL_WIDTH), F32)],
        compiler_params=_cparams(("parallel",), 40),
    )(x2d, gain, w_in, cos, sin, token)


def _lane_masks(dtype):
    lane = lax.broadcasted_iota(jnp.int32, (BLOCK, LANES), 1)
    return lane < HEAD_DIM, jnp.zeros((BLOCK, LANES), dtype)


def _head_operand(dup, head, low, zero):
    kv = head // Q_PER_KV
    block = dup[:, kv * LANES:(kv + 1) * LANES]
    return jnp.where(low, block, zero) if head % 2 == 0 else jnp.where(low, zero, block)


def _window_masks(n):
    qi = lax.broadcasted_iota(jnp.int32, (BLOCK, BLOCK), 0)
    kj = lax.broadcasted_iota(jnp.int32, (BLOCK, BLOCK), 1)
    return kj <= qi, jnp.logical_and(kj > qi, n > 0)


def _head_probs(q_pair, kc_h, kp_h, cur_mask, prev_mask, sink):
    sc = jnp.where(cur_mask, _dot_nt(q_pair, kc_h), NEG)
    sp = jnp.where(prev_mask, _dot_nt(q_pair, kp_h), NEG)
    m = jnp.maximum(jnp.maximum(jnp.max(sc, axis=-1, keepdims=True), jnp.max(sp, axis=-1, keepdims=True)), sink)
    pc, pp, ps = jnp.exp(sc - m), jnp.exp(sp - m), jnp.exp(sink - m)
    inv = 1.0 / (jnp.sum(pc, axis=-1, keepdims=True) + jnp.sum(pp, axis=-1, keepdims=True) + ps)
    return pc * inv, pp * inv, ps * inv


def _attention_forward(sinks, q, kd, vd, n_seq, seq):
    n_blocks = seq // BLOCK

    def body(sink_ref, q_ref, k_ref, v_ref, o_ref):
        low, zero = _lane_masks(BF16)

        def block(n, carry):
            r0 = pl.multiple_of(n * BLOCK, BLOCK)
            p0 = pl.multiple_of(jnp.maximum(n - 1, 0) * BLOCK, BLOCK)
            cur_mask, prev_mask = _window_masks(n)
            kc, kp = k_ref[pl.ds(r0, BLOCK), :], k_ref[pl.ds(p0, BLOCK), :]
            vc, vp = v_ref[pl.ds(r0, BLOCK), :], v_ref[pl.ds(p0, BLOCK), :]
            for pair in range(N_Q_HEADS // 2):
                q_pair = q_ref[pl.ds(r0, BLOCK), pair * LANES:(pair + 1) * LANES]
                out = jnp.zeros((BLOCK, LANES), F32)
                for head in (2 * pair, 2 * pair + 1):
                    pc, pp, _ = _head_probs(q_pair, _head_operand(kc, head, low, zero), _head_operand(kp, head, low, zero),
                                            cur_mask, prev_mask, sink_ref[0, head])
                    out += _dot(pc.astype(BF16), _head_operand(vc, head, low, zero))
                    out += _dot(pp.astype(BF16), _head_operand(vp, head, low, zero))
                o_ref[pl.ds(r0, BLOCK), pair * LANES:(pair + 1) * LANES] = out.astype(BF16)
            return carry

        lax.fori_loop(0, n_blocks, block, 0)

    seq_block = lambda width: pl.BlockSpec((seq, width), lambda b: (b, 0))
    return pl.pallas_call(
        body, name="attention_forward", grid=(n_seq,),
        in_specs=[pl.BlockSpec(memory_space=pltpu.SMEM), seq_block(ATTN_WIDTH), seq_block(2 * KV_WIDTH), seq_block(2 * KV_WIDTH)],
        out_specs=seq_block(ATTN_WIDTH),
        out_shape=jax.ShapeDtypeStruct((n_seq * seq, ATTN_WIDTH), BF16),
        compiler_params=_cparams(("parallel",), 40),
    )(sinks, q, kd, vd)


def _trailing(x, window, t, seq):
    k = 1
    while k < window:
        x = x + jnp.where(t >= k, pltpu.roll(x, k, 0), 0.0)
        k *= 2
    return x


def _leading(x, window, t, seq):
    k = 1
    while k < window:
        x = x + jnp.where(t < seq - k, pltpu.roll(x, seq - k, 0), 0.0)
        k *= 2
    return x


def _pool_features(u_g, window, t, seq):
    count = jnp.minimum(t + 1, window).astype(F32)
    return (_trailing(u_g, window, t, seq) / count - u_g).astype(BF16), count


def _pool_forward(u, w_pool, pool_scale, n_seq, seq):
    def body(u_ref, w_ref, s_ref, o_ref):
        t = lax.broadcasted_iota(jnp.int32, (seq, 1), 0)
        for g, window in enumerate(POOL_WINDOWS):
            cols = slice(g * POOL_GROUP_DIM, (g + 1) * POOL_GROUP_DIM)
            d, _ = _pool_features(u_ref[:, cols], window, t, seq)
            o_ref[:, cols] = (_dot(d, w_ref[g]) * s_ref[:, cols]).astype(BF16)

    seq_block = pl.BlockSpec((seq, POOL_WIDTH), lambda b: (b, 0))
    return pl.pallas_call(
        body, name="pool_forward", grid=(n_seq,),
        in_specs=[seq_block, pl.BlockSpec((len(POOL_WINDOWS), POOL_GROUP_DIM, POOL_GROUP_DIM), lambda b: (0, 0, 0)),
                  pl.BlockSpec((1, POOL_WIDTH), lambda b: (0, 0))],
        out_specs=seq_block,
        out_shape=jax.ShapeDtypeStruct((n_seq * seq, POOL_WIDTH), BF16),
        compiler_params=_cparams(("parallel",), 40),
    )(u, w_pool, pool_scale)


def _outproj_norm(x2d, attn, pool, w_out, gain, tm):
    rows = x2d.shape[0]

    def body(x_ref, a_ref, p_ref, w_ref, g_ref, x2_ref, h_ref):
        x2 = x_ref[...] + _dot(a_ref[...], w_ref[:ATTN_WIDTH, :]) + _dot(p_ref[...], w_ref[ATTN_WIDTH:, :])
        x2_ref[...] = x2
        r = lax.rsqrt(jnp.mean(x2 * x2, axis=-1, keepdims=True) + EPS)
        h_ref[...] = (x2 * r * g_ref[...]).astype(BF16)

    row = lambda width: pl.BlockSpec((tm, width), lambda i: (i, 0))
    return pl.pallas_call(
        body, name="outproj_norm", grid=(rows // tm,),
        in_specs=[row(D_MODEL), row(ATTN_WIDTH), row(POOL_WIDTH), pl.BlockSpec((D_MODEL, D_MODEL), lambda i: (0, 0)),
                  pl.BlockSpec((1, D_MODEL), lambda i: (0, 0))],
        out_specs=[row(D_MODEL), row(D_MODEL)],
        out_shape=[jax.ShapeDtypeStruct((rows, D_MODEL), F32), jax.ShapeDtypeStruct((rows, D_MODEL), BF16)],
        compiler_params=_cparams(("parallel",), 40),
    )(x2d, attn, pool, w_out, gain)


def _mlp_forward_loss(h2, x2, w_up, w_down, gain, target, tm):
    rows = h2.shape[0]
    last = N_DEV - 1

    def body(h_ref, x_ref, up_ref, down_ref, g_ref, t_ref, a_ref, dx_ref, dxb_ref, loss_ref, dg_ref, acc_ref):
        i, d = pl.program_id(0), pl.program_id(1)
        a = _dot(h_ref[...], up_ref[...])
        a_ref[...] = a
        r = jnp.maximum(a, 0.0)
        part = _dot((r * r).astype(BF16), down_ref[...])

        @pl.when(d == 0)
        def _():
            acc_ref[...] = x_ref[...] + part

        @pl.when(d > 0)
        def _():
            acc_ref[...] += part

        @pl.when(jnp.logical_and(i == 0, d == 0))
        def _():
            loss_ref[...] = jnp.zeros_like(loss_ref)
            dg_ref[...] = jnp.zeros_like(dg_ref)

        @pl.when(d == last)
        def _():
            x3 = acc_ref[...]
            rn = lax.rsqrt(jnp.mean(x3 * x3, axis=-1, keepdims=True) + EPS)
            err = x3 * rn * g_ref[...] - t_ref[...]
            loss_ref[...] += jnp.sum(err * err, axis=0, keepdims=True)
            dx, dg = _rms_backward(err / D_MODEL, x3, g_ref[...])
            dg_ref[...] += dg
            dx_ref[...] = dx
            dxb_ref[...] = dx.astype(BF16)

    row = lambda width: pl.BlockSpec((tm, width), lambda i, d: (i, 0))
    vec = pl.BlockSpec((1, D_MODEL), lambda i, d: (0, 0))
    return pl.pallas_call(
        body, name="mlp_forward_loss", grid=(rows // tm, N_DEV),
        in_specs=[row(D_MODEL), row(D_MODEL), pl.BlockSpec((None, D_MODEL, FF_BLOCK), lambda i, d: (d, 0, 0)),
                  pl.BlockSpec((FF_BLOCK, D_MODEL), lambda i, d: (d, 0)), vec, row(D_MODEL)],
        out_specs=[pl.BlockSpec((tm, FF_BLOCK), lambda i, d: (i, d)), row(D_MODEL), row(D_MODEL), vec, vec],
        out_shape=[jax.ShapeDtypeStruct((rows, D_FF), F32), jax.ShapeDtypeStruct((rows, D_MODEL), F32),
                   jax.ShapeDtypeStruct((rows, D_MODEL), BF16), jax.ShapeDtypeStruct((1, D_MODEL), F32),
                   jax.ShapeDtypeStruct((1, D_MODEL), F32)],
        scratch_shapes=[pltpu.VMEM((tm, D_MODEL), F32)],
        compiler_params=_cparams(("arbitrary", "arbitrary"), 48),
    )(h2, x2, w_up, w_down, gain, target)


def _mlp_backward_data(dx3b, a, w_down, w_up, dx3, x2, gain, w_out, attn, pool, tm):
    rows = dx3b.shape[0]
    last = N_DEV - 1

    def body(dxb_ref, a_ref, down_ref, up_ref, dx3_ref, x2_ref, g_ref, wo_ref, attn_ref, pool_ref,
             da_ref, f_ref, dx2_ref, dattn_ref, dpool_ref, dg_ref, dwo_ref, acc_ref):
        i, d = pl.program_id(0), pl.program_id(1)
        r = jnp.maximum(a_ref[...], 0.0)
        f_ref[...] = (r * r).astype(BF16)
        da = (_dot_nt(dxb_ref[...], down_ref[...]) * (2.0 * r)).astype(BF16)
        da_ref[...] = da
        part = _dot_nt(da, up_ref[...])

        @pl.when(d == 0)
        def _():
            acc_ref[...] = part

        @pl.when(d > 0)
        def _():
            acc_ref[...] += part

        @pl.when(jnp.logical_and(i == 0, d == 0))
        def _():
            dg_ref[...] = jnp.zeros_like(dg_ref)
            dwo_ref[...] = jnp.zeros_like(dwo_ref)

        @pl.when(d == last)
        def _():
            dnorm, dg = _rms_backward(acc_ref[...], x2_ref[...], g_ref[...])
            dg_ref[...] += dg
            dx2 = dx3_ref[...] + dnorm
            dx2_ref[...] = dx2
            dx2b = dx2.astype(BF16)
            dmix = _dot_nt(dx2b, wo_ref[...])
            dattn_ref[...] = dmix[:, :ATTN_WIDTH].astype(BF16)
            dpool_ref[...] = dmix[:, ATTN_WIDTH:]
            dwo_ref[:ATTN_WIDTH, :] += _dot_tn(attn_ref[...], dx2b)
            dwo_ref[ATTN_WIDTH:, :] += _dot_tn(pool_ref[...], dx2b)

    row = lambda width: pl.BlockSpec((tm, width), lambda i, d: (i, 0))
    hidden = pl.BlockSpec((tm, FF_BLOCK), lambda i, d: (i, d))
    vec = pl.BlockSpec((1, D_MODEL), lambda i, d: (0, 0))
    square = pl.BlockSpec((D_MODEL, D_MODEL), lambda i, d: (0, 0))
    return pl.pallas_call(
        body, name="mlp_backward_data", grid=(rows // tm, N_DEV),
        in_specs=[row(D_MODEL), hidden, pl.BlockSpec((FF_BLOCK, D_MODEL), lambda i, d: (d, 0)),
                  pl.BlockSpec((None, D_MODEL, FF_BLOCK), lambda i, d: (d, 0, 0)), row(D_MODEL), row(D_MODEL), vec, square,
                  row(ATTN_WIDTH), row(POOL_WIDTH)],
        out_specs=[hidden, hidden, row(D_MODEL), row(ATTN_WIDTH), row(POOL_WIDTH), vec, square],
        out_shape=[jax.ShapeDtypeStruct((rows, D_FF), BF16), jax.ShapeDtypeStruct((rows, D_FF), BF16),
                   jax.ShapeDtypeStruct((rows, D_MODEL), F32), jax.ShapeDtypeStruct((rows, ATTN_WIDTH), BF16),
                   jax.ShapeDtypeStruct((rows, POOL_WIDTH), F32), jax.ShapeDtypeStruct((1, D_MODEL), F32),
                   jax.ShapeDtypeStruct((D_MODEL, D_MODEL), F32)],
        scratch_shapes=[pltpu.VMEM((tm, D_MODEL), F32)],
        compiler_params=_cparams(("arbitrary", "arbitrary"), 56),
    )(dx3b, a, w_down, w_up, dx3, x2, gain, w_out, attn, pool)


def _mlp_backward_weights(f, da, dx3b, h2, tm):
    rows = f.shape[0]

    def body(f_ref, da_ref, dx_ref, h_ref, ddown_ref, dup_ref):
        @pl.when(pl.program_id(1) == 0)
        def _():
            ddown_ref[...] = jnp.zeros_like(ddown_ref)
            dup_ref[...] = jnp.zeros_like(dup_ref)

        ddown_ref[...] += _dot_tn(f_ref[...], dx_ref[...])
        dup_ref[...] += _dot_tn(h_ref[...], da_ref[...])

    hidden = pl.BlockSpec((tm, FF_BLOCK), lambda d, i: (i, d))
    row = pl.BlockSpec((tm, D_MODEL), lambda d, i: (i, 0))
    return pl.pallas_call(
        body, name="mlp_backward_weights", grid=(N_DEV, rows // tm),
        in_specs=[hidden, hidden, row, row],
        out_specs=[pl.BlockSpec((None, FF_BLOCK, D_MODEL), lambda d, i: (d, 0, 0)),
                   pl.BlockSpec((None, D_MODEL, FF_BLOCK), lambda d, i: (d, 0, 0))],
        out_shape=[jax.ShapeDtypeStruct((N_DEV, FF_BLOCK, D_MODEL), F32), jax.ShapeDtypeStruct((N_DEV, D_MODEL, FF_BLOCK), F32)],
        compiler_params=_cparams(("parallel", "arbitrary"), 40),
    )(f, da, dx3b, h2)


def _attention_backward(sinks, q, kd, vd, dout, cos, sin, token, n_seq, seq):
    n_blocks = seq // BLOCK

    def body(sink_ref, q_ref, k_ref, v_ref, do_ref, cos_ref, sin_ref, token_ref, dq_ref, dk_ref, dv_ref, dsink_ref, dk_acc, dv_acc):
        low, zero = _lane_masks(BF16)
        lane_row = lax.broadcasted_iota(jnp.int32, (1, LANES), 1)

        @pl.when(pl.program_id(0) == 0)
        def _():
            dsink_ref[...] = jnp.zeros_like(dsink_ref)

        def fold(x):
            return x + pltpu.roll(x, HEAD_DIM, 1)

        def block(n, dsink):
            r0 = pl.multiple_of(n * BLOCK, BLOCK)
            p0 = pl.multiple_of(jnp.maximum(n - 1, 0) * BLOCK, BLOCK)
            cur_mask, prev_mask = _window_masks(n)
            kc, kp = k_ref[pl.ds(r0, BLOCK), :], k_ref[pl.ds(p0, BLOCK), :]
            vc, vp = v_ref[pl.ds(r0, BLOCK), :], v_ref[pl.ds(p0, BLOCK), :]
            dk_cur, dk_prev, dv_cur, dv_prev = [], [], [], []
            for kv in range(N_Q_HEADS // Q_PER_KV):
                acc = [jnp.zeros((BLOCK, LANES), F32) for _ in range(4)]
                for pair in range(kv * 2, kv * 2 + 2):
                    q_pair = q_ref[pl.ds(r0, BLOCK), pair * LANES:(pair + 1) * LANES]
                    do_pair = do_ref[pl.ds(r0, BLOCK), pair * LANES:(pair + 1) * LANES]
                    dq = jnp.zeros((BLOCK, LANES), F32)
                    picked = []
                    for head in (2 * pair, 2 * pair + 1):
                        kc_h, kp_h = _head_operand(kc, head, low, zero), _head_operand(kp, head, low, zero)
                        pc, pp, ps = _head_probs(q_pair, kc_h, kp_h, cur_mask, prev_mask, sink_ref[0, head])
                        dpc = _dot_nt(do_pair, _head_operand(vc, head, low, zero))
                        dpp = _dot_nt(do_pair, _head_operand(vp, head, low, zero))
                        delta = jnp.sum(pc * dpc, axis=-1, keepdims=True) + jnp.sum(pp * dpp, axis=-1, keepdims=True)
                        dsc, dsp = (pc * (dpc - delta)).astype(BF16), (pp * (dpp - delta)).astype(BF16)
                        dq += _dot(dsc, kc_h) + _dot(dsp, kp_h)
                        dsink = dsink + jnp.where(lane_row == head, -jnp.sum(ps * delta, axis=0, keepdims=True), 0.0)
                        picked.append((_dot_tn(dsc, q_pair), _dot_tn(dsp, q_pair),
                                       _dot_tn(pc.astype(BF16), do_pair), _dot_tn(pp.astype(BF16), do_pair)))
                    for j in range(4):
                        acc[j] += jnp.where(low, picked[0][j], picked[1][j])
                    dq = _rope(dq * (HEAD_DIM ** -0.5), cos_ref[pl.ds(r0, BLOCK), :], -sin_ref[pl.ds(r0, BLOCK), :])
                    dq_ref[pl.ds(r0, BLOCK), pair * LANES:(pair + 1) * LANES] = dq.astype(BF16)
                for j, out in enumerate((dk_cur, dk_prev, dv_cur, dv_prev)):
                    out.append(fold(acc[j]))
            merge = lambda parts: jnp.where(low, parts[0], parts[1])
            dk_acc[pl.ds(r0, BLOCK), :] = merge(dk_cur)
            dv_acc[pl.ds(r0, BLOCK), :] = merge(dv_cur)

            @pl.when(n > 0)
            def _():
                dk_acc[pl.ds(p0, BLOCK), :] += merge(dk_prev)
                dv_acc[pl.ds(p0, BLOCK), :] += merge(dv_prev)

            return dsink

        dsink_ref[...] += lax.fori_loop(0, n_blocks, block, jnp.zeros((1, LANES), F32))
        dk_ref[...] = _rope(dk_acc[...], cos_ref[...], -sin_ref[...]).astype(BF16)
        dv_ref[...] = dv_acc[...].astype(BF16)

    seq_block = lambda width: pl.BlockSpec((seq, width), lambda b: (b, 0))
    table = pl.BlockSpec((seq, LANES), lambda b: (0, 0))
    return pl.pallas_call(
        body, name="attention_backward", grid=(n_seq,),
        in_specs=[pl.BlockSpec(memory_space=pltpu.SMEM), seq_block(ATTN_WIDTH), seq_block(2 * KV_WIDTH), seq_block(2 * KV_WIDTH),
                  seq_block(ATTN_WIDTH), table, table, pl.BlockSpec((8, LANES), lambda b: (0, 0))],
        out_specs=[seq_block(ATTN_WIDTH), seq_block(KV_WIDTH), seq_block(KV_WIDTH), pl.BlockSpec((1, LANES), lambda b: (0, 0))],
        out_shape=[jax.ShapeDtypeStruct((n_seq * seq, ATTN_WIDTH), BF16), jax.ShapeDtypeStruct((n_seq * seq, KV_WIDTH), BF16),
                   jax.ShapeDtypeStruct((n_seq * seq, KV_WIDTH), BF16), jax.ShapeDtypeStruct((1, LANES), F32)],
        scratch_shapes=[pltpu.VMEM((seq, KV_WIDTH), F32), pltpu.VMEM((seq, KV_WIDTH), F32)],
        compiler_params=_cparams(("arbitrary",), 40),
    )(sinks, q, kd, vd, dout, cos, sin, token)


def _pool_backward(u, dpool, w_pool, pool_scale, n_seq, seq):
    groups = len(POOL_WINDOWS)

    def body(u_ref, dp_ref, w_ref, s_ref, du_ref, dw_ref, ds_ref):
        @pl.when(pl.program_id(0) == 0)
        def _():
            dw_ref[...] = jnp.zeros_like(dw_ref)
            ds_ref[...] = jnp.zeros_like(ds_ref)

        t = lax.broadcasted_iota(jnp.int32, (seq, 1), 0)
        for g, window in enumerate(POOL_WINDOWS):
            cols = slice(g * POOL_GROUP_DIM, (g + 1) * POOL_GROUP_DIM)
            d, count = _pool_features(u_ref[:, cols], window, t, seq)
            dpool_g = dp_ref[:, cols]
            ds_ref[:, cols] += jnp.sum(dpool_g * _dot(d, w_ref[g]), axis=0, keepdims=True)
            dy = (dpool_g * s_ref[:, cols]).astype(BF16)
            dw_ref[g] += _dot_tn(d, dy)
            dd = _dot_nt(dy, w_ref[g])
            du_ref[:, cols] = (_leading(dd / count, window, t, seq) - dd).astype(BF16)

    seq_block = pl.BlockSpec((seq, POOL_WIDTH), lambda b: (b, 0))
    weights = pl.BlockSpec((groups, POOL_GROUP_DIM, POOL_GROUP_DIM), lambda b: (0, 0, 0))
    scale = pl.BlockSpec((1, POOL_WIDTH), lambda b: (0, 0))
    return pl.pallas_call(
        body, name="pool_backward", grid=(n_seq,),
        in_specs=[seq_block, seq_block, weights, scale],
        out_specs=[seq_block, weights, scale],
        out_shape=[jax.ShapeDtypeStruct((n_seq * seq, POOL_WIDTH), BF16),
                   jax.ShapeDtypeStruct((groups, POOL_GROUP_DIM, POOL_GROUP_DIM), F32), jax.ShapeDtypeStruct((1, POOL_WIDTH), F32)],
        compiler_params=_cparams(("arbitrary",), 40),
    )(u, dpool, w_pool, pool_scale)


def _inproj_backward(dq, dk, dv, du, w_in, h1, x2d, dx2, gain, tm):
    rows = x2d.shape[0]

    def body(dq_ref, dk_ref, dv_ref, du_ref, w_ref, h_ref, x_ref, dx2_ref, g_ref, dx_ref, dw_ref, dg_ref):
        @pl.when(pl.program_id(0) == 0)
        def _():
            dw_ref[...] = jnp.zeros_like(dw_ref)
            dg_ref[...] = jnp.zeros_like(dg_ref)

        dproj = jnp.concatenate([dq_ref[...], dk_ref[...], dv_ref[...], du_ref[...]], axis=1)
        dw_ref[...] += _dot_tn(h_ref[...], dproj)
        dnorm, dg = _rms_backward(_dot_nt(dproj, w_ref[...]), x_ref[...], g_ref[...])
        dg_ref[...] += dg
        dx_ref[...] = dx2_ref[...] + dnorm

    row = lambda width: pl.BlockSpec((tm, width), lambda i: (i, 0))
    vec = pl.BlockSpec((1, D_MODEL), lambda i: (0, 0))
    weight = pl.BlockSpec((D_MODEL, IN_WIDTH), lambda i: (0, 0))
    return pl.pallas_call(
        body, name="inproj_backward", grid=(rows // tm,),
        in_specs=[row(ATTN_WIDTH), row(KV_WIDTH), row(KV_WIDTH), row(POOL_WIDTH), weight, row(D_MODEL), row(D_MODEL), row(D_MODEL), vec],
        out_specs=[row(D_MODEL), weight, vec],
        out_shape=[jax.ShapeDtypeStruct((rows, D_MODEL), F32), jax.ShapeDtypeStruct((D_MODEL, IN_WIDTH), F32),
                   jax.ShapeDtypeStruct((1, D_MODEL), F32)],
        compiler_params=_cparams(("arbitrary",), 48),
    )(dq, dk, dv, du, w_in, h1, x2d, dx2, gain)


def _place():
    return lax.axis_index("x"), lax.axis_index("y"), lax.axis_index("c")


def _peer(x, y, c, rel):
    return (1 - x if rel & 4 else x, 1 - y if rel & 2 else y, 1 - c if rel & 1 else c)


def _index(px, py, pc):
    return 4 * px + 2 * py + pc


def _gather_weights(shards, n_full):
    n = len(shards)

    def body(*refs):
        ins, outs = refs[:n], refs[n:2 * n]
        stage = refs[2 * n:3 * n]
        send_sems, recv_sems, local_sems = refs[3 * n:]
        x, y, c = _place()
        me, sibling = (x, y, c), (x, y, 1 - c)
        chips = [(1 - x, y), (x, 1 - y), (1 - x, 1 - y)]

        def copy(a, k, block, to, src=None):
            slot = outs[a].at[_index(*block)]
            return pltpu.make_async_remote_copy(src_ref=slot if src is None else src, dst_ref=slot, send_sem=send_sems.at[a, k],
                                                recv_sem=recv_sems.at[a, k], device_id=to, device_id_type=MESH)

        mine, first = [], []
        for a in range(n):
            stage[a][...] = ins[a][...].astype(BF16)
            mine.append(pltpu.make_async_copy(stage[a], outs[a].at[_index(*me)], local_sems.at[a]))
            mine[-1].start()
            if a < n_full:
                first.append(copy(a, 0, me, sibling, src=stage[a]))
                first += [copy(a, 1 + j, me, (*chip, c), src=stage[a]) for j, chip in enumerate(chips)]
        for cp in first:
            cp.start()
        passed = []
        for j, chip in enumerate(chips):
            for a in range(n_full):
                copy(a, 1 + j, (*chip, c), me).wait_recv()
                passed.append(copy(a, 4 + j, (*chip, c), sibling))
                passed[-1].start()
        for a in range(n_full):
            copy(a, 0, sibling, me).wait_recv()
            for j, chip in enumerate(chips):
                copy(a, 4 + j, (*chip, 1 - c), me).wait_recv()
        for cp in first + passed:
            cp.wait_send()
        for cp in mine:
            cp.wait()

    return pl.pallas_call(
        body, name="gather_weights",
        in_specs=[pl.BlockSpec(memory_space=pltpu.VMEM)] * n,
        out_specs=[pl.BlockSpec(memory_space=pl.ANY)] * n,
        out_shape=[jax.ShapeDtypeStruct((N_DEV,) + s.shape, BF16) for s in shards],
        scratch_shapes=[pltpu.VMEM(s.shape, BF16) for s in shards]
        + [pltpu.SemaphoreType.DMA((n, 7)), pltpu.SemaphoreType.DMA((n, 7)), pltpu.SemaphoreType.DMA((n,))],
        compiler_params=pltpu.CompilerParams(vmem_limit_bytes=32 * MIB),
    )(*shards)


def _exchange_gradients(grads, small):
    n = len(grads)

    def body(*refs):
        ins, small_ref = refs[:n], refs[n]
        outs, total_ref = refs[n + 1:2 * n + 1], refs[2 * n + 1]
        gathered, send_sems, recv_sems, small_send, small_recv = refs[2 * n + 2:]
        x, y, c = _place()
        me = _index(x, y, c)
        gathered[0] = small_ref[...]
        copies = []
        for rel in range(1, N_DEV):
            to = _peer(x, y, c, rel)
            copies.append(pltpu.make_async_remote_copy(src_ref=small_ref, dst_ref=gathered.at[rel], send_sem=small_send.at[rel - 1],
                                                       recv_sem=small_recv.at[rel - 1], device_id=to, device_id_type=MESH))
            for a in range(n):
                copies.append(pltpu.make_async_remote_copy(src_ref=ins[a].at[_index(*to)], dst_ref=outs[a].at[rel - 1],
                                                           send_sem=send_sems.at[a, rel - 1], recv_sem=recv_sems.at[a, rel - 1],
                                                           device_id=to, device_id_type=MESH))
        for cp in copies:
            cp.start()
        for cp in copies:
            cp.wait()
        total = gathered[me]
        for source in range(1, N_DEV):
            total = total + gathered[jnp.bitwise_xor(me, source)]
        total_ref[...] = total

    return pl.pallas_call(
        body, name="exchange_gradients",
        in_specs=[pl.BlockSpec(memory_space=pl.ANY)] * n + [pl.BlockSpec(memory_space=pltpu.VMEM)],
        out_specs=[pl.BlockSpec(memory_space=pl.ANY)] * n + [pl.BlockSpec(memory_space=pltpu.VMEM)],
        out_shape=[jax.ShapeDtypeStruct((N_DEV - 1,) + g.shape[1:], F32) for g in grads] + [jax.ShapeDtypeStruct(small.shape, F32)],
        scratch_shapes=[pltpu.VMEM((N_DEV,) + small.shape, F32), pltpu.SemaphoreType.DMA((n, 7)), pltpu.SemaphoreType.DMA((n, 7)),
                        pltpu.SemaphoreType.DMA((7,)), pltpu.SemaphoreType.DMA((7,))],
        compiler_params=pltpu.CompilerParams(vmem_limit_bytes=32 * MIB),
    )(*grads, small)


def _own_slot_to_all(a, rel, me, to, ins, lands):
    return lands[a].at[me], lands[a].at[me]


def _block_to_owner(a, rel, me, to, ins, lands):
    return ins[a].at[to], lands[a].at[rel - 1]


def _split_copies(plan, ins, lands, send_sems, recv_sems):
    x, y, c = _place()
    copies = []
    for rel in range(1, N_DEV):
        to = _peer(x, y, c, rel)
        for a in range(len(lands)):
            src, dst = plan(a, rel, _index(x, y, c), _index(*to), ins, lands)
            k = a * (N_DEV - 1) + rel - 1
            copies.append(pltpu.make_async_remote_copy(src_ref=src, dst_ref=dst, send_sem=send_sems.at[k],
                                                       recv_sem=recv_sems.at[k], device_id=to, device_id_type=MESH))
    return copies


HBM_SPEC = pl.BlockSpec(memory_space=pltpu.HBM)
SEM_SPEC = pl.BlockSpec(memory_space=pltpu.SEMAPHORE)
EFFECT = pltpu.SideEffectType.DATAFLOW_SIDE_EFFECTING


def _start_copies(name, plan, ins, lands):
    n_in, n = len(ins), len(ins) + len(lands)

    def body(*refs):
        for cp in _split_copies(plan, refs[:n_in], refs[n_in:n], refs[n], refs[n + 1]):
            cp.start()
        refs[-1][...] = jnp.zeros_like(refs[-1])

    arrays = [pltpu.with_memory_space_constraint(v, pltpu.HBM) for v in (*ins, *lands)]
    sems = pltpu.SemaphoreType.DMA((len(lands) * (N_DEV - 1),))
    send_sems, recv_sems, *flying, token = pl.pallas_call(
        body, name=name,
        out_shape=(sems, sems, *[pltpu.HBM(v.shape, v.dtype) for v in arrays], jax.ShapeDtypeStruct((8, LANES), F32)),
        in_specs=[HBM_SPEC] * n, out_specs=(SEM_SPEC, SEM_SPEC, *[HBM_SPEC] * n, pl.BlockSpec(memory_space=pltpu.VMEM)),
        input_output_aliases={i: 2 + i for i in range(n)},
        compiler_params=pltpu.CompilerParams(has_side_effects=EFFECT),
    )(*arrays)
    return send_sems, recv_sems, flying, token


def _wait_copies(name, plan, n_in, send_sems, recv_sems, flying, after):
    n = len(flying)

    def body(*refs):
        for cp in _split_copies(plan, refs[:n_in], refs[n_in:n], refs[n], refs[n + 1]):
            cp.wait_send()
            cp.wait_recv()

    landed = pl.pallas_call(
        body, name=name, out_shape=tuple(pltpu.HBM(v.shape, v.dtype) for v in flying),
        in_specs=[HBM_SPEC] * n + [SEM_SPEC, SEM_SPEC, pl.BlockSpec(memory_space=pl.ANY)], out_specs=tuple([HBM_SPEC] * n),
        input_output_aliases={i: i for i in range(n)},
        compiler_params=pltpu.CompilerParams(has_side_effects=EFFECT),
    )(*flying, send_sems, recv_sems, after)
    return landed[:n_in], landed[n_in:]


def _adamw_math(w, g, m, v):
    m = ADAM_B1 * m + (1.0 - ADAM_B1) * g
    v = ADAM_B2 * v + (1.0 - ADAM_B2) * (g * g)
    m_hat = m / (1.0 - ADAM_B1 ** ADAM_STEP)
    v_hat = v / (1.0 - ADAM_B2 ** ADAM_STEP)
    return -ADAM_LR * (m_hat / (jnp.sqrt(v_hat) + ADAM_EPS) + ADAM_WD * w), m, v


def _adamw_sharded(me, own, received, w, m, v, tr):
    rows, cols = w.shape

    def body(me_ref, own_ref, rec_ref, w_ref, m_ref, v_ref, g_ref, d_ref, nm_ref, nv_ref):
        g = own_ref[...]
        for r in range(N_DEV - 1):
            g = g + rec_ref[r]
        g_ref[...] = g
        d_ref[...], nm_ref[...], nv_ref[...] = _adamw_math(w_ref[...], g, m_ref[...], v_ref[...])

    tile = pl.BlockSpec((tr, cols), lambda i, me_ref: (i, 0))
    shape = jax.ShapeDtypeStruct((rows, cols), F32)
    return pl.pallas_call(
        body, name="adamw_sharded",
        grid_spec=pltpu.PrefetchScalarGridSpec(
            num_scalar_prefetch=1, grid=(rows // tr,),
            in_specs=[pl.BlockSpec((None, tr, cols), lambda i, me_ref: (me_ref[0], i, 0)),
                      pl.BlockSpec((N_DEV - 1, tr, cols), lambda i, me_ref: (0, i, 0)), tile, tile, tile],
            out_specs=[tile, tile, tile, tile]),
        out_shape=[shape, shape, shape, shape],
        compiler_params=_cparams(("parallel",), 40),
    )(me, own, received, w, m, v)


def _adamw_small(w, g, m, v):
    def body(w_ref, g_ref, m_ref, v_ref, d_ref, nm_ref, nv_ref):
        d_ref[...], nm_ref[...], nv_ref[...] = _adamw_math(w_ref[...], g_ref[...], m_ref[...], v_ref[...])

    shape = jax.ShapeDtypeStruct(w.shape, F32)
    return pl.pallas_call(body, name="adamw_small", out_shape=[shape, shape, shape])(w, g, m, v)


def _pack_small(g1, g2, g3, ps, sink, loss, wp):
    def rows(a, n):
        a = a.reshape(-1, LANES)
        return jnp.pad(a, ((0, n - a.shape[0]), (0, 0)))

    sink = jnp.pad(sink.reshape(1, -1), ((0, 0), (0, LANES - sink.size)))
    return jnp.concatenate([rows(g1, 8), rows(g2, 8), rows(g3, 8), rows(ps, 8), rows(sink, 8), rows(loss, 8), rows(wp, 512)], axis=0)


def _unpack_small(p, like):
    g1, g2, g3, ps, sink, wp = like
    return (p[ROW_G1:ROW_G1 + 8].reshape(g1.shape), p[ROW_G2:ROW_G2 + 8].reshape(g2.shape), p[ROW_G3:ROW_G3 + 8].reshape(g3.shape),
            p[ROW_PS:ROW_PS + 4].reshape(ps.shape), p[ROW_SINK, :N_Q_HEADS].reshape(sink.shape), p[ROW_WP:ROW_WP + 512].reshape(wp.shape))


def _local_step(x, target, attn_norm_g, w_in_full, attn_sinks, w_pool, pool_scale, w_out_full, mlp_norm_g, final_norm_g,
                front_token, mlp_weights, ship):
    n_seq, seq, _ = x.shape
    rows = n_seq * seq
    tm = min(512, seq)
    x2d, t2d = x.reshape(rows, D_MODEL), target.reshape(rows, D_MODEL)
    g3 = final_norm_g.reshape(1, D_MODEL)
    cos, sin = _rope_tables(seq)
    wp_b = w_pool[0].astype(BF16)

    h1, q, kd, vd, u = _norm_inproj(x2d, attn_norm_g, w_in_full, cos, sin, front_token, seq, tm)
    attn = _attention_forward(attn_sinks, q, kd, vd, n_seq, seq)
    pool = _pool_forward(u, wp_b, pool_scale, n_seq, seq)
    x2, h2 = _outproj_norm(x2d, attn, pool, w_out_full, mlp_norm_g, tm)
    w_up_blocks, w_down_full = mlp_weights(h2)
    a, dx3, dx3b, loss_cols, dg3 = _mlp_forward_loss(h2, x2, w_up_blocks, w_down_full, g3, t2d, tm)

    da, f, dx2, dattn, dpool, dg2, d_w_out = _mlp_backward_data(dx3b, a, w_down_full, w_up_blocks, dx3, x2, mlp_norm_g, w_out_full,
                                                               attn, pool, tm)
    d_w_down, d_w_up = _mlp_backward_weights(f, da, dx3b, h2, min(1024, seq))
    back_token = ship(d_w_out, d_w_up, d_w_down)
    dq, dk, dv, dsink = _attention_backward(attn_sinks, q, kd, vd, dattn, cos, sin, back_token, n_seq, seq)
    du, d_w_pool, d_pool_scale = _pool_backward(u, dpool, wp_b, pool_scale, n_seq, seq)
    grad_x, d_w_in, dg1 = _inproj_backward(dq, dk, dv, du, w_in_full, h1, x2d, dx2, attn_norm_g, tm)
    return (grad_x.reshape(x.shape), loss_cols, dg1, d_w_in, dsink, d_w_pool, d_pool_scale, dg2, dg3)


def kernel(x, attn_norm_g, w_in, attn_sinks, w_pool, pool_scale, w_out, mlp_norm_g, w_up, w_down, final_norm_g, loss_target, m_attn_norm_g, m_w_in, m_attn_sinks, m_w_pool, m_pool_scale, m_w_out, m_mlp_norm_g, m_w_up, m_w_down, m_final_norm_g, v_attn_norm_g, v_w_in, v_attn_sinks, v_w_pool, v_pool_scale, v_w_out, v_mlp_norm_g, v_w_up, v_w_down, v_final_norm_g):
    me = (4 * lax.axis_index("x") + 2 * lax.axis_index("y") + lax.axis_index("c")).astype(jnp.int32).reshape(1)

    win_g, wout_g, wup_land, wdown_land = _gather_weights([w_in[0], w_out[0], w_up[0], w_down[0]], 2)
    w_in_full = jnp.transpose(win_g, (1, 0, 2)).reshape(D_MODEL, IN_WIDTH)
    spread_send, spread_recv, spread_flying, front_token = _start_copies("spread_start", _own_slot_to_all, [], [wup_land, wdown_land])

    def mlp_weights(after):
        _, (wup_g, wdown_g) = _wait_copies("spread_wait", _own_slot_to_all, 0, spread_send, spread_recv, spread_flying, after)
        return wup_g, wdown_g.reshape(D_FF, D_MODEL)

    delivery = []

    def ship(d_w_out, d_w_up, d_w_down):
        grads = [d_w_out.reshape(N_DEV, OUT_BLOCK, D_MODEL), d_w_up, d_w_down]
        lands = [lax.empty((N_DEV - 1,) + g.shape[1:], F32) for g in grads]
        delivery.extend(_start_copies("deliver_start", _block_to_owner, grads, lands))
        return delivery[3]

    (grad_x, loss_cols, dg1, d_w_in, dsink, d_w_pool, d_pool_scale, dg2, dg3) = _local_step(
        x, loss_target, attn_norm_g, w_in_full, attn_sinks, w_pool, pool_scale, wout_g.reshape(D_MODEL, D_MODEL), mlp_norm_g,
        final_norm_g, front_token, mlp_weights, ship)
    own_mlp, received_mlp = _wait_copies("deliver_wait", _block_to_owner, 3, delivery[0], delivery[1], delivery[2], grad_x)

    d_w_in_blocks = jnp.transpose(d_w_in.reshape(D_MODEL, N_DEV, IN_BLOCK), (1, 0, 2))
    small = _pack_small(dg1, dg2, dg3, d_pool_scale, dsink[0, :N_Q_HEADS], loss_cols, d_w_pool)
    received_in, small_sum = _exchange_gradients([d_w_in_blocks], small)
    big, received = [d_w_in_blocks, *own_mlp], [received_in, *received_mlp]

    loss = 0.5 * jnp.sum(small_sum[ROW_LOSS:ROW_LOSS + 8]) / D_MODEL
    shards = [(w_in, m_w_in, v_w_in, 1024), (w_out, m_w_out, v_w_out, 128), (w_up, m_w_up, v_w_up, 256), (w_down, m_w_down, v_w_down, 256)]
    sharded = [_adamw_sharded(me, own, rec, w[0], m[0], v[0], tr) for own, rec, (w, m, v, tr) in zip(big, received, shards)]
    g_in, g_out, g_up, g_down = [[r[None] for r in res] for res in sharded]

    like = (attn_norm_g, mlp_norm_g, final_norm_g, pool_scale, attn_sinks, w_pool)
    zeros = jnp.zeros((1, D_MODEL), F32)
    pack = lambda t: _pack_small(t[0], t[1], t[2], t[3], t[4], zeros, t[5])
    small_delta, small_m, small_v = _adamw_small(
        pack(like), small_sum, pack((m_attn_norm_g, m_mlp_norm_g, m_final_norm_g, m_pool_scale, m_attn_sinks, m_w_pool)),
        pack((v_attn_norm_g, v_mlp_norm_g, v_final_norm_g, v_pool_scale, v_attn_sinks, v_w_pool)))
    s_grad, s_delta, s_m, s_v = [_unpack_small(p, like) for p in (small_sum, small_delta, small_m, small_v)]

    def ordered(small_parts, k):
        n1, n2, n3, ps, sink, wp = small_parts
        return [n1, g_in[k], sink, wp, ps, g_out[k], n2, g_up[k], g_down[k], n3]

    return (loss, grad_x, *ordered(s_grad, 0), *ordered(s_delta, 1), *ordered(s_m, 2), *ordered(s_v, 3))
```

```python
import functools

import jax
import jax.numpy as jnp
from jax import lax
from jax.experimental import pallas as pl
from jax.experimental.pallas import tpu as pltpu

F32 = jnp.float32
BF16 = jnp.bfloat16

D_MODEL = 1024
HEAD_DIM = 64
N_Q_HEADS = 8
Q_PER_KV = 4
ATTN_WIDTH = 512
KV_WIDTH = 128
BLOCK = 128
ROPE_THETA = 10000.0
POOL_WINDOWS = (2, 4, 8, 16)
POOL_WIDTH = 512
POOL_GROUP_DIM = 128
IN_WIDTH = 1280
D_FF = 4096
EPS = 1e-6
N_DEV = 8
FF_BLOCK = D_FF // N_DEV
IN_BLOCK = IN_WIDTH // N_DEV
OUT_BLOCK = D_MODEL // N_DEV
ADAM_LR = 0.001
ADAM_B1 = 0.9
ADAM_B2 = 0.999
ADAM_EPS = 1e-08
ADAM_WD = 0.01
ADAM_STEP = 10
NEG = -1e30
FORWARD_CHAINS = 4
BACKWARD_CHAINS = 2
LANES = 128
MIB = 1024 * 1024
MESH = pl.DeviceIdType.MESH

ROW_G1, ROW_G2, ROW_G3, ROW_PS, ROW_SINK, ROW_LOSS, ROW_WP, SMALL_ROWS = 0, 8, 16, 24, 32, 40, 48, 560


def _cparams(semantics, vmem_mib):
    return pltpu.CompilerParams(dimension_semantics=semantics, vmem_limit_bytes=vmem_mib * MIB)


def _dot(a, b):
    return jnp.dot(a, b, preferred_element_type=F32)


def _dot_nt(a, b):
    return lax.dot_general(a, b, (((1,), (1,)), ((), ())), preferred_element_type=F32)


def _dot_tn(a, b):
    return lax.dot_general(a, b, (((0,), (0,)), ((), ())), preferred_element_type=F32)


def _swap_halves(x):
    width = x.shape[1]
    lane = lax.broadcasted_iota(jnp.int32, x.shape, 1)
    ahead = pltpu.roll(x, width - HEAD_DIM // 2, 1)
    behind = pltpu.roll(x, HEAD_DIM // 2, 1)
    return jnp.where(lane % HEAD_DIM < HEAD_DIM // 2, ahead, behind)


def _rope(x, cos, sin):
    reps = x.shape[1] // LANES
    if reps > 1:
        cos = jnp.tile(cos, (1, reps))
        sin = jnp.tile(sin, (1, reps))
    return x * cos + _swap_halves(x) * sin


def _rope_tables(seq):
    half = HEAD_DIM // 2
    inv_freq = ROPE_THETA ** (-jnp.arange(half, dtype=F32) / half)
    ang = jnp.arange(seq).astype(F32)[:, None] * inv_freq[None, :]
    cos, sin = jnp.cos(ang), jnp.sin(ang)
    cos = jnp.tile(cos, (1, LANES // half))
    sin = jnp.tile(jnp.concatenate([-sin, sin], axis=1), (1, LANES // HEAD_DIM))
    return cos, sin


def _both_halves(x):
    lane = lax.broadcasted_iota(jnp.int32, x.shape, 1)
    other = pltpu.roll(x, HEAD_DIM, 1)
    low = lane < HEAD_DIM
    return jnp.where(low, x, other), jnp.where(low, other, x)


def _rms_backward(dh, xin, gain):
    r = lax.rsqrt(jnp.mean(xin * xin, axis=-1, keepdims=True) + EPS)
    xhat = xin * r
    dxhat = dh * gain
    dx = r * (dxhat - xhat * jnp.mean(dxhat * xhat, axis=-1, keepdims=True))
    return dx, jnp.sum(dh * xhat, axis=0, keepdims=True)


def _norm_inproj(x2d, gain, w_in, cos, sin, token, seq, tm):
    rows = x2d.shape[0]
    tiles_per_seq = seq // tm

    def body(x_ref, g_ref, w_ref, cos_ref, sin_ref, token_ref, h_ref, q_ref, k_ref, v_ref, u_ref):
        x = x_ref[...]
        r = lax.rsqrt(jnp.mean(x * x, axis=-1, keepdims=True) + EPS)
        h = (x * r * g_ref[...]).astype(BF16)
        h_ref[...] = h
        proj = _dot(h, w_ref[...])
        cos_t, sin_t = cos_ref[...], sin_ref[...]
        q = _rope(proj[:, :ATTN_WIDTH], cos_t, sin_t) * (HEAD_DIM ** -0.5)
        q_ref[...] = q.astype(BF16)
        k = _rope(proj[:, ATTN_WIDTH:ATTN_WIDTH + KV_WIDTH], cos_t, sin_t)
        k0, k1 = _both_halves(k)
        k_ref[...] = jnp.concatenate([k0, k1], axis=1).astype(BF16)
        v0, v1 = _both_halves(proj[:, ATTN_WIDTH + KV_WIDTH:ATTN_WIDTH + 2 * KV_WIDTH])
        v_ref[...] = jnp.concatenate([v0, v1], axis=1).astype(BF16)
        u_ref[...] = proj[:, ATTN_WIDTH + 2 * KV_WIDTH:]

    row = lambda width: pl.BlockSpec((tm, width), lambda i: (i, 0))
    table = pl.BlockSpec((tm, LANES), lambda i: (i % tiles_per_seq, 0))
    return pl.pallas_call(
        body, name="norm_inproj", grid=(rows // tm,),
        in_specs=[row(D_MODEL), pl.BlockSpec((1, D_MODEL), lambda i: (0, 0)),
                  pl.BlockSpec((D_MODEL, IN_WIDTH), lambda i: (0, 0)), table, table, pl.BlockSpec((8, LANES), lambda i: (0, 0))],
        out_specs=[row(D_MODEL), row(ATTN_WIDTH), row(2 * KV_WIDTH), row(2 * KV_WIDTH), row(POOL_WIDTH)],
        out_shape=[jax.ShapeDtypeStruct((rows, D_MODEL), BF16), jax.ShapeDtypeStruct((rows, ATTN_WIDTH), BF16),
                   jax.ShapeDtypeStruct((rows, 2 * KV_WIDTH), BF16), jax.ShapeDtypeStruct((rows, 2 * KV_WIDTH), BF16),
                   jax.ShapeDtypeStruct((rows, POOL_WIDTH), F32)],
        compiler_params=_cparams(("parallel",), 40),
    )(x2d, gain, w_in, cos, sin, token)


def _window_masks(n):
    qi = lax.broadcasted_iota(jnp.int32, (BLOCK, BLOCK), 0)
    kj = lax.broadcasted_iota(jnp.int32, (BLOCK, BLOCK), 1)
    return kj <= qi, jnp.logical_and(kj > qi, n > 0)


def _window_operand(ref, r0, p0, kv):
    low = lax.broadcasted_iota(jnp.int32, (BLOCK, LANES), 1) < HEAD_DIM
    cur = ref[pl.ds(r0, BLOCK), kv * LANES:(kv + 1) * LANES]
    prev = ref[pl.ds(p0, BLOCK), kv * LANES:(kv + 1) * LANES]
    zero = jnp.zeros_like(cur)
    return jnp.concatenate([jnp.where(low, cur, zero), jnp.where(low, zero, cur), jnp.where(low, prev, zero), jnp.where(low, zero, prev)], axis=0)


def _merged_window(wide, parity, cur_mask, prev_mask, fill):
    cur = wide[:, parity * LANES:(parity + 1) * LANES]
    prev = wide[:, (2 + parity) * LANES:(3 + parity) * LANES]
    return jnp.where(cur_mask, cur, jnp.where(prev_mask, prev, fill))


def _softmax_with_sink(scores, sink):
    m = jnp.maximum(jnp.max(scores, axis=-1, keepdims=True), sink)
    p, ps = jnp.exp(scores - m), jnp.exp(sink - m)
    inv = 1.0 / (jnp.sum(p, axis=-1, keepdims=True) + ps)
    return p * inv, ps * inv


def _attention_forward(sinks, q, kd, vd, n_seq, seq):
    n_blocks = seq // BLOCK
    n_pairs = N_Q_HEADS // 2
    chains = min(FORWARD_CHAINS, n_blocks)

    def body(sink_ref, q_ref, k_ref, v_ref, o_ref, s_ref, p_ref):
        def block(n, u):
            r0 = pl.multiple_of(n * BLOCK, BLOCK)
            p0 = pl.multiple_of(jnp.maximum(n - 1, 0) * BLOCK, BLOCK)
            cur_mask, prev_mask = _window_masks(n)
            keys = [_window_operand(k_ref, r0, p0, kv) for kv in range(2)]
            values = [_window_operand(v_ref, r0, p0, kv) for kv in range(2)]
            for pair in range(n_pairs):
                wide = _dot_nt(q_ref[pl.ds(r0, BLOCK), pair * LANES:(pair + 1) * LANES], keys[pair // 2])
                for parity in range(2):
                    s_ref[u, 2 * pair + parity] = _merged_window(wide, parity, cur_mask, prev_mask, NEG)
            probs, _ = _softmax_with_sink(s_ref[u], sink_ref[:, :, 0:1])
            probs = probs.astype(BF16)
            zero = jnp.zeros((BLOCK, BLOCK), BF16)
            for pair in range(n_pairs):
                for parity in range(2):
                    ph = probs[2 * pair + parity]
                    p_ref[u, pair, :, parity * LANES:(parity + 1) * LANES] = jnp.where(cur_mask, ph, zero)
                    p_ref[u, pair, :, (2 + parity) * LANES:(3 + parity) * LANES] = jnp.where(cur_mask, zero, ph)
            for pair in range(n_pairs):
                o_ref[pl.ds(r0, BLOCK), pair * LANES:(pair + 1) * LANES] = _dot(p_ref[u, pair], values[pair // 2]).astype(BF16)

        def step(i, carry):
            for u in range(chains):
                block(i * chains + u, u)
            return carry

        lax.fori_loop(0, n_blocks // chains, step, 0)

    seq_block = lambda width: pl.BlockSpec((seq, width), lambda b: (b, 0))
    return pl.pallas_call(
        body, name="attention_forward", grid=(n_seq,),
        in_specs=[pl.BlockSpec((N_Q_HEADS, 1, LANES), lambda b: (0, 0, 0)), seq_block(ATTN_WIDTH), seq_block(2 * KV_WIDTH),
                  seq_block(2 * KV_WIDTH)],
        out_specs=seq_block(ATTN_WIDTH),
        out_shape=jax.ShapeDtypeStruct((n_seq * seq, ATTN_WIDTH), BF16),
        scratch_shapes=[pltpu.VMEM((chains, N_Q_HEADS, BLOCK, BLOCK), F32), pltpu.VMEM((chains, n_pairs, BLOCK, 4 * LANES), BF16)],
        compiler_params=_cparams(("parallel",), 40),
    )(sinks, q, kd, vd)


def _trailing(x, window, t, seq):
    k = 1
    while k < window:
        x = x + jnp.where(t >= k, pltpu.roll(x, k, 0), 0.0)
        k *= 2
    return x


def _leading(x, window, t, seq):
    k = 1
    while k < window:
        x = x + jnp.where(t < seq - k, pltpu.roll(x, seq - k, 0), 0.0)
        k *= 2
    return x


def _pool_features(u_g, window, t, seq):
    count = jnp.minimum(t + 1, window).astype(F32)
    return (_trailing(u_g, window, t, seq) / count - u_g).astype(BF16), count


def _pool_forward(u, w_pool, pool_scale, n_seq, seq):
    def body(u_ref, w_ref, s_ref, o_ref):
        t = lax.broadcasted_iota(jnp.int32, (seq, 1), 0)
        for g, window in enumerate(POOL_WINDOWS):
            cols = slice(g * POOL_GROUP_DIM, (g + 1) * POOL_GROUP_DIM)
            d, _ = _pool_features(u_ref[:, cols], window, t, seq)
            o_ref[:, cols] = (_dot(d, w_ref[g]) * s_ref[:, cols]).astype(BF16)

    seq_block = pl.BlockSpec((seq, POOL_WIDTH), lambda b: (b, 0))
    return pl.pallas_call(
        body, name="pool_forward", grid=(n_seq,),
        in_specs=[seq_block, pl.BlockSpec((len(POOL_WINDOWS), POOL_GROUP_DIM, POOL_GROUP_DIM), lambda b: (0, 0, 0)),
                  pl.BlockSpec((1, POOL_WIDTH), lambda b: (0, 0))],
        out_specs=seq_block,
        out_shape=jax.ShapeDtypeStruct((n_seq * seq, POOL_WIDTH), BF16),
        compiler_params=_cparams(("parallel",), 40),
    )(u, w_pool, pool_scale)


def _outproj_norm(x2d, attn, pool, w_out, gain, tm):
    rows = x2d.shape[0]

    def body(x_ref, a_ref, p_ref, w_ref, g_ref, x2_ref, h_ref):
        x2 = x_ref[...] + _dot(a_ref[...], w_ref[:ATTN_WIDTH, :]) + _dot(p_ref[...], w_ref[ATTN_WIDTH:, :])
        x2_ref[...] = x2
        r = lax.rsqrt(jnp.mean(x2 * x2, axis=-1, keepdims=True) + EPS)
        h_ref[...] = (x2 * r * g_ref[...]).astype(BF16)

    row = lambda width: pl.BlockSpec((tm, width), lambda i: (i, 0))
    return pl.pallas_call(
        body, name="outproj_norm", grid=(rows // tm,),
        in_specs=[row(D_MODEL), row(ATTN_WIDTH), row(POOL_WIDTH), pl.BlockSpec((D_MODEL, D_MODEL), lambda i: (0, 0)),
                  pl.BlockSpec((1, D_MODEL), lambda i: (0, 0))],
        out_specs=[row(D_MODEL), row(D_MODEL)],
        out_shape=[jax.ShapeDtypeStruct((rows, D_MODEL), F32), jax.ShapeDtypeStruct((rows, D_MODEL), BF16)],
        compiler_params=_cparams(("parallel",), 40),
    )(x2d, attn, pool, w_out, gain)


def _resident(shape):
    return pl.BlockSpec(shape, lambda i: (0,) * len(shape), pipeline_mode=pl.Buffered(1))


def _mlp_forward_loss(h2, x2, w_up, w_down, gain, target, tm):
    rows = h2.shape[0]
    chunk = D_MODEL

    def body(h_ref, x_ref, up_ref, down_ref, g_ref, t_ref, a_ref, f_ref, dx_ref, dxb_ref, loss_ref, dg_ref):
        @pl.when(pl.program_id(0) == 0)
        def _():
            loss_ref[...] = jnp.zeros_like(loss_ref)
            dg_ref[...] = jnp.zeros_like(dg_ref)

        h = h_ref[...]
        for c in range(D_FF // chunk):
            cols = slice(c * chunk, (c + 1) * chunk)
            a = _dot(h, up_ref[:, cols])
            a_ref[:, cols] = a
            r = jnp.maximum(a, 0.0)
            f_ref[:, cols] = (r * r).astype(BF16)
        x3 = x_ref[...] + _dot(f_ref[...], down_ref[...])
        rn = lax.rsqrt(jnp.mean(x3 * x3, axis=-1, keepdims=True) + EPS)
        err = x3 * rn * g_ref[...] - t_ref[...]
        loss_ref[...] += jnp.sum(err * err, axis=0, keepdims=True)
        dx, dg = _rms_backward(err / D_MODEL, x3, g_ref[...])
        dg_ref[...] += dg
        dx_ref[...] = dx
        dxb_ref[...] = dx.astype(BF16)

    row = lambda width: pl.BlockSpec((tm, width), lambda i: (i, 0))
    vec = pl.BlockSpec((1, D_MODEL), lambda i: (0, 0))
    return pl.pallas_call(
        body, name="mlp_forward_loss", grid=(rows // tm,),
        in_specs=[row(D_MODEL), row(D_MODEL), _resident((D_MODEL, D_FF)), _resident((D_FF, D_MODEL)), vec, row(D_MODEL)],
        out_specs=[row(D_FF), row(D_FF), row(D_MODEL), row(D_MODEL), vec, vec],
        out_shape=[jax.ShapeDtypeStruct((rows, D_FF), F32), jax.ShapeDtypeStruct((rows, D_FF), BF16),
                   jax.ShapeDtypeStruct((rows, D_MODEL), F32), jax.ShapeDtypeStruct((rows, D_MODEL), BF16),
                   jax.ShapeDtypeStruct((1, D_MODEL), F32), jax.ShapeDtypeStruct((1, D_MODEL), F32)],
        compiler_params=_cparams(("arbitrary",), 56),
    )(h2, x2, w_up, w_down, gain, target)


def _mlp_backward_data(dx3b, a, w_down, w_up, dx3, x2, gain, w_out, attn, pool, token, tm):
    rows = dx3b.shape[0]
    steps = rows // tm
    chunk = D_MODEL

    def body(dxb_ref, a_ref, down_ref, up_ref, dx3_ref, x2_ref, g_ref, wo_ref, attn_ref, pool_ref, token_ref,
             da_ref, dx2_ref, dattn_ref, dpool_ref, dg_ref, dwo_hbm, dwo_acc, sem):
        @pl.when(pl.program_id(0) == 0)
        def _():
            dg_ref[...] = jnp.zeros_like(dg_ref)
            dwo_acc[...] = jnp.zeros_like(dwo_acc)

        dxb = dxb_ref[...]
        for c in range(D_FF // chunk):
            cols = slice(c * chunk, (c + 1) * chunk)
            r = jnp.maximum(a_ref[:, cols], 0.0)
            da_ref[:, cols] = (_dot_nt(dxb, down_ref[cols, :]) * (2.0 * r)).astype(BF16)
        dnorm, dg = _rms_backward(_dot_nt(da_ref[...], up_ref[...]), x2_ref[...], g_ref[...])
        dg_ref[...] += dg
        dx2 = dx3_ref[...] + dnorm
        dx2_ref[...] = dx2
        dx2b = dx2.astype(BF16)
        dmix = _dot_nt(dx2b, wo_ref[...])
        dattn_ref[...] = dmix[:, :ATTN_WIDTH].astype(BF16)
        dpool_ref[...] = dmix[:, ATTN_WIDTH:]
        dwo_acc[:ATTN_WIDTH, :] += _dot_tn(attn_ref[...], dx2b)
        dwo_acc[ATTN_WIDTH:, :] += _dot_tn(pool_ref[...], dx2b)

        @pl.when(pl.program_id(0) == steps - 1)
        def _():
            out = pltpu.make_async_copy(dwo_acc, dwo_hbm, sem)
            out.start()
            out.wait()

    row = lambda width: pl.BlockSpec((tm, width), lambda i: (i, 0))
    vec = pl.BlockSpec((1, D_MODEL), lambda i: (0, 0))
    return pl.pallas_call(
        body, name="mlp_backward_data", grid=(steps,),
        in_specs=[row(D_MODEL), row(D_FF), _resident((D_FF, D_MODEL)), _resident((D_MODEL, D_FF)), row(D_MODEL), row(D_MODEL), vec,
                  _resident((D_MODEL, D_MODEL)), row(ATTN_WIDTH), row(POOL_WIDTH), pl.BlockSpec((8, LANES), lambda i: (0, 0))],
        out_specs=[row(D_FF), row(D_MODEL), row(ATTN_WIDTH), row(POOL_WIDTH), vec, pl.BlockSpec(memory_space=pl.ANY)],
        out_shape=[jax.ShapeDtypeStruct((rows, D_FF), BF16), jax.ShapeDtypeStruct((rows, D_MODEL), F32),
                   jax.ShapeDtypeStruct((rows, ATTN_WIDTH), BF16), jax.ShapeDtypeStruct((rows, POOL_WIDTH), F32),
                   jax.ShapeDtypeStruct((1, D_MODEL), F32), jax.ShapeDtypeStruct((D_MODEL, D_MODEL), F32)],
        scratch_shapes=[pltpu.VMEM((D_MODEL, D_MODEL), F32), pltpu.SemaphoreType.DMA],
        compiler_params=_cparams(("arbitrary",), 56),
    )(dx3b, a, w_down, w_up, dx3, x2, gain, w_out, attn, pool, token)


def _weight_gradient(name, lhs, rhs, block_lhs, tm):
    rows = lhs.shape[0]
    blocked, whole = pl.BlockSpec((tm, FF_BLOCK), lambda d, i: (i, d)), lambda a: pl.BlockSpec((tm, a.shape[1]), lambda d, i: (i, 0))
    out = (FF_BLOCK, rhs.shape[1]) if block_lhs else (lhs.shape[1], FF_BLOCK)

    def body(l_ref, r_ref, o_ref):
        @pl.when(pl.program_id(1) == 0)
        def _():
            o_ref[...] = jnp.zeros_like(o_ref)

        o_ref[...] += _dot_tn(l_ref[...], r_ref[...])

    return pl.pallas_call(
        body, name=name, grid=(N_DEV, rows // tm),
        in_specs=[blocked, whole(rhs)] if block_lhs else [whole(lhs), blocked],
        out_specs=pl.BlockSpec((None,) + out, lambda d, i: (d, 0, 0)),
        out_shape=jax.ShapeDtypeStruct((N_DEV,) + out, F32),
        compiler_params=_cparams(("parallel", "arbitrary"), 40),
    )(lhs, rhs)


def _attention_backward(sinks, q, kd, vd, dout, cos, sin, token, n_seq, seq):
    n_blocks = seq // BLOCK
    n_pairs = N_Q_HEADS // 2
    n_kv = N_Q_HEADS // Q_PER_KV
    chains = min(BACKWARD_CHAINS, n_blocks)

    def body(sink_ref, q_ref, k_ref, v_ref, do_ref, cos_ref, sin_ref, token_ref, dq_ref, dk_ref, dv_ref, dsink_ref,
             s_all, dp_all, dsc_all, dsp_all, pc_all, pp_all, dk_acc, dv_acc):
        low = lax.broadcasted_iota(jnp.int32, (BLOCK, LANES), 1) < HEAD_DIM

        @pl.when(pl.program_id(0) == 0)
        def _():
            dsink_ref[...] = jnp.zeros_like(dsink_ref)

        def fold(x):
            return x + pltpu.roll(x, HEAD_DIM, 1)

        def block(n, u, dsink):
            s_ref, dp_ref, dsc_ref, dsp_ref = s_all.at[u], dp_all.at[u], dsc_all.at[u], dsp_all.at[u]
            pc_ref, pp_ref = pc_all.at[u], pp_all.at[u]
            r0 = pl.multiple_of(n * BLOCK, BLOCK)
            p0 = pl.multiple_of(jnp.maximum(n - 1, 0) * BLOCK, BLOCK)
            cur_mask, prev_mask = _window_masks(n)
            keys = [_window_operand(k_ref, r0, p0, kv) for kv in range(n_kv)]
            values = [_window_operand(v_ref, r0, p0, kv) for kv in range(n_kv)]
            q_pairs = [q_ref[pl.ds(r0, BLOCK), pair * LANES:(pair + 1) * LANES] for pair in range(n_pairs)]
            do_pairs = [do_ref[pl.ds(r0, BLOCK), pair * LANES:(pair + 1) * LANES] for pair in range(n_pairs)]
            for pair in range(n_pairs):
                wide_s = _dot_nt(q_pairs[pair], keys[pair // 2])
                wide_dp = _dot_nt(do_pairs[pair], values[pair // 2])
                for parity in range(2):
                    s_ref[2 * pair + parity] = _merged_window(wide_s, parity, cur_mask, prev_mask, NEG)
                    dp_ref[2 * pair + parity] = _merged_window(wide_dp, parity, cur_mask, prev_mask, 0.0)

            probs, p_sink = _softmax_with_sink(s_ref[...], sink_ref[:, :, 0:1])
            dprobs = dp_ref[...]
            delta = jnp.sum(probs * dprobs, axis=-1, keepdims=True)
            dscores = (probs * (dprobs - delta)).astype(BF16)
            dsink = dsink - jnp.sum(p_sink * delta, axis=1, keepdims=True)
            probs = probs.astype(BF16)
            zero = jnp.zeros((BLOCK, BLOCK), BF16)
            for head in range(N_Q_HEADS):
                group, rows = 2 * (head // Q_PER_KV) + head % 2, pl.ds(((head % Q_PER_KV) // 2) * BLOCK, BLOCK)
                dsc_ref[group, rows, :] = jnp.where(cur_mask, dscores[head], zero)
                dsp_ref[group, rows, :] = jnp.where(cur_mask, zero, dscores[head])
                pc_ref[group, rows, :] = jnp.where(cur_mask, probs[head], zero)
                pp_ref[group, rows, :] = jnp.where(cur_mask, zero, probs[head])

            for pair in range(n_pairs):
                kv, rows = pair // 2, pl.ds((pair % 2) * BLOCK, BLOCK)
                wide = jnp.concatenate([dsc_ref[2 * kv, rows, :], dsc_ref[2 * kv + 1, rows, :],
                                        dsp_ref[2 * kv, rows, :], dsp_ref[2 * kv + 1, rows, :]], axis=1)
                dq = _dot(wide, keys[kv]) * (HEAD_DIM ** -0.5)
                dq = _rope(dq, cos_ref[pl.ds(r0, BLOCK), :], -sin_ref[pl.ds(r0, BLOCK), :])
                dq_ref[pl.ds(r0, BLOCK), pair * LANES:(pair + 1) * LANES] = dq.astype(BF16)

            def onto_keys(ref, kv, other):
                even, odd = _dot_tn(ref[2 * kv], other), _dot_tn(ref[2 * kv + 1], other)
                return fold(jnp.where(low, even, odd))

            parts = []
            for kv in range(n_kv):
                q_rows = jnp.concatenate([q_pairs[2 * kv], q_pairs[2 * kv + 1]], axis=0)
                do_rows = jnp.concatenate([do_pairs[2 * kv], do_pairs[2 * kv + 1]], axis=0)
                parts.append((onto_keys(dsc_ref, kv, q_rows), onto_keys(dsp_ref, kv, q_rows),
                              onto_keys(pc_ref, kv, do_rows), onto_keys(pp_ref, kv, do_rows)))
            merge = lambda j: jnp.where(low, parts[0][j], parts[1][j])
            dk_acc[pl.ds(r0, BLOCK), :] = merge(0)
            dv_acc[pl.ds(r0, BLOCK), :] = merge(2)

            @pl.when(n > 0)
            def _():
                dk_acc[pl.ds(p0, BLOCK), :] += merge(1)
                dv_acc[pl.ds(p0, BLOCK), :] += merge(3)

            return dsink

        def step(i, dsink):
            for u in range(chains):
                dsink = block(i * chains + u, u, dsink)
            return dsink

        dsink = lax.fori_loop(0, n_blocks // chains, step, jnp.zeros((N_Q_HEADS, 1, 1), F32))
        dsink_ref[...] += jnp.broadcast_to(dsink, dsink_ref.shape)
        dk_ref[...] = _rope(dk_acc[...], cos_ref[...], -sin_ref[...]).astype(BF16)
        dv_ref[...] = dv_acc[...].astype(BF16)

    seq_block = lambda width: pl.BlockSpec((seq, width), lambda b: (b, 0))
    table = pl.BlockSpec((seq, LANES), lambda b: (0, 0))
    per_head = pl.BlockSpec((N_Q_HEADS, 1, LANES), lambda b: (0, 0, 0))
    per_block = pltpu.VMEM((chains, N_Q_HEADS, BLOCK, BLOCK), F32)
    grouped = pltpu.VMEM((chains, 2 * n_kv, 2 * BLOCK, BLOCK), BF16)
    return pl.pallas_call(
        body, name="attention_backward", grid=(n_seq,),
        in_specs=[per_head, seq_block(ATTN_WIDTH), seq_block(2 * KV_WIDTH), seq_block(2 * KV_WIDTH),
                  seq_block(ATTN_WIDTH), table, table, pl.BlockSpec((8, LANES), lambda b: (0, 0))],
        out_specs=[seq_block(ATTN_WIDTH), seq_block(KV_WIDTH), seq_block(KV_WIDTH), per_head],
        out_shape=[jax.ShapeDtypeStruct((n_seq * seq, ATTN_WIDTH), BF16), jax.ShapeDtypeStruct((n_seq * seq, KV_WIDTH), BF16),
                   jax.ShapeDtypeStruct((n_seq * seq, KV_WIDTH), BF16), jax.ShapeDtypeStruct((N_Q_HEADS, 1, LANES), F32)],
        scratch_shapes=[per_block, per_block, grouped, grouped, grouped, grouped, pltpu.VMEM((seq, KV_WIDTH), F32), pltpu.VMEM((seq, KV_WIDTH), F32)],
        compiler_params=_cparams(("arbitrary",), 40),
    )(sinks, q, kd, vd, dout, cos, sin, token)


def _pool_backward(u, dpool, w_pool, pool_scale, n_seq, seq):
    groups = len(POOL_WINDOWS)

    def body(u_ref, dp_ref, w_ref, s_ref, du_ref, dw_ref, ds_ref):
        @pl.when(pl.program_id(0) == 0)
        def _():
            dw_ref[...] = jnp.zeros_like(dw_ref)
            ds_ref[...] = jnp.zeros_like(ds_ref)

        t = lax.broadcasted_iota(jnp.int32, (seq, 1), 0)
        for g, window in enumerate(POOL_WINDOWS):
            cols = slice(g * POOL_GROUP_DIM, (g + 1) * POOL_GROUP_DIM)
            d, count = _pool_features(u_ref[:, cols], window, t, seq)
            dpool_g = dp_ref[:, cols]
            ds_ref[:, cols] += jnp.sum(dpool_g * _dot(d, w_ref[g]), axis=0, keepdims=True)
            dy = (dpool_g * s_ref[:, cols]).astype(BF16)
            dw_ref[g] += _dot_tn(d, dy)
            dd = _dot_nt(dy, w_ref[g])
            du_ref[:, cols] = (_leading(dd / count, window, t, seq) - dd).astype(BF16)

    seq_block = pl.BlockSpec((seq, POOL_WIDTH), lambda b: (b, 0))
    weights = pl.BlockSpec((groups, POOL_GROUP_DIM, POOL_GROUP_DIM), lambda b: (0, 0, 0))
    scale = pl.BlockSpec((1, POOL_WIDTH), lambda b: (0, 0))
    return pl.pallas_call(
        body, name="pool_backward", grid=(n_seq,),
        in_specs=[seq_block, seq_block, weights, scale],
        out_specs=[seq_block, weights, scale],
        out_shape=[jax.ShapeDtypeStruct((n_seq * seq, POOL_WIDTH), BF16),
                   jax.ShapeDtypeStruct((groups, POOL_GROUP_DIM, POOL_GROUP_DIM), F32), jax.ShapeDtypeStruct((1, POOL_WIDTH), F32)],
        compiler_params=_cparams(("arbitrary",), 40),
    )(u, dpool, w_pool, pool_scale)


def _inproj_backward(dq, dk, dv, du, w_in, h1, x2d, dx2, gain, tm):
    rows = x2d.shape[0]

    def body(dq_ref, dk_ref, dv_ref, du_ref, w_ref, h_ref, x_ref, dx2_ref, g_ref, dx_ref, dw_ref, dg_ref):
        @pl.when(pl.program_id(0) == 0)
        def _():
            dw_ref[...] = jnp.zeros_like(dw_ref)
            dg_ref[...] = jnp.zeros_like(dg_ref)

        dproj = jnp.concatenate([dq_ref[...], dk_ref[...], dv_ref[...], du_ref[...]], axis=1)
        dw_ref[...] += _dot_tn(h_ref[...], dproj)
        dnorm, dg = _rms_backward(_dot_nt(dproj, w_ref[...]), x_ref[...], g_ref[...])
        dg_ref[...] += dg
        dx_ref[...] = dx2_ref[...] + dnorm

    row = lambda width: pl.BlockSpec((tm, width), lambda i: (i, 0))
    vec = pl.BlockSpec((1, D_MODEL), lambda i: (0, 0))
    weight = pl.BlockSpec((D_MODEL, IN_WIDTH), lambda i: (0, 0))
    return pl.pallas_call(
        body, name="inproj_backward", grid=(rows // tm,),
        in_specs=[row(ATTN_WIDTH), row(KV_WIDTH), row(KV_WIDTH), row(POOL_WIDTH), weight, row(D_MODEL), row(D_MODEL), row(D_MODEL), vec],
        out_specs=[row(D_MODEL), weight, vec],
        out_shape=[jax.ShapeDtypeStruct((rows, D_MODEL), F32), jax.ShapeDtypeStruct((D_MODEL, IN_WIDTH), F32),
                   jax.ShapeDtypeStruct((1, D_MODEL), F32)],
        compiler_params=_cparams(("arbitrary",), 48),
    )(dq, dk, dv, du, w_in, h1, x2d, dx2, gain)


def _place():
    return lax.axis_index("x"), lax.axis_index("y"), lax.axis_index("c")


def _peer(x, y, c, rel):
    return (1 - x if rel & 4 else x, 1 - y if rel & 2 else y, 1 - c if rel & 1 else c)


def _index(px, py, pc):
    return 4 * px + 2 * py + pc


def _gather_weights(shards, n_full):
    n = len(shards)

    def body(*refs):
        ins, outs = refs[:n], refs[n:2 * n]
        stage = refs[2 * n:3 * n]
        send_sems, recv_sems, local_sems = refs[3 * n:]
        x, y, c = _place()
        me, sibling = (x, y, c), (x, y, 1 - c)
        chips = [(1 - x, y), (x, 1 - y), (1 - x, 1 - y)]

        def copy(a, k, block, to, src=None):
            slot = outs[a].at[_index(*block)]
            return pltpu.make_async_remote_copy(src_ref=slot if src is None else src, dst_ref=slot, send_sem=send_sems.at[a, k],
                                                recv_sem=recv_sems.at[a, k], device_id=to, device_id_type=MESH)

        mine, first = [], []
        for a in range(n):
            stage[a][...] = ins[a][...].astype(BF16)
            mine.append(pltpu.make_async_copy(stage[a], outs[a].at[_index(*me)], local_sems.at[a]))
            mine[-1].start()
            if a < n_full:
                first.append(copy(a, 0, me, sibling, src=stage[a]))
                first += [copy(a, 1 + j, me, (*chip, c), src=stage[a]) for j, chip in enumerate(chips)]
        for cp in first:
            cp.start()
        passed = []
        for j, chip in enumerate(chips):
            for a in range(n_full):
                copy(a, 1 + j, (*chip, c), me).wait_recv()
                passed.append(copy(a, 4 + j, (*chip, c), sibling))
                passed[-1].start()
        for a in range(n_full):
            copy(a, 0, sibling, me).wait_recv()
            for j, chip in enumerate(chips):
                copy(a, 4 + j, (*chip, 1 - c), me).wait_recv()
        for cp in first + passed:
            cp.wait_send()
        for cp in mine:
            cp.wait()

    return pl.pallas_call(
        body, name="gather_weights",
        in_specs=[pl.BlockSpec(memory_space=pltpu.VMEM)] * n,
        out_specs=[pl.BlockSpec(memory_space=pl.ANY)] * n,
        out_shape=[jax.ShapeDtypeStruct((N_DEV,) + s.shape, BF16) for s in shards],
        scratch_shapes=[pltpu.VMEM(s.shape, BF16) for s in shards]
        + [pltpu.SemaphoreType.DMA((n, 7)), pltpu.SemaphoreType.DMA((n, 7)), pltpu.SemaphoreType.DMA((n,))],
        compiler_params=pltpu.CompilerParams(vmem_limit_bytes=32 * MIB),
    )(*shards)


def _exchange_gradients(grads, small):
    n = len(grads)

    def body(*refs):
        ins, small_ref = refs[:n], refs[n]
        outs, total_ref = refs[n + 1:2 * n + 1], refs[2 * n + 1]
        gathered, send_sems, recv_sems, small_send, small_recv = refs[2 * n + 2:]
        x, y, c = _place()
        me = _index(x, y, c)
        gathered[0] = small_ref[...]
        copies = []
        for rel in range(1, N_DEV):
            to = _peer(x, y, c, rel)
            copies.append(pltpu.make_async_remote_copy(src_ref=small_ref, dst_ref=gathered.at[rel], send_sem=small_send.at[rel - 1],
                                                       recv_sem=small_recv.at[rel - 1], device_id=to, device_id_type=MESH))
            for a in range(n):
                copies.append(pltpu.make_async_remote_copy(src_ref=ins[a].at[_index(*to)], dst_ref=outs[a].at[rel - 1],
                                                           send_sem=send_sems.at[a, rel - 1], recv_sem=recv_sems.at[a, rel - 1],
                                                           device_id=to, device_id_type=MESH))
        for cp in copies:
            cp.start()
        for cp in copies:
            cp.wait()
        total = gathered[me]
        for source in range(1, N_DEV):
            total = total + gathered[jnp.bitwise_xor(me, source)]
        total_ref[...] = total

    return pl.pallas_call(
        body, name="exchange_gradients",
        in_specs=[pl.BlockSpec(memory_space=pl.ANY)] * n + [pl.BlockSpec(memory_space=pltpu.VMEM)],
        out_specs=[pl.BlockSpec(memory_space=pl.ANY)] * n + [pl.BlockSpec(memory_space=pltpu.VMEM)],
        out_shape=[jax.ShapeDtypeStruct((N_DEV - 1,) + g.shape[1:], F32) for g in grads] + [jax.ShapeDtypeStruct(small.shape, F32)],
        scratch_shapes=[pltpu.VMEM((N_DEV,) + small.shape, F32), pltpu.SemaphoreType.DMA((n, 7)), pltpu.SemaphoreType.DMA((n, 7)),
                        pltpu.SemaphoreType.DMA((7,)), pltpu.SemaphoreType.DMA((7,))],
        compiler_params=pltpu.CompilerParams(vmem_limit_bytes=32 * MIB),
    )(*grads, small)


def _own_slot_to_all(a, rel, me, to, ins, lands):
    return lands[a].at[me], lands[a].at[me]


def _block_to_owner(a, rel, me, to, ins, lands):
    return ins[a].at[to], lands[a].at[rel - 1]


def _split_copies(plan, ins, lands, send_sems, recv_sems):
    x, y, c = _place()
    copies = []
    for rel in range(1, N_DEV):
        to = _peer(x, y, c, rel)
        for a in range(len(lands)):
            src, dst = plan(a, rel, _index(x, y, c), _index(*to), ins, lands)
            k = a * (N_DEV - 1) + rel - 1
            copies.append(pltpu.make_async_remote_copy(src_ref=src, dst_ref=dst, send_sem=send_sems.at[k],
                                                       recv_sem=recv_sems.at[k], device_id=to, device_id_type=MESH))
    return copies


HBM_SPEC = pl.BlockSpec(memory_space=pltpu.HBM)
SEM_SPEC = pl.BlockSpec(memory_space=pltpu.SEMAPHORE)
EFFECT = pltpu.SideEffectType.DATAFLOW_SIDE_EFFECTING


def _start_copies(name, plan, ins, lands):
    n_in, n = len(ins), len(ins) + len(lands)

    def body(*refs):
        for cp in _split_copies(plan, refs[:n_in], refs[n_in:n], refs[n], refs[n + 1]):
            cp.start()
        refs[-1][...] = jnp.zeros_like(refs[-1])

    arrays = [pltpu.with_memory_space_constraint(v, pltpu.HBM) for v in (*ins, *lands)]
    sems = pltpu.SemaphoreType.DMA((len(lands) * (N_DEV - 1),))
    send_sems, recv_sems, *flying, token = pl.pallas_call(
        body, name=name,
        out_shape=(sems, sems, *[pltpu.HBM(v.shape, v.dtype) for v in arrays], jax.ShapeDtypeStruct((8, LANES), F32)),
        in_specs=[HBM_SPEC] * n, out_specs=(SEM_SPEC, SEM_SPEC, *[HBM_SPEC] * n, pl.BlockSpec(memory_space=pltpu.VMEM)),
        input_output_aliases={i: 2 + i for i in range(n)},
        compiler_params=pltpu.CompilerParams(has_side_effects=EFFECT),
    )(*arrays)
    return send_sems, recv_sems, flying, token


def _wait_copies(name, plan, n_in, send_sems, recv_sems, flying, after):
    n = len(flying)

    def body(*refs):
        for cp in _split_copies(plan, refs[:n_in], refs[n_in:n], refs[n], refs[n + 1]):
            cp.wait_send()
            cp.wait_recv()

    landed = pl.pallas_call(
        body, name=name, out_shape=tuple(pltpu.HBM(v.shape, v.dtype) for v in flying),
        in_specs=[HBM_SPEC] * n + [SEM_SPEC, SEM_SPEC, pl.BlockSpec(memory_space=pl.ANY)], out_specs=tuple([HBM_SPEC] * n),
        input_output_aliases={i: i for i in range(n)},
        compiler_params=pltpu.CompilerParams(has_side_effects=EFFECT),
    )(*flying, send_sems, recv_sems, after)
    return landed[:n_in], landed[n_in:]


def _adamw_math(w, g, m, v):
    m = ADAM_B1 * m + (1.0 - ADAM_B1) * g
    v = ADAM_B2 * v + (1.0 - ADAM_B2) * (g * g)
    m_hat = m / (1.0 - ADAM_B1 ** ADAM_STEP)
    v_hat = v / (1.0 - ADAM_B2 ** ADAM_STEP)
    return -ADAM_LR * (m_hat / (jnp.sqrt(v_hat) + ADAM_EPS) + ADAM_WD * w), m, v


def _adamw_sharded(me, own, received, w, m, v, tr):
    rows, cols = w.shape

    def body(me_ref, own_ref, rec_ref, w_ref, m_ref, v_ref, g_ref, d_ref, nm_ref, nv_ref):
        g = own_ref[...]
        for r in range(N_DEV - 1):
            g = g + rec_ref[r]
        g_ref[...] = g
        d_ref[...], nm_ref[...], nv_ref[...] = _adamw_math(w_ref[...], g, m_ref[...], v_ref[...])

    tile = pl.BlockSpec((tr, cols), lambda i, me_ref: (i, 0))
    shape = jax.ShapeDtypeStruct((rows, cols), F32)
    return pl.pallas_call(
        body, name="adamw_sharded",
        grid_spec=pltpu.PrefetchScalarGridSpec(
            num_scalar_prefetch=1, grid=(rows // tr,),
            in_specs=[pl.BlockSpec((None, tr, cols), lambda i, me_ref: (me_ref[0], i, 0)),
                      pl.BlockSpec((N_DEV - 1, tr, cols), lambda i, me_ref: (0, i, 0)), tile, tile, tile],
            out_specs=[tile, tile, tile, tile]),
        out_shape=[shape, shape, shape, shape],
        compiler_params=_cparams(("parallel",), 40),
    )(me, own, received, w, m, v)


def _adamw_small(w, g, m, v):
    def body(w_ref, g_ref, m_ref, v_ref, d_ref, nm_ref, nv_ref):
        d_ref[...], nm_ref[...], nv_ref[...] = _adamw_math(w_ref[...], g_ref[...], m_ref[...], v_ref[...])

    shape = jax.ShapeDtypeStruct(w.shape, F32)
    return pl.pallas_call(body, name="adamw_small", out_shape=[shape, shape, shape])(w, g, m, v)


def _pack_small(g1, g2, g3, ps, sink, loss, wp):
    def rows(a, n):
        a = a.reshape(-1, LANES)
        return jnp.pad(a, ((0, n - a.shape[0]), (0, 0)))

    sink = jnp.pad(sink.reshape(1, -1), ((0, 0), (0, LANES - sink.size)))
    return jnp.concatenate([rows(g1, 8), rows(g2, 8), rows(g3, 8), rows(ps, 8), rows(sink, 8), rows(loss, 8), rows(wp, 512)], axis=0)


def _unpack_small(p, like):
    g1, g2, g3, ps, sink, wp = like
    return (p[ROW_G1:ROW_G1 + 8].reshape(g1.shape), p[ROW_G2:ROW_G2 + 8].reshape(g2.shape), p[ROW_G3:ROW_G3 + 8].reshape(g3.shape),
            p[ROW_PS:ROW_PS + 4].reshape(ps.shape), p[ROW_SINK, :N_Q_HEADS].reshape(sink.shape), p[ROW_WP:ROW_WP + 512].reshape(wp.shape))


def _local_step(x, target, attn_norm_g, w_in_full, attn_sinks, w_pool, pool_scale, w_out_full, mlp_norm_g, final_norm_g,
                front_token, mlp_weights, ship_down, ship_up):
    n_seq, seq, _ = x.shape
    rows = n_seq * seq
    tm, tm_mlp, tm_grad = min(512, seq), min(256, seq), min(1024, seq)
    x2d, t2d = x.reshape(rows, D_MODEL), target.reshape(rows, D_MODEL)
    g3 = final_norm_g.reshape(1, D_MODEL)
    cos, sin = _rope_tables(seq)
    wp_b = w_pool[0].astype(BF16)

    sink_rows = jnp.broadcast_to(attn_sinks.reshape(N_Q_HEADS, 1, 1), (N_Q_HEADS, 1, LANES))
    h1, q, kd, vd, u = _norm_inproj(x2d, attn_norm_g, w_in_full, cos, sin, front_token, seq, tm)
    attn = _attention_forward(sink_rows, q, kd, vd, n_seq, seq)
    pool = _pool_forward(u, wp_b, pool_scale, n_seq, seq)
    x2, h2 = _outproj_norm(x2d, attn, pool, w_out_full, mlp_norm_g, tm)
    w_up_full, w_down_full = mlp_weights(h2)
    a, f, dx3, dx3b, loss_cols, dg3 = _mlp_forward_loss(h2, x2, w_up_full, w_down_full, g3, t2d, tm_mlp)

    down_token = ship_down(_weight_gradient("down_gradient", f, dx3b, True, tm_grad))
    da, dx2, dattn, dpool, dg2, d_w_out = _mlp_backward_data(dx3b, a, w_down_full, w_up_full, dx3, x2, mlp_norm_g, w_out_full,
                                                            attn, pool, down_token, tm_mlp)
    up_token = ship_up(d_w_out, _weight_gradient("up_gradient", h2, da, False, tm_grad))
    dq, dk, dv, dsink = _attention_backward(sink_rows, q, kd, vd, dattn, cos, sin, up_token, n_seq, seq)
    du, d_w_pool, d_pool_scale = _pool_backward(u, dpool, wp_b, pool_scale, n_seq, seq)
    grad_x, d_w_in, dg1 = _inproj_backward(dq, dk, dv, du, w_in_full, h1, x2d, dx2, attn_norm_g, tm)
    return (grad_x.reshape(x.shape), loss_cols, dg1, d_w_in, dsink[:, 0, 0], d_w_pool, d_pool_scale, dg2, dg3)


def kernel(x, attn_norm_g, w_in, attn_sinks, w_pool, pool_scale, w_out, mlp_norm_g, w_up, w_down, final_norm_g, loss_target, m_attn_norm_g, m_w_in, m_attn_sinks, m_w_pool, m_pool_scale, m_w_out, m_mlp_norm_g, m_w_up, m_w_down, m_final_norm_g, v_attn_norm_g, v_w_in, v_attn_sinks, v_w_pool, v_pool_scale, v_w_out, v_mlp_norm_g, v_w_up, v_w_down, v_final_norm_g):
    me = (4 * lax.axis_index("x") + 2 * lax.axis_index("y") + lax.axis_index("c")).astype(jnp.int32).reshape(1)

    win_g, wout_g, wup_land, wdown_land = _gather_weights([w_in[0], w_out[0], w_up[0], w_down[0]], 2)
    w_in_full = jnp.transpose(win_g, (1, 0, 2)).reshape(D_MODEL, IN_WIDTH)
    spread_send, spread_recv, spread_flying, front_token = _start_copies("spread_start", _own_slot_to_all, [], [wup_land, wdown_land])

    def mlp_weights(after):
        _, (wup_g, wdown_g) = _wait_copies("spread_wait", _own_slot_to_all, 0, spread_send, spread_recv, spread_flying, after)
        return jnp.transpose(wup_g, (1, 0, 2)).reshape(D_MODEL, D_FF), wdown_g.reshape(D_FF, D_MODEL)

    deliveries = {}

    def deliver(name, grads):
        lands = [lax.empty((N_DEV - 1,) + g.shape[1:], F32) for g in grads]
        deliveries[name] = _start_copies(name + "_start", _block_to_owner, grads, lands)
        return deliveries[name][3]

    (grad_x, loss_cols, dg1, d_w_in, dsink, d_w_pool, d_pool_scale, dg2, dg3) = _local_step(
        x, loss_target, attn_norm_g, w_in_full, attn_sinks, w_pool, pool_scale, wout_g.reshape(D_MODEL, D_MODEL), mlp_norm_g,
        final_norm_g, front_token, mlp_weights, lambda d_w_down: deliver("deliver_down", [d_w_down]),
        lambda d_w_out, d_w_up: deliver("deliver_up", [d_w_out.reshape(N_DEV, OUT_BLOCK, D_MODEL), d_w_up]))
    landed = {name: _wait_copies(name + "_wait", _block_to_owner, len(flying) // 2, send, recv, flying, grad_x)
              for name, (send, recv, flying, _) in deliveries.items()}
    own_mlp = [*landed["deliver_up"][0], *landed["deliver_down"][0]]
    received_mlp = [*landed["deliver_up"][1], *landed["deliver_down"][1]]

    d_w_in_blocks = jnp.transpose(d_w_in.reshape(D_MODEL, N_DEV, IN_BLOCK), (1, 0, 2))
    small = _pack_small(dg1, dg2, dg3, d_pool_scale, dsink, loss_cols, d_w_pool)
    received_in, small_sum = _exchange_gradients([d_w_in_blocks], small)
    big, received = [d_w_in_blocks, *own_mlp], [received_in, *received_mlp]

    loss = 0.5 * jnp.sum(small_sum[ROW_LOSS:ROW_LOSS + 8]) / D_MODEL
    shards = [(w_in, m_w_in, v_w_in, 1024), (w_out, m_w_out, v_w_out, 128), (w_up, m_w_up, v_w_up, 256), (w_down, m_w_down, v_w_down, 256)]
    sharded = [_adamw_sharded(me, own, rec, w[0], m[0], v[0], tr) for own, rec, (w, m, v, tr) in zip(big, received, shards)]
    g_in, g_out, g_up, g_down = [[r[None] for r in res] for res in sharded]

    like = (attn_norm_g, mlp_norm_g, final_norm_g, pool_scale, attn_sinks, w_pool)
    zeros = jnp.zeros((1, D_MODEL), F32)
    pack = lambda t: _pack_small(t[0], t[1], t[2], t[3], t[4], zeros, t[5])
    small_delta, small_m, small_v = _adamw_small(
        pack(like), small_sum, pack((m_attn_norm_g, m_mlp_norm_g, m_final_norm_g, m_pool_scale, m_attn_sinks, m_w_pool)),
        pack((v_attn_norm_g, v_mlp_norm_g, v_final_norm_g, v_pool_scale, v_attn_sinks, v_w_pool)))
    s_grad, s_delta, s_m, s_v = [_unpack_small(p, like) for p in (small_sum, small_delta, small_m, small_v)]

    def ordered(small_parts, k):
        n1, n2, n3, ps, sink, wp = small_parts
        return [n1, g_in[k], sink, wp, ps, g_out[k], n2, g_up[k], g_down[k], n3]

    return (loss, grad_x, *ordered(s_grad, 0), *ordered(s_delta, 1), *ordered(s_m, 2), *ordered(s_v, 3))
```

```python
import functools

import jax
import jax.numpy as jnp
from jax import lax
from jax.experimental import pallas as pl
from jax.experimental.pallas import tpu as pltpu

F32 = jnp.float32
BF16 = jnp.bfloat16

D_MODEL = 1024
HEAD_DIM = 64
N_Q_HEADS = 8
Q_PER_KV = 4
ATTN_WIDTH = 512
KV_WIDTH = 128
BLOCK = 128
ROPE_THETA = 10000.0
POOL_WINDOWS = (2, 4, 8, 16)
POOL_WIDTH = 512
POOL_GROUP_DIM = 128
IN_WIDTH = 1280
D_FF = 4096
EPS = 1e-6
N_DEV = 8
FF_BLOCK = D_FF // N_DEV
IN_BLOCK = IN_WIDTH // N_DEV
OUT_BLOCK = D_MODEL // N_DEV
ADAM_LR = 0.001
ADAM_B1 = 0.9
ADAM_B2 = 0.999
ADAM_EPS = 1e-08
ADAM_WD = 0.01
ADAM_STEP = 10
NEG = -1e30
FORWARD_CHAINS = 4
BACKWARD_CHAINS = 2
LANES = 128
MIB = 1024 * 1024
MESH = pl.DeviceIdType.MESH

ROW_G1, ROW_G2, ROW_G3, ROW_PS, ROW_SINK, ROW_LOSS, ROW_WP, SMALL_ROWS = 0, 8, 16, 24, 32, 40, 48, 560


def _cparams(semantics, vmem_mib):
    return pltpu.CompilerParams(dimension_semantics=semantics, vmem_limit_bytes=vmem_mib * MIB)


def _dot(a, b):
    return jnp.dot(a, b, preferred_element_type=F32)


def _dot_nt(a, b):
    return lax.dot_general(a, b, (((1,), (1,)), ((), ())), preferred_element_type=F32)


def _dot_tn(a, b):
    return lax.dot_general(a, b, (((0,), (0,)), ((), ())), preferred_element_type=F32)


def _swap_halves(x):
    width = x.shape[1]
    lane = lax.broadcasted_iota(jnp.int32, x.shape, 1)
    ahead = pltpu.roll(x, width - HEAD_DIM // 2, 1)
    behind = pltpu.roll(x, HEAD_DIM // 2, 1)
    return jnp.where(lane % HEAD_DIM < HEAD_DIM // 2, ahead, behind)


def _rope(x, cos, sin):
    reps = x.shape[1] // LANES
    if reps > 1:
        cos = jnp.tile(cos, (1, reps))
        sin = jnp.tile(sin, (1, reps))
    return x * cos + _swap_halves(x) * sin


def _rope_tables(seq):
    half = HEAD_DIM // 2
    inv_freq = ROPE_THETA ** (-jnp.arange(half, dtype=F32) / half)
    ang = jnp.arange(seq).astype(F32)[:, None] * inv_freq[None, :]
    cos, sin = jnp.cos(ang), jnp.sin(ang)
    cos = jnp.tile(cos, (1, LANES // half))
    sin = jnp.tile(jnp.concatenate([-sin, sin], axis=1), (1, LANES // HEAD_DIM))
    return cos, sin


def _both_halves(x):
    lane = lax.broadcasted_iota(jnp.int32, x.shape, 1)
    other = pltpu.roll(x, HEAD_DIM, 1)
    low = lane < HEAD_DIM
    return jnp.where(low, x, other), jnp.where(low, other, x)


def _rms_backward(dh, xin, gain):
    r = lax.rsqrt(jnp.mean(xin * xin, axis=-1, keepdims=True) + EPS)
    xhat = xin * r
    dxhat = dh * gain
    dx = r * (dxhat - xhat * jnp.mean(dxhat * xhat, axis=-1, keepdims=True))
    return dx, jnp.sum(dh * xhat, axis=0, keepdims=True)


def _norm_inproj(x2d, gain, w_in, cos, sin, token, seq, tm):
    rows = x2d.shape[0]
    tiles_per_seq = seq // tm

    def body(x_ref, g_ref, w_ref, cos_ref, sin_ref, token_ref, h_ref, q_ref, k_ref, v_ref, u_ref):
        x = x_ref[...]
        r = lax.rsqrt(jnp.mean(x * x, axis=-1, keepdims=True) + EPS)
        h = (x * r * g_ref[...]).astype(BF16)
        h_ref[...] = h
        proj = _dot(h, w_ref[...])
        cos_t, sin_t = cos_ref[...], sin_ref[...]
        q = _rope(proj[:, :ATTN_WIDTH], cos_t, sin_t) * (HEAD_DIM ** -0.5)
        q_ref[...] = q.astype(BF16)
        k = _rope(proj[:, ATTN_WIDTH:ATTN_WIDTH + KV_WIDTH], cos_t, sin_t)
        k0, k1 = _both_halves(k)
        k_ref[...] = jnp.concatenate([k0, k1], axis=1).astype(BF16)
        v0, v1 = _both_halves(proj[:, ATTN_WIDTH + KV_WIDTH:ATTN_WIDTH + 2 * KV_WIDTH])
        v_ref[...] = jnp.concatenate([v0, v1], axis=1).astype(BF16)
        u_ref[...] = proj[:, ATTN_WIDTH + 2 * KV_WIDTH:]

    row = lambda width: pl.BlockSpec((tm, width), lambda i: (i, 0))
    table = pl.BlockSpec((tm, LANES), lambda i: (i % tiles_per_seq, 0))
    return pl.pallas_call(
        body, name="norm_inproj", grid=(rows // tm,),
        in_specs=[row(D_MODEL), pl.BlockSpec((1, D_MODEL), lambda i: (0, 0)),
                  pl.BlockSpec((D_MODEL, IN_WIDTH), lambda i: (0, 0)), table, table, pl.BlockSpec((8, LANES), lambda i: (0, 0))],
        out_specs=[row(D_MODEL), row(ATTN_WIDTH), row(2 * KV_WIDTH), row(2 * KV_WIDTH), row(POOL_WIDTH)],
        out_shape=[jax.ShapeDtypeStruct((rows, D_MODEL), BF16), jax.ShapeDtypeStruct((rows, ATTN_WIDTH), BF16),
                   jax.ShapeDtypeStruct((rows, 2 * KV_WIDTH), BF16), jax.ShapeDtypeStruct((rows, 2 * KV_WIDTH), BF16),
                   jax.ShapeDtypeStruct((rows, POOL_WIDTH), F32)],
        compiler_params=_cparams(("parallel",), 40),
    )(x2d, gain, w_in, cos, sin, token)


def _window_masks(n):
    qi = lax.broadcasted_iota(jnp.int32, (BLOCK, BLOCK), 0)
    kj = lax.broadcasted_iota(jnp.int32, (BLOCK, BLOCK), 1)
    return kj <= qi, jnp.logical_and(kj > qi, n > 0)


def _window_operand(ref, r0, p0, kv):
    low = lax.broadcasted_iota(jnp.int32, (BLOCK, LANES), 1) < HEAD_DIM
    cur = ref[pl.ds(r0, BLOCK), kv * LANES:(kv + 1) * LANES]
    prev = ref[pl.ds(p0, BLOCK), kv * LANES:(kv + 1) * LANES]
    zero = jnp.zeros_like(cur)
    return jnp.concatenate([jnp.where(low, cur, zero), jnp.where(low, zero, cur), jnp.where(low, prev, zero), jnp.where(low, zero, prev)], axis=0)


def _merged_window(wide, parity, cur_mask, prev_mask, fill):
    cur = wide[:, parity * LANES:(parity + 1) * LANES]
    prev = wide[:, (2 + parity) * LANES:(3 + parity) * LANES]
    return jnp.where(cur_mask, cur, jnp.where(prev_mask, prev, fill))


def _softmax_with_sink(scores, sink):
    m = jnp.maximum(jnp.max(scores, axis=-1, keepdims=True), sink)
    p, ps = jnp.exp(scores - m), jnp.exp(sink - m)
    inv = 1.0 / (jnp.sum(p, axis=-1, keepdims=True) + ps)
    return p * inv, ps * inv


def _attention_forward(sinks, q, kd, vd, n_seq, seq):
    n_blocks = seq // BLOCK
    n_pairs = N_Q_HEADS // 2
    chains = min(FORWARD_CHAINS, n_blocks)

    def body(sink_ref, q_ref, k_ref, v_ref, o_ref, s_ref, p_ref):
        def block(n, u):
            r0 = pl.multiple_of(n * BLOCK, BLOCK)
            p0 = pl.multiple_of(jnp.maximum(n - 1, 0) * BLOCK, BLOCK)
            cur_mask, prev_mask = _window_masks(n)
            keys = [_window_operand(k_ref, r0, p0, kv) for kv in range(2)]
            values = [_window_operand(v_ref, r0, p0, kv) for kv in range(2)]
            for pair in range(n_pairs):
                wide = _dot_nt(q_ref[pl.ds(r0, BLOCK), pair * LANES:(pair + 1) * LANES], keys[pair // 2])
                for parity in range(2):
                    s_ref[u, 2 * pair + parity] = _merged_window(wide, parity, cur_mask, prev_mask, NEG)
            probs, _ = _softmax_with_sink(s_ref[u], sink_ref[:, :, 0:1])
            probs = probs.astype(BF16)
            zero = jnp.zeros((BLOCK, BLOCK), BF16)
            for pair in range(n_pairs):
                for parity in range(2):
                    ph = probs[2 * pair + parity]
                    p_ref[u, pair, :, parity * LANES:(parity + 1) * LANES] = jnp.where(cur_mask, ph, zero)
                    p_ref[u, pair, :, (2 + parity) * LANES:(3 + parity) * LANES] = jnp.where(cur_mask, zero, ph)
            for pair in range(n_pairs):
                o_ref[pl.ds(r0, BLOCK), pair * LANES:(pair + 1) * LANES] = _dot(p_ref[u, pair], values[pair // 2]).astype(BF16)

        def step(i, carry):
            for u in range(chains):
                block(i * chains + u, u)
            return carry

        lax.fori_loop(0, n_blocks // chains, step, 0)

    seq_block = lambda width: pl.BlockSpec((seq, width), lambda b: (b, 0))
    return pl.pallas_call(
        body, name="attention_forward", grid=(n_seq,),
        in_specs=[pl.BlockSpec((N_Q_HEADS, 1, LANES), lambda b: (0, 0, 0)), seq_block(ATTN_WIDTH), seq_block(2 * KV_WIDTH),
                  seq_block(2 * KV_WIDTH)],
        out_specs=seq_block(ATTN_WIDTH),
        out_shape=jax.ShapeDtypeStruct((n_seq * seq, ATTN_WIDTH), BF16),
        scratch_shapes=[pltpu.VMEM((chains, N_Q_HEADS, BLOCK, BLOCK), F32), pltpu.VMEM((chains, n_pairs, BLOCK, 4 * LANES), BF16)],
        compiler_params=_cparams(("parallel",), 40),
    )(sinks, q, kd, vd)


def _trailing(x, window, t, seq):
    k = 1
    while k < window:
        x = x + jnp.where(t >= k, pltpu.roll(x, k, 0), 0.0)
        k *= 2
    return x


def _leading(x, window, t, seq):
    k = 1
    while k < window:
        x = x + jnp.where(t < seq - k, pltpu.roll(x, seq - k, 0), 0.0)
        k *= 2
    return x


def _pool_features(u_g, window, t, seq):
    count = jnp.minimum(t + 1, window).astype(F32)
    return (_trailing(u_g, window, t, seq) / count - u_g).astype(BF16), count


def _pool_forward(u, w_pool, pool_scale, n_seq, seq):
    def body(u_ref, w_ref, s_ref, o_ref):
        t = lax.broadcasted_iota(jnp.int32, (seq, 1), 0)
        for g, window in enumerate(POOL_WINDOWS):
            cols = slice(g * POOL_GROUP_DIM, (g + 1) * POOL_GROUP_DIM)
            d, _ = _pool_features(u_ref[:, cols], window, t, seq)
            o_ref[:, cols] = (_dot(d, w_ref[g]) * s_ref[:, cols]).astype(BF16)

    seq_block = pl.BlockSpec((seq, POOL_WIDTH), lambda b: (b, 0))
    return pl.pallas_call(
        body, name="pool_forward", grid=(n_seq,),
        in_specs=[seq_block, pl.BlockSpec((len(POOL_WINDOWS), POOL_GROUP_DIM, POOL_GROUP_DIM), lambda b: (0, 0, 0)),
                  pl.BlockSpec((1, POOL_WIDTH), lambda b: (0, 0))],
        out_specs=seq_block,
        out_shape=jax.ShapeDtypeStruct((n_seq * seq, POOL_WIDTH), BF16),
        compiler_params=_cparams(("parallel",), 40),
    )(u, w_pool, pool_scale)


def _outproj_norm(x2d, attn, pool, w_out, gain, tm):
    rows = x2d.shape[0]

    def body(x_ref, a_ref, p_ref, w_ref, g_ref, x2_ref, h_ref):
        x2 = x_ref[...] + _dot(a_ref[...], w_ref[:ATTN_WIDTH, :]) + _dot(p_ref[...], w_ref[ATTN_WIDTH:, :])
        x2_ref[...] = x2
        r = lax.rsqrt(jnp.mean(x2 * x2, axis=-1, keepdims=True) + EPS)
        h_ref[...] = (x2 * r * g_ref[...]).astype(BF16)

    row = lambda width: pl.BlockSpec((tm, width), lambda i: (i, 0))
    return pl.pallas_call(
        body, name="outproj_norm", grid=(rows // tm,),
        in_specs=[row(D_MODEL), row(ATTN_WIDTH), row(POOL_WIDTH), pl.BlockSpec((D_MODEL, D_MODEL), lambda i: (0, 0)),
                  pl.BlockSpec((1, D_MODEL), lambda i: (0, 0))],
        out_specs=[row(D_MODEL), row(D_MODEL)],
        out_shape=[jax.ShapeDtypeStruct((rows, D_MODEL), F32), jax.ShapeDtypeStruct((rows, D_MODEL), BF16)],
        compiler_params=_cparams(("parallel",), 40),
    )(x2d, attn, pool, w_out, gain)


def _resident(shape):
    return pl.BlockSpec(shape, lambda i: (0,) * len(shape), pipeline_mode=pl.Buffered(1))


def _mlp_forward_loss(h2, x2, w_up, w_down, gain, target, tm):
    rows = h2.shape[0]
    chunk = D_MODEL

    def body(h_ref, x_ref, up_ref, down_ref, g_ref, t_ref, a_ref, f_ref, dx_ref, dxb_ref, loss_ref, dg_ref):
        @pl.when(pl.program_id(0) == 0)
        def _():
            loss_ref[...] = jnp.zeros_like(loss_ref)
            dg_ref[...] = jnp.zeros_like(dg_ref)

        h = h_ref[...]
        for c in range(D_FF // chunk):
            cols = slice(c * chunk, (c + 1) * chunk)
            a = _dot(h, up_ref[:, cols])
            a_ref[:, cols] = a
            r = jnp.maximum(a, 0.0)
            f_ref[:, cols] = (r * r).astype(BF16)
        x3 = x_ref[...] + _dot(f_ref[...], down_ref[...])
        rn = lax.rsqrt(jnp.mean(x3 * x3, axis=-1, keepdims=True) + EPS)
        err = x3 * rn * g_ref[...] - t_ref[...]
        loss_ref[...] += jnp.sum(err * err, axis=0, keepdims=True)
        dx, dg = _rms_backward(err / D_MODEL, x3, g_ref[...])
        dg_ref[...] += dg
        dx_ref[...] = dx
        dxb_ref[...] = dx.astype(BF16)

    row = lambda width: pl.BlockSpec((tm, width), lambda i: (i, 0))
    vec = pl.BlockSpec((1, D_MODEL), lambda i: (0, 0))
    return pl.pallas_call(
        body, name="mlp_forward_loss", grid=(rows // tm,),
        in_specs=[row(D_MODEL), row(D_MODEL), _resident((D_MODEL, D_FF)), _resident((D_FF, D_MODEL)), vec, row(D_MODEL)],
        out_specs=[row(D_FF), row(D_FF), row(D_MODEL), row(D_MODEL), vec, vec],
        out_shape=[jax.ShapeDtypeStruct((rows, D_FF), F32), jax.ShapeDtypeStruct((rows, D_FF), BF16),
                   jax.ShapeDtypeStruct((rows, D_MODEL), F32), jax.ShapeDtypeStruct((rows, D_MODEL), BF16),
                   jax.ShapeDtypeStruct((1, D_MODEL), F32), jax.ShapeDtypeStruct((1, D_MODEL), F32)],
        compiler_params=_cparams(("arbitrary",), 56),
    )(h2, x2, w_up, w_down, gain, target)


def _mlp_backward_data(dx3b, a, w_down, w_up, dx3, x2, gain, w_out, attn, pool, token, tm):
    rows = dx3b.shape[0]
    steps = rows // tm
    chunk = D_MODEL

    def body(dxb_ref, a_ref, down_ref, up_ref, dx3_ref, x2_ref, g_ref, wo_ref, attn_ref, pool_ref, token_ref,
             da_ref, dx2_ref, dattn_ref, dpool_ref, dg_ref, dwo_hbm, dwo_acc, sem):
        @pl.when(pl.program_id(0) == 0)
        def _():
            dg_ref[...] = jnp.zeros_like(dg_ref)
            dwo_acc[...] = jnp.zeros_like(dwo_acc)

        dxb = dxb_ref[...]
        for c in range(D_FF // chunk):
            cols = slice(c * chunk, (c + 1) * chunk)
            r = jnp.maximum(a_ref[:, cols], 0.0)
            da_ref[:, cols] = (_dot_nt(dxb, down_ref[cols, :]) * (2.0 * r)).astype(BF16)
        dnorm, dg = _rms_backward(_dot_nt(da_ref[...], up_ref[...]), x2_ref[...], g_ref[...])
        dg_ref[...] += dg
        dx2 = dx3_ref[...] + dnorm
        dx2_ref[...] = dx2
        dx2b = dx2.astype(BF16)
        dmix = _dot_nt(dx2b, wo_ref[...])
        dattn_ref[...] = dmix[:, :ATTN_WIDTH].astype(BF16)
        dpool_ref[...] = dmix[:, ATTN_WIDTH:]
        dwo_acc[:ATTN_WIDTH, :] += _dot_tn(attn_ref[...], dx2b)
        dwo_acc[ATTN_WIDTH:, :] += _dot_tn(pool_ref[...], dx2b)

        @pl.when(pl.program_id(0) == steps - 1)
        def _():
            out = pltpu.make_async_copy(dwo_acc, dwo_hbm, sem)
            out.start()
            out.wait()

    row = lambda width: pl.BlockSpec((tm, width), lambda i: (i, 0))
    vec = pl.BlockSpec((1, D_MODEL), lambda i: (0, 0))
    return pl.pallas_call(
        body, name="mlp_backward_data", grid=(steps,),
        in_specs=[row(D_MODEL), row(D_FF), _resident((D_FF, D_MODEL)), _resident((D_MODEL, D_FF)), row(D_MODEL), row(D_MODEL), vec,
                  _resident((D_MODEL, D_MODEL)), row(ATTN_WIDTH), row(POOL_WIDTH), pl.BlockSpec((8, LANES), lambda i: (0, 0))],
        out_specs=[row(D_FF), row(D_MODEL), row(ATTN_WIDTH), row(POOL_WIDTH), vec, pl.BlockSpec(memory_space=pl.ANY)],
        out_shape=[jax.ShapeDtypeStruct((rows, D_FF), BF16), jax.ShapeDtypeStruct((rows, D_MODEL), F32),
                   jax.ShapeDtypeStruct((rows, ATTN_WIDTH), BF16), jax.ShapeDtypeStruct((rows, POOL_WIDTH), F32),
                   jax.ShapeDtypeStruct((1, D_MODEL), F32), jax.ShapeDtypeStruct((D_MODEL, D_MODEL), F32)],
        scratch_shapes=[pltpu.VMEM((D_MODEL, D_MODEL), F32), pltpu.SemaphoreType.DMA],
        compiler_params=_cparams(("arbitrary",), 56),
    )(dx3b, a, w_down, w_up, dx3, x2, gain, w_out, attn, pool, token)


def _weight_gradient(name, lhs, rhs, block_lhs, tm):
    rows = lhs.shape[0]
    steps = rows // tm
    out = (N_DEV, FF_BLOCK, rhs.shape[1]) if block_lhs else (N_DEV, lhs.shape[1], FF_BLOCK)

    def body(l_ref, r_ref, o_hbm, acc, sem):
        @pl.when(pl.program_id(0) == 0)
        def _():
            acc[...] = jnp.zeros_like(acc)

        for d in range(N_DEV):
            cols = slice(d * FF_BLOCK, (d + 1) * FF_BLOCK)
            acc[d] += _dot_tn(l_ref[:, cols], r_ref[...]) if block_lhs else _dot_tn(l_ref[...], r_ref[:, cols])

        @pl.when(pl.program_id(0) == steps - 1)
        def _():
            done = pltpu.make_async_copy(acc, o_hbm, sem)
            done.start()
            done.wait()

    return pl.pallas_call(
        body, name=name, grid=(steps,),
        in_specs=[pl.BlockSpec((tm, lhs.shape[1]), lambda i: (i, 0)), pl.BlockSpec((tm, rhs.shape[1]), lambda i: (i, 0))],
        out_specs=pl.BlockSpec(memory_space=pl.ANY),
        out_shape=jax.ShapeDtypeStruct(out, F32),
        scratch_shapes=[pltpu.VMEM(out, F32), pltpu.SemaphoreType.DMA],
        compiler_params=_cparams(("arbitrary",), 56),
    )(lhs, rhs)


def _attention_backward(sinks, q, kd, vd, dout, cos, sin, token, n_seq, seq):
    n_blocks = seq // BLOCK
    n_pairs = N_Q_HEADS // 2
    n_kv = N_Q_HEADS // Q_PER_KV
    chains = min(BACKWARD_CHAINS, n_blocks)

    def body(sink_ref, q_ref, k_ref, v_ref, do_ref, cos_ref, sin_ref, token_ref, dq_ref, dk_ref, dv_ref, dsink_ref,
             s_all, dp_all, dsc_all, dsp_all, pc_all, pp_all, dk_acc, dv_acc):
        low = lax.broadcasted_iota(jnp.int32, (BLOCK, LANES), 1) < HEAD_DIM

        @pl.when(pl.program_id(0) == 0)
        def _():
            dsink_ref[...] = jnp.zeros_like(dsink_ref)

        def fold(x):
            return x + pltpu.roll(x, HEAD_DIM, 1)

        def block(n, u, dsink):
            s_ref, dp_ref, dsc_ref, dsp_ref = s_all.at[u], dp_all.at[u], dsc_all.at[u], dsp_all.at[u]
            pc_ref, pp_ref = pc_all.at[u], pp_all.at[u]
            r0 = pl.multiple_of(n * BLOCK, BLOCK)
            p0 = pl.multiple_of(jnp.maximum(n - 1, 0) * BLOCK, BLOCK)
            cur_mask, prev_mask = _window_masks(n)
            keys = [_window_operand(k_ref, r0, p0, kv) for kv in range(n_kv)]
            values = [_window_operand(v_ref, r0, p0, kv) for kv in range(n_kv)]
            q_pairs = [q_ref[pl.ds(r0, BLOCK), pair * LANES:(pair + 1) * LANES] for pair in range(n_pairs)]
            do_pairs = [do_ref[pl.ds(r0, BLOCK), pair * LANES:(pair + 1) * LANES] for pair in range(n_pairs)]
            for pair in range(n_pairs):
                wide_s = _dot_nt(q_pairs[pair], keys[pair // 2])
                wide_dp = _dot_nt(do_pairs[pair], values[pair // 2])
                for parity in range(2):
                    s_ref[2 * pair + parity] = _merged_window(wide_s, parity, cur_mask, prev_mask, NEG)
                    dp_ref[2 * pair + parity] = _merged_window(wide_dp, parity, cur_mask, prev_mask, 0.0)

            probs, p_sink = _softmax_with_sink(s_ref[...], sink_ref[:, :, 0:1])
            dprobs = dp_ref[...]
            delta = jnp.sum(probs * dprobs, axis=-1, keepdims=True)
            dscores = (probs * (dprobs - delta)).astype(BF16)
            dsink = dsink - jnp.sum(p_sink * delta, axis=1, keepdims=True)
            probs = probs.astype(BF16)
            zero = jnp.zeros((BLOCK, BLOCK), BF16)
            for head in range(N_Q_HEADS):
                group, rows = 2 * (head // Q_PER_KV) + head % 2, pl.ds(((head % Q_PER_KV) // 2) * BLOCK, BLOCK)
                dsc_ref[group, rows, :] = jnp.where(cur_mask, dscores[head], zero)
                dsp_ref[group, rows, :] = jnp.where(cur_mask, zero, dscores[head])
                pc_ref[group, rows, :] = jnp.where(cur_mask, probs[head], zero)
                pp_ref[group, rows, :] = jnp.where(cur_mask, zero, probs[head])

            for pair in range(n_pairs):
                kv, rows = pair // 2, pl.ds((pair % 2) * BLOCK, BLOCK)
                wide = jnp.concatenate([dsc_ref[2 * kv, rows, :], dsc_ref[2 * kv + 1, rows, :],
                                        dsp_ref[2 * kv, rows, :], dsp_ref[2 * kv + 1, rows, :]], axis=1)
                dq = _dot(wide, keys[kv]) * (HEAD_DIM ** -0.5)
                dq = _rope(dq, cos_ref[pl.ds(r0, BLOCK), :], -sin_ref[pl.ds(r0, BLOCK), :])
                dq_ref[pl.ds(r0, BLOCK), pair * LANES:(pair + 1) * LANES] = dq.astype(BF16)

            def onto_keys(ref, kv, other):
                even, odd = _dot_tn(ref[2 * kv], other), _dot_tn(ref[2 * kv + 1], other)
                return fold(jnp.where(low, even, odd))

            parts = []
            for kv in range(n_kv):
                q_rows = jnp.concatenate([q_pairs[2 * kv], q_pairs[2 * kv + 1]], axis=0)
                do_rows = jnp.concatenate([do_pairs[2 * kv], do_pairs[2 * kv + 1]], axis=0)
                parts.append((onto_keys(dsc_ref, kv, q_rows), onto_keys(dsp_ref, kv, q_rows),
                              onto_keys(pc_ref, kv, do_rows), onto_keys(pp_ref, kv, do_rows)))
            merge = lambda j: jnp.where(low, parts[0][j], parts[1][j])
            dk_acc[pl.ds(r0, BLOCK), :] = merge(0)
            dv_acc[pl.ds(r0, BLOCK), :] = merge(2)

            @pl.when(n > 0)
            def _():
                dk_acc[pl.ds(p0, BLOCK), :] += merge(1)
                dv_acc[pl.ds(p0, BLOCK), :] += merge(3)

            return dsink

        def step(i, dsink):
            for u in range(chains):
                dsink = block(i * chains + u, u, dsink)
            return dsink

        dsink = lax.fori_loop(0, n_blocks // chains, step, jnp.zeros((N_Q_HEADS, 1, 1), F32))
        dsink_ref[...] += jnp.broadcast_to(dsink, dsink_ref.shape)
        dk_ref[...] = _rope(dk_acc[...], cos_ref[...], -sin_ref[...]).astype(BF16)
        dv_ref[...] = dv_acc[...].astype(BF16)

    seq_block = lambda width: pl.BlockSpec((seq, width), lambda b: (b, 0))
    table = pl.BlockSpec((seq, LANES), lambda b: (0, 0))
    per_head = pl.BlockSpec((N_Q_HEADS, 1, LANES), lambda b: (0, 0, 0))
    per_block = pltpu.VMEM((chains, N_Q_HEADS, BLOCK, BLOCK), F32)
    grouped = pltpu.VMEM((chains, 2 * n_kv, 2 * BLOCK, BLOCK), BF16)
    return pl.pallas_call(
        body, name="attention_backward", grid=(n_seq,),
        in_specs=[per_head, seq_block(ATTN_WIDTH), seq_block(2 * KV_WIDTH), seq_block(2 * KV_WIDTH),
                  seq_block(ATTN_WIDTH), table, table, pl.BlockSpec((8, LANES), lambda b: (0, 0))],
        out_specs=[seq_block(ATTN_WIDTH), seq_block(KV_WIDTH), seq_block(KV_WIDTH), per_head],
        out_shape=[jax.ShapeDtypeStruct((n_seq * seq, ATTN_WIDTH), BF16), jax.ShapeDtypeStruct((n_seq * seq, KV_WIDTH), BF16),
                   jax.ShapeDtypeStruct((n_seq * seq, KV_WIDTH), BF16), jax.ShapeDtypeStruct((N_Q_HEADS, 1, LANES), F32)],
        scratch_shapes=[per_block, per_block, grouped, grouped, grouped, grouped, pltpu.VMEM((seq, KV_WIDTH), F32), pltpu.VMEM((seq, KV_WIDTH), F32)],
        compiler_params=_cparams(("arbitrary",), 40),
    )(sinks, q, kd, vd, dout, cos, sin, token)


def _pool_backward(u, dpool, w_pool, pool_scale, n_seq, seq):
    groups = len(POOL_WINDOWS)

    def body(u_ref, dp_ref, w_ref, s_ref, du_ref, dw_ref, ds_ref):
        @pl.when(pl.program_id(0) == 0)
        def _():
            dw_ref[...] = jnp.zeros_like(dw_ref)
            ds_ref[...] = jnp.zeros_like(ds_ref)

        t = lax.broadcasted_iota(jnp.int32, (seq, 1), 0)
        for g, window in enumerate(POOL_WINDOWS):
            cols = slice(g * POOL_GROUP_DIM, (g + 1) * POOL_GROUP_DIM)
            d, count = _pool_features(u_ref[:, cols], window, t, seq)
            dpool_g = dp_ref[:, cols]
            ds_ref[:, cols] += jnp.sum(dpool_g * _dot(d, w_ref[g]), axis=0, keepdims=True)
            dy = (dpool_g * s_ref[:, cols]).astype(BF16)
            dw_ref[g] += _dot_tn(d, dy)
            dd = _dot_nt(dy, w_ref[g])
            du_ref[:, cols] = (_leading(dd / count, window, t, seq) - dd).astype(BF16)

    seq_block = pl.BlockSpec((seq, POOL_WIDTH), lambda b: (b, 0))
    weights = pl.BlockSpec((groups, POOL_GROUP_DIM, POOL_GROUP_DIM), lambda b: (0, 0, 0))
    scale = pl.BlockSpec((1, POOL_WIDTH), lambda b: (0, 0))
    return pl.pallas_call(
        body, name="pool_backward", grid=(n_seq,),
        in_specs=[seq_block, seq_block, weights, scale],
        out_specs=[seq_block, weights, scale],
        out_shape=[jax.ShapeDtypeStruct((n_seq * seq, POOL_WIDTH), BF16),
                   jax.ShapeDtypeStruct((groups, POOL_GROUP_DIM, POOL_GROUP_DIM), F32), jax.ShapeDtypeStruct((1, POOL_WIDTH), F32)],
        compiler_params=_cparams(("arbitrary",), 40),
    )(u, dpool, w_pool, pool_scale)


def _inproj_gradient(dq, dk, dv, du, h1, tm):
    rows = h1.shape[0]

    def body(dq_ref, dk_ref, dv_ref, du_ref, h_ref, dw_ref):
        @pl.when(pl.program_id(0) == 0)
        def _():
            dw_ref[...] = jnp.zeros_like(dw_ref)

        dproj = jnp.concatenate([dq_ref[...], dk_ref[...], dv_ref[...], du_ref[...]], axis=1)
        dw_ref[...] += _dot_tn(h_ref[...], dproj)

    row = lambda width: pl.BlockSpec((tm, width), lambda i: (i, 0))
    return pl.pallas_call(
        body, name="inproj_gradient", grid=(rows // tm,),
        in_specs=[row(ATTN_WIDTH), row(KV_WIDTH), row(KV_WIDTH), row(POOL_WIDTH), row(D_MODEL)],
        out_specs=pl.BlockSpec((D_MODEL, IN_WIDTH), lambda i: (0, 0)),
        out_shape=jax.ShapeDtypeStruct((D_MODEL, IN_WIDTH), F32),
        compiler_params=_cparams(("arbitrary",), 48),
    )(dq, dk, dv, du, h1)


def _inproj_backward(dq, dk, dv, du, w_in, x2d, dx2, gain, token, tm):
    rows = x2d.shape[0]

    def body(dq_ref, dk_ref, dv_ref, du_ref, w_ref, x_ref, dx2_ref, g_ref, token_ref, dx_ref, dg_ref):
        @pl.when(pl.program_id(0) == 0)
        def _():
            dg_ref[...] = jnp.zeros_like(dg_ref)

        dproj = jnp.concatenate([dq_ref[...], dk_ref[...], dv_ref[...], du_ref[...]], axis=1)
        dnorm, dg = _rms_backward(_dot_nt(dproj, w_ref[...]), x_ref[...], g_ref[...])
        dg_ref[...] += dg
        dx_ref[...] = dx2_ref[...] + dnorm

    row = lambda width: pl.BlockSpec((tm, width), lambda i: (i, 0))
    vec = pl.BlockSpec((1, D_MODEL), lambda i: (0, 0))
    return pl.pallas_call(
        body, name="inproj_backward", grid=(rows // tm,),
        in_specs=[row(ATTN_WIDTH), row(KV_WIDTH), row(KV_WIDTH), row(POOL_WIDTH), pl.BlockSpec((D_MODEL, IN_WIDTH), lambda i: (0, 0)),
                  row(D_MODEL), row(D_MODEL), vec, pl.BlockSpec((8, LANES), lambda i: (0, 0))],
        out_specs=[row(D_MODEL), vec],
        out_shape=[jax.ShapeDtypeStruct((rows, D_MODEL), F32), jax.ShapeDtypeStruct((1, D_MODEL), F32)],
        compiler_params=_cparams(("arbitrary",), 48),
    )(dq, dk, dv, du, w_in, x2d, dx2, gain, token)


def _place():
    return lax.axis_index("x"), lax.axis_index("y"), lax.axis_index("c")


def _peer(x, y, c, rel):
    return (1 - x if rel & 4 else x, 1 - y if rel & 2 else y, 1 - c if rel & 1 else c)


def _index(px, py, pc):
    return 4 * px + 2 * py + pc


def _row_slot(ref, d):
    return ref.at[d]


def _column_slot(ref, d):
    return ref.at[:, pl.ds(pl.multiple_of(d * FF_BLOCK, FF_BLOCK), FF_BLOCK)]


def _gather_weights(shards, slots, shapes, n_full):
    n = len(shards)

    def body(*refs):
        ins, outs = refs[:n], refs[n:2 * n]
        stage = refs[2 * n:3 * n]
        send_sems, recv_sems, local_sems = refs[3 * n:]
        x, y, c = _place()
        me, sibling = (x, y, c), (x, y, 1 - c)
        chips = [(1 - x, y), (x, 1 - y), (1 - x, 1 - y)]

        def copy(a, k, block, to, src=None):
            slot = slots[a](outs[a], _index(*block))
            return pltpu.make_async_remote_copy(src_ref=slot if src is None else src, dst_ref=slot, send_sem=send_sems.at[a, k],
                                                recv_sem=recv_sems.at[a, k], device_id=to, device_id_type=MESH)

        mine, first = [], []
        for a in range(n):
            stage[a][...] = ins[a][...].astype(BF16)
            mine.append(pltpu.make_async_copy(stage[a], slots[a](outs[a], _index(*me)), local_sems.at[a]))
            mine[-1].start()
            if a < n_full:
                first.append(copy(a, 0, me, sibling, src=stage[a]))
                first += [copy(a, 1 + j, me, (*chip, c), src=stage[a]) for j, chip in enumerate(chips)]
        for cp in first:
            cp.start()
        passed = []
        for j, chip in enumerate(chips):
            for a in range(n_full):
                copy(a, 1 + j, (*chip, c), me).wait_recv()
                passed.append(copy(a, 4 + j, (*chip, c), sibling))
                passed[-1].start()
        for a in range(n_full):
            copy(a, 0, sibling, me).wait_recv()
            for j, chip in enumerate(chips):
                copy(a, 4 + j, (*chip, 1 - c), me).wait_recv()
        for cp in first + passed:
            cp.wait_send()
        for cp in mine:
            cp.wait()

    return pl.pallas_call(
        body, name="gather_weights",
        in_specs=[pl.BlockSpec(memory_space=pltpu.VMEM)] * n,
        out_specs=[pl.BlockSpec(memory_space=pl.ANY)] * n,
        out_shape=[jax.ShapeDtypeStruct(shape, BF16) for shape in shapes],
        scratch_shapes=[pltpu.VMEM(s.shape, BF16) for s in shards]
        + [pltpu.SemaphoreType.DMA((n, 7)), pltpu.SemaphoreType.DMA((n, 7)), pltpu.SemaphoreType.DMA((n,))],
        compiler_params=pltpu.CompilerParams(vmem_limit_bytes=32 * MIB),
    )(*shards)


def _own_slot_to_all(slots):
    def plan(a, rel, me, to, ins, lands):
        return slots[a](lands[a], me), slots[a](lands[a], me)
    return plan


def _whole_to_all(a, rel, me, to, ins, lands):
    return ins[a], lands[a].at[rel - 1]


def _block_to_owner(a, rel, me, to, ins, lands):
    return ins[a].at[to], lands[a].at[rel - 1]


def _split_copies(plan, ins, lands, send_sems, recv_sems):
    x, y, c = _place()
    copies = []
    for rel in range(1, N_DEV):
        to = _peer(x, y, c, rel)
        for a in range(len(lands)):
            src, dst = plan(a, rel, _index(x, y, c), _index(*to), ins, lands)
            k = a * (N_DEV - 1) + rel - 1
            copies.append(pltpu.make_async_remote_copy(src_ref=src, dst_ref=dst, send_sem=send_sems.at[k],
                                                       recv_sem=recv_sems.at[k], device_id=to, device_id_type=MESH))
    return copies


HBM_SPEC = pl.BlockSpec(memory_space=pltpu.HBM)
SEM_SPEC = pl.BlockSpec(memory_space=pltpu.SEMAPHORE)
EFFECT = pltpu.SideEffectType.DATAFLOW_SIDE_EFFECTING


def _start_copies(name, plan, ins, lands, after):
    n_in, n = len(ins), len(ins) + len(lands)

    def body(*refs):
        send_sems, recv_sems = refs[n + 1], refs[n + 2]
        for cp in _split_copies(plan, refs[:n_in], refs[n_in:n], send_sems, recv_sems):
            cp.start()
        refs[-1][...] = jnp.zeros_like(refs[-1])

    arrays = [pltpu.with_memory_space_constraint(v, pltpu.HBM) for v in (*ins, *lands)]
    sems = pltpu.SemaphoreType.DMA((len(lands) * (N_DEV - 1),))
    send_sems, recv_sems, *flying, token = pl.pallas_call(
        body, name=name,
        out_shape=(sems, sems, *[pltpu.HBM(v.shape, v.dtype) for v in arrays], jax.ShapeDtypeStruct((8, LANES), F32)),
        in_specs=[HBM_SPEC] * n + [pl.BlockSpec(memory_space=pl.ANY)],
        out_specs=(SEM_SPEC, SEM_SPEC, *[HBM_SPEC] * n, pl.BlockSpec(memory_space=pltpu.VMEM)),
        input_output_aliases={i: 2 + i for i in range(n)},
        compiler_params=pltpu.CompilerParams(has_side_effects=EFFECT),
    )(*arrays, after)
    return send_sems, recv_sems, flying, token


def _wait_copies(name, plan, n_in, send_sems, recv_sems, flying, after):
    n = len(flying)

    def body(*refs):
        for cp in _split_copies(plan, refs[:n_in], refs[n_in:n], refs[n], refs[n + 1]):
            cp.wait_send()
            cp.wait_recv()

    landed = pl.pallas_call(
        body, name=name, out_shape=tuple(pltpu.HBM(v.shape, v.dtype) for v in flying),
        in_specs=[HBM_SPEC] * n + [SEM_SPEC, SEM_SPEC, pl.BlockSpec(memory_space=pl.ANY)], out_specs=tuple([HBM_SPEC] * n),
        input_output_aliases={i: i for i in range(n)},
        compiler_params=pltpu.CompilerParams(has_side_effects=EFFECT),
    )(*flying, send_sems, recv_sems, after)
    return landed[:n_in], landed[n_in:]


def _adamw_math(w, g, m, v):
    m = ADAM_B1 * m + (1.0 - ADAM_B1) * g
    v = ADAM_B2 * v + (1.0 - ADAM_B2) * (g * g)
    m_hat = m / (1.0 - ADAM_B1 ** ADAM_STEP)
    v_hat = v / (1.0 - ADAM_B2 ** ADAM_STEP)
    return -ADAM_LR * (m_hat / (jnp.sqrt(v_hat) + ADAM_EPS) + ADAM_WD * w), m, v


def _adamw_sharded(me, own, received, w, m, v, tr):
    rows, cols = w.shape

    def body(me_ref, own_ref, rec_ref, w_ref, m_ref, v_ref, g_ref, d_ref, nm_ref, nv_ref):
        g = own_ref[...]
        for r in range(N_DEV - 1):
            g = g + rec_ref[r]
        g_ref[...] = g
        d_ref[...], nm_ref[...], nv_ref[...] = _adamw_math(w_ref[...], g, m_ref[...], v_ref[...])

    tile = pl.BlockSpec((tr, cols), lambda i, me_ref: (i, 0))
    shape = jax.ShapeDtypeStruct((rows, cols), F32)
    return pl.pallas_call(
        body, name="adamw_sharded",
        grid_spec=pltpu.PrefetchScalarGridSpec(
            num_scalar_prefetch=1, grid=(rows // tr,),
            in_specs=[pl.BlockSpec((None, tr, cols), lambda i, me_ref: (me_ref[0], i, 0)),
                      pl.BlockSpec((N_DEV - 1, tr, cols), lambda i, me_ref: (0, i, 0)), tile, tile, tile],
            out_specs=[tile, tile, tile, tile]),
        out_shape=[shape, shape, shape, shape],
        compiler_params=_cparams(("parallel",), 40),
    )(me, own, received, w, m, v)


VECTOR_ROWS = D_MODEL // LANES
POOL_ROWS = len(POOL_WINDOWS) * POOL_GROUP_DIM


def _pack_small(dg1, dg2, dg3, dps, dsink, loss_cols, dwp):
    def body(g1_ref, g2_ref, g3_ref, ps_ref, sink_ref, loss_ref, wp_ref, o_ref):
        o_ref[...] = jnp.zeros_like(o_ref)
        for base, ref, n in ((ROW_G1, g1_ref, VECTOR_ROWS), (ROW_G2, g2_ref, VECTOR_ROWS), (ROW_G3, g3_ref, VECTOR_ROWS),
                             (ROW_LOSS, loss_ref, VECTOR_ROWS), (ROW_PS, ps_ref, POOL_WIDTH // LANES)):
            for r in range(n):
                o_ref[base + r:base + r + 1, :] = ref[:, r * LANES:(r + 1) * LANES]
        heads = sink_ref[:, 0, :]
        on_diagonal = lax.broadcasted_iota(jnp.int32, heads.shape, 0) == lax.broadcasted_iota(jnp.int32, heads.shape, 1)
        o_ref[ROW_SINK:ROW_SINK + 1, :] = jnp.sum(jnp.where(on_diagonal, heads, 0.0), axis=0, keepdims=True)
        o_ref[ROW_WP:ROW_WP + POOL_ROWS, :] = wp_ref[...].reshape(POOL_ROWS, LANES)

    return pl.pallas_call(body, name="pack_small", out_shape=jax.ShapeDtypeStruct((SMALL_ROWS, LANES), F32))(
        dg1, dg2, dg3, dps, dsink, loss_cols, dwp)


def _finish_small(me, own, landed, params):
    flat = [a for group in params for a in group]

    def body(me_ref, own_ref, landed_ref, *refs):
        ins, outs = refs[:len(flat)], refs[len(flat):]
        total = None
        for source in range(N_DEV):
            rel = jnp.bitwise_xor(me_ref[0], source)
            piece = jnp.where(rel == 0, own_ref[...], landed_ref[jnp.maximum(rel, 1) - 1])
            total = piece if total is None else total + piece
        outs[0][...] = (0.5 / D_MODEL) * jnp.sum(jnp.sum(total[ROW_LOSS:ROW_LOSS + VECTOR_ROWS], axis=1, keepdims=True), axis=0, keepdims=True)
        row = lambda base, n: jnp.concatenate([total[base + r:base + r + 1, :] for r in range(n)], axis=1)
        grads = [row(ROW_G1, VECTOR_ROWS), row(ROW_G2, VECTOR_ROWS), row(ROW_G3, VECTOR_ROWS), row(ROW_PS, POOL_WIDTH // LANES),
                 total[ROW_SINK:ROW_SINK + 1, :N_Q_HEADS], total[ROW_WP:ROW_WP + POOL_ROWS].reshape(params[5][0].shape)]
        for k, g in enumerate(grads):
            w_ref, m_ref, v_ref = ins[3 * k:3 * k + 3]
            g_out, d_out, m_out, v_out = outs[1 + 4 * k:5 + 4 * k]
            g_out[...] = g
            d_out[...], m_out[...], v_out[...] = _adamw_math(w_ref[...], g, m_ref[...], v_ref[...])

    shapes = [jax.ShapeDtypeStruct((1, 1), F32)] + [jax.ShapeDtypeStruct(w.shape, F32) for w, _, _ in params for _ in range(4)]
    vmem = pl.BlockSpec(memory_space=pltpu.VMEM)
    res = pl.pallas_call(body, name="finish_small", in_specs=[pl.BlockSpec(memory_space=pltpu.SMEM)] + [vmem] * (2 + len(flat)),
                         out_shape=shapes)(me, own, landed, *flat)
    return res[0], [res[1 + 4 * k:5 + 4 * k] for k in range(len(params))]


def _local_step(x, target, attn_norm_g, w_in_full, attn_sinks, w_pool, pool_scale, mlp_norm_g, final_norm_g,
                front_token, out_weight, mlp_weights, ship_down, ship_up, ship_in):
    n_seq, seq, _ = x.shape
    rows = n_seq * seq
    tm, tm_mlp, tm_grad = min(512, seq), min(256, seq), min(1024, seq)
    x2d, t2d = x.reshape(rows, D_MODEL), target.reshape(rows, D_MODEL)
    g3 = final_norm_g.reshape(1, D_MODEL)
    cos, sin = _rope_tables(seq)
    wp_b = w_pool[0].astype(BF16)

    sink_rows = jnp.broadcast_to(attn_sinks.reshape(N_Q_HEADS, 1, 1), (N_Q_HEADS, 1, LANES))
    h1, q, kd, vd, u = _norm_inproj(x2d, attn_norm_g, w_in_full, cos, sin, front_token, seq, tm)
    attn = _attention_forward(sink_rows, q, kd, vd, n_seq, seq)
    pool = _pool_forward(u, wp_b, pool_scale, n_seq, seq)
    w_out_full = out_weight(pool)
    x2, h2 = _outproj_norm(x2d, attn, pool, w_out_full, mlp_norm_g, tm)
    w_up_full, w_down_full = mlp_weights(h2)
    a, f, dx3, dx3b, loss_cols, dg3 = _mlp_forward_loss(h2, x2, w_up_full, w_down_full, g3, t2d, tm_mlp)

    down_token = ship_down(_weight_gradient("down_gradient", f, dx3b, True, tm_grad))
    da, dx2, dattn, dpool, dg2, d_w_out = _mlp_backward_data(dx3b, a, w_down_full, w_up_full, dx3, x2, mlp_norm_g, w_out_full,
                                                            attn, pool, down_token, tm_mlp)
    up_token = ship_up(d_w_out, _weight_gradient("up_gradient", h2, da, False, tm_grad))
    dq, dk, dv, dsink = _attention_backward(sink_rows, q, kd, vd, dattn, cos, sin, up_token, n_seq, seq)
    du, d_w_pool, d_pool_scale = _pool_backward(u, dpool, wp_b, pool_scale, n_seq, seq)
    in_token = ship_in(_inproj_gradient(dq, dk, dv, du, h1, tm_grad))
    grad_x, dg1 = _inproj_backward(dq, dk, dv, du, w_in_full, x2d, dx2, attn_norm_g, in_token, tm)
    return grad_x.reshape(x.shape), _pack_small(dg1, dg2, dg3, d_pool_scale, dsink, loss_cols, d_w_pool)


def kernel(x, attn_norm_g, w_in, attn_sinks, w_pool, pool_scale, w_out, mlp_norm_g, w_up, w_down, final_norm_g, loss_target, m_attn_norm_g, m_w_in, m_attn_sinks, m_w_pool, m_pool_scale, m_w_out, m_mlp_norm_g, m_w_up, m_w_down, m_final_norm_g, v_attn_norm_g, v_w_in, v_attn_sinks, v_w_pool, v_pool_scale, v_w_out, v_mlp_norm_g, v_w_up, v_w_down, v_final_norm_g):
    me = (4 * lax.axis_index("x") + 2 * lax.axis_index("y") + lax.axis_index("c")).astype(jnp.int32).reshape(1)

    win_g, wout_land, wup_land, wdown_land = _gather_weights(
        [w_in[0], w_out[0], w_up[0], w_down[0]], [_row_slot, _row_slot, _column_slot, _row_slot],
        [(N_DEV, D_MODEL, IN_BLOCK), (N_DEV, OUT_BLOCK, D_MODEL), (D_MODEL, D_FF), (N_DEV, FF_BLOCK, D_MODEL)], 1)
    w_in_full = jnp.transpose(win_g, (1, 0, 2)).reshape(D_MODEL, IN_WIDTH)
    out_plan, mlp_plan = _own_slot_to_all([_row_slot]), _own_slot_to_all([_column_slot, _row_slot])
    out_copies = _start_copies("spread_out_start", out_plan, [], [wout_land], win_g)
    mlp_copies = _start_copies("spread_mlp_start", mlp_plan, [], [wup_land, wdown_land], out_copies[3])

    def out_weight(after):
        _, (wout_g,) = _wait_copies("spread_out_wait", out_plan, 0, *out_copies[:3], after)
        return wout_g.reshape(D_MODEL, D_MODEL)

    def mlp_weights(after):
        _, (wup_g, wdown_g) = _wait_copies("spread_mlp_wait", mlp_plan, 0, *mlp_copies[:3], after)
        return wup_g, wdown_g.reshape(D_FF, D_MODEL)

    deliveries = {}
    unordered = jnp.zeros((8, LANES), F32)

    def deliver(name, plan, grads, after):
        lands = [lax.empty((N_DEV - 1,) + (g.shape[1:] if plan is _block_to_owner else g.shape), F32) for g in grads]
        deliveries[name] = _start_copies(name + "_start", plan, grads, lands, after)
        return deliveries[name][3]

    def landed(name, plan, after):
        send, recv, flying, _ = deliveries[name]
        return _wait_copies(name + "_wait", plan, len(flying) // 2, send, recv, flying, after)

    grad_x, small = _local_step(
        x, loss_target, attn_norm_g, w_in_full, attn_sinks, w_pool, pool_scale, mlp_norm_g, final_norm_g, mlp_copies[3], out_weight,
        mlp_weights, lambda d_w_down: deliver("deliver_down", _block_to_owner, [d_w_down], unordered),
        lambda d_w_out, d_w_up: deliver("deliver_up", _block_to_owner, [d_w_out.reshape(N_DEV, OUT_BLOCK, D_MODEL), d_w_up], unordered),
        lambda d_w_in: deliver("deliver_in", _block_to_owner, [jnp.transpose(d_w_in.reshape(D_MODEL, N_DEV, IN_BLOCK), (1, 0, 2))], unordered))
    small_token = deliver("deliver_small", _whole_to_all, [small], unordered)

    (own_down,), (got_down,) = landed("deliver_down", _block_to_owner, small_token)
    (own_out, own_up), (got_out, got_up) = landed("deliver_up", _block_to_owner, small_token)
    g_down = _adamw_sharded(me, own_down, got_down, w_down[0], m_w_down[0], v_w_down[0], 256)
    g_up = _adamw_sharded(me, own_up, got_up, w_up[0], m_w_up[0], v_w_up[0], 256)
    g_out = _adamw_sharded(me, own_out, got_out, w_out[0], m_w_out[0], v_w_out[0], 128)
    (own_in,), (got_in,) = landed("deliver_in", _block_to_owner, g_out[0])
    g_in = _adamw_sharded(me, own_in, got_in, w_in[0], m_w_in[0], v_w_in[0], 1024)
    (own_small,), (got_small,) = landed("deliver_small", _whole_to_all, g_in[0])

    row = lambda a: a.reshape(1, D_MODEL)
    params = [(attn_norm_g, m_attn_norm_g, v_attn_norm_g), (mlp_norm_g, m_mlp_norm_g, v_mlp_norm_g),
              (row(final_norm_g), row(m_final_norm_g), row(v_final_norm_g)), (pool_scale, m_pool_scale, v_pool_scale),
              (attn_sinks, m_attn_sinks, v_attn_sinks), (w_pool[0], m_w_pool[0], v_w_pool[0])]
    loss, (s_norm1, s_norm2, s_norm3, s_scale, s_sinks, s_pool) = _finish_small(me, own_small, got_small, params)
    s_norm3 = [a.reshape(D_MODEL) for a in s_norm3]
    s_pool = [a[None] for a in s_pool]

    def ordered(k):
        return [s_norm1[k], g_in[k][None], s_sinks[k], s_pool[k], s_scale[k], g_out[k][None], s_norm2[k], g_up[k][None], g_down[k][None],
                s_norm3[k]]

    return (loss.reshape(()), grad_x, *ordered(0), *ordered(1), *ordered(2), *ordered(3))
```

```python
import functools

import jax
import jax.numpy as jnp
from jax import lax
from jax.experimental import pallas as pl
from jax.experimental.pallas import tpu as pltpu

F32 = jnp.float32
BF16 = jnp.bfloat16

D_MODEL = 1024
HEAD_DIM = 64
N_Q_HEADS = 8
Q_PER_KV = 4
ATTN_WIDTH = 512
KV_WIDTH = 128
BLOCK = 128
ROPE_THETA = 10000.0
POOL_WINDOWS = (2, 4, 8, 16)
POOL_WIDTH = 512
POOL_GROUP_DIM = 128
IN_WIDTH = 1280
D_FF = 4096
EPS = 1e-6
N_DEV = 8
FF_BLOCK = D_FF // N_DEV
IN_BLOCK = IN_WIDTH // N_DEV
OUT_BLOCK = D_MODEL // N_DEV
ADAM_LR = 0.001
ADAM_B1 = 0.9
ADAM_B2 = 0.999
ADAM_EPS = 1e-08
ADAM_WD = 0.01
ADAM_STEP = 10
NEG = -1e30
FORWARD_CHAINS = 4
BACKWARD_CHAINS = 2
LANES = 128
MIB = 1024 * 1024
MESH = pl.DeviceIdType.MESH

ROW_G1, ROW_G2, ROW_G3, ROW_PS, ROW_SINK, ROW_LOSS, ROW_WP, SMALL_ROWS = 0, 8, 16, 24, 32, 40, 48, 560


def _cparams(semantics, vmem_mib):
    return pltpu.CompilerParams(dimension_semantics=semantics, vmem_limit_bytes=vmem_mib * MIB)


def _dot(a, b):
    return jnp.dot(a, b, preferred_element_type=F32)


def _dot_nt(a, b):
    return lax.dot_general(a, b, (((1,), (1,)), ((), ())), preferred_element_type=F32)


def _dot_tn(a, b):
    return lax.dot_general(a, b, (((0,), (0,)), ((), ())), preferred_element_type=F32)


def _swap_halves(x):
    width = x.shape[1]
    lane = lax.broadcasted_iota(jnp.int32, x.shape, 1)
    ahead = pltpu.roll(x, width - HEAD_DIM // 2, 1)
    behind = pltpu.roll(x, HEAD_DIM // 2, 1)
    return jnp.where(lane % HEAD_DIM < HEAD_DIM // 2, ahead, behind)


def _rope(x, cos, sin):
    reps = x.shape[1] // LANES
    if reps > 1:
        cos = jnp.tile(cos, (1, reps))
        sin = jnp.tile(sin, (1, reps))
    return x * cos + _swap_halves(x) * sin


def _rope_tables(seq):
    half = HEAD_DIM // 2
    inv_freq = ROPE_THETA ** (-jnp.arange(half, dtype=F32) / half)
    ang = jnp.arange(seq).astype(F32)[:, None] * inv_freq[None, :]
    cos, sin = jnp.cos(ang), jnp.sin(ang)
    cos = jnp.tile(cos, (1, LANES // half))
    sin = jnp.tile(jnp.concatenate([-sin, sin], axis=1), (1, LANES // HEAD_DIM))
    return cos, sin


def _both_halves(x):
    lane = lax.broadcasted_iota(jnp.int32, x.shape, 1)
    other = pltpu.roll(x, HEAD_DIM, 1)
    low = lane < HEAD_DIM
    return jnp.where(low, x, other), jnp.where(low, other, x)


def _rms_backward(dh, xin, gain):
    r = lax.rsqrt(jnp.mean(xin * xin, axis=-1, keepdims=True) + EPS)
    xhat = xin * r
    dxhat = dh * gain
    dx = r * (dxhat - xhat * jnp.mean(dxhat * xhat, axis=-1, keepdims=True))
    return dx, jnp.sum(dh * xhat, axis=0, keepdims=True)


def _norm_inproj(x2d, gain, w_in, cos, sin, token, seq, tm):
    rows = x2d.shape[0]
    tiles_per_seq = seq // tm

    def body(x_ref, g_ref, w_ref, cos_ref, sin_ref, token_ref, h_ref, q_ref, k_ref, v_ref, u_ref):
        x = x_ref[...]
        r = lax.rsqrt(jnp.mean(x * x, axis=-1, keepdims=True) + EPS)
        h = (x * r * g_ref[...]).astype(BF16)
        h_ref[...] = h
        proj = _dot(h, w_ref[...])
        cos_t, sin_t = cos_ref[...], sin_ref[...]
        q = _rope(proj[:, :ATTN_WIDTH], cos_t, sin_t) * (HEAD_DIM ** -0.5)
        q_ref[...] = q.astype(BF16)
        k = _rope(proj[:, ATTN_WIDTH:ATTN_WIDTH + KV_WIDTH], cos_t, sin_t)
        k0, k1 = _both_halves(k)
        k_ref[...] = jnp.concatenate([k0, k1], axis=1).astype(BF16)
        v0, v1 = _both_halves(proj[:, ATTN_WIDTH + KV_WIDTH:ATTN_WIDTH + 2 * KV_WIDTH])
        v_ref[...] = jnp.concatenate([v0, v1], axis=1).astype(BF16)
        u_ref[...] = proj[:, ATTN_WIDTH + 2 * KV_WIDTH:]

    row = lambda width: pl.BlockSpec((tm, width), lambda i: (i, 0))
    table = pl.BlockSpec((tm, LANES), lambda i: (i % tiles_per_seq, 0))
    return pl.pallas_call(
        body, name="norm_inproj", grid=(rows // tm,),
        in_specs=[row(D_MODEL), pl.BlockSpec((1, D_MODEL), lambda i: (0, 0)),
                  pl.BlockSpec((D_MODEL, IN_WIDTH), lambda i: (0, 0)), table, table, pl.BlockSpec((8, LANES), lambda i: (0, 0))],
        out_specs=[row(D_MODEL), row(ATTN_WIDTH), row(2 * KV_WIDTH), row(2 * KV_WIDTH), row(POOL_WIDTH)],
        out_shape=[jax.ShapeDtypeStruct((rows, D_MODEL), BF16), jax.ShapeDtypeStruct((rows, ATTN_WIDTH), BF16),
                   jax.ShapeDtypeStruct((rows, 2 * KV_WIDTH), BF16), jax.ShapeDtypeStruct((rows, 2 * KV_WIDTH), BF16),
                   jax.ShapeDtypeStruct((rows, POOL_WIDTH), F32)],
        compiler_params=_cparams(("parallel",), 40),
    )(x2d, gain, w_in, cos, sin, token)


def _window_masks(n):
    qi = lax.broadcasted_iota(jnp.int32, (BLOCK, BLOCK), 0)
    kj = lax.broadcasted_iota(jnp.int32, (BLOCK, BLOCK), 1)
    return kj <= qi, jnp.logical_and(kj > qi, n > 0)


def _window_operand(ref, r0, p0, kv):
    low = lax.broadcasted_iota(jnp.int32, (BLOCK, LANES), 1) < HEAD_DIM
    cur = ref[pl.ds(r0, BLOCK), kv * LANES:(kv + 1) * LANES]
    prev = ref[pl.ds(p0, BLOCK), kv * LANES:(kv + 1) * LANES]
    zero = jnp.zeros_like(cur)
    return jnp.concatenate([jnp.where(low, cur, zero), jnp.where(low, zero, cur), jnp.where(low, prev, zero), jnp.where(low, zero, prev)], axis=0)


def _merged_window(wide, parity, cur_mask, prev_mask, fill):
    cur = wide[:, parity * LANES:(parity + 1) * LANES]
    prev = wide[:, (2 + parity) * LANES:(3 + parity) * LANES]
    return jnp.where(cur_mask, cur, jnp.where(prev_mask, prev, fill))


def _lane_sums(x):
    flat = x.reshape(-1, x.shape[-1])
    high = flat.astype(BF16)
    low = (flat - high.astype(F32)).astype(BF16)
    ones = jnp.ones((x.shape[-1], LANES), BF16)
    return (_dot(high, ones) + _dot(low, ones)).reshape(x.shape[:-1] + (LANES,))


def _softmax_with_sink(scores, sink):
    m = jnp.broadcast_to(jnp.maximum(jnp.max(scores, axis=-1, keepdims=True), sink), scores.shape)
    p, ps = jnp.exp(scores - m), jnp.exp(sink - m)
    inv = 1.0 / (_lane_sums(p) + ps)
    return p * inv, ps * inv


def _attention_forward(sinks, q, kd, vd, n_seq, seq):
    n_blocks = seq // BLOCK
    n_pairs = N_Q_HEADS // 2
    chains = min(FORWARD_CHAINS, n_blocks)

    def body(sink_ref, q_ref, k_ref, v_ref, o_ref, s_ref, p_ref):
        def step(i, carry):
            starts, values = [], []
            for u in range(chains):
                n = i * chains + u
                r0 = pl.multiple_of(n * BLOCK, BLOCK)
                p0 = pl.multiple_of(jnp.maximum(n - 1, 0) * BLOCK, BLOCK)
                cur_mask, prev_mask = _window_masks(n)
                keys = [_window_operand(k_ref, r0, p0, kv) for kv in range(2)]
                starts.append(r0)
                values.append([_window_operand(v_ref, r0, p0, kv) for kv in range(2)])
                for pair in range(n_pairs):
                    wide = _dot_nt(q_ref[pl.ds(r0, BLOCK), pair * LANES:(pair + 1) * LANES], keys[pair // 2])
                    for parity in range(2):
                        s_ref[u * N_Q_HEADS + 2 * pair + parity] = _merged_window(wide, parity, cur_mask, prev_mask, NEG)
            probs, _ = _softmax_with_sink(s_ref[...], jnp.tile(sink_ref[:, :, 0:1], (chains, 1, 1)))
            probs = probs.astype(BF16)
            zero = jnp.zeros((BLOCK, BLOCK), BF16)
            for u in range(chains):
                for pair in range(n_pairs):
                    for parity in range(2):
                        ph = probs[u * N_Q_HEADS + 2 * pair + parity]
                        p_ref[u, pair, :, parity * LANES:(parity + 1) * LANES] = jnp.where(cur_mask, ph, zero)
                        p_ref[u, pair, :, (2 + parity) * LANES:(3 + parity) * LANES] = jnp.where(cur_mask, zero, ph)
            for u in range(chains):
                for pair in range(n_pairs):
                    out = _dot(p_ref[u, pair], values[u][pair // 2])
                    o_ref[pl.ds(starts[u], BLOCK), pair * LANES:(pair + 1) * LANES] = out.astype(BF16)
            return carry

        lax.fori_loop(0, n_blocks // chains, step, 0)

    seq_block = lambda width: pl.BlockSpec((seq, width), lambda b: (b, 0))
    return pl.pallas_call(
        body, name="attention_forward", grid=(n_seq,),
        in_specs=[pl.BlockSpec((N_Q_HEADS, 1, LANES), lambda b: (0, 0, 0)), seq_block(ATTN_WIDTH), seq_block(2 * KV_WIDTH),
                  seq_block(2 * KV_WIDTH)],
        out_specs=seq_block(ATTN_WIDTH),
        out_shape=jax.ShapeDtypeStruct((n_seq * seq, ATTN_WIDTH), BF16),
        scratch_shapes=[pltpu.VMEM((chains * N_Q_HEADS, BLOCK, BLOCK), F32), pltpu.VMEM((chains, n_pairs, BLOCK, 4 * LANES), BF16)],
        compiler_params=_cparams(("parallel",), 40),
    )(sinks, q, kd, vd)


def _trailing(x, window, t, seq):
    k = 1
    while k < window:
        x = x + jnp.where(t >= k, pltpu.roll(x, k, 0), 0.0)
        k *= 2
    return x


def _leading(x, window, t, seq):
    k = 1
    while k < window:
        x = x + jnp.where(t < seq - k, pltpu.roll(x, seq - k, 0), 0.0)
        k *= 2
    return x


def _pool_features(u_g, window, t, seq):
    count = jnp.minimum(t + 1, window).astype(F32)
    return (_trailing(u_g, window, t, seq) / count - u_g).astype(BF16), count


def _pool_forward(u, w_pool, pool_scale, n_seq, seq):
    def body(u_ref, w_ref, s_ref, o_ref):
        t = lax.broadcasted_iota(jnp.int32, (seq, 1), 0)
        for g, window in enumerate(POOL_WINDOWS):
            cols = slice(g * POOL_GROUP_DIM, (g + 1) * POOL_GROUP_DIM)
            d, _ = _pool_features(u_ref[:, cols], window, t, seq)
            o_ref[:, cols] = (_dot(d, w_ref[g]) * s_ref[:, cols]).astype(BF16)

    seq_block = pl.BlockSpec((seq, POOL_WIDTH), lambda b: (b, 0))
    return pl.pallas_call(
        body, name="pool_forward", grid=(n_seq,),
        in_specs=[seq_block, pl.BlockSpec((len(POOL_WINDOWS), POOL_GROUP_DIM, POOL_GROUP_DIM), lambda b: (0, 0, 0)),
                  pl.BlockSpec((1, POOL_WIDTH), lambda b: (0, 0))],
        out_specs=seq_block,
        out_shape=jax.ShapeDtypeStruct((n_seq * seq, POOL_WIDTH), BF16),
        compiler_params=_cparams(("parallel",), 40),
    )(u, w_pool, pool_scale)


def _outproj_norm(x2d, attn, pool, w_out, gain, tm):
    rows = x2d.shape[0]

    def body(x_ref, a_ref, p_ref, w_ref, g_ref, x2_ref, h_ref):
        x2 = x_ref[...] + _dot(a_ref[...], w_ref[:ATTN_WIDTH, :]) + _dot(p_ref[...], w_ref[ATTN_WIDTH:, :])
        x2_ref[...] = x2
        r = lax.rsqrt(jnp.mean(x2 * x2, axis=-1, keepdims=True) + EPS)
        h_ref[...] = (x2 * r * g_ref[...]).astype(BF16)

    row = lambda width: pl.BlockSpec((tm, width), lambda i: (i, 0))
    return pl.pallas_call(
        body, name="outproj_norm", grid=(rows // tm,),
        in_specs=[row(D_MODEL), row(ATTN_WIDTH), row(POOL_WIDTH), pl.BlockSpec((D_MODEL, D_MODEL), lambda i: (0, 0)),
                  pl.BlockSpec((1, D_MODEL), lambda i: (0, 0))],
        out_specs=[row(D_MODEL), row(D_MODEL)],
        out_shape=[jax.ShapeDtypeStruct((rows, D_MODEL), F32), jax.ShapeDtypeStruct((rows, D_MODEL), BF16)],
        compiler_params=_cparams(("parallel",), 40),
    )(x2d, attn, pool, w_out, gain)


def _resident(shape):
    return pl.BlockSpec(shape, lambda i: (0,) * len(shape), pipeline_mode=pl.Buffered(1))


def _mlp_forward_loss(h2, x2, w_up, w_down, gain, target, tm):
    rows = h2.shape[0]
    chunk = D_MODEL

    def body(h_ref, x_ref, up_ref, down_ref, g_ref, t_ref, a_ref, f_ref, dx_ref, dxb_ref, loss_ref, dg_ref):
        @pl.when(pl.program_id(0) == 0)
        def _():
            loss_ref[...] = jnp.zeros_like(loss_ref)
            dg_ref[...] = jnp.zeros_like(dg_ref)

        h = h_ref[...]
        for c in range(D_FF // chunk):
            cols = slice(c * chunk, (c + 1) * chunk)
            a = _dot(h, up_ref[:, cols])
            a_ref[:, cols] = a
            r = jnp.maximum(a, 0.0)
            f_ref[:, cols] = (r * r).astype(BF16)
        x3 = x_ref[...] + _dot(f_ref[...], down_ref[...])
        rn = lax.rsqrt(jnp.mean(x3 * x3, axis=-1, keepdims=True) + EPS)
        err = x3 * rn * g_ref[...] - t_ref[...]
        loss_ref[...] += jnp.sum(err * err, axis=0, keepdims=True)
        dx, dg = _rms_backward(err / D_MODEL, x3, g_ref[...])
        dg_ref[...] += dg
        dx_ref[...] = dx
        dxb_ref[...] = dx.astype(BF16)

    row = lambda width: pl.BlockSpec((tm, width), lambda i: (i, 0))
    vec = pl.BlockSpec((1, D_MODEL), lambda i: (0, 0))
    return pl.pallas_call(
        body, name="mlp_forward_loss", grid=(rows // tm,),
        in_specs=[row(D_MODEL), row(D_MODEL), _resident((D_MODEL, D_FF)), _resident((D_FF, D_MODEL)), vec, row(D_MODEL)],
        out_specs=[row(D_FF), row(D_FF), row(D_MODEL), row(D_MODEL), vec, vec],
        out_shape=[jax.ShapeDtypeStruct((rows, D_FF), F32), jax.ShapeDtypeStruct((rows, D_FF), BF16),
                   jax.ShapeDtypeStruct((rows, D_MODEL), F32), jax.ShapeDtypeStruct((rows, D_MODEL), BF16),
                   jax.ShapeDtypeStruct((1, D_MODEL), F32), jax.ShapeDtypeStruct((1, D_MODEL), F32)],
        compiler_params=_cparams(("arbitrary",), 56),
    )(h2, x2, w_up, w_down, gain, target)


def _mlp_backward_data(dx3b, a, w_down, w_up, dx3, x2, gain, w_out, attn, pool, token, tm):
    rows = dx3b.shape[0]
    steps = rows // tm
    chunk = D_MODEL

    def body(dxb_ref, a_ref, down_ref, up_ref, dx3_ref, x2_ref, g_ref, wo_ref, attn_ref, pool_ref, token_ref,
             da_ref, dx2_ref, dattn_ref, dpool_ref, dg_ref, dwo_hbm, dwo_acc, sem):
        @pl.when(pl.program_id(0) == 0)
        def _():
            dg_ref[...] = jnp.zeros_like(dg_ref)
            dwo_acc[...] = jnp.zeros_like(dwo_acc)

        dxb = dxb_ref[...]
        for c in range(D_FF // chunk):
            cols = slice(c * chunk, (c + 1) * chunk)
            r = jnp.maximum(a_ref[:, cols], 0.0)
            da_ref[:, cols] = (_dot_nt(dxb, down_ref[cols, :]) * (2.0 * r)).astype(BF16)
        dnorm, dg = _rms_backward(_dot_nt(da_ref[...], up_ref[...]), x2_ref[...], g_ref[...])
        dg_ref[...] += dg
        dx2 = dx3_ref[...] + dnorm
        dx2_ref[...] = dx2
        dx2b = dx2.astype(BF16)
        dmix = _dot_nt(dx2b, wo_ref[...])
        dattn_ref[...] = dmix[:, :ATTN_WIDTH].astype(BF16)
        dpool_ref[...] = dmix[:, ATTN_WIDTH:]
        dwo_acc[:ATTN_WIDTH, :] += _dot_tn(attn_ref[...], dx2b)
        dwo_acc[ATTN_WIDTH:, :] += _dot_tn(pool_ref[...], dx2b)

        @pl.when(pl.program_id(0) == steps - 1)
        def _():
            out = pltpu.make_async_copy(dwo_acc, dwo_hbm, sem)
            out.start()
            out.wait()

    row = lambda width: pl.BlockSpec((tm, width), lambda i: (i, 0))
    vec = pl.BlockSpec((1, D_MODEL), lambda i: (0, 0))
    return pl.pallas_call(
        body, name="mlp_backward_data", grid=(steps,),
        in_specs=[row(D_MODEL), row(D_FF), _resident((D_FF, D_MODEL)), _resident((D_MODEL, D_FF)), row(D_MODEL), row(D_MODEL), vec,
                  _resident((D_MODEL, D_MODEL)), row(ATTN_WIDTH), row(POOL_WIDTH), pl.BlockSpec((8, LANES), lambda i: (0, 0))],
        out_specs=[row(D_FF), row(D_MODEL), row(ATTN_WIDTH), row(POOL_WIDTH), vec, pl.BlockSpec(memory_space=pl.ANY)],
        out_shape=[jax.ShapeDtypeStruct((rows, D_FF), BF16), jax.ShapeDtypeStruct((rows, D_MODEL), F32),
                   jax.ShapeDtypeStruct((rows, ATTN_WIDTH), BF16), jax.ShapeDtypeStruct((rows, POOL_WIDTH), F32),
                   jax.ShapeDtypeStruct((1, D_MODEL), F32), jax.ShapeDtypeStruct((D_MODEL, D_MODEL), F32)],
        scratch_shapes=[pltpu.VMEM((D_MODEL, D_MODEL), F32), pltpu.SemaphoreType.DMA],
        compiler_params=_cparams(("arbitrary",), 56),
    )(dx3b, a, w_down, w_up, dx3, x2, gain, w_out, attn, pool, token)


def _weight_gradient(name, lhs, rhs, block_lhs, tm):
    rows = lhs.shape[0]
    steps = rows // tm
    out = (N_DEV, FF_BLOCK, rhs.shape[1]) if block_lhs else (N_DEV, lhs.shape[1], FF_BLOCK)

    def body(l_ref, r_ref, o_hbm, acc, sem):
        @pl.when(pl.program_id(0) == 0)
        def _():
            acc[...] = jnp.zeros_like(acc)

        for d in range(N_DEV):
            cols = slice(d * FF_BLOCK, (d + 1) * FF_BLOCK)
            acc[d] += _dot_tn(l_ref[:, cols], r_ref[...]) if block_lhs else _dot_tn(l_ref[...], r_ref[:, cols])

        @pl.when(pl.program_id(0) == steps - 1)
        def _():
            done = pltpu.make_async_copy(acc, o_hbm, sem)
            done.start()
            done.wait()

    return pl.pallas_call(
        body, name=name, grid=(steps,),
        in_specs=[pl.BlockSpec((tm, lhs.shape[1]), lambda i: (i, 0)), pl.BlockSpec((tm, rhs.shape[1]), lambda i: (i, 0))],
        out_specs=pl.BlockSpec(memory_space=pl.ANY),
        out_shape=jax.ShapeDtypeStruct(out, F32),
        scratch_shapes=[pltpu.VMEM(out, F32), pltpu.SemaphoreType.DMA],
        compiler_params=_cparams(("arbitrary",), 56),
    )(lhs, rhs)


def _attention_backward(sinks, q, kd, vd, dout, cos, sin, token, n_seq, seq):
    n_blocks = seq // BLOCK
    n_pairs = N_Q_HEADS // 2
    n_kv = N_Q_HEADS // Q_PER_KV
    chains = min(BACKWARD_CHAINS, n_blocks)

    def body(sink_ref, q_ref, k_ref, v_ref, do_ref, cos_ref, sin_ref, token_ref, dq_ref, dk_ref, dv_ref, dsink_ref,
             s_all, dp_all, dsc_all, dsp_all, pc_all, pp_all, dk_acc, dv_acc):
        low = lax.broadcasted_iota(jnp.int32, (BLOCK, LANES), 1) < HEAD_DIM

        @pl.when(pl.program_id(0) == 0)
        def _():
            dsink_ref[...] = jnp.zeros_like(dsink_ref)

        def fold(x):
            return x + pltpu.roll(x, HEAD_DIM, 1)

        def onto_keys(ref, kv, other):
            even, odd = _dot_tn(ref[2 * kv], other), _dot_tn(ref[2 * kv + 1], other)
            return fold(jnp.where(low, even, odd))

        def step(i, dsink):
            blocks = []
            for u in range(chains):
                n = i * chains + u
                r0 = pl.multiple_of(n * BLOCK, BLOCK)
                p0 = pl.multiple_of(jnp.maximum(n - 1, 0) * BLOCK, BLOCK)
                cur_mask, prev_mask = _window_masks(n)
                keys = [_window_operand(k_ref, r0, p0, kv) for kv in range(n_kv)]
                values = [_window_operand(v_ref, r0, p0, kv) for kv in range(n_kv)]
                q_pairs = [q_ref[pl.ds(r0, BLOCK), pair * LANES:(pair + 1) * LANES] for pair in range(n_pairs)]
                do_pairs = [do_ref[pl.ds(r0, BLOCK), pair * LANES:(pair + 1) * LANES] for pair in range(n_pairs)]
                for pair in range(n_pairs):
                    wide_s = _dot_nt(q_pairs[pair], keys[pair // 2])
                    wide_dp = _dot_nt(do_pairs[pair], values[pair // 2])
                    for parity in range(2):
                        s_all[u * N_Q_HEADS + 2 * pair + parity] = _merged_window(wide_s, parity, cur_mask, prev_mask, NEG)
                        dp_all[u * N_Q_HEADS + 2 * pair + parity] = _merged_window(wide_dp, parity, cur_mask, prev_mask, 0.0)
                blocks.append((n, r0, p0, keys, q_pairs, do_pairs))

            probs, p_sink = _softmax_with_sink(s_all[...], jnp.tile(sink_ref[:, :, 0:1], (chains, 1, 1)))
            dprobs = dp_all[...]
            delta = _lane_sums(probs * dprobs)
            dscores = (probs * (dprobs - delta)).astype(BF16)
            sink_terms = jnp.sum((p_sink * delta)[:, :, 0:1], axis=1, keepdims=True)
            probs = probs.astype(BF16)
            zero = jnp.zeros((BLOCK, BLOCK), BF16)
            for u in range(chains):
                dsink = dsink - sink_terms[u * N_Q_HEADS:(u + 1) * N_Q_HEADS]
                for head in range(N_Q_HEADS):
                    group, rows = 2 * (head // Q_PER_KV) + head % 2, pl.ds(((head % Q_PER_KV) // 2) * BLOCK, BLOCK)
                    ds_h, p_h = dscores[u * N_Q_HEADS + head], probs[u * N_Q_HEADS + head]
                    dsc_all[u, group, rows, :] = jnp.where(cur_mask, ds_h, zero)
                    dsp_all[u, group, rows, :] = jnp.where(cur_mask, zero, ds_h)
                    pc_all[u, group, rows, :] = jnp.where(cur_mask, p_h, zero)
                    pp_all[u, group, rows, :] = jnp.where(cur_mask, zero, p_h)

            for u, (n, r0, p0, keys, q_pairs, do_pairs) in enumerate(blocks):
                dsc_ref, dsp_ref, pc_ref, pp_ref = dsc_all.at[u], dsp_all.at[u], pc_all.at[u], pp_all.at[u]
                for pair in range(n_pairs):
                    kv, rows = pair // 2, pl.ds((pair % 2) * BLOCK, BLOCK)
                    wide = jnp.concatenate([dsc_ref[2 * kv, rows, :], dsc_ref[2 * kv + 1, rows, :],
                                            dsp_ref[2 * kv, rows, :], dsp_ref[2 * kv + 1, rows, :]], axis=1)
                    dq = _dot(wide, keys[kv]) * (HEAD_DIM ** -0.5)
                    dq = _rope(dq, cos_ref[pl.ds(r0, BLOCK), :], -sin_ref[pl.ds(r0, BLOCK), :])
                    dq_ref[pl.ds(r0, BLOCK), pair * LANES:(pair + 1) * LANES] = dq.astype(BF16)

                parts = []
                for kv in range(n_kv):
                    q_rows = jnp.concatenate([q_pairs[2 * kv], q_pairs[2 * kv + 1]], axis=0)
                    do_rows = jnp.concatenate([do_pairs[2 * kv], do_pairs[2 * kv + 1]], axis=0)
                    parts.append((onto_keys(dsc_ref, kv, q_rows), onto_keys(dsp_ref, kv, q_rows),
                                  onto_keys(pc_ref, kv, do_rows), onto_keys(pp_ref, kv, do_rows)))
                dk_acc[pl.ds(r0, BLOCK), :] = jnp.where(low, parts[0][0], parts[1][0])
                dv_acc[pl.ds(r0, BLOCK), :] = jnp.where(low, parts[0][2], parts[1][2])

                @pl.when(n > 0)
                def _():
                    dk_acc[pl.ds(p0, BLOCK), :] += jnp.where(low, parts[0][1], parts[1][1])
                    dv_acc[pl.ds(p0, BLOCK), :] += jnp.where(low, parts[0][3], parts[1][3])

            return dsink

        dsink = lax.fori_loop(0, n_blocks // chains, step, jnp.zeros((N_Q_HEADS, 1, 1), F32))
        dsink_ref[...] += jnp.broadcast_to(dsink, dsink_ref.shape)
        dk_ref[...] = _rope(dk_acc[...], cos_ref[...], -sin_ref[...]).astype(BF16)
        dv_ref[...] = dv_acc[...].astype(BF16)

    seq_block = lambda width: pl.BlockSpec((seq, width), lambda b: (b, 0))
    table = pl.BlockSpec((seq, LANES), lambda b: (0, 0))
    per_head = pl.BlockSpec((N_Q_HEADS, 1, LANES), lambda b: (0, 0, 0))
    per_block = pltpu.VMEM((chains * N_Q_HEADS, BLOCK, BLOCK), F32)
    grouped = pltpu.VMEM((chains, 2 * n_kv, 2 * BLOCK, BLOCK), BF16)
    return pl.pallas_call(
        body, name="attention_backward", grid=(n_seq,),
        in_specs=[per_head, seq_block(ATTN_WIDTH), seq_block(2 * KV_WIDTH), seq_block(2 * KV_WIDTH),
                  seq_block(ATTN_WIDTH), table, table, pl.BlockSpec((8, LANES), lambda b: (0, 0))],
        out_specs=[seq_block(ATTN_WIDTH), seq_block(KV_WIDTH), seq_block(KV_WIDTH), per_head],
        out_shape=[jax.ShapeDtypeStruct((n_seq * seq, ATTN_WIDTH), BF16), jax.ShapeDtypeStruct((n_seq * seq, KV_WIDTH), BF16),
                   jax.ShapeDtypeStruct((n_seq * seq, KV_WIDTH), BF16), jax.ShapeDtypeStruct((N_Q_HEADS, 1, LANES), F32)],
        scratch_shapes=[per_block, per_block, grouped, grouped, grouped, grouped, pltpu.VMEM((seq, KV_WIDTH), F32), pltpu.VMEM((seq, KV_WIDTH), F32)],
        compiler_params=_cparams(("arbitrary",), 40),
    )(sinks, q, kd, vd, dout, cos, sin, token)


def _pool_backward(u, dpool, w_pool, pool_scale, n_seq, seq):
    groups = len(POOL_WINDOWS)

    def body(u_ref, dp_ref, w_ref, s_ref, du_ref, dw_ref, ds_ref):
        @pl.when(pl.program_id(0) == 0)
        def _():
            dw_ref[...] = jnp.zeros_like(dw_ref)
            ds_ref[...] = jnp.zeros_like(ds_ref)

        t = lax.broadcasted_iota(jnp.int32, (seq, 1), 0)
        for g, window in enumerate(POOL_WINDOWS):
            cols = slice(g * POOL_GROUP_DIM, (g + 1) * POOL_GROUP_DIM)
            d, count = _pool_features(u_ref[:, cols], window, t, seq)
            dpool_g = dp_ref[:, cols]
            ds_ref[:, cols] += jnp.sum(dpool_g * _dot(d, w_ref[g]), axis=0, keepdims=True)
            dy = (dpool_g * s_ref[:, cols]).astype(BF16)
            dw_ref[g] += _dot_tn(d, dy)
            dd = _dot_nt(dy, w_ref[g])
            du_ref[:, cols] = (_leading(dd / count, window, t, seq) - dd).astype(BF16)

    seq_block = pl.BlockSpec((seq, POOL_WIDTH), lambda b: (b, 0))
    weights = pl.BlockSpec((groups, POOL_GROUP_DIM, POOL_GROUP_DIM), lambda b: (0, 0, 0))
    scale = pl.BlockSpec((1, POOL_WIDTH), lambda b: (0, 0))
    return pl.pallas_call(
        body, name="pool_backward", grid=(n_seq,),
        in_specs=[seq_block, seq_block, weights, scale],
        out_specs=[seq_block, weights, scale],
        out_shape=[jax.ShapeDtypeStruct((n_seq * seq, POOL_WIDTH), BF16),
                   jax.ShapeDtypeStruct((groups, POOL_GROUP_DIM, POOL_GROUP_DIM), F32), jax.ShapeDtypeStruct((1, POOL_WIDTH), F32)],
        compiler_params=_cparams(("arbitrary",), 40),
    )(u, dpool, w_pool, pool_scale)


def _inproj_gradient(dq, dk, dv, du, h1, tm):
    rows = h1.shape[0]

    def body(dq_ref, dk_ref, dv_ref, du_ref, h_ref, dw_ref):
        @pl.when(pl.program_id(0) == 0)
        def _():
            dw_ref[...] = jnp.zeros_like(dw_ref)

        dproj = jnp.concatenate([dq_ref[...], dk_ref[...], dv_ref[...], du_ref[...]], axis=1)
        dw_ref[...] += _dot_tn(h_ref[...], dproj)

    row = lambda width: pl.BlockSpec((tm, width), lambda i: (i, 0))
    return pl.pallas_call(
        body, name="inproj_gradient", grid=(rows // tm,),
        in_specs=[row(ATTN_WIDTH), row(KV_WIDTH), row(KV_WIDTH), row(POOL_WIDTH), row(D_MODEL)],
        out_specs=pl.BlockSpec((D_MODEL, IN_WIDTH), lambda i: (0, 0)),
        out_shape=jax.ShapeDtypeStruct((D_MODEL, IN_WIDTH), F32),
        compiler_params=_cparams(("arbitrary",), 48),
    )(dq, dk, dv, du, h1)


def _inproj_backward(dq, dk, dv, du, w_in, x2d, dx2, gain, token, tm):
    rows = x2d.shape[0]

    def body(dq_ref, dk_ref, dv_ref, du_ref, w_ref, x_ref, dx2_ref, g_ref, token_ref, dx_ref, dg_ref):
        @pl.when(pl.program_id(0) == 0)
        def _():
            dg_ref[...] = jnp.zeros_like(dg_ref)

        dproj = jnp.concatenate([dq_ref[...], dk_ref[...], dv_ref[...], du_ref[...]], axis=1)
        dnorm, dg = _rms_backward(_dot_nt(dproj, w_ref[...]), x_ref[...], g_ref[...])
        dg_ref[...] += dg
        dx_ref[...] = dx2_ref[...] + dnorm

    row = lambda width: pl.BlockSpec((tm, width), lambda i: (i, 0))
    vec = pl.BlockSpec((1, D_MODEL), lambda i: (0, 0))
    return pl.pallas_call(
        body, name="inproj_backward", grid=(rows // tm,),
        in_specs=[row(ATTN_WIDTH), row(KV_WIDTH), row(KV_WIDTH), row(POOL_WIDTH), pl.BlockSpec((D_MODEL, IN_WIDTH), lambda i: (0, 0)),
                  row(D_MODEL), row(D_MODEL), vec, pl.BlockSpec((8, LANES), lambda i: (0, 0))],
        out_specs=[row(D_MODEL), vec],
        out_shape=[jax.ShapeDtypeStruct((rows, D_MODEL), F32), jax.ShapeDtypeStruct((1, D_MODEL), F32)],
        compiler_params=_cparams(("arbitrary",), 48),
    )(dq, dk, dv, du, w_in, x2d, dx2, gain, token)


def _place():
    return lax.axis_index("x"), lax.axis_index("y"), lax.axis_index("c")


def _peer(x, y, c, rel):
    return (1 - x if rel & 4 else x, 1 - y if rel & 2 else y, 1 - c if rel & 1 else c)


def _index(px, py, pc):
    return 4 * px + 2 * py + pc


def _row_slot(ref, d):
    return ref.at[d]


def _column_slot(ref, d):
    return ref.at[:, pl.ds(pl.multiple_of(d * FF_BLOCK, FF_BLOCK), FF_BLOCK)]


def _gather_weights(shards, slots, shapes, n_full):
    n = len(shards)

    def body(*refs):
        ins, outs = refs[:n], refs[n:2 * n]
        stage = refs[2 * n:3 * n]
        send_sems, recv_sems, local_sems = refs[3 * n:]
        x, y, c = _place()
        me, sibling = (x, y, c), (x, y, 1 - c)
        chips = [(1 - x, y), (x, 1 - y), (1 - x, 1 - y)]

        def copy(a, k, block, to, src=None):
            slot = slots[a](outs[a], _index(*block))
            return pltpu.make_async_remote_copy(src_ref=slot if src is None else src, dst_ref=slot, send_sem=send_sems.at[a, k],
                                                recv_sem=recv_sems.at[a, k], device_id=to, device_id_type=MESH)

        mine, first = [], []
        for a in range(n):
            stage[a][...] = ins[a][...].astype(BF16)
            mine.append(pltpu.make_async_copy(stage[a], slots[a](outs[a], _index(*me)), local_sems.at[a]))
            mine[-1].start()
            if a < n_full:
                first.append(copy(a, 0, me, sibling, src=stage[a]))
                first += [copy(a, 1 + j, me, (*chip, c), src=stage[a]) for j, chip in enumerate(chips)]
        for cp in first:
            cp.start()
        passed = []
        for j, chip in enumerate(chips):
            for a in range(n_full):
                copy(a, 1 + j, (*chip, c), me).wait_recv()
                passed.append(copy(a, 4 + j, (*chip, c), sibling))
                passed[-1].start()
        for a in range(n_full):
            copy(a, 0, sibling, me).wait_recv()
            for j, chip in enumerate(chips):
                copy(a, 4 + j, (*chip, 1 - c), me).wait_recv()
        for cp in first + passed:
            cp.wait_send()
        for cp in mine:
            cp.wait()

    return pl.pallas_call(
        body, name="gather_weights",
        in_specs=[pl.BlockSpec(memory_space=pltpu.VMEM)] * n,
        out_specs=[pl.BlockSpec(memory_space=pl.ANY)] * n,
        out_shape=[jax.ShapeDtypeStruct(shape, BF16) for shape in shapes],
        scratch_shapes=[pltpu.VMEM(s.shape, BF16) for s in shards]
        + [pltpu.SemaphoreType.DMA((n, 7)), pltpu.SemaphoreType.DMA((n, 7)), pltpu.SemaphoreType.DMA((n,))],
        compiler_params=pltpu.CompilerParams(vmem_limit_bytes=32 * MIB),
    )(*shards)


def _own_slot_to_all(slots):
    def plan(a, rel, me, to, ins, lands):
        return slots[a](lands[a], me), slots[a](lands[a], me)
    return plan


def _whole_to_all(a, rel, me, to, ins, lands):
    return ins[a], lands[a].at[rel - 1]


def _block_to_owner(a, rel, me, to, ins, lands):
    return ins[a].at[to], lands[a].at[rel - 1]


def _split_copies(plan, ins, lands, send_sems, recv_sems):
    x, y, c = _place()
    copies = []
    for rel in range(1, N_DEV):
        to = _peer(x, y, c, rel)
        for a in range(len(lands)):
            src, dst = plan(a, rel, _index(x, y, c), _index(*to), ins, lands)
            k = a * (N_DEV - 1) + rel - 1
            copies.append(pltpu.make_async_remote_copy(src_ref=src, dst_ref=dst, send_sem=send_sems.at[k],
                                                       recv_sem=recv_sems.at[k], device_id=to, device_id_type=MESH))
    return copies


HBM_SPEC = pl.BlockSpec(memory_space=pltpu.HBM)
SEM_SPEC = pl.BlockSpec(memory_space=pltpu.SEMAPHORE)
EFFECT = pltpu.SideEffectType.DATAFLOW_SIDE_EFFECTING


def _start_copies(name, plan, ins, lands, after):
    n_in, n = len(ins), len(ins) + len(lands)

    def body(*refs):
        send_sems, recv_sems = refs[n + 1], refs[n + 2]
        for cp in _split_copies(plan, refs[:n_in], refs[n_in:n], send_sems, recv_sems):
            cp.start()
        refs[-1][...] = jnp.zeros_like(refs[-1])

    arrays = [pltpu.with_memory_space_constraint(v, pltpu.HBM) for v in (*ins, *lands)]
    sems = pltpu.SemaphoreType.DMA((len(lands) * (N_DEV - 1),))
    send_sems, recv_sems, *flying, token = pl.pallas_call(
        body, name=name,
        out_shape=(sems, sems, *[pltpu.HBM(v.shape, v.dtype) for v in arrays], jax.ShapeDtypeStruct((8, LANES), F32)),
        in_specs=[HBM_SPEC] * n + [pl.BlockSpec(memory_space=pl.ANY)],
        out_specs=(SEM_SPEC, SEM_SPEC, *[HBM_SPEC] * n, pl.BlockSpec(memory_space=pltpu.VMEM)),
        input_output_aliases={i: 2 + i for i in range(n)},
        compiler_params=pltpu.CompilerParams(has_side_effects=EFFECT),
    )(*arrays, after)
    return send_sems, recv_sems, flying, token


def _wait_copies(name, plan, n_in, send_sems, recv_sems, flying, after):
    n = len(flying)

    def body(*refs):
        for cp in _split_copies(plan, refs[:n_in], refs[n_in:n], refs[n], refs[n + 1]):
            cp.wait_send()
            cp.wait_recv()

    landed = pl.pallas_call(
        body, name=name, out_shape=tuple(pltpu.HBM(v.shape, v.dtype) for v in flying),
        in_specs=[HBM_SPEC] * n + [SEM_SPEC, SEM_SPEC, pl.BlockSpec(memory_space=pl.ANY)], out_specs=tuple([HBM_SPEC] * n),
        input_output_aliases={i: i for i in range(n)},
        compiler_params=pltpu.CompilerParams(has_side_effects=EFFECT),
    )(*flying, send_sems, recv_sems, after)
    return landed[:n_in], landed[n_in:]


def _adamw_math(w, g, m, v):
    m = ADAM_B1 * m + (1.0 - ADAM_B1) * g
    v = ADAM_B2 * v + (1.0 - ADAM_B2) * (g * g)
    m_hat = m / (1.0 - ADAM_B1 ** ADAM_STEP)
    v_hat = v / (1.0 - ADAM_B2 ** ADAM_STEP)
    return -ADAM_LR * (m_hat / (jnp.sqrt(v_hat) + ADAM_EPS) + ADAM_WD * w), m, v


def _adamw_sharded(me, own, received, w, m, v, tr):
    rows, cols = w.shape

    def body(me_ref, own_ref, rec_ref, w_ref, m_ref, v_ref, g_ref, d_ref, nm_ref, nv_ref):
        g = own_ref[...]
        for r in range(N_DEV - 1):
            g = g + rec_ref[r]
        g_ref[...] = g
        d_ref[...], nm_ref[...], nv_ref[...] = _adamw_math(w_ref[...], g, m_ref[...], v_ref[...])

    tile = pl.BlockSpec((tr, cols), lambda i, me_ref: (i, 0))
    shape = jax.ShapeDtypeStruct((rows, cols), F32)
    return pl.pallas_call(
        body, name="adamw_sharded",
        grid_spec=pltpu.PrefetchScalarGridSpec(
            num_scalar_prefetch=1, grid=(rows // tr,),
            in_specs=[pl.BlockSpec((None, tr, cols), lambda i, me_ref: (me_ref[0], i, 0)),
                      pl.BlockSpec((N_DEV - 1, tr, cols), lambda i, me_ref: (0, i, 0)), tile, tile, tile],
            out_specs=[tile, tile, tile, tile]),
        out_shape=[shape, shape, shape, shape],
        compiler_params=_cparams(("parallel",), 40),
    )(me, own, received, w, m, v)


VECTOR_ROWS = D_MODEL // LANES
POOL_ROWS = len(POOL_WINDOWS) * POOL_GROUP_DIM


def _pack_small(dg1, dg2, dg3, dps, dsink, loss_cols, dwp):
    def body(g1_ref, g2_ref, g3_ref, ps_ref, sink_ref, loss_ref, wp_ref, o_ref):
        o_ref[...] = jnp.zeros_like(o_ref)
        for base, ref, n in ((ROW_G1, g1_ref, VECTOR_ROWS), (ROW_G2, g2_ref, VECTOR_ROWS), (ROW_G3, g3_ref, VECTOR_ROWS),
                             (ROW_LOSS, loss_ref, VECTOR_ROWS), (ROW_PS, ps_ref, POOL_WIDTH // LANES)):
            for r in range(n):
                o_ref[base + r:base + r + 1, :] = ref[:, r * LANES:(r + 1) * LANES]
        heads = sink_ref[:, 0, :]
        on_diagonal = lax.broadcasted_iota(jnp.int32, heads.shape, 0) == lax.broadcasted_iota(jnp.int32, heads.shape, 1)
        o_ref[ROW_SINK:ROW_SINK + 1, :] = jnp.sum(jnp.where(on_diagonal, heads, 0.0), axis=0, keepdims=True)
        o_ref[ROW_WP:ROW_WP + POOL_ROWS, :] = wp_ref[...].reshape(POOL_ROWS, LANES)

    return pl.pallas_call(body, name="pack_small", out_shape=jax.ShapeDtypeStruct((SMALL_ROWS, LANES), F32))(
        dg1, dg2, dg3, dps, dsink, loss_cols, dwp)


def _finish_small(me, own, landed, params):
    flat = [a for group in params for a in group]

    def body(me_ref, own_ref, landed_ref, *refs):
        ins, outs = refs[:len(flat)], refs[len(flat):]
        total = None
        for source in range(N_DEV):
            rel = jnp.bitwise_xor(me_ref[0], source)
            piece = jnp.where(rel == 0, own_ref[...], landed_ref[jnp.maximum(rel, 1) - 1])
            total = piece if total is None else total + piece
        outs[0][...] = (0.5 / D_MODEL) * jnp.sum(jnp.sum(total[ROW_LOSS:ROW_LOSS + VECTOR_ROWS], axis=1, keepdims=True), axis=0, keepdims=True)
        row = lambda base, n: jnp.concatenate([total[base + r:base + r + 1, :] for r in range(n)], axis=1)
        grads = [row(ROW_G1, VECTOR_ROWS), row(ROW_G2, VECTOR_ROWS), row(ROW_G3, VECTOR_ROWS), row(ROW_PS, POOL_WIDTH // LANES),
                 total[ROW_SINK:ROW_SINK + 1, :N_Q_HEADS], total[ROW_WP:ROW_WP + POOL_ROWS].reshape(params[5][0].shape)]
        for k, g in enumerate(grads):
            w_ref, m_ref, v_ref = ins[3 * k:3 * k + 3]
            g_out, d_out, m_out, v_out = outs[1 + 4 * k:5 + 4 * k]
            g_out[...] = g
            d_out[...], m_out[...], v_out[...] = _adamw_math(w_ref[...], g, m_ref[...], v_ref[...])

    shapes = [jax.ShapeDtypeStruct((1, 1), F32)] + [jax.ShapeDtypeStruct(w.shape, F32) for w, _, _ in params for _ in range(4)]
    vmem = pl.BlockSpec(memory_space=pltpu.VMEM)
    res = pl.pallas_call(body, name="finish_small", in_specs=[pl.BlockSpec(memory_space=pltpu.SMEM)] + [vmem] * (2 + len(flat)),
                         out_shape=shapes)(me, own, landed, *flat)
    return res[0], [res[1 + 4 * k:5 + 4 * k] for k in range(len(params))]


def _local_step(x, target, attn_norm_g, w_in_full, attn_sinks, w_pool, pool_scale, mlp_norm_g, final_norm_g,
                front_token, out_weight, mlp_weights, ship_down, ship_up, ship_in):
    n_seq, seq, _ = x.shape
    rows = n_seq * seq
    tm, tm_mlp, tm_grad = min(512, seq), min(256, seq), min(1024, seq)
    x2d, t2d = x.reshape(rows, D_MODEL), target.reshape(rows, D_MODEL)
    g3 = final_norm_g.reshape(1, D_MODEL)
    cos, sin = _rope_tables(seq)
    wp_b = w_pool[0].astype(BF16)

    sink_rows = jnp.broadcast_to(attn_sinks.reshape(N_Q_HEADS, 1, 1), (N_Q_HEADS, 1, LANES))
    h1, q, kd, vd, u = _norm_inproj(x2d, attn_norm_g, w_in_full, cos, sin, front_token, seq, tm)
    attn = _attention_forward(sink_rows, q, kd, vd, n_seq, seq)
    pool = _pool_forward(u, wp_b, pool_scale, n_seq, seq)
    w_out_full = out_weight(pool)
    x2, h2 = _outproj_norm(x2d, attn, pool, w_out_full, mlp_norm_g, tm)
    w_up_full, w_down_full = mlp_weights(h2)
    a, f, dx3, dx3b, loss_cols, dg3 = _mlp_forward_loss(h2, x2, w_up_full, w_down_full, g3, t2d, tm_mlp)

    down_token = ship_down(_weight_gradient("down_gradient", f, dx3b, True, tm_grad))
    da, dx2, dattn, dpool, dg2, d_w_out = _mlp_backward_data(dx3b, a, w_down_full, w_up_full, dx3, x2, mlp_norm_g, w_out_full,
                                                            attn, pool, down_token, tm_mlp)
    up_token = ship_up(d_w_out, _weight_gradient("up_gradient", h2, da, False, tm_grad))
    dq, dk, dv, dsink = _attention_backward(sink_rows, q, kd, vd, dattn, cos, sin, up_token, n_seq, seq)
    du, d_w_pool, d_pool_scale = _pool_backward(u, dpool, wp_b, pool_scale, n_seq, seq)
    in_token = ship_in(_inproj_gradient(dq, dk, dv, du, h1, tm_grad))
    grad_x, dg1 = _inproj_backward(dq, dk, dv, du, w_in_full, x2d, dx2, attn_norm_g, in_token, tm)
    return grad_x.reshape(x.shape), _pack_small(dg1, dg2, dg3, d_pool_scale, dsink, loss_cols, d_w_pool)


def kernel(x, attn_norm_g, w_in, attn_sinks, w_pool, pool_scale, w_out, mlp_norm_g, w_up, w_down, final_norm_g, loss_target, m_attn_norm_g, m_w_in, m_attn_sinks, m_w_pool, m_pool_scale, m_w_out, m_mlp_norm_g, m_w_up, m_w_down, m_final_norm_g, v_attn_norm_g, v_w_in, v_attn_sinks, v_w_pool, v_pool_scale, v_w_out, v_mlp_norm_g, v_w_up, v_w_down, v_final_norm_g):
    me = (4 * lax.axis_index("x") + 2 * lax.axis_index("y") + lax.axis_index("c")).astype(jnp.int32).reshape(1)

    win_g, wout_land, wup_land, wdown_land = _gather_weights(
        [w_in[0], w_out[0], w_up[0], w_down[0]], [_row_slot, _row_slot, _column_slot, _row_slot],
        [(N_DEV, D_MODEL, IN_BLOCK), (N_DEV, OUT_BLOCK, D_MODEL), (D_MODEL, D_FF), (N_DEV, FF_BLOCK, D_MODEL)], 1)
    w_in_full = jnp.transpose(win_g, (1, 0, 2)).reshape(D_MODEL, IN_WIDTH)
    out_plan, mlp_plan = _own_slot_to_all([_row_slot]), _own_slot_to_all([_column_slot, _row_slot])
    out_copies = _start_copies("spread_out_start", out_plan, [], [wout_land], win_g)
    mlp_copies = _start_copies("spread_mlp_start", mlp_plan, [], [wup_land, wdown_land], out_copies[3])

    def out_weight(after):
        _, (wout_g,) = _wait_copies("spread_out_wait", out_plan, 0, *out_copies[:3], after)
        return wout_g.reshape(D_MODEL, D_MODEL)

    def mlp_weights(after):
        _, (wup_g, wdown_g) = _wait_copies("spread_mlp_wait", mlp_plan, 0, *mlp_copies[:3], after)
        return wup_g, wdown_g.reshape(D_FF, D_MODEL)

    deliveries = {}
    unordered = jnp.zeros((8, LANES), F32)

    def deliver(name, plan, grads, after):
        lands = [lax.empty((N_DEV - 1,) + (g.shape[1:] if plan is _block_to_owner else g.shape), F32) for g in grads]
        deliveries[name] = _start_copies(name + "_start", plan, grads, lands, after)
        return deliveries[name][3]

    def landed(name, plan, after):
        send, recv, flying, _ = deliveries[name]
        return _wait_copies(name + "_wait", plan, len(flying) // 2, send, recv, flying, after)

    grad_x, small = _local_step(
        x, loss_target, attn_norm_g, w_in_full, attn_sinks, w_pool, pool_scale, mlp_norm_g, final_norm_g, mlp_copies[3], out_weight,
        mlp_weights, lambda d_w_down: deliver("deliver_down", _block_to_owner, [d_w_down], unordered),
        lambda d_w_out, d_w_up: deliver("deliver_up", _block_to_owner, [d_w_out.reshape(N_DEV, OUT_BLOCK, D_MODEL), d_w_up], unordered),
        lambda d_w_in: deliver("deliver_in", _block_to_owner, [jnp.transpose(d_w_in.reshape(D_MODEL, N_DEV, IN_BLOCK), (1, 0, 2))], unordered))
    small_token = deliver("deliver_small", _whole_to_all, [small], unordered)

    (own_down,), (got_down,) = landed("deliver_down", _block_to_owner, small_token)
    (own_out, own_up), (got_out, got_up) = landed("deliver_up", _block_to_owner, small_token)
    g_down = _adamw_sharded(me, own_down, got_down, w_down[0], m_w_down[0], v_w_down[0], 256)
    g_up = _adamw_sharded(me, own_up, got_up, w_up[0], m_w_up[0], v_w_up[0], 256)
    g_out = _adamw_sharded(me, own_out, got_out, w_out[0], m_w_out[0], v_w_out[0], 128)
    (own_in,), (got_in,) = landed("deliver_in", _block_to_owner, g_out[0])
    g_in = _adamw_sharded(me, own_in, got_in, w_in[0], m_w_in[0], v_w_in[0], 1024)
    (own_small,), (got_small,) = landed("deliver_small", _whole_to_all, g_in[0])

    row = lambda a: a.reshape(1, D_MODEL)
    params = [(attn_norm_g, m_attn_norm_g, v_attn_norm_g), (mlp_norm_g, m_mlp_norm_g, v_mlp_norm_g),
              (row(final_norm_g), row(m_final_norm_g), row(v_final_norm_g)), (pool_scale, m_pool_scale, v_pool_scale),
              (attn_sinks, m_attn_sinks, v_attn_sinks), (w_pool[0], m_w_pool[0], v_w_pool[0])]
    loss, (s_norm1, s_norm2, s_norm3, s_scale, s_sinks, s_pool) = _finish_small(me, own_small, got_small, params)
    s_norm3 = [a.reshape(D_MODEL) for a in s_norm3]
    s_pool = [a[None] for a in s_pool]

    def ordered(k):
        return [s_norm1[k], g_in[k][None], s_sinks[k], s_pool[k], s_scale[k], g_out[k][None], s_norm2[k], g_up[k][None], g_down[k][None],
                s_norm3[k]]

    return (loss.reshape(()), grad_x, *ordered(0), *ordered(1), *ordered(2), *ordered(3))
```

```python
import functools

import jax
import jax.numpy as jnp
from jax import lax
from jax.experimental import pallas as pl
from jax.experimental.pallas import tpu as pltpu

F32 = jnp.float32
BF16 = jnp.bfloat16

D_MODEL = 1024
HEAD_DIM = 64
N_Q_HEADS = 8
Q_PER_KV = 4
ATTN_WIDTH = 512
KV_WIDTH = 128
BLOCK = 128
ROPE_THETA = 10000.0
POOL_WINDOWS = (2, 4, 8, 16)
POOL_WIDTH = 512
POOL_GROUP_DIM = 128
IN_WIDTH = 1280
D_FF = 4096
EPS = 1e-6
N_DEV = 8
FF_BLOCK = D_FF // N_DEV
IN_BLOCK = IN_WIDTH // N_DEV
OUT_BLOCK = D_MODEL // N_DEV
ADAM_LR = 0.001
ADAM_B1 = 0.9
ADAM_B2 = 0.999
ADAM_EPS = 1e-08
ADAM_WD = 0.01
ADAM_STEP = 10
NEG = -1e30
FORWARD_CHAINS = 4
BACKWARD_CHAINS = 2
LANES = 128
MIB = 1024 * 1024
MESH = pl.DeviceIdType.MESH

ROW_G1, ROW_G2, ROW_G3, ROW_PS, ROW_SINK, ROW_LOSS, ROW_WP, SMALL_ROWS = 0, 8, 16, 24, 32, 40, 48, 560


def _cparams(semantics, vmem_mib):
    return pltpu.CompilerParams(dimension_semantics=semantics, vmem_limit_bytes=vmem_mib * MIB)


def _dot(a, b):
    return jnp.dot(a, b, preferred_element_type=F32)


def _dot_nt(a, b):
    return lax.dot_general(a, b, (((1,), (1,)), ((), ())), preferred_element_type=F32)


def _dot_tn(a, b):
    return lax.dot_general(a, b, (((0,), (0,)), ((), ())), preferred_element_type=F32)


def _swap_halves(x):
    width = x.shape[1]
    lane = lax.broadcasted_iota(jnp.int32, x.shape, 1)
    ahead = pltpu.roll(x, width - HEAD_DIM // 2, 1)
    behind = pltpu.roll(x, HEAD_DIM // 2, 1)
    return jnp.where(lane % HEAD_DIM < HEAD_DIM // 2, ahead, behind)


def _rope(x, cos, sin):
    reps = x.shape[1] // LANES
    if reps > 1:
        cos = jnp.tile(cos, (1, reps))
        sin = jnp.tile(sin, (1, reps))
    return x * cos + _swap_halves(x) * sin


def _rope_tables(seq):
    half = HEAD_DIM // 2
    inv_freq = ROPE_THETA ** (-jnp.arange(half, dtype=F32) / half)
    ang = jnp.arange(seq).astype(F32)[:, None] * inv_freq[None, :]
    cos, sin = jnp.cos(ang), jnp.sin(ang)
    cos = jnp.tile(cos, (1, LANES // half))
    sin = jnp.tile(jnp.concatenate([-sin, sin], axis=1), (1, LANES // HEAD_DIM))
    return cos, sin


def _both_halves(x):
    lane = lax.broadcasted_iota(jnp.int32, x.shape, 1)
    other = pltpu.roll(x, HEAD_DIM, 1)
    low = lane < HEAD_DIM
    return jnp.where(low, x, other), jnp.where(low, other, x)


def _rms_backward(dh, xin, gain):
    r = lax.rsqrt(jnp.mean(xin * xin, axis=-1, keepdims=True) + EPS)
    xhat = xin * r
    dxhat = dh * gain
    dx = r * (dxhat - xhat * jnp.mean(dxhat * xhat, axis=-1, keepdims=True))
    return dx, jnp.sum(dh * xhat, axis=0, keepdims=True)


def _norm_inproj(x2d, gain, w_in, cos, sin, token, seq, tm):
    rows = x2d.shape[0]
    tiles_per_seq = seq // tm

    def body(x_ref, g_ref, w_ref, cos_ref, sin_ref, token_ref, h_ref, q_ref, k_ref, v_ref, u_ref):
        x = x_ref[...]
        r = lax.rsqrt(jnp.mean(x * x, axis=-1, keepdims=True) + EPS)
        h = (x * r * g_ref[...]).astype(BF16)
        h_ref[...] = h
        proj = _dot(h, w_ref[...])
        cos_t, sin_t = cos_ref[...], sin_ref[...]
        q = _rope(proj[:, :ATTN_WIDTH], cos_t, sin_t) * (HEAD_DIM ** -0.5)
        q_ref[...] = q.astype(BF16)
        k = _rope(proj[:, ATTN_WIDTH:ATTN_WIDTH + KV_WIDTH], cos_t, sin_t)
        k0, k1 = _both_halves(k)
        k_ref[...] = jnp.concatenate([k0, k1], axis=1).astype(BF16)
        v0, v1 = _both_halves(proj[:, ATTN_WIDTH + KV_WIDTH:ATTN_WIDTH + 2 * KV_WIDTH])
        v_ref[...] = jnp.concatenate([v0, v1], axis=1).astype(BF16)
        u_ref[...] = proj[:, ATTN_WIDTH + 2 * KV_WIDTH:]

    row = lambda width: pl.BlockSpec((tm, width), lambda i: (i, 0))
    table = pl.BlockSpec((tm, LANES), lambda i: (i % tiles_per_seq, 0))
    return pl.pallas_call(
        body, name="norm_inproj", grid=(rows // tm,),
        in_specs=[row(D_MODEL), pl.BlockSpec((1, D_MODEL), lambda i: (0, 0)),
                  pl.BlockSpec((D_MODEL, IN_WIDTH), lambda i: (0, 0)), table, table, pl.BlockSpec((8, LANES), lambda i: (0, 0))],
        out_specs=[row(D_MODEL), row(ATTN_WIDTH), row(2 * KV_WIDTH), row(2 * KV_WIDTH), row(POOL_WIDTH)],
        out_shape=[jax.ShapeDtypeStruct((rows, D_MODEL), BF16), jax.ShapeDtypeStruct((rows, ATTN_WIDTH), BF16),
                   jax.ShapeDtypeStruct((rows, 2 * KV_WIDTH), BF16), jax.ShapeDtypeStruct((rows, 2 * KV_WIDTH), BF16),
                   jax.ShapeDtypeStruct((rows, POOL_WIDTH), F32)],
        compiler_params=_cparams(("parallel",), 40),
    )(x2d, gain, w_in, cos, sin, token)


def _window_masks(n):
    qi = lax.broadcasted_iota(jnp.int32, (BLOCK, BLOCK), 0)
    kj = lax.broadcasted_iota(jnp.int32, (BLOCK, BLOCK), 1)
    return kj <= qi, jnp.logical_and(kj > qi, n > 0)


def _window_operand(ref, r0, p0, kv):
    low = lax.broadcasted_iota(jnp.int32, (BLOCK, LANES), 1) < HEAD_DIM
    cur = ref[pl.ds(r0, BLOCK), kv * LANES:(kv + 1) * LANES]
    prev = ref[pl.ds(p0, BLOCK), kv * LANES:(kv + 1) * LANES]
    zero = jnp.zeros_like(cur)
    return jnp.concatenate([jnp.where(low, cur, zero), jnp.where(low, zero, cur), jnp.where(low, prev, zero), jnp.where(low, zero, prev)], axis=0)


def _merged_window(wide, parity, cur_mask, prev_mask, fill):
    cur = wide[:, parity * LANES:(parity + 1) * LANES]
    prev = wide[:, (2 + parity) * LANES:(3 + parity) * LANES]
    return jnp.where(cur_mask, cur, jnp.where(prev_mask, prev, fill))


def _lane_sums(x):
    flat = x.reshape(-1, x.shape[-1])
    high = flat.astype(BF16)
    low = (flat - high.astype(F32)).astype(BF16)
    ones = jnp.ones((x.shape[-1], LANES), BF16)
    return (_dot(high, ones) + _dot(low, ones)).reshape(x.shape[:-1] + (LANES,))


def _softmax_with_sink(scores, sink):
    m = jnp.broadcast_to(jnp.maximum(jnp.max(scores, axis=-1, keepdims=True), sink), scores.shape)
    p, ps = jnp.exp(scores - m), jnp.exp(sink - m)
    inv = 1.0 / (_lane_sums(p) + ps)
    return p * inv, ps * inv


def _attention_forward(sinks, q, kd, vd, n_seq, seq):
    n_blocks = seq // BLOCK
    n_pairs = N_Q_HEADS // 2
    chains = min(FORWARD_CHAINS, n_blocks)

    def body(sink_ref, q_ref, k_ref, v_ref, o_ref, s_ref, p_ref):
        def step(i, carry):
            starts, values = [], []
            for u in range(chains):
                n = i * chains + u
                r0 = pl.multiple_of(n * BLOCK, BLOCK)
                p0 = pl.multiple_of(jnp.maximum(n - 1, 0) * BLOCK, BLOCK)
                cur_mask, prev_mask = _window_masks(n)
                keys = [_window_operand(k_ref, r0, p0, kv) for kv in range(2)]
                starts.append(r0)
                values.append([_window_operand(v_ref, r0, p0, kv) for kv in range(2)])
                for pair in range(n_pairs):
                    wide = _dot_nt(q_ref[pl.ds(r0, BLOCK), pair * LANES:(pair + 1) * LANES], keys[pair // 2])
                    for parity in range(2):
                        s_ref[u * N_Q_HEADS + 2 * pair + parity] = _merged_window(wide, parity, cur_mask, prev_mask, NEG)
            probs, _ = _softmax_with_sink(s_ref[...], jnp.tile(sink_ref[:, :, 0:1], (chains, 1, 1)))
            probs = probs.astype(BF16)
            zero = jnp.zeros((BLOCK, BLOCK), BF16)
            for u in range(chains):
                for pair in range(n_pairs):
                    for parity in range(2):
                        ph = probs[u * N_Q_HEADS + 2 * pair + parity]
                        p_ref[u, pair, :, parity * LANES:(parity + 1) * LANES] = jnp.where(cur_mask, ph, zero)
                        p_ref[u, pair, :, (2 + parity) * LANES:(3 + parity) * LANES] = jnp.where(cur_mask, zero, ph)
            for u in range(chains):
                for pair in range(n_pairs):
                    out = _dot(p_ref[u, pair], values[u][pair // 2])
                    o_ref[pl.ds(starts[u], BLOCK), pair * LANES:(pair + 1) * LANES] = out.astype(BF16)
            return carry

        lax.fori_loop(0, n_blocks // chains, step, 0)

    seq_block = lambda width: pl.BlockSpec((seq, width), lambda b: (b, 0))
    return pl.pallas_call(
        body, name="attention_forward", grid=(n_seq,),
        in_specs=[pl.BlockSpec((N_Q_HEADS, 1, LANES), lambda b: (0, 0, 0)), seq_block(ATTN_WIDTH), seq_block(2 * KV_WIDTH),
                  seq_block(2 * KV_WIDTH)],
        out_specs=seq_block(ATTN_WIDTH),
        out_shape=jax.ShapeDtypeStruct((n_seq * seq, ATTN_WIDTH), BF16),
        scratch_shapes=[pltpu.VMEM((chains * N_Q_HEADS, BLOCK, BLOCK), F32), pltpu.VMEM((chains, n_pairs, BLOCK, 4 * LANES), BF16)],
        compiler_params=_cparams(("parallel",), 40),
    )(sinks, q, kd, vd)


def _trailing(x, window, t, seq):
    k = 1
    while k < window:
        x = x + jnp.where(t >= k, pltpu.roll(x, k, 0), 0.0)
        k *= 2
    return x


def _leading(x, window, t, seq):
    k = 1
    while k < window:
        x = x + jnp.where(t < seq - k, pltpu.roll(x, seq - k, 0), 0.0)
        k *= 2
    return x


def _pool_features(u_g, window, t, seq):
    count = jnp.minimum(t + 1, window).astype(F32)
    return (_trailing(u_g, window, t, seq) / count - u_g).astype(BF16), count


def _pool_forward(u, w_pool, pool_scale, n_seq, seq):
    def body(u_ref, w_ref, s_ref, o_ref):
        t = lax.broadcasted_iota(jnp.int32, (seq, 1), 0)
        for g, window in enumerate(POOL_WINDOWS):
            cols = slice(g * POOL_GROUP_DIM, (g + 1) * POOL_GROUP_DIM)
            d, _ = _pool_features(u_ref[:, cols], window, t, seq)
            o_ref[:, cols] = (_dot(d, w_ref[g]) * s_ref[:, cols]).astype(BF16)

    seq_block = pl.BlockSpec((seq, POOL_WIDTH), lambda b: (b, 0))
    return pl.pallas_call(
        body, name="pool_forward", grid=(n_seq,),
        in_specs=[seq_block, pl.BlockSpec((len(POOL_WINDOWS), POOL_GROUP_DIM, POOL_GROUP_DIM), lambda b: (0, 0, 0)),
                  pl.BlockSpec((1, POOL_WIDTH), lambda b: (0, 0))],
        out_specs=seq_block,
        out_shape=jax.ShapeDtypeStruct((n_seq * seq, POOL_WIDTH), BF16),
        compiler_params=_cparams(("parallel",), 40),
    )(u, w_pool, pool_scale)


def _outproj_norm(x2d, attn, pool, w_out, gain, tm):
    rows = x2d.shape[0]

    def body(x_ref, a_ref, p_ref, w_ref, g_ref, x2_ref, h_ref):
        x2 = x_ref[...] + _dot(a_ref[...], w_ref[:ATTN_WIDTH, :]) + _dot(p_ref[...], w_ref[ATTN_WIDTH:, :])
        x2_ref[...] = x2
        r = lax.rsqrt(jnp.mean(x2 * x2, axis=-1, keepdims=True) + EPS)
        h_ref[...] = (x2 * r * g_ref[...]).astype(BF16)

    row = lambda width: pl.BlockSpec((tm, width), lambda i: (i, 0))
    return pl.pallas_call(
        body, name="outproj_norm", grid=(rows // tm,),
        in_specs=[row(D_MODEL), row(ATTN_WIDTH), row(POOL_WIDTH), pl.BlockSpec((D_MODEL, D_MODEL), lambda i: (0, 0)),
                  pl.BlockSpec((1, D_MODEL), lambda i: (0, 0))],
        out_specs=[row(D_MODEL), row(D_MODEL)],
        out_shape=[jax.ShapeDtypeStruct((rows, D_MODEL), F32), jax.ShapeDtypeStruct((rows, D_MODEL), BF16)],
        compiler_params=_cparams(("parallel",), 40),
    )(x2d, attn, pool, w_out, gain)


def _resident(shape):
    return pl.BlockSpec(shape, lambda i: (0,) * len(shape), pipeline_mode=pl.Buffered(1))


def _mlp_forward_loss(h2, x2, w_up, w_down, gain, target, tm):
    rows = h2.shape[0]
    chunk = D_MODEL

    def body(h_ref, x_ref, up_ref, down_ref, g_ref, t_ref, slope_ref, f_ref, dx_ref, dxb_ref, loss_ref, dg_ref):
        @pl.when(pl.program_id(0) == 0)
        def _():
            loss_ref[...] = jnp.zeros_like(loss_ref)
            dg_ref[...] = jnp.zeros_like(dg_ref)

        h = h_ref[...]
        for c in range(D_FF // chunk):
            cols = slice(c * chunk, (c + 1) * chunk)
            r = jnp.maximum(_dot(h, up_ref[:, cols]), 0.0)
            slope_ref[:, cols] = (r + r).astype(BF16)
            f_ref[:, cols] = (r * r).astype(BF16)
        x3 = x_ref[...] + _dot(f_ref[...], down_ref[...])
        rn = lax.rsqrt(jnp.mean(x3 * x3, axis=-1, keepdims=True) + EPS)
        err = x3 * rn * g_ref[...] - t_ref[...]
        loss_ref[...] += jnp.sum(err * err, axis=0, keepdims=True)
        dx, dg = _rms_backward(err / D_MODEL, x3, g_ref[...])
        dg_ref[...] += dg
        dx_ref[...] = dx
        dxb_ref[...] = dx.astype(BF16)

    row = lambda width: pl.BlockSpec((tm, width), lambda i: (i, 0))
    vec = pl.BlockSpec((1, D_MODEL), lambda i: (0, 0))
    return pl.pallas_call(
        body, name="mlp_forward_loss", grid=(rows // tm,),
        in_specs=[row(D_MODEL), row(D_MODEL), _resident((D_MODEL, D_FF)), _resident((D_FF, D_MODEL)), vec, row(D_MODEL)],
        out_specs=[row(D_FF), row(D_FF), row(D_MODEL), row(D_MODEL), vec, vec],
        out_shape=[jax.ShapeDtypeStruct((rows, D_FF), BF16), jax.ShapeDtypeStruct((rows, D_FF), BF16),
                   jax.ShapeDtypeStruct((rows, D_MODEL), F32), jax.ShapeDtypeStruct((rows, D_MODEL), BF16),
                   jax.ShapeDtypeStruct((1, D_MODEL), F32), jax.ShapeDtypeStruct((1, D_MODEL), F32)],
        compiler_params=_cparams(("arbitrary",), 56),
    )(h2, x2, w_up, w_down, gain, target)


def _mlp_backward_data(dx3b, slope, w_down, w_up, dx3, x2, gain, w_out, attn, pool, token, tm):
    rows = dx3b.shape[0]
    steps = rows // tm
    chunk = D_MODEL

    def body(dxb_ref, slope_ref, down_ref, up_ref, dx3_ref, x2_ref, g_ref, wo_ref, attn_ref, pool_ref, token_ref,
             da_ref, dx2_ref, dattn_ref, dpool_ref, dg_ref, dwo_hbm, wire_hbm, dwo_acc, wire, sems):
        @pl.when(pl.program_id(0) == 0)
        def _():
            dg_ref[...] = jnp.zeros_like(dg_ref)
            dwo_acc[...] = jnp.zeros_like(dwo_acc)

        dxb = dxb_ref[...]
        for c in range(D_FF // chunk):
            cols = slice(c * chunk, (c + 1) * chunk)
            da_ref[:, cols] = (_dot_nt(dxb, down_ref[cols, :]) * slope_ref[:, cols].astype(F32)).astype(BF16)
        dnorm, dg = _rms_backward(_dot_nt(da_ref[...], up_ref[...]), x2_ref[...], g_ref[...])
        dg_ref[...] += dg
        dx2 = dx3_ref[...] + dnorm
        dx2_ref[...] = dx2
        dx2b = dx2.astype(BF16)
        dmix = _dot_nt(dx2b, wo_ref[...])
        dattn_ref[...] = dmix[:, :ATTN_WIDTH].astype(BF16)
        dpool_ref[...] = dmix[:, ATTN_WIDTH:]
        dwo_acc[:ATTN_WIDTH, :] += _dot_tn(attn_ref[...], dx2b)
        dwo_acc[ATTN_WIDTH:, :] += _dot_tn(pool_ref[...], dx2b)

        @pl.when(pl.program_id(0) == steps - 1)
        def _():
            done = pltpu.make_async_copy(dwo_acc, dwo_hbm, sems.at[0])
            done.start()
            wire[...] = dwo_acc[...].astype(BF16)
            sent = pltpu.make_async_copy(wire, wire_hbm, sems.at[1])
            sent.start()
            done.wait()
            sent.wait()

    row = lambda width: pl.BlockSpec((tm, width), lambda i: (i, 0))
    vec = pl.BlockSpec((1, D_MODEL), lambda i: (0, 0))
    return pl.pallas_call(
        body, name="mlp_backward_data", grid=(steps,),
        in_specs=[row(D_MODEL), row(D_FF), _resident((D_FF, D_MODEL)), _resident((D_MODEL, D_FF)), row(D_MODEL), row(D_MODEL), vec,
                  _resident((D_MODEL, D_MODEL)), row(ATTN_WIDTH), row(POOL_WIDTH), pl.BlockSpec((8, LANES), lambda i: (0, 0))],
        out_specs=[row(D_FF), row(D_MODEL), row(ATTN_WIDTH), row(POOL_WIDTH), vec, pl.BlockSpec(memory_space=pl.ANY),
                   pl.BlockSpec(memory_space=pl.ANY)],
        out_shape=[jax.ShapeDtypeStruct((rows, D_FF), BF16), jax.ShapeDtypeStruct((rows, D_MODEL), F32),
                   jax.ShapeDtypeStruct((rows, ATTN_WIDTH), BF16), jax.ShapeDtypeStruct((rows, POOL_WIDTH), F32),
                   jax.ShapeDtypeStruct((1, D_MODEL), F32), jax.ShapeDtypeStruct((D_MODEL, D_MODEL), F32),
                   jax.ShapeDtypeStruct((D_MODEL, D_MODEL), BF16)],
        scratch_shapes=[pltpu.VMEM((D_MODEL, D_MODEL), F32), pltpu.VMEM((D_MODEL, D_MODEL), BF16), pltpu.SemaphoreType.DMA((2,))],
        compiler_params=_cparams(("arbitrary",), 56),
    )(dx3b, slope, w_down, w_up, dx3, x2, gain, w_out, attn, pool, token)


def _weight_gradient(name, lhs, rhs, block_lhs, tm):
    rows = lhs.shape[0]
    steps = rows // tm
    out = (N_DEV, FF_BLOCK, rhs.shape[1]) if block_lhs else (N_DEV, lhs.shape[1], FF_BLOCK)

    def body(l_ref, r_ref, o_hbm, wire_hbm, acc, wire, sems):
        @pl.when(pl.program_id(0) == 0)
        def _():
            acc[...] = jnp.zeros_like(acc)

        for d in range(N_DEV):
            cols = slice(d * FF_BLOCK, (d + 1) * FF_BLOCK)
            acc[d] += _dot_tn(l_ref[:, cols], r_ref[...]) if block_lhs else _dot_tn(l_ref[...], r_ref[:, cols])

        @pl.when(pl.program_id(0) == steps - 1)
        def _():
            done = pltpu.make_async_copy(acc, o_hbm, sems.at[0])
            done.start()
            wire[...] = acc[...].astype(BF16)
            sent = pltpu.make_async_copy(wire, wire_hbm, sems.at[1])
            sent.start()
            done.wait()
            sent.wait()

    return pl.pallas_call(
        body, name=name, grid=(steps,),
        in_specs=[pl.BlockSpec((tm, lhs.shape[1]), lambda i: (i, 0)), pl.BlockSpec((tm, rhs.shape[1]), lambda i: (i, 0))],
        out_specs=[pl.BlockSpec(memory_space=pl.ANY), pl.BlockSpec(memory_space=pl.ANY)],
        out_shape=[jax.ShapeDtypeStruct(out, F32), jax.ShapeDtypeStruct(out, BF16)],
        scratch_shapes=[pltpu.VMEM(out, F32), pltpu.VMEM(out, BF16), pltpu.SemaphoreType.DMA((2,))],
        compiler_params=_cparams(("arbitrary",), 60),
    )(lhs, rhs)


def _attention_backward(sinks, q, kd, vd, dout, cos, sin, token, n_seq, seq):
    n_blocks = seq // BLOCK
    n_pairs = N_Q_HEADS // 2
    n_kv = N_Q_HEADS // Q_PER_KV
    chains = min(BACKWARD_CHAINS, n_blocks)

    def body(sink_ref, q_ref, k_ref, v_ref, do_ref, cos_ref, sin_ref, token_ref, dq_ref, dk_ref, dv_ref, dsink_ref,
             s_all, dp_all, dsc_all, dsp_all, pc_all, pp_all, dk_acc, dv_acc):
        low = lax.broadcasted_iota(jnp.int32, (BLOCK, LANES), 1) < HEAD_DIM

        @pl.when(pl.program_id(0) == 0)
        def _():
            dsink_ref[...] = jnp.zeros_like(dsink_ref)

        def fold(x):
            return x + pltpu.roll(x, HEAD_DIM, 1)

        def onto_keys(ref, kv, other):
            even, odd = _dot_tn(ref[2 * kv], other), _dot_tn(ref[2 * kv + 1], other)
            return fold(jnp.where(low, even, odd))

        def step(i, dsink):
            blocks = []
            for u in range(chains):
                n = i * chains + u
                r0 = pl.multiple_of(n * BLOCK, BLOCK)
                p0 = pl.multiple_of(jnp.maximum(n - 1, 0) * BLOCK, BLOCK)
                cur_mask, prev_mask = _window_masks(n)
                keys = [_window_operand(k_ref, r0, p0, kv) for kv in range(n_kv)]
                values = [_window_operand(v_ref, r0, p0, kv) for kv in range(n_kv)]
                q_pairs = [q_ref[pl.ds(r0, BLOCK), pair * LANES:(pair + 1) * LANES] for pair in range(n_pairs)]
                do_pairs = [do_ref[pl.ds(r0, BLOCK), pair * LANES:(pair + 1) * LANES] for pair in range(n_pairs)]
                for pair in range(n_pairs):
                    wide_s = _dot_nt(q_pairs[pair], keys[pair // 2])
                    wide_dp = _dot_nt(do_pairs[pair], values[pair // 2])
                    for parity in range(2):
                        s_all[u * N_Q_HEADS + 2 * pair + parity] = _merged_window(wide_s, parity, cur_mask, prev_mask, NEG)
                        dp_all[u * N_Q_HEADS + 2 * pair + parity] = _merged_window(wide_dp, parity, cur_mask, prev_mask, 0.0)
                blocks.append((n, r0, p0, keys, q_pairs, do_pairs))

            probs, p_sink = _softmax_with_sink(s_all[...], jnp.tile(sink_ref[:, :, 0:1], (chains, 1, 1)))
            dprobs = dp_all[...]
            delta = _lane_sums(probs * dprobs)
            dscores = (probs * (dprobs - delta)).astype(BF16)
            sink_terms = jnp.sum((p_sink * delta)[:, :, 0:1], axis=1, keepdims=True)
            probs = probs.astype(BF16)
            zero = jnp.zeros((BLOCK, BLOCK), BF16)
            for u in range(chains):
                dsink = dsink - sink_terms[u * N_Q_HEADS:(u + 1) * N_Q_HEADS]
                for head in range(N_Q_HEADS):
                    group, rows = 2 * (head // Q_PER_KV) + head % 2, pl.ds(((head % Q_PER_KV) // 2) * BLOCK, BLOCK)
                    ds_h, p_h = dscores[u * N_Q_HEADS + head], probs[u * N_Q_HEADS + head]
                    dsc_all[u, group, rows, :] = jnp.where(cur_mask, ds_h, zero)
                    dsp_all[u, group, rows, :] = jnp.where(cur_mask, zero, ds_h)
                    pc_all[u, group, rows, :] = jnp.where(cur_mask, p_h, zero)
                    pp_all[u, group, rows, :] = jnp.where(cur_mask, zero, p_h)

            for u, (n, r0, p0, keys, q_pairs, do_pairs) in enumerate(blocks):
                dsc_ref, dsp_ref, pc_ref, pp_ref = dsc_all.at[u], dsp_all.at[u], pc_all.at[u], pp_all.at[u]
                for pair in range(n_pairs):
                    kv, rows = pair // 2, pl.ds((pair % 2) * BLOCK, BLOCK)
                    wide = jnp.concatenate([dsc_ref[2 * kv, rows, :], dsc_ref[2 * kv + 1, rows, :],
                                            dsp_ref[2 * kv, rows, :], dsp_ref[2 * kv + 1, rows, :]], axis=1)
                    dq = _dot(wide, keys[kv]) * (HEAD_DIM ** -0.5)
                    dq = _rope(dq, cos_ref[pl.ds(r0, BLOCK), :], -sin_ref[pl.ds(r0, BLOCK), :])
                    dq_ref[pl.ds(r0, BLOCK), pair * LANES:(pair + 1) * LANES] = dq.astype(BF16)

                parts = []
                for kv in range(n_kv):
                    q_rows = jnp.concatenate([q_pairs[2 * kv], q_pairs[2 * kv + 1]], axis=0)
                    do_rows = jnp.concatenate([do_pairs[2 * kv], do_pairs[2 * kv + 1]], axis=0)
                    parts.append((onto_keys(dsc_ref, kv, q_rows), onto_keys(dsp_ref, kv, q_rows),
                                  onto_keys(pc_ref, kv, do_rows), onto_keys(pp_ref, kv, do_rows)))
                dk_acc[pl.ds(r0, BLOCK), :] = jnp.where(low, parts[0][0], parts[1][0])
                dv_acc[pl.ds(r0, BLOCK), :] = jnp.where(low, parts[0][2], parts[1][2])

                @pl.when(n > 0)
                def _():
                    dk_acc[pl.ds(p0, BLOCK), :] += jnp.where(low, parts[0][1], parts[1][1])
                    dv_acc[pl.ds(p0, BLOCK), :] += jnp.where(low, parts[0][3], parts[1][3])

            return dsink

        dsink = lax.fori_loop(0, n_blocks // chains, step, jnp.zeros((N_Q_HEADS, 1, 1), F32))
        dsink_ref[...] += jnp.broadcast_to(dsink, dsink_ref.shape)
        dk_ref[...] = _rope(dk_acc[...], cos_ref[...], -sin_ref[...]).astype(BF16)
        dv_ref[...] = dv_acc[...].astype(BF16)

    seq_block = lambda width: pl.BlockSpec((seq, width), lambda b: (b, 0))
    table = pl.BlockSpec((seq, LANES), lambda b: (0, 0))
    per_head = pl.BlockSpec((N_Q_HEADS, 1, LANES), lambda b: (0, 0, 0))
    per_block = pltpu.VMEM((chains * N_Q_HEADS, BLOCK, BLOCK), F32)
    grouped = pltpu.VMEM((chains, 2 * n_kv, 2 * BLOCK, BLOCK), BF16)
    return pl.pallas_call(
        body, name="attention_backward", grid=(n_seq,),
        in_specs=[per_head, seq_block(ATTN_WIDTH), seq_block(2 * KV_WIDTH), seq_block(2 * KV_WIDTH),
                  seq_block(ATTN_WIDTH), table, table, pl.BlockSpec((8, LANES), lambda b: (0, 0))],
        out_specs=[seq_block(ATTN_WIDTH), seq_block(KV_WIDTH), seq_block(KV_WIDTH), per_head],
        out_shape=[jax.ShapeDtypeStruct((n_seq * seq, ATTN_WIDTH), BF16), jax.ShapeDtypeStruct((n_seq * seq, KV_WIDTH), BF16),
                   jax.ShapeDtypeStruct((n_seq * seq, KV_WIDTH), BF16), jax.ShapeDtypeStruct((N_Q_HEADS, 1, LANES), F32)],
        scratch_shapes=[per_block, per_block, grouped, grouped, grouped, grouped, pltpu.VMEM((seq, KV_WIDTH), F32), pltpu.VMEM((seq, KV_WIDTH), F32)],
        compiler_params=_cparams(("arbitrary",), 40),
    )(sinks, q, kd, vd, dout, cos, sin, token)


def _pool_backward(u, dpool, w_pool, pool_scale, n_seq, seq):
    groups = len(POOL_WINDOWS)

    def body(u_ref, dp_ref, w_ref, s_ref, du_ref, dw_ref, ds_ref):
        @pl.when(pl.program_id(0) == 0)
        def _():
            dw_ref[...] = jnp.zeros_like(dw_ref)
            ds_ref[...] = jnp.zeros_like(ds_ref)

        t = lax.broadcasted_iota(jnp.int32, (seq, 1), 0)
        for g, window in enumerate(POOL_WINDOWS):
            cols = slice(g * POOL_GROUP_DIM, (g + 1) * POOL_GROUP_DIM)
            d, count = _pool_features(u_ref[:, cols], window, t, seq)
            dpool_g = dp_ref[:, cols]
            ds_ref[:, cols] += jnp.sum(dpool_g * _dot(d, w_ref[g]), axis=0, keepdims=True)
            dy = (dpool_g * s_ref[:, cols]).astype(BF16)
            dw_ref[g] += _dot_tn(d, dy)
            dd = _dot_nt(dy, w_ref[g])
            du_ref[:, cols] = (_leading(dd / count, window, t, seq) - dd).astype(BF16)

    seq_block = pl.BlockSpec((seq, POOL_WIDTH), lambda b: (b, 0))
    weights = pl.BlockSpec((groups, POOL_GROUP_DIM, POOL_GROUP_DIM), lambda b: (0, 0, 0))
    scale = pl.BlockSpec((1, POOL_WIDTH), lambda b: (0, 0))
    return pl.pallas_call(
        body, name="pool_backward", grid=(n_seq,),
        in_specs=[seq_block, seq_block, weights, scale],
        out_specs=[seq_block, weights, scale],
        out_shape=[jax.ShapeDtypeStruct((n_seq * seq, POOL_WIDTH), BF16),
                   jax.ShapeDtypeStruct((groups, POOL_GROUP_DIM, POOL_GROUP_DIM), F32), jax.ShapeDtypeStruct((1, POOL_WIDTH), F32)],
        compiler_params=_cparams(("arbitrary",), 40),
    )(u, dpool, w_pool, pool_scale)


def _inproj_gradient(dq, dk, dv, du, h1, tm):
    rows = h1.shape[0]

    def body(dq_ref, dk_ref, dv_ref, du_ref, h_ref, dw_ref):
        @pl.when(pl.program_id(0) == 0)
        def _():
            dw_ref[...] = jnp.zeros_like(dw_ref)

        dproj = jnp.concatenate([dq_ref[...], dk_ref[...], dv_ref[...], du_ref[...]], axis=1)
        dw_ref[...] += _dot_tn(h_ref[...], dproj)

    row = lambda width: pl.BlockSpec((tm, width), lambda i: (i, 0))
    return pl.pallas_call(
        body, name="inproj_gradient", grid=(rows // tm,),
        in_specs=[row(ATTN_WIDTH), row(KV_WIDTH), row(KV_WIDTH), row(POOL_WIDTH), row(D_MODEL)],
        out_specs=pl.BlockSpec((D_MODEL, IN_WIDTH), lambda i: (0, 0)),
        out_shape=jax.ShapeDtypeStruct((D_MODEL, IN_WIDTH), F32),
        compiler_params=_cparams(("arbitrary",), 48),
    )(dq, dk, dv, du, h1)


def _inproj_backward(dq, dk, dv, du, w_in, x2d, dx2, gain, token, tm):
    rows = x2d.shape[0]

    def body(dq_ref, dk_ref, dv_ref, du_ref, w_ref, x_ref, dx2_ref, g_ref, token_ref, dx_ref, dg_ref):
        @pl.when(pl.program_id(0) == 0)
        def _():
            dg_ref[...] = jnp.zeros_like(dg_ref)

        dproj = jnp.concatenate([dq_ref[...], dk_ref[...], dv_ref[...], du_ref[...]], axis=1)
        dnorm, dg = _rms_backward(_dot_nt(dproj, w_ref[...]), x_ref[...], g_ref[...])
        dg_ref[...] += dg
        dx_ref[...] = dx2_ref[...] + dnorm

    row = lambda width: pl.BlockSpec((tm, width), lambda i: (i, 0))
    vec = pl.BlockSpec((1, D_MODEL), lambda i: (0, 0))
    return pl.pallas_call(
        body, name="inproj_backward", grid=(rows // tm,),
        in_specs=[row(ATTN_WIDTH), row(KV_WIDTH), row(KV_WIDTH), row(POOL_WIDTH), pl.BlockSpec((D_MODEL, IN_WIDTH), lambda i: (0, 0)),
                  row(D_MODEL), row(D_MODEL), vec, pl.BlockSpec((8, LANES), lambda i: (0, 0))],
        out_specs=[row(D_MODEL), vec],
        out_shape=[jax.ShapeDtypeStruct((rows, D_MODEL), F32), jax.ShapeDtypeStruct((1, D_MODEL), F32)],
        compiler_params=_cparams(("arbitrary",), 48),
    )(dq, dk, dv, du, w_in, x2d, dx2, gain, token)


def _place():
    return lax.axis_index("x"), lax.axis_index("y"), lax.axis_index("c")


def _peer(x, y, c, rel):
    return (1 - x if rel & 4 else x, 1 - y if rel & 2 else y, 1 - c if rel & 1 else c)


def _index(px, py, pc):
    return 4 * px + 2 * py + pc


def _row_slot(ref, d):
    return ref.at[d]


def _column_slot(ref, d):
    return ref.at[:, pl.ds(pl.multiple_of(d * FF_BLOCK, FF_BLOCK), FF_BLOCK)]


def _stage_weights(shards, slots, shapes):
    n = len(shards)

    def body(*refs):
        ins, outs, stage, sems = refs[:n], refs[n:2 * n], refs[2 * n:3 * n], refs[3 * n]
        me = _index(*_place())
        mine = []
        for a in range(n):
            stage[a][...] = ins[a][...].astype(BF16)
            mine.append(pltpu.make_async_copy(stage[a], slots[a](outs[a], me), sems.at[a]))
            mine[-1].start()
        for cp in mine:
            cp.wait()

    return pl.pallas_call(
        body, name="stage_weights",
        in_specs=[pl.BlockSpec(memory_space=pltpu.VMEM)] * n,
        out_specs=[pl.BlockSpec(memory_space=pl.ANY)] * n,
        out_shape=[jax.ShapeDtypeStruct(shape, BF16) for shape in shapes],
        scratch_shapes=[pltpu.VMEM(s.shape, BF16) for s in shards] + [pltpu.SemaphoreType.DMA((n,))],
        compiler_params=pltpu.CompilerParams(vmem_limit_bytes=32 * MIB),
    )(*shards)


def _own_slot_to_all(slots):
    def plan(a, rel, me, to, ins, lands):
        return slots[a](lands[a], me), slots[a](lands[a], me)
    return plan


def _whole_to_all(a, rel, me, to, ins, lands):
    return ins[a], lands[a].at[rel - 1]


def _block_to_owner(a, rel, me, to, ins, lands):
    return ins[a].at[to], lands[a].at[rel - 1]


def _split_copies(plan, ins, lands, send_sems, recv_sems):
    x, y, c = _place()
    copies = []
    for rel in range(1, N_DEV):
        to = _peer(x, y, c, rel)
        for a in range(len(lands)):
            src, dst = plan(a, rel, _index(x, y, c), _index(*to), ins, lands)
            k = a * (N_DEV - 1) + rel - 1
            copies.append(pltpu.make_async_remote_copy(src_ref=src, dst_ref=dst, send_sem=send_sems.at[k],
                                                       recv_sem=recv_sems.at[k], device_id=to, device_id_type=MESH))
    return copies


HBM_SPEC = pl.BlockSpec(memory_space=pltpu.HBM)
SEM_SPEC = pl.BlockSpec(memory_space=pltpu.SEMAPHORE)
EFFECT = pltpu.SideEffectType.DATAFLOW_SIDE_EFFECTING


def _start_copies(name, plan, ins, lands, after):
    n_in, n = len(ins), len(ins) + len(lands)

    def body(*refs):
        send_sems, recv_sems = refs[n + 1], refs[n + 2]
        for cp in _split_copies(plan, refs[:n_in], refs[n_in:n], send_sems, recv_sems):
            cp.start()
        refs[-1][...] = jnp.zeros_like(refs[-1])

    arrays = [pltpu.with_memory_space_constraint(v, pltpu.HBM) for v in (*ins, *lands)]
    sems = pltpu.SemaphoreType.DMA((len(lands) * (N_DEV - 1),))
    send_sems, recv_sems, *flying, token = pl.pallas_call(
        body, name=name,
        out_shape=(sems, sems, *[pltpu.HBM(v.shape, v.dtype) for v in arrays], jax.ShapeDtypeStruct((8, LANES), F32)),
        in_specs=[HBM_SPEC] * n + [pl.BlockSpec(memory_space=pl.ANY)],
        out_specs=(SEM_SPEC, SEM_SPEC, *[HBM_SPEC] * n, pl.BlockSpec(memory_space=pltpu.VMEM)),
        input_output_aliases={i: 2 + i for i in range(n)},
        compiler_params=pltpu.CompilerParams(has_side_effects=EFFECT),
    )(*arrays, after)
    return send_sems, recv_sems, flying, token


def _wait_copies(name, plan, n_in, send_sems, recv_sems, flying, after):
    n = len(flying)

    def body(*refs):
        for cp in _split_copies(plan, refs[:n_in], refs[n_in:n], refs[n], refs[n + 1]):
            cp.wait_send()
            cp.wait_recv()

    landed = pl.pallas_call(
        body, name=name, out_shape=tuple(pltpu.HBM(v.shape, v.dtype) for v in flying),
        in_specs=[HBM_SPEC] * n + [SEM_SPEC, SEM_SPEC, pl.BlockSpec(memory_space=pl.ANY)], out_specs=tuple([HBM_SPEC] * n),
        input_output_aliases={i: i for i in range(n)},
        compiler_params=pltpu.CompilerParams(has_side_effects=EFFECT),
    )(*flying, send_sems, recv_sems, after)
    return landed[:n_in], landed[n_in:]


def _adamw_math(w, g, m, v):
    m = ADAM_B1 * m + (1.0 - ADAM_B1) * g
    v = ADAM_B2 * v + (1.0 - ADAM_B2) * (g * g)
    m_hat = m / (1.0 - ADAM_B1 ** ADAM_STEP)
    v_hat = v / (1.0 - ADAM_B2 ** ADAM_STEP)
    return -ADAM_LR * (m_hat / (jnp.sqrt(v_hat) + ADAM_EPS) + ADAM_WD * w), m, v


def _adamw_sharded(me, own, received, w, m, v, tr):
    rows, cols = w.shape

    def body(me_ref, own_ref, rec_ref, w_ref, m_ref, v_ref, g_ref, d_ref, nm_ref, nv_ref):
        g = own_ref[...]
        for r in range(N_DEV - 1):
            g = g + rec_ref[r].astype(F32)
        g_ref[...] = g
        d_ref[...], nm_ref[...], nv_ref[...] = _adamw_math(w_ref[...], g, m_ref[...], v_ref[...])

    tile = pl.BlockSpec((tr, cols), lambda i, me_ref: (i, 0))
    shape = jax.ShapeDtypeStruct((rows, cols), F32)
    return pl.pallas_call(
        body, name="adamw_sharded",
        grid_spec=pltpu.PrefetchScalarGridSpec(
            num_scalar_prefetch=1, grid=(rows // tr,),
            in_specs=[pl.BlockSpec((None, tr, cols), lambda i, me_ref: (me_ref[0], i, 0)),
                      pl.BlockSpec((N_DEV - 1, tr, cols), lambda i, me_ref: (0, i, 0)), tile, tile, tile],
            out_specs=[tile, tile, tile, tile]),
        out_shape=[shape, shape, shape, shape],
        compiler_params=_cparams(("parallel",), 40),
    )(me, own, received, w, m, v)


VECTOR_ROWS = D_MODEL // LANES
POOL_ROWS = len(POOL_WINDOWS) * POOL_GROUP_DIM


def _pack_small(dg1, dg2, dg3, dps, dsink, loss_cols, dwp):
    def body(g1_ref, g2_ref, g3_ref, ps_ref, sink_ref, loss_ref, wp_ref, o_ref):
        o_ref[...] = jnp.zeros_like(o_ref)
        for base, ref, n in ((ROW_G1, g1_ref, VECTOR_ROWS), (ROW_G2, g2_ref, VECTOR_ROWS), (ROW_G3, g3_ref, VECTOR_ROWS),
                             (ROW_LOSS, loss_ref, VECTOR_ROWS), (ROW_PS, ps_ref, POOL_WIDTH // LANES)):
            for r in range(n):
                o_ref[base + r:base + r + 1, :] = ref[:, r * LANES:(r + 1) * LANES]
        heads = sink_ref[:, 0, :]
        on_diagonal = lax.broadcasted_iota(jnp.int32, heads.shape, 0) == lax.broadcasted_iota(jnp.int32, heads.shape, 1)
        o_ref[ROW_SINK:ROW_SINK + 1, :] = jnp.sum(jnp.where(on_diagonal, heads, 0.0), axis=0, keepdims=True)
        o_ref[ROW_WP:ROW_WP + POOL_ROWS, :] = wp_ref[...].reshape(POOL_ROWS, LANES)

    return pl.pallas_call(body, name="pack_small", out_shape=jax.ShapeDtypeStruct((SMALL_ROWS, LANES), F32))(
        dg1, dg2, dg3, dps, dsink, loss_cols, dwp)


def _finish_small(me, own, landed, params):
    flat = [a for group in params for a in group]

    def body(me_ref, own_ref, landed_ref, *refs):
        ins, outs = refs[:len(flat)], refs[len(flat):]
        total = None
        for source in range(N_DEV):
            rel = jnp.bitwise_xor(me_ref[0], source)
            piece = jnp.where(rel == 0, own_ref[...], landed_ref[jnp.maximum(rel, 1) - 1])
            total = piece if total is None else total + piece
        outs[0][...] = (0.5 / D_MODEL) * jnp.sum(jnp.sum(total[ROW_LOSS:ROW_LOSS + VECTOR_ROWS], axis=1, keepdims=True), axis=0, keepdims=True)
        row = lambda base, n: jnp.concatenate([total[base + r:base + r + 1, :] for r in range(n)], axis=1)
        grads = [row(ROW_G1, VECTOR_ROWS), row(ROW_G2, VECTOR_ROWS), row(ROW_G3, VECTOR_ROWS), row(ROW_PS, POOL_WIDTH // LANES),
                 total[ROW_SINK:ROW_SINK + 1, :N_Q_HEADS], total[ROW_WP:ROW_WP + POOL_ROWS].reshape(params[5][0].shape)]
        for k, g in enumerate(grads):
            w_ref, m_ref, v_ref = ins[3 * k:3 * k + 3]
            g_out, d_out, m_out, v_out = outs[1 + 4 * k:5 + 4 * k]
            g_out[...] = g
            d_out[...], m_out[...], v_out[...] = _adamw_math(w_ref[...], g, m_ref[...], v_ref[...])

    shapes = [jax.ShapeDtypeStruct((1, 1), F32)] + [jax.ShapeDtypeStruct(w.shape, F32) for w, _, _ in params for _ in range(4)]
    vmem = pl.BlockSpec(memory_space=pltpu.VMEM)
    res = pl.pallas_call(body, name="finish_small", in_specs=[pl.BlockSpec(memory_space=pltpu.SMEM)] + [vmem] * (2 + len(flat)),
                         out_shape=shapes)(me, own, landed, *flat)
    return res[0], [res[1 + 4 * k:5 + 4 * k] for k in range(len(params))]


def _local_step(x, target, attn_norm_g, w_in_full, attn_sinks, w_pool, pool_scale, mlp_norm_g, final_norm_g,
                front_token, out_weight, mlp_weights, ship_down, ship_up, ship_in):
    n_seq, seq, _ = x.shape
    rows = n_seq * seq
    tm, tm_mlp, tm_grad = min(512, seq), min(256, seq), min(1024, seq)
    x2d, t2d = x.reshape(rows, D_MODEL), target.reshape(rows, D_MODEL)
    g3 = final_norm_g.reshape(1, D_MODEL)
    cos, sin = _rope_tables(seq)
    wp_b = w_pool[0].astype(BF16)

    sink_rows = jnp.broadcast_to(attn_sinks.reshape(N_Q_HEADS, 1, 1), (N_Q_HEADS, 1, LANES))
    h1, q, kd, vd, u = _norm_inproj(x2d, attn_norm_g, w_in_full, cos, sin, front_token, seq, tm)
    attn = _attention_forward(sink_rows, q, kd, vd, n_seq, seq)
    pool = _pool_forward(u, wp_b, pool_scale, n_seq, seq)
    w_out_full = out_weight(pool)
    x2, h2 = _outproj_norm(x2d, attn, pool, w_out_full, mlp_norm_g, tm)
    w_up_full, w_down_full = mlp_weights(h2)
    slope, f, dx3, dx3b, loss_cols, dg3 = _mlp_forward_loss(h2, x2, w_up_full, w_down_full, g3, t2d, tm_mlp)

    down_token = ship_down(_weight_gradient("down_gradient", f, dx3b, True, tm_grad))
    da, dx2, dattn, dpool, dg2, d_w_out, d_w_out_wire = _mlp_backward_data(
        dx3b, slope, w_down_full, w_up_full, dx3, x2, mlp_norm_g, w_out_full, attn, pool, down_token, tm_mlp)
    up_token = ship_up((d_w_out, d_w_out_wire), _weight_gradient("up_gradient", h2, da, False, tm_grad))
    dq, dk, dv, dsink = _attention_backward(sink_rows, q, kd, vd, dattn, cos, sin, up_token, n_seq, seq)
    du, d_w_pool, d_pool_scale = _pool_backward(u, dpool, wp_b, pool_scale, n_seq, seq)
    in_token = ship_in(_inproj_gradient(dq, dk, dv, du, h1, tm_grad))
    grad_x, dg1 = _inproj_backward(dq, dk, dv, du, w_in_full, x2d, dx2, attn_norm_g, in_token, tm)
    return grad_x.reshape(x.shape), _pack_small(dg1, dg2, dg3, d_pool_scale, dsink, loss_cols, d_w_pool)


def kernel(x, attn_norm_g, w_in, attn_sinks, w_pool, pool_scale, w_out, mlp_norm_g, w_up, w_down, final_norm_g, loss_target, m_attn_norm_g, m_w_in, m_attn_sinks, m_w_pool, m_pool_scale, m_w_out, m_mlp_norm_g, m_w_up, m_w_down, m_final_norm_g, v_attn_norm_g, v_w_in, v_attn_sinks, v_w_pool, v_pool_scale, v_w_out, v_mlp_norm_g, v_w_up, v_w_down, v_final_norm_g):
    me = (4 * lax.axis_index("x") + 2 * lax.axis_index("y") + lax.axis_index("c")).astype(jnp.int32).reshape(1)

    unordered = jnp.zeros((8, LANES), F32)

    win_land, wout_land, wup_land, wdown_land = _stage_weights(
        [w_in[0], w_out[0], w_up[0], w_down[0]], [_row_slot, _row_slot, _column_slot, _row_slot],
        [(N_DEV, D_MODEL, IN_BLOCK), (N_DEV, OUT_BLOCK, D_MODEL), (D_MODEL, D_FF), (N_DEV, FF_BLOCK, D_MODEL)])
    out_plan, mlp_plan = _own_slot_to_all([_row_slot]), _own_slot_to_all([_column_slot, _row_slot])
    in_copies = _start_copies("spread_in_start", out_plan, [], [win_land], unordered)
    out_copies = _start_copies("spread_out_start", out_plan, [], [wout_land], in_copies[3])
    mlp_copies = _start_copies("spread_mlp_start", mlp_plan, [], [wup_land, wdown_land], out_copies[3])
    _, (win_g,) = _wait_copies("spread_in_wait", out_plan, 0, *in_copies[:3], mlp_copies[3])
    w_in_full = jnp.transpose(win_g, (1, 0, 2)).reshape(D_MODEL, IN_WIDTH)

    def out_weight(after):
        _, (wout_g,) = _wait_copies("spread_out_wait", out_plan, 0, *out_copies[:3], after)
        return wout_g.reshape(D_MODEL, D_MODEL)

    def mlp_weights(after):
        _, (wup_g, wdown_g) = _wait_copies("spread_mlp_wait", mlp_plan, 0, *mlp_copies[:3], after)
        return wup_g, wdown_g.reshape(D_FF, D_MODEL)

    deliveries, kept = {}, {}

    def deliver(name, plan, wires):
        lands = [lax.empty((N_DEV - 1,) + (g.shape[1:] if plan is _block_to_owner else g.shape), g.dtype) for g in wires]
        deliveries[name] = _start_copies(name + "_start", plan, wires, lands, unordered)
        return deliveries[name][3]

    def landed(name, plan, after):
        send, recv, flying, _ = deliveries[name]
        return _wait_copies(name + "_wait", plan, len(flying) // 2, send, recv, flying, after)

    def ship_down(d_w_down):
        kept["down"] = d_w_down[0]
        return deliver("deliver_down", _block_to_owner, [d_w_down[1]])

    def ship_up(d_w_out, d_w_up):
        kept["out"], kept["up"] = d_w_out[0].reshape(N_DEV, OUT_BLOCK, D_MODEL), d_w_up[0]
        return deliver("deliver_up", _block_to_owner, [d_w_out[1].reshape(N_DEV, OUT_BLOCK, D_MODEL), d_w_up[1]])

    def ship_in(d_w_in):
        kept["in"] = jnp.transpose(d_w_in.reshape(D_MODEL, N_DEV, IN_BLOCK), (1, 0, 2))
        return deliver("deliver_in", _block_to_owner, [kept["in"].astype(BF16)])

    grad_x, small = _local_step(x, loss_target, attn_norm_g, w_in_full, attn_sinks, w_pool, pool_scale, mlp_norm_g, final_norm_g,
                                mlp_copies[3], out_weight, mlp_weights, ship_down, ship_up, ship_in)
    small_token = deliver("deliver_small", _whole_to_all, [small])

    _, (got_down,) = landed("deliver_down", _block_to_owner, small_token)
    _, (got_out, got_up) = landed("deliver_up", _block_to_owner, small_token)
    g_down = _adamw_sharded(me, kept["down"], got_down, w_down[0], m_w_down[0], v_w_down[0], 256)
    g_up = _adamw_sharded(me, kept["up"], got_up, w_up[0], m_w_up[0], v_w_up[0], 256)
    g_out = _adamw_sharded(me, kept["out"], got_out, w_out[0], m_w_out[0], v_w_out[0], 128)
    _, (got_in,) = landed("deliver_in", _block_to_owner, g_out[0])
    g_in = _adamw_sharded(me, kept["in"], got_in, w_in[0], m_w_in[0], v_w_in[0], 1024)
    (own_small,), (got_small,) = landed("deliver_small", _whole_to_all, g_in[0])

    row = lambda a: a.reshape(1, D_MODEL)
    params = [(attn_norm_g, m_attn_norm_g, v_attn_norm_g), (mlp_norm_g, m_mlp_norm_g, v_mlp_norm_g),
              (row(final_norm_g), row(m_final_norm_g), row(v_final_norm_g)), (pool_scale, m_pool_scale, v_pool_scale),
              (attn_sinks, m_attn_sinks, v_attn_sinks), (w_pool[0], m_w_pool[0], v_w_pool[0])]
    loss, (s_norm1, s_norm2, s_norm3, s_scale, s_sinks, s_pool) = _finish_small(me, own_small, got_small, params)
    s_norm3 = [a.reshape(D_MODEL) for a in s_norm3]
    s_pool = [a[None] for a in s_pool]

    def ordered(k):
        return [s_norm1[k], g_in[k][None], s_sinks[k], s_pool[k], s_scale[k], g_out[k][None], s_norm2[k], g_up[k][None], g_down[k][None],
                s_norm3[k]]

    return (loss.reshape(()), grad_x, *ordered(0), *ordered(1), *ordered(2), *ordered(3))
```

```python
import functools

import jax
import jax.numpy as jnp
from jax import lax
from jax.experimental import pallas as pl
from jax.experimental.pallas import tpu as pltpu

F32 = jnp.float32
BF16 = jnp.bfloat16

D_MODEL = 1024
HEAD_DIM = 64
N_Q_HEADS = 8
Q_PER_KV = 4
ATTN_WIDTH = 512
KV_WIDTH = 128
BLOCK = 128
ROPE_THETA = 10000.0
POOL_WINDOWS = (2, 4, 8, 16)
POOL_WIDTH = 512
POOL_GROUP_DIM = 128
IN_WIDTH = 1280
D_FF = 4096
EPS = 1e-6
N_DEV = 8
FF_BLOCK = D_FF // N_DEV
IN_BLOCK = IN_WIDTH // N_DEV
OUT_BLOCK = D_MODEL // N_DEV
ADAM_LR = 0.001
ADAM_B1 = 0.9
ADAM_B2 = 0.999
ADAM_EPS = 1e-08
ADAM_WD = 0.01
ADAM_STEP = 10
NEG = -1e30
FORWARD_CHAINS = 4
BACKWARD_CHAINS = 2
LANES = 128
MIB = 1024 * 1024
MESH = pl.DeviceIdType.MESH

ROW_G1, ROW_G2, ROW_G3, ROW_PS, ROW_SINK, ROW_LOSS, ROW_WP, SMALL_ROWS = 0, 8, 16, 24, 32, 40, 48, 560


def _cparams(semantics, vmem_mib):
    return pltpu.CompilerParams(dimension_semantics=semantics, vmem_limit_bytes=vmem_mib * MIB)


def _dot(a, b):
    return jnp.dot(a, b, preferred_element_type=F32)


def _dot_nt(a, b):
    return lax.dot_general(a, b, (((1,), (1,)), ((), ())), preferred_element_type=F32)


def _dot_tn(a, b):
    return lax.dot_general(a, b, (((0,), (0,)), ((), ())), preferred_element_type=F32)


def _swap_halves(x):
    width = x.shape[1]
    lane = lax.broadcasted_iota(jnp.int32, x.shape, 1)
    ahead = pltpu.roll(x, width - HEAD_DIM // 2, 1)
    behind = pltpu.roll(x, HEAD_DIM // 2, 1)
    return jnp.where(lane % HEAD_DIM < HEAD_DIM // 2, ahead, behind)


def _rope(x, cos, sin):
    reps = x.shape[1] // LANES
    if reps > 1:
        cos = jnp.tile(cos, (1, reps))
        sin = jnp.tile(sin, (1, reps))
    return x * cos + _swap_halves(x) * sin


def _rope_tables(seq):
    half = HEAD_DIM // 2
    inv_freq = ROPE_THETA ** (-jnp.arange(half, dtype=F32) / half)
    ang = jnp.arange(seq).astype(F32)[:, None] * inv_freq[None, :]
    cos, sin = jnp.cos(ang), jnp.sin(ang)
    cos = jnp.tile(cos, (1, LANES // half))
    sin = jnp.tile(jnp.concatenate([-sin, sin], axis=1), (1, LANES // HEAD_DIM))
    return cos, sin


def _both_halves(x):
    lane = lax.broadcasted_iota(jnp.int32, x.shape, 1)
    other = pltpu.roll(x, HEAD_DIM, 1)
    low = lane < HEAD_DIM
    return jnp.where(low, x, other), jnp.where(low, other, x)


def _rms_backward(dh, xin, gain):
    r = lax.rsqrt(jnp.mean(xin * xin, axis=-1, keepdims=True) + EPS)
    xhat = xin * r
    dxhat = dh * gain
    dx = r * (dxhat - xhat * jnp.mean(dxhat * xhat, axis=-1, keepdims=True))
    return dx, jnp.sum(dh * xhat, axis=0, keepdims=True)


def _norm_inproj(x2d, gain, w_in, cos, sin, token, seq, tm):
    rows = x2d.shape[0]
    tiles_per_seq = seq // tm

    def body(x_ref, g_ref, w_ref, cos_ref, sin_ref, token_ref, h_ref, q_ref, k_ref, v_ref, u_ref):
        x = x_ref[...]
        r = lax.rsqrt(jnp.mean(x * x, axis=-1, keepdims=True) + EPS)
        h = (x * r * g_ref[...]).astype(BF16)
        h_ref[...] = h
        proj = _dot(h, w_ref[...])
        cos_t, sin_t = cos_ref[...], sin_ref[...]
        q = _rope(proj[:, :ATTN_WIDTH], cos_t, sin_t) * (HEAD_DIM ** -0.5)
        q_ref[...] = q.astype(BF16)
        k = _rope(proj[:, ATTN_WIDTH:ATTN_WIDTH + KV_WIDTH], cos_t, sin_t)
        k0, k1 = _both_halves(k)
        k_ref[...] = jnp.concatenate([k0, k1], axis=1).astype(BF16)
        v0, v1 = _both_halves(proj[:, ATTN_WIDTH + KV_WIDTH:ATTN_WIDTH + 2 * KV_WIDTH])
        v_ref[...] = jnp.concatenate([v0, v1], axis=1).astype(BF16)
        u_ref[...] = proj[:, ATTN_WIDTH + 2 * KV_WIDTH:]

    row = lambda width: pl.BlockSpec((tm, width), lambda i: (i, 0))
    table = pl.BlockSpec((tm, LANES), lambda i: (i % tiles_per_seq, 0))
    return pl.pallas_call(
        body, name="norm_inproj", grid=(rows // tm,),
        in_specs=[row(D_MODEL), pl.BlockSpec((1, D_MODEL), lambda i: (0, 0)),
                  pl.BlockSpec((D_MODEL, IN_WIDTH), lambda i: (0, 0)), table, table, pl.BlockSpec((8, LANES), lambda i: (0, 0))],
        out_specs=[row(D_MODEL), row(ATTN_WIDTH), row(2 * KV_WIDTH), row(2 * KV_WIDTH), row(POOL_WIDTH)],
        out_shape=[jax.ShapeDtypeStruct((rows, D_MODEL), BF16), jax.ShapeDtypeStruct((rows, ATTN_WIDTH), BF16),
                   jax.ShapeDtypeStruct((rows, 2 * KV_WIDTH), BF16), jax.ShapeDtypeStruct((rows, 2 * KV_WIDTH), BF16),
                   jax.ShapeDtypeStruct((rows, POOL_WIDTH), F32)],
        compiler_params=_cparams(("parallel",), 40),
    )(x2d, gain, w_in, cos, sin, token)


def _window_masks(n):
    qi = lax.broadcasted_iota(jnp.int32, (BLOCK, BLOCK), 0)
    kj = lax.broadcasted_iota(jnp.int32, (BLOCK, BLOCK), 1)
    return kj <= qi, jnp.logical_and(kj > qi, n > 0)


def _window_operand(ref, r0, p0, kv):
    low = lax.broadcasted_iota(jnp.int32, (BLOCK, LANES), 1) < HEAD_DIM
    cur = ref[pl.ds(r0, BLOCK), kv * LANES:(kv + 1) * LANES]
    prev = ref[pl.ds(p0, BLOCK), kv * LANES:(kv + 1) * LANES]
    zero = jnp.zeros_like(cur)
    return jnp.concatenate([jnp.where(low, cur, zero), jnp.where(low, zero, cur), jnp.where(low, prev, zero), jnp.where(low, zero, prev)], axis=0)


def _merged_window(wide, parity, cur_mask, prev_mask, fill):
    cur = wide[:, parity * LANES:(parity + 1) * LANES]
    prev = wide[:, (2 + parity) * LANES:(3 + parity) * LANES]
    return jnp.where(cur_mask, cur, jnp.where(prev_mask, prev, fill))


def _lane_sums(x):
    flat = x.reshape(-1, x.shape[-1])
    high = flat.astype(BF16)
    low = (flat - high.astype(F32)).astype(BF16)
    ones = jnp.ones((x.shape[-1], LANES), BF16)
    return (_dot(high, ones) + _dot(low, ones)).reshape(x.shape[:-1] + (LANES,))


def _softmax_with_sink(scores, sink):
    m = jnp.broadcast_to(jnp.maximum(jnp.max(scores, axis=-1, keepdims=True), sink), scores.shape)
    p, ps = jnp.exp(scores - m), jnp.exp(sink - m)
    inv = 1.0 / (_lane_sums(p) + ps)
    return p * inv, ps * inv


def _attention_forward(sinks, q, kd, vd, n_seq, seq):
    n_blocks = seq // BLOCK
    n_pairs = N_Q_HEADS // 2
    chains = min(FORWARD_CHAINS, n_blocks)

    def body(sink_ref, q_ref, k_ref, v_ref, o_ref, s_ref, p_ref):
        def step(i, carry):
            starts, values = [], []
            for u in range(chains):
                n = i * chains + u
                r0 = pl.multiple_of(n * BLOCK, BLOCK)
                p0 = pl.multiple_of(jnp.maximum(n - 1, 0) * BLOCK, BLOCK)
                cur_mask, prev_mask = _window_masks(n)
                keys = [_window_operand(k_ref, r0, p0, kv) for kv in range(2)]
                starts.append(r0)
                values.append([_window_operand(v_ref, r0, p0, kv) for kv in range(2)])
                for pair in range(n_pairs):
                    wide = _dot_nt(q_ref[pl.ds(r0, BLOCK), pair * LANES:(pair + 1) * LANES], keys[pair // 2])
                    for parity in range(2):
                        s_ref[u * N_Q_HEADS + 2 * pair + parity] = _merged_window(wide, parity, cur_mask, prev_mask, NEG)
            probs, _ = _softmax_with_sink(s_ref[...], jnp.tile(sink_ref[:, :, 0:1], (chains, 1, 1)))
            probs = probs.astype(BF16)
            zero = jnp.zeros((BLOCK, BLOCK), BF16)
            for u in range(chains):
                for pair in range(n_pairs):
                    for parity in range(2):
                        ph = probs[u * N_Q_HEADS + 2 * pair + parity]
                        p_ref[u, pair, :, parity * LANES:(parity + 1) * LANES] = jnp.where(cur_mask, ph, zero)
                        p_ref[u, pair, :, (2 + parity) * LANES:(3 + parity) * LANES] = jnp.where(cur_mask, zero, ph)
            for u in range(chains):
                for pair in range(n_pairs):
                    out = _dot(p_ref[u, pair], values[u][pair // 2])
                    o_ref[pl.ds(starts[u], BLOCK), pair * LANES:(pair + 1) * LANES] = out.astype(BF16)
            return carry

        lax.fori_loop(0, n_blocks // chains, step, 0)

    seq_block = lambda width: pl.BlockSpec((seq, width), lambda b: (b, 0))
    return pl.pallas_call(
        body, name="attention_forward", grid=(n_seq,),
        in_specs=[pl.BlockSpec((N_Q_HEADS, 1, LANES), lambda b: (0, 0, 0)), seq_block(ATTN_WIDTH), seq_block(2 * KV_WIDTH),
                  seq_block(2 * KV_WIDTH)],
        out_specs=seq_block(ATTN_WIDTH),
        out_shape=jax.ShapeDtypeStruct((n_seq * seq, ATTN_WIDTH), BF16),
        scratch_shapes=[pltpu.VMEM((chains * N_Q_HEADS, BLOCK, BLOCK), F32), pltpu.VMEM((chains, n_pairs, BLOCK, 4 * LANES), BF16)],
        compiler_params=_cparams(("parallel",), 40),
    )(sinks, q, kd, vd)


def _trailing(x, window, t, seq):
    k = 1
    while k < window:
        x = x + jnp.where(t >= k, pltpu.roll(x, k, 0), 0.0)
        k *= 2
    return x


def _leading(x, window, t, seq):
    k = 1
    while k < window:
        x = x + jnp.where(t < seq - k, pltpu.roll(x, seq - k, 0), 0.0)
        k *= 2
    return x


def _pool_features(u_g, window, t, seq):
    count = jnp.minimum(t + 1, window).astype(F32)
    return (_trailing(u_g, window, t, seq) / count - u_g).astype(BF16), count


def _pool_forward(u, w_pool, pool_scale, token, n_seq, seq):
    def body(u_ref, w_ref, s_ref, token_ref, o_ref):
        t = lax.broadcasted_iota(jnp.int32, (seq, 1), 0)
        for g, window in enumerate(POOL_WINDOWS):
            cols = slice(g * POOL_GROUP_DIM, (g + 1) * POOL_GROUP_DIM)
            d, _ = _pool_features(u_ref[:, cols], window, t, seq)
            o_ref[:, cols] = (_dot(d, w_ref[g]) * s_ref[:, cols]).astype(BF16)

    seq_block = pl.BlockSpec((seq, POOL_WIDTH), lambda b: (b, 0))
    return pl.pallas_call(
        body, name="pool_forward", grid=(n_seq,),
        in_specs=[seq_block, pl.BlockSpec((len(POOL_WINDOWS), POOL_GROUP_DIM, POOL_GROUP_DIM), lambda b: (0, 0, 0)),
                  pl.BlockSpec((1, POOL_WIDTH), lambda b: (0, 0)), pl.BlockSpec((8, LANES), lambda b: (0, 0))],
        out_specs=seq_block,
        out_shape=jax.ShapeDtypeStruct((n_seq * seq, POOL_WIDTH), BF16),
        compiler_params=_cparams(("parallel",), 40),
    )(u, w_pool, pool_scale, token)


def _outproj_norm(x2d, attn, pool, w_out, gain, tm):
    rows = x2d.shape[0]

    def body(x_ref, a_ref, p_ref, w_ref, g_ref, x2_ref, h_ref):
        x2 = x_ref[...] + _dot(a_ref[...], w_ref[:ATTN_WIDTH, :]) + _dot(p_ref[...], w_ref[ATTN_WIDTH:, :])
        x2_ref[...] = x2
        r = lax.rsqrt(jnp.mean(x2 * x2, axis=-1, keepdims=True) + EPS)
        h_ref[...] = (x2 * r * g_ref[...]).astype(BF16)

    row = lambda width: pl.BlockSpec((tm, width), lambda i: (i, 0))
    return pl.pallas_call(
        body, name="outproj_norm", grid=(rows // tm,),
        in_specs=[row(D_MODEL), row(ATTN_WIDTH), row(POOL_WIDTH), pl.BlockSpec((D_MODEL, D_MODEL), lambda i: (0, 0)),
                  pl.BlockSpec((1, D_MODEL), lambda i: (0, 0))],
        out_specs=[row(D_MODEL), row(D_MODEL)],
        out_shape=[jax.ShapeDtypeStruct((rows, D_MODEL), F32), jax.ShapeDtypeStruct((rows, D_MODEL), BF16)],
        compiler_params=_cparams(("parallel",), 40),
    )(x2d, attn, pool, w_out, gain)


def _resident(shape):
    return pl.BlockSpec(shape, lambda i: (0,) * len(shape), pipeline_mode=pl.Buffered(1))


def _mlp_forward_loss(h2, x2, w_up, w_down, gain, target, tm):
    rows = h2.shape[0]
    chunk = D_MODEL

    def body(h_ref, x_ref, up_ref, down_ref, g_ref, t_ref, slope_ref, f_ref, dx_ref, dxb_ref, loss_ref, dg_ref):
        @pl.when(pl.program_id(0) == 0)
        def _():
            loss_ref[...] = jnp.zeros_like(loss_ref)
            dg_ref[...] = jnp.zeros_like(dg_ref)

        h = h_ref[...]
        for c in range(D_FF // chunk):
            cols = slice(c * chunk, (c + 1) * chunk)
            r = jnp.maximum(_dot(h, up_ref[:, cols]), 0.0)
            slope_ref[:, cols] = (r + r).astype(BF16)
            f_ref[:, cols] = (r * r).astype(BF16)
        x3 = x_ref[...] + _dot(f_ref[...], down_ref[...])
        rn = lax.rsqrt(jnp.mean(x3 * x3, axis=-1, keepdims=True) + EPS)
        err = x3 * rn * g_ref[...] - t_ref[...]
        loss_ref[...] += jnp.sum(err * err, axis=0, keepdims=True)
        dx, dg = _rms_backward(err / D_MODEL, x3, g_ref[...])
        dg_ref[...] += dg
        dx_ref[...] = dx
        dxb_ref[...] = dx.astype(BF16)

    row = lambda width: pl.BlockSpec((tm, width), lambda i: (i, 0))
    vec = pl.BlockSpec((1, D_MODEL), lambda i: (0, 0))
    return pl.pallas_call(
        body, name="mlp_forward_loss", grid=(rows // tm,),
        in_specs=[row(D_MODEL), row(D_MODEL), _resident((D_MODEL, D_FF)), _resident((D_FF, D_MODEL)), vec, row(D_MODEL)],
        out_specs=[row(D_FF), row(D_FF), row(D_MODEL), row(D_MODEL), vec, vec],
        out_shape=[jax.ShapeDtypeStruct((rows, D_FF), BF16), jax.ShapeDtypeStruct((rows, D_FF), BF16),
                   jax.ShapeDtypeStruct((rows, D_MODEL), F32), jax.ShapeDtypeStruct((rows, D_MODEL), BF16),
                   jax.ShapeDtypeStruct((1, D_MODEL), F32), jax.ShapeDtypeStruct((1, D_MODEL), F32)],
        compiler_params=_cparams(("arbitrary",), 56),
    )(h2, x2, w_up, w_down, gain, target)


def _mlp_backward_data(dx3b, slope, w_down, w_up, dx3, x2, gain, w_out, attn, pool, token, tm):
    rows = dx3b.shape[0]
    steps = rows // tm
    chunk = D_MODEL

    def body(dxb_ref, slope_ref, down_ref, up_ref, dx3_ref, x2_ref, g_ref, wo_ref, attn_ref, pool_ref, token_ref,
             da_ref, dx2_ref, dattn_ref, dpool_ref, dg_ref, dwo_hbm, wire_hbm, dwo_acc, wire, sems):
        @pl.when(pl.program_id(0) == 0)
        def _():
            dg_ref[...] = jnp.zeros_like(dg_ref)
            dwo_acc[...] = jnp.zeros_like(dwo_acc)

        dxb = dxb_ref[...]
        for c in range(D_FF // chunk):
            cols = slice(c * chunk, (c + 1) * chunk)
            da_ref[:, cols] = (_dot_nt(dxb, down_ref[cols, :]) * slope_ref[:, cols].astype(F32)).astype(BF16)
        dnorm, dg = _rms_backward(_dot_nt(da_ref[...], up_ref[...]), x2_ref[...], g_ref[...])
        dg_ref[...] += dg
        dx2 = dx3_ref[...] + dnorm
        dx2_ref[...] = dx2
        dx2b = dx2.astype(BF16)
        dmix = _dot_nt(dx2b, wo_ref[...])
        dattn_ref[...] = dmix[:, :ATTN_WIDTH].astype(BF16)
        dpool_ref[...] = dmix[:, ATTN_WIDTH:]
        dwo_acc[:ATTN_WIDTH, :] += _dot_tn(attn_ref[...], dx2b)
        dwo_acc[ATTN_WIDTH:, :] += _dot_tn(pool_ref[...], dx2b)

        @pl.when(pl.program_id(0) == steps - 1)
        def _():
            done = pltpu.make_async_copy(dwo_acc, dwo_hbm, sems.at[0])
            done.start()
            wire[...] = dwo_acc[...].astype(BF16)
            sent = pltpu.make_async_copy(wire, wire_hbm, sems.at[1])
            sent.start()
            done.wait()
            sent.wait()

    row = lambda width: pl.BlockSpec((tm, width), lambda i: (i, 0))
    vec = pl.BlockSpec((1, D_MODEL), lambda i: (0, 0))
    return pl.pallas_call(
        body, name="mlp_backward_data", grid=(steps,),
        in_specs=[row(D_MODEL), row(D_FF), _resident((D_FF, D_MODEL)), _resident((D_MODEL, D_FF)), row(D_MODEL), row(D_MODEL), vec,
                  _resident((D_MODEL, D_MODEL)), row(ATTN_WIDTH), row(POOL_WIDTH), pl.BlockSpec((8, LANES), lambda i: (0, 0))],
        out_specs=[row(D_FF), row(D_MODEL), row(ATTN_WIDTH), row(POOL_WIDTH), vec, pl.BlockSpec(memory_space=pl.ANY),
                   pl.BlockSpec(memory_space=pl.ANY)],
        out_shape=[jax.ShapeDtypeStruct((rows, D_FF), BF16), jax.ShapeDtypeStruct((rows, D_MODEL), F32),
                   jax.ShapeDtypeStruct((rows, ATTN_WIDTH), BF16), jax.ShapeDtypeStruct((rows, POOL_WIDTH), F32),
                   jax.ShapeDtypeStruct((1, D_MODEL), F32), jax.ShapeDtypeStruct((D_MODEL, D_MODEL), F32),
                   jax.ShapeDtypeStruct((D_MODEL, D_MODEL), BF16)],
        scratch_shapes=[pltpu.VMEM((D_MODEL, D_MODEL), F32), pltpu.VMEM((D_MODEL, D_MODEL), BF16), pltpu.SemaphoreType.DMA((2,))],
        compiler_params=_cparams(("arbitrary",), 56),
    )(dx3b, slope, w_down, w_up, dx3, x2, gain, w_out, attn, pool, token)


def _weight_gradient(name, lhs, rhs, block_lhs, tm):
    rows = lhs.shape[0]
    steps = rows // tm
    out = (N_DEV, FF_BLOCK, rhs.shape[1]) if block_lhs else (N_DEV, lhs.shape[1], FF_BLOCK)

    def body(l_ref, r_ref, o_hbm, wire_hbm, acc, wire, sems):
        @pl.when(pl.program_id(0) == 0)
        def _():
            acc[...] = jnp.zeros_like(acc)

        for d in range(N_DEV):
            cols = slice(d * FF_BLOCK, (d + 1) * FF_BLOCK)
            acc[d] += _dot_tn(l_ref[:, cols], r_ref[...]) if block_lhs else _dot_tn(l_ref[...], r_ref[:, cols])

        @pl.when(pl.program_id(0) == steps - 1)
        def _():
            done = pltpu.make_async_copy(acc, o_hbm, sems.at[0])
            done.start()
            wire[...] = acc[...].astype(BF16)
            sent = pltpu.make_async_copy(wire, wire_hbm, sems.at[1])
            sent.start()
            done.wait()
            sent.wait()

    return pl.pallas_call(
        body, name=name, grid=(steps,),
        in_specs=[pl.BlockSpec((tm, lhs.shape[1]), lambda i: (i, 0)), pl.BlockSpec((tm, rhs.shape[1]), lambda i: (i, 0))],
        out_specs=[pl.BlockSpec(memory_space=pl.ANY), pl.BlockSpec(memory_space=pl.ANY)],
        out_shape=[jax.ShapeDtypeStruct(out, F32), jax.ShapeDtypeStruct(out, BF16)],
        scratch_shapes=[pltpu.VMEM(out, F32), pltpu.VMEM(out, BF16), pltpu.SemaphoreType.DMA((2,))],
        compiler_params=_cparams(("arbitrary",), 60),
    )(lhs, rhs)


def _attention_backward(sinks, q, kd, vd, dout, cos, sin, token, n_seq, seq):
    n_blocks = seq // BLOCK
    n_pairs = N_Q_HEADS // 2
    n_kv = N_Q_HEADS // Q_PER_KV
    chains = min(BACKWARD_CHAINS, n_blocks)

    def body(sink_ref, q_ref, k_ref, v_ref, do_ref, cos_ref, sin_ref, token_ref, dq_ref, dk_ref, dv_ref, dsink_ref,
             s_all, dp_all, dsc_all, dsp_all, pc_all, pp_all, dk_acc, dv_acc):
        low = lax.broadcasted_iota(jnp.int32, (BLOCK, LANES), 1) < HEAD_DIM

        @pl.when(pl.program_id(0) == 0)
        def _():
            dsink_ref[...] = jnp.zeros_like(dsink_ref)

        def fold(x):
            return x + pltpu.roll(x, HEAD_DIM, 1)

        def onto_keys(ref, kv, other):
            even, odd = _dot_tn(ref[2 * kv], other), _dot_tn(ref[2 * kv + 1], other)
            return fold(jnp.where(low, even, odd))

        def step(i, dsink):
            blocks = []
            for u in range(chains):
                n = i * chains + u
                r0 = pl.multiple_of(n * BLOCK, BLOCK)
                p0 = pl.multiple_of(jnp.maximum(n - 1, 0) * BLOCK, BLOCK)
                cur_mask, prev_mask = _window_masks(n)
                keys = [_window_operand(k_ref, r0, p0, kv) for kv in range(n_kv)]
                values = [_window_operand(v_ref, r0, p0, kv) for kv in range(n_kv)]
                q_pairs = [q_ref[pl.ds(r0, BLOCK), pair * LANES:(pair + 1) * LANES] for pair in range(n_pairs)]
                do_pairs = [do_ref[pl.ds(r0, BLOCK), pair * LANES:(pair + 1) * LANES] for pair in range(n_pairs)]
                for pair in range(n_pairs):
                    wide_s = _dot_nt(q_pairs[pair], keys[pair // 2])
                    wide_dp = _dot_nt(do_pairs[pair], values[pair // 2])
                    for parity in range(2):
                        s_all[u * N_Q_HEADS + 2 * pair + parity] = _merged_window(wide_s, parity, cur_mask, prev_mask, NEG)
                        dp_all[u * N_Q_HEADS + 2 * pair + parity] = _merged_window(wide_dp, parity, cur_mask, prev_mask, 0.0)
                blocks.append((n, r0, p0, keys, q_pairs, do_pairs))

            probs, p_sink = _softmax_with_sink(s_all[...], jnp.tile(sink_ref[:, :, 0:1], (chains, 1, 1)))
            dprobs = dp_all[...]
            delta = _lane_sums(probs * dprobs)
            dscores = (probs * (dprobs - delta)).astype(BF16)
            sink_terms = jnp.sum((p_sink * delta)[:, :, 0:1], axis=1, keepdims=True)
            probs = probs.astype(BF16)
            zero = jnp.zeros((BLOCK, BLOCK), BF16)
            for u in range(chains):
                dsink = dsink - sink_terms[u * N_Q_HEADS:(u + 1) * N_Q_HEADS]
                for head in range(N_Q_HEADS):
                    group, rows = 2 * (head // Q_PER_KV) + head % 2, pl.ds(((head % Q_PER_KV) // 2) * BLOCK, BLOCK)
                    ds_h, p_h = dscores[u * N_Q_HEADS + head], probs[u * N_Q_HEADS + head]
                    dsc_all[u, group, rows, :] = jnp.where(cur_mask, ds_h, zero)
                    dsp_all[u, group, rows, :] = jnp.where(cur_mask, zero, ds_h)
                    pc_all[u, group, rows, :] = jnp.where(cur_mask, p_h, zero)
                    pp_all[u, group, rows, :] = jnp.where(cur_mask, zero, p_h)

            for u, (n, r0, p0, keys, q_pairs, do_pairs) in enumerate(blocks):
                dsc_ref, dsp_ref, pc_ref, pp_ref = dsc_all.at[u], dsp_all.at[u], pc_all.at[u], pp_all.at[u]
                for pair in range(n_pairs):
                    kv, rows = pair // 2, pl.ds((pair % 2) * BLOCK, BLOCK)
                    wide = jnp.concatenate([dsc_ref[2 * kv, rows, :], dsc_ref[2 * kv + 1, rows, :],
                                            dsp_ref[2 * kv, rows, :], dsp_ref[2 * kv + 1, rows, :]], axis=1)
                    dq = _dot(wide, keys[kv]) * (HEAD_DIM ** -0.5)
                    dq = _rope(dq, cos_ref[pl.ds(r0, BLOCK), :], -sin_ref[pl.ds(r0, BLOCK), :])
                    dq_ref[pl.ds(r0, BLOCK), pair * LANES:(pair + 1) * LANES] = dq.astype(BF16)

                parts = []
                for kv in range(n_kv):
                    q_rows = jnp.concatenate([q_pairs[2 * kv], q_pairs[2 * kv + 1]], axis=0)
                    do_rows = jnp.concatenate([do_pairs[2 * kv], do_pairs[2 * kv + 1]], axis=0)
                    parts.append((onto_keys(dsc_ref, kv, q_rows), onto_keys(dsp_ref, kv, q_rows),
                                  onto_keys(pc_ref, kv, do_rows), onto_keys(pp_ref, kv, do_rows)))
                dk_acc[pl.ds(r0, BLOCK), :] = jnp.where(low, parts[0][0], parts[1][0])
                dv_acc[pl.ds(r0, BLOCK), :] = jnp.where(low, parts[0][2], parts[1][2])

                @pl.when(n > 0)
                def _():
                    dk_acc[pl.ds(p0, BLOCK), :] += jnp.where(low, parts[0][1], parts[1][1])
                    dv_acc[pl.ds(p0, BLOCK), :] += jnp.where(low, parts[0][3], parts[1][3])

            return dsink

        dsink = lax.fori_loop(0, n_blocks // chains, step, jnp.zeros((N_Q_HEADS, 1, 1), F32))
        dsink_ref[...] += jnp.broadcast_to(dsink, dsink_ref.shape)
        dk_ref[...] = _rope(dk_acc[...], cos_ref[...], -sin_ref[...]).astype(BF16)
        dv_ref[...] = dv_acc[...].astype(BF16)

    seq_block = lambda width: pl.BlockSpec((seq, width), lambda b: (b, 0))
    table = pl.BlockSpec((seq, LANES), lambda b: (0, 0))
    per_head = pl.BlockSpec((N_Q_HEADS, 1, LANES), lambda b: (0, 0, 0))
    per_block = pltpu.VMEM((chains * N_Q_HEADS, BLOCK, BLOCK), F32)
    grouped = pltpu.VMEM((chains, 2 * n_kv, 2 * BLOCK, BLOCK), BF16)
    return pl.pallas_call(
        body, name="attention_backward", grid=(n_seq,),
        in_specs=[per_head, seq_block(ATTN_WIDTH), seq_block(2 * KV_WIDTH), seq_block(2 * KV_WIDTH),
                  seq_block(ATTN_WIDTH), table, table, pl.BlockSpec((8, LANES), lambda b: (0, 0))],
        out_specs=[seq_block(ATTN_WIDTH), seq_block(KV_WIDTH), seq_block(KV_WIDTH), per_head],
        out_shape=[jax.ShapeDtypeStruct((n_seq * seq, ATTN_WIDTH), BF16), jax.ShapeDtypeStruct((n_seq * seq, KV_WIDTH), BF16),
                   jax.ShapeDtypeStruct((n_seq * seq, KV_WIDTH), BF16), jax.ShapeDtypeStruct((N_Q_HEADS, 1, LANES), F32)],
        scratch_shapes=[per_block, per_block, grouped, grouped, grouped, grouped, pltpu.VMEM((seq, KV_WIDTH), F32), pltpu.VMEM((seq, KV_WIDTH), F32)],
        compiler_params=_cparams(("arbitrary",), 40),
    )(sinks, q, kd, vd, dout, cos, sin, token)


def _pool_backward(u, dpool, w_pool, pool_scale, n_seq, seq):
    groups = len(POOL_WINDOWS)

    def body(u_ref, dp_ref, w_ref, s_ref, du_ref, dw_ref, ds_ref):
        @pl.when(pl.program_id(0) == 0)
        def _():
            dw_ref[...] = jnp.zeros_like(dw_ref)
            ds_ref[...] = jnp.zeros_like(ds_ref)

        t = lax.broadcasted_iota(jnp.int32, (seq, 1), 0)
        for g, window in enumerate(POOL_WINDOWS):
            cols = slice(g * POOL_GROUP_DIM, (g + 1) * POOL_GROUP_DIM)
            d, count = _pool_features(u_ref[:, cols], window, t, seq)
            dpool_g = dp_ref[:, cols]
            ds_ref[:, cols] += jnp.sum(dpool_g * _dot(d, w_ref[g]), axis=0, keepdims=True)
            dy = (dpool_g * s_ref[:, cols]).astype(BF16)
            dw_ref[g] += _dot_tn(d, dy)
            dd = _dot_nt(dy, w_ref[g])
            du_ref[:, cols] = (_leading(dd / count, window, t, seq) - dd).astype(BF16)

    seq_block = pl.BlockSpec((seq, POOL_WIDTH), lambda b: (b, 0))
    weights = pl.BlockSpec((groups, POOL_GROUP_DIM, POOL_GROUP_DIM), lambda b: (0, 0, 0))
    scale = pl.BlockSpec((1, POOL_WIDTH), lambda b: (0, 0))
    return pl.pallas_call(
        body, name="pool_backward", grid=(n_seq,),
        in_specs=[seq_block, seq_block, weights, scale],
        out_specs=[seq_block, weights, scale],
        out_shape=[jax.ShapeDtypeStruct((n_seq * seq, POOL_WIDTH), BF16),
                   jax.ShapeDtypeStruct((groups, POOL_GROUP_DIM, POOL_GROUP_DIM), F32), jax.ShapeDtypeStruct((1, POOL_WIDTH), F32)],
        compiler_params=_cparams(("arbitrary",), 40),
    )(u, dpool, w_pool, pool_scale)


def _inproj_gradient(dq, dk, dv, du, h1, tm):
    rows = h1.shape[0]

    def body(dq_ref, dk_ref, dv_ref, du_ref, h_ref, dw_ref):
        @pl.when(pl.program_id(0) == 0)
        def _():
            dw_ref[...] = jnp.zeros_like(dw_ref)

        dproj = jnp.concatenate([dq_ref[...], dk_ref[...], dv_ref[...], du_ref[...]], axis=1)
        dw_ref[...] += _dot_tn(h_ref[...], dproj)

    row = lambda width: pl.BlockSpec((tm, width), lambda i: (i, 0))
    return pl.pallas_call(
        body, name="inproj_gradient", grid=(rows // tm,),
        in_specs=[row(ATTN_WIDTH), row(KV_WIDTH), row(KV_WIDTH), row(POOL_WIDTH), row(D_MODEL)],
        out_specs=pl.BlockSpec((D_MODEL, IN_WIDTH), lambda i: (0, 0)),
        out_shape=jax.ShapeDtypeStruct((D_MODEL, IN_WIDTH), F32),
        compiler_params=_cparams(("arbitrary",), 48),
    )(dq, dk, dv, du, h1)


def _inproj_backward(dq, dk, dv, du, w_in, x2d, dx2, gain, token, tm):
    rows = x2d.shape[0]

    def body(dq_ref, dk_ref, dv_ref, du_ref, w_ref, x_ref, dx2_ref, g_ref, token_ref, dx_ref, dg_ref):
        @pl.when(pl.program_id(0) == 0)
        def _():
            dg_ref[...] = jnp.zeros_like(dg_ref)

        dproj = jnp.concatenate([dq_ref[...], dk_ref[...], dv_ref[...], du_ref[...]], axis=1)
        dnorm, dg = _rms_backward(_dot_nt(dproj, w_ref[...]), x_ref[...], g_ref[...])
        dg_ref[...] += dg
        dx_ref[...] = dx2_ref[...] + dnorm

    row = lambda width: pl.BlockSpec((tm, width), lambda i: (i, 0))
    vec = pl.BlockSpec((1, D_MODEL), lambda i: (0, 0))
    return pl.pallas_call(
        body, name="inproj_backward", grid=(rows // tm,),
        in_specs=[row(ATTN_WIDTH), row(KV_WIDTH), row(KV_WIDTH), row(POOL_WIDTH), pl.BlockSpec((D_MODEL, IN_WIDTH), lambda i: (0, 0)),
                  row(D_MODEL), row(D_MODEL), vec, pl.BlockSpec((8, LANES), lambda i: (0, 0))],
        out_specs=[row(D_MODEL), vec],
        out_shape=[jax.ShapeDtypeStruct((rows, D_MODEL), F32), jax.ShapeDtypeStruct((1, D_MODEL), F32)],
        compiler_params=_cparams(("arbitrary",), 48),
    )(dq, dk, dv, du, w_in, x2d, dx2, gain, token)


def _place():
    return lax.axis_index("x"), lax.axis_index("y"), lax.axis_index("c")


def _peer(x, y, c, rel):
    return (1 - x if rel & 4 else x, 1 - y if rel & 2 else y, 1 - c if rel & 1 else c)


def _index(px, py, pc):
    return 4 * px + 2 * py + pc


def _row_slot(ref, d):
    return ref.at[d]


def _column_slot(ref, d):
    return ref.at[:, pl.ds(pl.multiple_of(d * FF_BLOCK, FF_BLOCK), FF_BLOCK)]


def _stage_weights(shards, slots, shapes):
    n = len(shards)

    def body(*refs):
        ins, outs, stage, sems = refs[:n], refs[n:2 * n], refs[2 * n:3 * n], refs[3 * n]
        me = _index(*_place())
        mine = []
        for a in range(n):
            stage[a][...] = ins[a][...].astype(BF16)
            mine.append(pltpu.make_async_copy(stage[a], slots[a](outs[a], me), sems.at[a]))
            mine[-1].start()
        for cp in mine:
            cp.wait()

    return pl.pallas_call(
        body, name="stage_weights",
        in_specs=[pl.BlockSpec(memory_space=pltpu.VMEM)] * n,
        out_specs=[pl.BlockSpec(memory_space=pl.ANY)] * n,
        out_shape=[jax.ShapeDtypeStruct(shape, BF16) for shape in shapes],
        scratch_shapes=[pltpu.VMEM(s.shape, BF16) for s in shards] + [pltpu.SemaphoreType.DMA((n,))],
        compiler_params=pltpu.CompilerParams(vmem_limit_bytes=32 * MIB),
    )(*shards)


ALL_PEERS = tuple(range(1, N_DEV))
FIRST_HOP = (1, 2, 4, 6)
OTHER_CHIPS = (2, 4, 6)


class _Plan:
    def __init__(self, per_array, copies):
        self.per_array, self.copies = per_array, copies


def _own_slot_first_hop(slots):
    def copies(x, y, c, ins, lands):
        me = _index(x, y, c)
        return [(slots[a](lands[a], me), slots[a](lands[a], me), _peer(x, y, c, rel)) for rel in FIRST_HOP for a in range(len(lands))]
    return _Plan(len(FIRST_HOP), copies)


def _landed_to_sibling(slots):
    def copies(x, y, c, ins, lands):
        blocks = [_index(*_peer(x, y, c, rel)) for rel in OTHER_CHIPS]
        return [(slots[a](lands[a], b), slots[a](lands[a], b), (x, y, 1 - c)) for b in blocks for a in range(len(lands))]
    return _Plan(len(OTHER_CHIPS), copies)


def _whole_to_all_copies(x, y, c, ins, lands):
    return [(ins[a], lands[a].at[rel - 1], _peer(x, y, c, rel)) for rel in ALL_PEERS for a in range(len(lands))]


def _block_to_owner_copies(x, y, c, ins, lands):
    return [(ins[a].at[_index(*_peer(x, y, c, rel))], lands[a].at[rel - 1], _peer(x, y, c, rel))
            for rel in ALL_PEERS for a in range(len(lands))]


_whole_to_all = _Plan(len(ALL_PEERS), _whole_to_all_copies)
_block_to_owner = _Plan(len(ALL_PEERS), _block_to_owner_copies)


def _split_copies(plan, ins, lands, send_sems, recv_sems):
    return [pltpu.make_async_remote_copy(src_ref=src, dst_ref=dst, send_sem=send_sems.at[k], recv_sem=recv_sems.at[k],
                                         device_id=to, device_id_type=MESH)
            for k, (src, dst, to) in enumerate(plan.copies(*_place(), ins, lands))]


HBM_SPEC = pl.BlockSpec(memory_space=pltpu.HBM)
SEM_SPEC = pl.BlockSpec(memory_space=pltpu.SEMAPHORE)
EFFECT = pltpu.SideEffectType.DATAFLOW_SIDE_EFFECTING


def _start_copies(name, plan, ins, lands, after):
    n_in, n = len(ins), len(ins) + len(lands)

    def body(*refs):
        send_sems, recv_sems = refs[n + 1], refs[n + 2]
        for cp in _split_copies(plan, refs[:n_in], refs[n_in:n], send_sems, recv_sems):
            cp.start()
        refs[-1][...] = jnp.zeros_like(refs[-1])

    arrays = [pltpu.with_memory_space_constraint(v, pltpu.HBM) for v in (*ins, *lands)]
    sems = pltpu.SemaphoreType.DMA((len(lands) * plan.per_array,))
    send_sems, recv_sems, *flying, token = pl.pallas_call(
        body, name=name,
        out_shape=(sems, sems, *[pltpu.HBM(v.shape, v.dtype) for v in arrays], jax.ShapeDtypeStruct((8, LANES), F32)),
        in_specs=[HBM_SPEC] * n + [pl.BlockSpec(memory_space=pl.ANY)],
        out_specs=(SEM_SPEC, SEM_SPEC, *[HBM_SPEC] * n, pl.BlockSpec(memory_space=pltpu.VMEM)),
        input_output_aliases={i: 2 + i for i in range(n)},
        compiler_params=pltpu.CompilerParams(has_side_effects=EFFECT),
    )(*arrays, after)
    return send_sems, recv_sems, flying, token


def _wait_copies(name, plan, n_in, send_sems, recv_sems, flying, after):
    n = len(flying)

    def body(*refs):
        for cp in _split_copies(plan, refs[:n_in], refs[n_in:n], refs[n], refs[n + 1]):
            cp.wait_send()
            cp.wait_recv()

    landed = pl.pallas_call(
        body, name=name, out_shape=tuple(pltpu.HBM(v.shape, v.dtype) for v in flying),
        in_specs=[HBM_SPEC] * n + [SEM_SPEC, SEM_SPEC, pl.BlockSpec(memory_space=pl.ANY)], out_specs=tuple([HBM_SPEC] * n),
        input_output_aliases={i: i for i in range(n)},
        compiler_params=pltpu.CompilerParams(has_side_effects=EFFECT),
    )(*flying, send_sems, recv_sems, after)
    return landed[:n_in], landed[n_in:]


def _adamw_math(w, g, m, v):
    m = ADAM_B1 * m + (1.0 - ADAM_B1) * g
    v = ADAM_B2 * v + (1.0 - ADAM_B2) * (g * g)
    m_hat = m / (1.0 - ADAM_B1 ** ADAM_STEP)
    v_hat = v / (1.0 - ADAM_B2 ** ADAM_STEP)
    return -ADAM_LR * (m_hat / (jnp.sqrt(v_hat) + ADAM_EPS) + ADAM_WD * w), m, v


def _adamw_sharded(me, own, received, w, m, v, tr):
    rows, cols = w.shape

    def body(me_ref, own_ref, rec_ref, w_ref, m_ref, v_ref, g_ref, d_ref, nm_ref, nv_ref):
        g = own_ref[...]
        for r in range(N_DEV - 1):
            g = g + rec_ref[r].astype(F32)
        g_ref[...] = g
        d_ref[...], nm_ref[...], nv_ref[...] = _adamw_math(w_ref[...], g, m_ref[...], v_ref[...])

    tile = pl.BlockSpec((tr, cols), lambda i, me_ref: (i, 0))
    shape = jax.ShapeDtypeStruct((rows, cols), F32)
    return pl.pallas_call(
        body, name="adamw_sharded",
        grid_spec=pltpu.PrefetchScalarGridSpec(
            num_scalar_prefetch=1, grid=(rows // tr,),
            in_specs=[pl.BlockSpec((None, tr, cols), lambda i, me_ref: (me_ref[0], i, 0)),
                      pl.BlockSpec((N_DEV - 1, tr, cols), lambda i, me_ref: (0, i, 0)), tile, tile, tile],
            out_specs=[tile, tile, tile, tile]),
        out_shape=[shape, shape, shape, shape],
        compiler_params=_cparams(("parallel",), 40),
    )(me, own, received, w, m, v)


VECTOR_ROWS = D_MODEL // LANES
POOL_ROWS = len(POOL_WINDOWS) * POOL_GROUP_DIM


def _pack_small(dg1, dg2, dg3, dps, dsink, loss_cols, dwp):
    def body(g1_ref, g2_ref, g3_ref, ps_ref, sink_ref, loss_ref, wp_ref, o_ref):
        o_ref[...] = jnp.zeros_like(o_ref)
        for base, ref, n in ((ROW_G1, g1_ref, VECTOR_ROWS), (ROW_G2, g2_ref, VECTOR_ROWS), (ROW_G3, g3_ref, VECTOR_ROWS),
                             (ROW_LOSS, loss_ref, VECTOR_ROWS), (ROW_PS, ps_ref, POOL_WIDTH // LANES)):
            for r in range(n):
                o_ref[base + r:base + r + 1, :] = ref[:, r * LANES:(r + 1) * LANES]
        heads = sink_ref[:, 0, :]
        on_diagonal = lax.broadcasted_iota(jnp.int32, heads.shape, 0) == lax.broadcasted_iota(jnp.int32, heads.shape, 1)
        o_ref[ROW_SINK:ROW_SINK + 1, :] = jnp.sum(jnp.where(on_diagonal, heads, 0.0), axis=0, keepdims=True)
        o_ref[ROW_WP:ROW_WP + POOL_ROWS, :] = wp_ref[...].reshape(POOL_ROWS, LANES)

    return pl.pallas_call(body, name="pack_small", out_shape=jax.ShapeDtypeStruct((SMALL_ROWS, LANES), F32))(
        dg1, dg2, dg3, dps, dsink, loss_cols, dwp)


def _finish_small(me, own, landed, params):
    flat = [a for group in params for a in group]

    def body(me_ref, own_ref, landed_ref, *refs):
        ins, outs = refs[:len(flat)], refs[len(flat):]
        total = None
        for source in range(N_DEV):
            rel = jnp.bitwise_xor(me_ref[0], source)
            piece = jnp.where(rel == 0, own_ref[...], landed_ref[jnp.maximum(rel, 1) - 1])
            total = piece if total is None else total + piece
        outs[0][...] = (0.5 / D_MODEL) * jnp.sum(jnp.sum(total[ROW_LOSS:ROW_LOSS + VECTOR_ROWS], axis=1, keepdims=True), axis=0, keepdims=True)
        row = lambda base, n: jnp.concatenate([total[base + r:base + r + 1, :] for r in range(n)], axis=1)
        grads = [row(ROW_G1, VECTOR_ROWS), row(ROW_G2, VECTOR_ROWS), row(ROW_G3, VECTOR_ROWS), row(ROW_PS, POOL_WIDTH // LANES),
                 total[ROW_SINK:ROW_SINK + 1, :N_Q_HEADS], total[ROW_WP:ROW_WP + POOL_ROWS].reshape(params[5][0].shape)]
        for k, g in enumerate(grads):
            w_ref, m_ref, v_ref = ins[3 * k:3 * k + 3]
            g_out, d_out, m_out, v_out = outs[1 + 4 * k:5 + 4 * k]
            g_out[...] = g
            d_out[...], m_out[...], v_out[...] = _adamw_math(w_ref[...], g, m_ref[...], v_ref[...])

    shapes = [jax.ShapeDtypeStruct((1, 1), F32)] + [jax.ShapeDtypeStruct(w.shape, F32) for w, _, _ in params for _ in range(4)]
    vmem = pl.BlockSpec(memory_space=pltpu.VMEM)
    res = pl.pallas_call(body, name="finish_small", in_specs=[pl.BlockSpec(memory_space=pltpu.SMEM)] + [vmem] * (2 + len(flat)),
                         out_shape=shapes)(me, own, landed, *flat)
    return res[0], [res[1 + 4 * k:5 + 4 * k] for k in range(len(params))]


def _local_step(x, target, attn_norm_g, w_in_full, attn_sinks, w_pool, pool_scale, mlp_norm_g, final_norm_g,
                front_token, second_hop, later_weights, ship_down, ship_up, ship_in):
    n_seq, seq, _ = x.shape
    rows = n_seq * seq
    tm, tm_mlp, tm_grad = min(512, seq), min(256, seq), min(1024, seq)
    x2d, t2d = x.reshape(rows, D_MODEL), target.reshape(rows, D_MODEL)
    g3 = final_norm_g.reshape(1, D_MODEL)
    cos, sin = _rope_tables(seq)
    wp_b = w_pool[0].astype(BF16)

    sink_rows = jnp.broadcast_to(attn_sinks.reshape(N_Q_HEADS, 1, 1), (N_Q_HEADS, 1, LANES))
    h1, q, kd, vd, u = _norm_inproj(x2d, attn_norm_g, w_in_full, cos, sin, front_token, seq, tm)
    attn = _attention_forward(sink_rows, q, kd, vd, n_seq, seq)
    pool = _pool_forward(u, wp_b, pool_scale, second_hop(attn), n_seq, seq)
    w_out_full, w_up_full, w_down_full = later_weights(pool)
    x2, h2 = _outproj_norm(x2d, attn, pool, w_out_full, mlp_norm_g, tm)
    slope, f, dx3, dx3b, loss_cols, dg3 = _mlp_forward_loss(h2, x2, w_up_full, w_down_full, g3, t2d, tm_mlp)

    down_token = ship_down(_weight_gradient("down_gradient", f, dx3b, True, tm_grad))
    da, dx2, dattn, dpool, dg2, d_w_out, d_w_out_wire = _mlp_backward_data(
        dx3b, slope, w_down_full, w_up_full, dx3, x2, mlp_norm_g, w_out_full, attn, pool, down_token, tm_mlp)
    up_token = ship_up((d_w_out, d_w_out_wire), _weight_gradient("up_gradient", h2, da, False, tm_grad))
    dq, dk, dv, dsink = _attention_backward(sink_rows, q, kd, vd, dattn, cos, sin, up_token, n_seq, seq)
    du, d_w_pool, d_pool_scale = _pool_backward(u, dpool, wp_b, pool_scale, n_seq, seq)
    in_token = ship_in(_inproj_gradient(dq, dk, dv, du, h1, tm_grad))
    grad_x, dg1 = _inproj_backward(dq, dk, dv, du, w_in_full, x2d, dx2, attn_norm_g, in_token, tm)
    return grad_x.reshape(x.shape), _pack_small(dg1, dg2, dg3, d_pool_scale, dsink, loss_cols, d_w_pool)


def kernel(x, attn_norm_g, w_in, attn_sinks, w_pool, pool_scale, w_out, mlp_norm_g, w_up, w_down, final_norm_g, loss_target, m_attn_norm_g, m_w_in, m_attn_sinks, m_w_pool, m_pool_scale, m_w_out, m_mlp_norm_g, m_w_up, m_w_down, m_final_norm_g, v_attn_norm_g, v_w_in, v_attn_sinks, v_w_pool, v_pool_scale, v_w_out, v_mlp_norm_g, v_w_up, v_w_down, v_final_norm_g):
    me = (4 * lax.axis_index("x") + 2 * lax.axis_index("y") + lax.axis_index("c")).astype(jnp.int32).reshape(1)

    unordered = jnp.zeros((8, LANES), F32)

    win_land, wout_land, wup_land, wdown_land = _stage_weights(
        [w_in[0], w_out[0], w_up[0], w_down[0]], [_row_slot, _row_slot, _column_slot, _row_slot],
        [(N_DEV, D_MODEL, IN_BLOCK), (N_DEV, OUT_BLOCK, D_MODEL), (D_MODEL, D_FF), (N_DEV, FF_BLOCK, D_MODEL)])
    in_slots, later_slots = [_row_slot], [_row_slot, _column_slot, _row_slot]
    in_copies = _start_copies("spread_in_start", _own_slot_first_hop(in_slots), [], [win_land], unordered)
    later_copies = _start_copies("spread_later_start", _own_slot_first_hop(later_slots), [], [wout_land, wup_land, wdown_land], in_copies[3])
    _, in_landed = _wait_copies("spread_in_wait", _own_slot_first_hop(in_slots), 0, *in_copies[:3], later_copies[3])
    in_passed = _start_copies("pass_in_start", _landed_to_sibling(in_slots), [], in_landed, unordered)
    _, (win_g,) = _wait_copies("pass_in_wait", _landed_to_sibling(in_slots), 0, *in_passed[:3], in_passed[3])
    w_in_full = jnp.transpose(win_g, (1, 0, 2)).reshape(D_MODEL, IN_WIDTH)
    passing = []

    def second_hop(after):
        _, later_landed = _wait_copies("spread_later_wait", _own_slot_first_hop(later_slots), 0, *later_copies[:3], after)
        passing.extend(_start_copies("pass_later_start", _landed_to_sibling(later_slots), [], later_landed, unordered))
        return passing[3]

    def later_weights(after):
        _, (wout_g, wup_g, wdown_g) = _wait_copies("pass_later_wait", _landed_to_sibling(later_slots), 0, *passing[:3], after)
        return wout_g.reshape(D_MODEL, D_MODEL), wup_g, wdown_g.reshape(D_FF, D_MODEL)

    deliveries, kept = {}, {}

    def deliver(name, plan, wires):
        lands = [lax.empty((N_DEV - 1,) + (g.shape[1:] if plan is _block_to_owner else g.shape), g.dtype) for g in wires]
        deliveries[name] = _start_copies(name + "_start", plan, wires, lands, unordered)
        return deliveries[name][3]

    def landed(name, plan, after):
        send, recv, flying, _ = deliveries[name]
        return _wait_copies(name + "_wait", plan, len(flying) // 2, send, recv, flying, after)

    def ship_down(d_w_down):
        kept["down"] = d_w_down[0]
        return deliver("deliver_down", _block_to_owner, [d_w_down[1]])

    def ship_up(d_w_out, d_w_up):
        kept["out"], kept["up"] = d_w_out[0].reshape(N_DEV, OUT_BLOCK, D_MODEL), d_w_up[0]
        return deliver("deliver_up", _block_to_owner, [d_w_out[1].reshape(N_DEV, OUT_BLOCK, D_MODEL), d_w_up[1]])

    def ship_in(d_w_in):
        kept["in"] = jnp.transpose(d_w_in.reshape(D_MODEL, N_DEV, IN_BLOCK), (1, 0, 2))
        return deliver("deliver_in", _block_to_owner, [kept["in"].astype(BF16)])

    grad_x, small = _local_step(x, loss_target, attn_norm_g, w_in_full, attn_sinks, w_pool, pool_scale, mlp_norm_g, final_norm_g,
                                in_passed[3], second_hop, later_weights, ship_down, ship_up, ship_in)
    small_token = deliver("deliver_small", _whole_to_all, [small])

    _, (got_down,) = landed("deliver_down", _block_to_owner, small_token)
    _, (got_out, got_up) = landed("deliver_up", _block_to_owner, small_token)
    g_down = _adamw_sharded(me, kept["down"], got_down, w_down[0], m_w_down[0], v_w_down[0], 256)
    g_up = _adamw_sharded(me, kept["up"], got_up, w_up[0], m_w_up[0], v_w_up[0], 256)
    g_out = _adamw_sharded(me, kept["out"], got_out, w_out[0], m_w_out[0], v_w_out[0], 128)
    _, (got_in,) = landed("deliver_in", _block_to_owner, g_out[0])
    g_in = _adamw_sharded(me, kept["in"], got_in, w_in[0], m_w_in[0], v_w_in[0], 1024)
    (own_small,), (got_small,) = landed("deliver_small", _whole_to_all, g_in[0])

    row = lambda a: a.reshape(1, D_MODEL)
    params = [(attn_norm_g, m_attn_norm_g, v_attn_norm_g), (mlp_norm_g, m_mlp_norm_g, v_mlp_norm_g),
              (row(final_norm_g), row(m_final_norm_g), row(v_final_norm_g)), (pool_scale, m_pool_scale, v_pool_scale),
              (attn_sinks, m_attn_sinks, v_attn_sinks), (w_pool[0], m_w_pool[0], v_w_pool[0])]
    loss, (s_norm1, s_norm2, s_norm3, s_scale, s_sinks, s_pool) = _finish_small(me, own_small, got_small, params)
    s_norm3 = [a.reshape(D_MODEL) for a in s_norm3]
    s_pool = [a[None] for a in s_pool]

    def ordered(k):
        return [s_norm1[k], g_in[k][None], s_sinks[k], s_pool[k], s_scale[k], g_out[k][None], s_norm2[k], g_up[k][None], g_down[k][None],
                s_norm3[k]]

    return (loss.reshape(()), grad_x, *ordered(0), *ordered(1), *ordered(2), *ordered(3))
```

```python
import functools

import jax
import jax.numpy as jnp
from jax import lax
from jax.experimental import pallas as pl
from jax.experimental.pallas import tpu as pltpu

F32 = jnp.float32
BF16 = jnp.bfloat16

D_MODEL = 1024
HEAD_DIM = 64
N_Q_HEADS = 8
Q_PER_KV = 4
ATTN_WIDTH = 512
KV_WIDTH = 128
BLOCK = 128
ROPE_THETA = 10000.0
POOL_WINDOWS = (2, 4, 8, 16)
POOL_WIDTH = 512
POOL_GROUP_DIM = 128
IN_WIDTH = 1280
D_FF = 4096
EPS = 1e-6
N_DEV = 8
FF_BLOCK = D_FF // N_DEV
IN_BLOCK = IN_WIDTH // N_DEV
OUT_BLOCK = D_MODEL // N_DEV
ADAM_LR = 0.001
ADAM_B1 = 0.9
ADAM_B2 = 0.999
ADAM_EPS = 1e-08
ADAM_WD = 0.01
ADAM_STEP = 10
NEG = -1e30
FORWARD_CHAINS = 4
BACKWARD_CHAINS = 2
LANES = 128
MIB = 1024 * 1024
MESH = pl.DeviceIdType.MESH

ROW_G1, ROW_G2, ROW_G3, ROW_PS, ROW_SINK, ROW_LOSS, ROW_WP, SMALL_ROWS = 0, 8, 16, 24, 32, 40, 48, 560


def _cparams(semantics, vmem_mib):
    return pltpu.CompilerParams(dimension_semantics=semantics, vmem_limit_bytes=vmem_mib * MIB)


def _dot(a, b):
    return jnp.dot(a, b, preferred_element_type=F32)


def _dot_nt(a, b):
    return lax.dot_general(a, b, (((1,), (1,)), ((), ())), preferred_element_type=F32)


def _dot_tn(a, b):
    return lax.dot_general(a, b, (((0,), (0,)), ((), ())), preferred_element_type=F32)


def _swap_halves(x):
    width = x.shape[1]
    lane = lax.broadcasted_iota(jnp.int32, x.shape, 1)
    ahead = pltpu.roll(x, width - HEAD_DIM // 2, 1)
    behind = pltpu.roll(x, HEAD_DIM // 2, 1)
    return jnp.where(lane % HEAD_DIM < HEAD_DIM // 2, ahead, behind)


def _rope(x, cos, sin):
    reps = x.shape[1] // LANES
    if reps > 1:
        cos = jnp.tile(cos, (1, reps))
        sin = jnp.tile(sin, (1, reps))
    return x * cos + _swap_halves(x) * sin


def _rope_tables(seq):
    half = HEAD_DIM // 2
    inv_freq = ROPE_THETA ** (-jnp.arange(half, dtype=F32) / half)
    ang = jnp.arange(seq).astype(F32)[:, None] * inv_freq[None, :]
    cos, sin = jnp.cos(ang), jnp.sin(ang)
    cos = jnp.tile(cos, (1, LANES // half))
    sin = jnp.tile(jnp.concatenate([-sin, sin], axis=1), (1, LANES // HEAD_DIM))
    return cos, sin


def _both_halves(x):
    lane = lax.broadcasted_iota(jnp.int32, x.shape, 1)
    other = pltpu.roll(x, HEAD_DIM, 1)
    low = lane < HEAD_DIM
    return jnp.where(low, x, other), jnp.where(low, other, x)


def _rms_backward(dh, xin, gain):
    r = lax.rsqrt(jnp.mean(xin * xin, axis=-1, keepdims=True) + EPS)
    xhat = xin * r
    dxhat = dh * gain
    dx = r * (dxhat - xhat * jnp.mean(dxhat * xhat, axis=-1, keepdims=True))
    return dx, jnp.sum(dh * xhat, axis=0, keepdims=True)


def _first_norm(x2d, gain, token, tm):
    rows = x2d.shape[0]

    def body(x_ref, g_ref, token_ref, h_ref):
        x = x_ref[...]
        r = lax.rsqrt(jnp.mean(x * x, axis=-1, keepdims=True) + EPS)
        h_ref[...] = (x * r * g_ref[...]).astype(BF16)

    row = pl.BlockSpec((tm, D_MODEL), lambda i: (i, 0))
    return pl.pallas_call(
        body, name="first_norm", grid=(rows // tm,),
        in_specs=[row, pl.BlockSpec((1, D_MODEL), lambda i: (0, 0)), pl.BlockSpec((8, LANES), lambda i: (0, 0))],
        out_specs=row, out_shape=jax.ShapeDtypeStruct((rows, D_MODEL), BF16),
        compiler_params=_cparams(("parallel",), 40),
    )(x2d, gain, token)


def _inproj(h1, w_in, cos, sin, seq, tm):
    rows = h1.shape[0]
    tiles_per_seq = seq // tm

    def body(h_ref, w_ref, cos_ref, sin_ref, q_ref, k_ref, v_ref, u_ref):
        proj = _dot(h_ref[...], w_ref[...])
        cos_t, sin_t = cos_ref[...], sin_ref[...]
        q = _rope(proj[:, :ATTN_WIDTH], cos_t, sin_t) * (HEAD_DIM ** -0.5)
        q_ref[...] = q.astype(BF16)
        k = _rope(proj[:, ATTN_WIDTH:ATTN_WIDTH + KV_WIDTH], cos_t, sin_t)
        k0, k1 = _both_halves(k)
        k_ref[...] = jnp.concatenate([k0, k1], axis=1).astype(BF16)
        v0, v1 = _both_halves(proj[:, ATTN_WIDTH + KV_WIDTH:ATTN_WIDTH + 2 * KV_WIDTH])
        v_ref[...] = jnp.concatenate([v0, v1], axis=1).astype(BF16)
        u_ref[...] = proj[:, ATTN_WIDTH + 2 * KV_WIDTH:]

    row = lambda width: pl.BlockSpec((tm, width), lambda i: (i, 0))
    table = pl.BlockSpec((tm, LANES), lambda i: (i % tiles_per_seq, 0))
    return pl.pallas_call(
        body, name="inproj", grid=(rows // tm,),
        in_specs=[row(D_MODEL), pl.BlockSpec((D_MODEL, IN_WIDTH), lambda i: (0, 0)), table, table],
        out_specs=[row(ATTN_WIDTH), row(2 * KV_WIDTH), row(2 * KV_WIDTH), row(POOL_WIDTH)],
        out_shape=[jax.ShapeDtypeStruct((rows, ATTN_WIDTH), BF16), jax.ShapeDtypeStruct((rows, 2 * KV_WIDTH), BF16),
                   jax.ShapeDtypeStruct((rows, 2 * KV_WIDTH), BF16), jax.ShapeDtypeStruct((rows, POOL_WIDTH), F32)],
        compiler_params=_cparams(("parallel",), 40),
    )(h1, w_in, cos, sin)


def _window_masks(n):
    qi = lax.broadcasted_iota(jnp.int32, (BLOCK, BLOCK), 0)
    kj = lax.broadcasted_iota(jnp.int32, (BLOCK, BLOCK), 1)
    return kj <= qi, jnp.logical_and(kj > qi, n > 0)


def _window_operand(ref, r0, p0, kv):
    low = lax.broadcasted_iota(jnp.int32, (BLOCK, LANES), 1) < HEAD_DIM
    cur = ref[pl.ds(r0, BLOCK), kv * LANES:(kv + 1) * LANES]
    prev = ref[pl.ds(p0, BLOCK), kv * LANES:(kv + 1) * LANES]
    zero = jnp.zeros_like(cur)
    return jnp.concatenate([jnp.where(low, cur, zero), jnp.where(low, zero, cur), jnp.where(low, prev, zero), jnp.where(low, zero, prev)], axis=0)


def _pair_rows(ref, r0, kv):
    return jnp.concatenate([ref[pl.ds(r0, BLOCK), (2 * kv + j) * LANES:(2 * kv + j + 1) * LANES] for j in range(2)], axis=0)


def _merged_window(wide, parity, cur_mask, prev_mask, fill):
    cur = wide[:, parity * LANES:(parity + 1) * LANES]
    prev = wide[:, (2 + parity) * LANES:(3 + parity) * LANES]
    return jnp.where(cur_mask, cur, jnp.where(prev_mask, prev, fill))


def _lane_sums(x, one_matmul):
    flat = x.reshape(-1, x.shape[-1])
    high = flat.astype(BF16)
    low = (flat - high.astype(F32)).astype(BF16)
    if one_matmul:
        sums = _dot(jnp.concatenate([high, low], axis=1), jnp.ones((2 * x.shape[-1], LANES), BF16))
    else:
        ones = jnp.ones((x.shape[-1], LANES), BF16)
        sums = _dot(high, ones) + _dot(low, ones)
    return sums.reshape(x.shape[:-1] + (LANES,))


def _softmax_with_sink(scores, sink, one_matmul):
    m = jnp.broadcast_to(jnp.maximum(jnp.max(scores, axis=-1, keepdims=True), sink), scores.shape)
    p, ps = jnp.exp(scores - m), jnp.exp(sink - m)
    inv = 1.0 / (_lane_sums(p, one_matmul) + ps)
    return p * inv, ps * inv


def _attention_forward(sinks, q, kd, vd, n_seq, seq):
    n_blocks = seq // BLOCK
    n_pairs = N_Q_HEADS // 2
    chains = min(FORWARD_CHAINS, n_blocks)

    def body(sink_ref, q_ref, k_ref, v_ref, o_ref, s_ref, p_ref):
        def step(i, carry):
            starts, values = [], []
            for u in range(chains):
                n = i * chains + u
                r0 = pl.multiple_of(n * BLOCK, BLOCK)
                p0 = pl.multiple_of(jnp.maximum(n - 1, 0) * BLOCK, BLOCK)
                cur_mask, prev_mask = _window_masks(n)
                keys = [_window_operand(k_ref, r0, p0, kv) for kv in range(2)]
                starts.append(r0)
                values.append([_window_operand(v_ref, r0, p0, kv) for kv in range(2)])
                for kv in range(2):
                    both = _dot_nt(_pair_rows(q_ref, r0, kv), keys[kv])
                    for j in range(2):
                        wide = both[j * BLOCK:(j + 1) * BLOCK]
                        for parity in range(2):
                            s_ref[u * N_Q_HEADS + 4 * kv + 2 * j + parity] = _merged_window(wide, parity, cur_mask, prev_mask, NEG)
            probs, _ = _softmax_with_sink(s_ref[...], jnp.tile(sink_ref[:, :, 0:1], (chains, 1, 1)), False)
            probs = probs.astype(BF16)
            zero = jnp.zeros((BLOCK, BLOCK), BF16)
            for u in range(chains):
                for pair in range(n_pairs):
                    for parity in range(2):
                        ph = probs[u * N_Q_HEADS + 2 * pair + parity]
                        p_ref[u, pair, :, parity * LANES:(parity + 1) * LANES] = jnp.where(cur_mask, ph, zero)
                        p_ref[u, pair, :, (2 + parity) * LANES:(3 + parity) * LANES] = jnp.where(cur_mask, zero, ph)
            for u in range(chains):
                for kv in range(2):
                    both = _dot(p_ref[u, 2 * kv:2 * kv + 2].reshape(2 * BLOCK, 4 * LANES), values[u][kv])
                    for j in range(2):
                        pair = 2 * kv + j
                        o_ref[pl.ds(starts[u], BLOCK), pair * LANES:(pair + 1) * LANES] = both[j * BLOCK:(j + 1) * BLOCK].astype(BF16)
            return carry

        lax.fori_loop(0, n_blocks // chains, step, 0)

    seq_block = lambda width: pl.BlockSpec((seq, width), lambda b: (b, 0))
    return pl.pallas_call(
        body, name="attention_forward", grid=(n_seq,),
        in_specs=[pl.BlockSpec((N_Q_HEADS, 1, LANES), lambda b: (0, 0, 0)), seq_block(ATTN_WIDTH), seq_block(2 * KV_WIDTH),
                  seq_block(2 * KV_WIDTH)],
        out_specs=seq_block(ATTN_WIDTH),
        out_shape=jax.ShapeDtypeStruct((n_seq * seq, ATTN_WIDTH), BF16),
        scratch_shapes=[pltpu.VMEM((chains * N_Q_HEADS, BLOCK, BLOCK), F32), pltpu.VMEM((chains, n_pairs, BLOCK, 4 * LANES), BF16)],
        compiler_params=_cparams(("parallel",), 40),
    )(sinks, q, kd, vd)


def _trailing(x, window, t, seq):
    k = 1
    while k < window:
        x = x + jnp.where(t >= k, pltpu.roll(x, k, 0), 0.0)
        k *= 2
    return x


def _leading(x, window, t, seq):
    k = 1
    while k < window:
        x = x + jnp.where(t < seq - k, pltpu.roll(x, seq - k, 0), 0.0)
        k *= 2
    return x


def _pool_features(u_g, window, t, seq):
    count = jnp.minimum(t + 1, window).astype(F32)
    return (_trailing(u_g, window, t, seq) / count - u_g).astype(BF16), count


def _pool_forward(u, w_pool, pool_scale, token, n_seq, seq):
    def body(u_ref, w_ref, s_ref, token_ref, o_ref):
        t = lax.broadcasted_iota(jnp.int32, (seq, 1), 0)
        for g, window in enumerate(POOL_WINDOWS):
            cols = slice(g * POOL_GROUP_DIM, (g + 1) * POOL_GROUP_DIM)
            d, _ = _pool_features(u_ref[:, cols], window, t, seq)
            o_ref[:, cols] = (_dot(d, w_ref[g]) * s_ref[:, cols]).astype(BF16)

    seq_block = pl.BlockSpec((seq, POOL_WIDTH), lambda b: (b, 0))
    return pl.pallas_call(
        body, name="pool_forward", grid=(n_seq,),
        in_specs=[seq_block, pl.BlockSpec((len(POOL_WINDOWS), POOL_GROUP_DIM, POOL_GROUP_DIM), lambda b: (0, 0, 0)),
                  pl.BlockSpec((1, POOL_WIDTH), lambda b: (0, 0)), pl.BlockSpec((8, LANES), lambda b: (0, 0))],
        out_specs=seq_block,
        out_shape=jax.ShapeDtypeStruct((n_seq * seq, POOL_WIDTH), BF16),
        compiler_params=_cparams(("parallel",), 40),
    )(u, w_pool, pool_scale, token)


def _outproj_norm(x2d, attn, pool, w_out, gain, tm):
    rows = x2d.shape[0]

    def body(x_ref, a_ref, p_ref, w_ref, g_ref, x2_ref, h_ref):
        x2 = x_ref[...] + _dot(a_ref[...], w_ref[:ATTN_WIDTH, :]) + _dot(p_ref[...], w_ref[ATTN_WIDTH:, :])
        x2_ref[...] = x2
        r = lax.rsqrt(jnp.mean(x2 * x2, axis=-1, keepdims=True) + EPS)
        h_ref[...] = (x2 * r * g_ref[...]).astype(BF16)

    row = lambda width: pl.BlockSpec((tm, width), lambda i: (i, 0))
    return pl.pallas_call(
        body, name="outproj_norm", grid=(rows // tm,),
        in_specs=[row(D_MODEL), row(ATTN_WIDTH), row(POOL_WIDTH), pl.BlockSpec((D_MODEL, D_MODEL), lambda i: (0, 0)),
                  pl.BlockSpec((1, D_MODEL), lambda i: (0, 0))],
        out_specs=[row(D_MODEL), row(D_MODEL)],
        out_shape=[jax.ShapeDtypeStruct((rows, D_MODEL), F32), jax.ShapeDtypeStruct((rows, D_MODEL), BF16)],
        compiler_params=_cparams(("parallel",), 40),
    )(x2d, attn, pool, w_out, gain)


def _resident(shape):
    return pl.BlockSpec(shape, lambda i: (0,) * len(shape), pipeline_mode=pl.Buffered(1))


def _mlp_forward_loss(h2, x2, w_up, w_down, gain, target, tm):
    rows = h2.shape[0]
    chunk = D_MODEL

    def body(h_ref, x_ref, up_ref, down_ref, g_ref, t_ref, slope_ref, f_ref, dx_ref, dxb_ref, loss_ref, dg_ref):
        @pl.when(pl.program_id(0) == 0)
        def _():
            loss_ref[...] = jnp.zeros_like(loss_ref)
            dg_ref[...] = jnp.zeros_like(dg_ref)

        h = h_ref[...]
        for c in range(D_FF // chunk):
            cols = slice(c * chunk, (c + 1) * chunk)
            r = jnp.maximum(_dot(h, up_ref[:, cols]), 0.0)
            slope_ref[:, cols] = (r + r).astype(BF16)
            f_ref[:, cols] = (r * r).astype(BF16)
        x3 = x_ref[...] + _dot(f_ref[...], down_ref[...])
        rn = lax.rsqrt(jnp.mean(x3 * x3, axis=-1, keepdims=True) + EPS)
        xhat = x3 * rn
        err = xhat * g_ref[...] - t_ref[...]
        loss_ref[...] += jnp.sum(err * err, axis=0, keepdims=True)
        dxhat = err * (g_ref[...] * (1.0 / D_MODEL))
        dx = rn * (dxhat - xhat * jnp.mean(dxhat * xhat, axis=-1, keepdims=True))
        dg_ref[...] += jnp.sum(err * xhat, axis=0, keepdims=True) * (1.0 / D_MODEL)
        dx_ref[...] = dx
        dxb_ref[...] = dx.astype(BF16)

    row = lambda width: pl.BlockSpec((tm, width), lambda i: (i, 0))
    vec = pl.BlockSpec((1, D_MODEL), lambda i: (0, 0))
    return pl.pallas_call(
        body, name="mlp_forward_loss", grid=(rows // tm,),
        in_specs=[row(D_MODEL), row(D_MODEL), _resident((D_MODEL, D_FF)), _resident((D_FF, D_MODEL)), vec, row(D_MODEL)],
        out_specs=[row(D_FF), row(D_FF), row(D_MODEL), row(D_MODEL), vec, vec],
        out_shape=[jax.ShapeDtypeStruct((rows, D_FF), BF16), jax.ShapeDtypeStruct((rows, D_FF), BF16),
                   jax.ShapeDtypeStruct((rows, D_MODEL), F32), jax.ShapeDtypeStruct((rows, D_MODEL), BF16),
                   jax.ShapeDtypeStruct((1, D_MODEL), F32), jax.ShapeDtypeStruct((1, D_MODEL), F32)],
        compiler_params=_cparams(("arbitrary",), 56),
    )(h2, x2, w_up, w_down, gain, target)


def _mlp_backward_data(dx3b, slope, w_down, w_up, dx3, x2, gain, w_out, attn, pool, token, tm):
    rows = dx3b.shape[0]
    steps = rows // tm
    chunk = D_MODEL

    def body(dxb_ref, slope_ref, down_ref, up_ref, dx3_ref, x2_ref, g_ref, wo_ref, attn_ref, pool_ref, token_ref,
             da_ref, dx2_ref, dattn_ref, dpool_ref, dg_ref, dwo_hbm, wire_hbm, dwo_acc, wire, sems):
        @pl.when(pl.program_id(0) == 0)
        def _():
            dg_ref[...] = jnp.zeros_like(dg_ref)
            dwo_acc[...] = jnp.zeros_like(dwo_acc)

        dxb = dxb_ref[...]
        for c in range(D_FF // chunk):
            cols = slice(c * chunk, (c + 1) * chunk)
            da_ref[:, cols] = (_dot_nt(dxb, down_ref[cols, :]) * slope_ref[:, cols].astype(F32)).astype(BF16)
        dnorm, dg = _rms_backward(_dot_nt(da_ref[...], up_ref[...]), x2_ref[...], g_ref[...])
        dg_ref[...] += dg
        dx2 = dx3_ref[...] + dnorm
        dx2_ref[...] = dx2
        dx2b = dx2.astype(BF16)
        dmix = _dot_nt(dx2b, wo_ref[...])
        dattn_ref[...] = dmix[:, :ATTN_WIDTH].astype(BF16)
        dpool_ref[...] = dmix[:, ATTN_WIDTH:]
        dwo_acc[:ATTN_WIDTH, :] += _dot_tn(attn_ref[...], dx2b)
        dwo_acc[ATTN_WIDTH:, :] += _dot_tn(pool_ref[...], dx2b)

        @pl.when(pl.program_id(0) == steps - 1)
        def _():
            done = pltpu.make_async_copy(dwo_acc, dwo_hbm, sems.at[0])
            done.start()
            wire[...] = dwo_acc[...].astype(BF16)
            sent = pltpu.make_async_copy(wire, wire_hbm, sems.at[1])
            sent.start()
            done.wait()
            sent.wait()

    row = lambda width: pl.BlockSpec((tm, width), lambda i: (i, 0))
    vec = pl.BlockSpec((1, D_MODEL), lambda i: (0, 0))
    return pl.pallas_call(
        body, name="mlp_backward_data", grid=(steps,),
        in_specs=[row(D_MODEL), row(D_FF), _resident((D_FF, D_MODEL)), _resident((D_MODEL, D_FF)), row(D_MODEL), row(D_MODEL), vec,
                  _resident((D_MODEL, D_MODEL)), row(ATTN_WIDTH), row(POOL_WIDTH), pl.BlockSpec((8, LANES), lambda i: (0, 0))],
        out_specs=[row(D_FF), row(D_MODEL), row(ATTN_WIDTH), row(POOL_WIDTH), vec, pl.BlockSpec(memory_space=pl.ANY),
                   pl.BlockSpec(memory_space=pl.ANY)],
        out_shape=[jax.ShapeDtypeStruct((rows, D_FF), BF16), jax.ShapeDtypeStruct((rows, D_MODEL), F32),
                   jax.ShapeDtypeStruct((rows, ATTN_WIDTH), BF16), jax.ShapeDtypeStruct((rows, POOL_WIDTH), F32),
                   jax.ShapeDtypeStruct((1, D_MODEL), F32), jax.ShapeDtypeStruct((D_MODEL, D_MODEL), F32),
                   jax.ShapeDtypeStruct((D_MODEL, D_MODEL), BF16)],
        scratch_shapes=[pltpu.VMEM((D_MODEL, D_MODEL), F32), pltpu.VMEM((D_MODEL, D_MODEL), BF16), pltpu.SemaphoreType.DMA((2,))],
        compiler_params=_cparams(("arbitrary",), 56),
    )(dx3b, slope, w_down, w_up, dx3, x2, gain, w_out, attn, pool, token)


def _weight_gradient(name, lhs, rhs, block_lhs, tm):
    rows = lhs.shape[0]
    steps = rows // tm
    out = (N_DEV, FF_BLOCK, rhs.shape[1]) if block_lhs else (N_DEV, lhs.shape[1], FF_BLOCK)

    def body(l_ref, r_ref, o_hbm, wire_hbm, acc, wire, sems):
        @pl.when(pl.program_id(0) == 0)
        def _():
            acc[...] = jnp.zeros_like(acc)

        for d in range(N_DEV):
            cols = slice(d * FF_BLOCK, (d + 1) * FF_BLOCK)
            acc[d] += _dot_tn(l_ref[:, cols], r_ref[...]) if block_lhs else _dot_tn(l_ref[...], r_ref[:, cols])

        @pl.when(pl.program_id(0) == steps - 1)
        def _():
            done = pltpu.make_async_copy(acc, o_hbm, sems.at[0])
            done.start()
            wire[...] = acc[...].astype(BF16)
            sent = pltpu.make_async_copy(wire, wire_hbm, sems.at[1])
            sent.start()
            done.wait()
            sent.wait()

    return pl.pallas_call(
        body, name=name, grid=(steps,),
        in_specs=[pl.BlockSpec((tm, lhs.shape[1]), lambda i: (i, 0)), pl.BlockSpec((tm, rhs.shape[1]), lambda i: (i, 0))],
        out_specs=[pl.BlockSpec(memory_space=pl.ANY), pl.BlockSpec(memory_space=pl.ANY)],
        out_shape=[jax.ShapeDtypeStruct(out, F32), jax.ShapeDtypeStruct(out, BF16)],
        scratch_shapes=[pltpu.VMEM(out, F32), pltpu.VMEM(out, BF16), pltpu.SemaphoreType.DMA((2,))],
        compiler_params=_cparams(("arbitrary",), 60),
    )(lhs, rhs)


def _attention_backward(sinks, q, kd, vd, dout, cos, sin, token, n_seq, seq):
    n_blocks = seq // BLOCK
    n_pairs = N_Q_HEADS // 2
    n_kv = N_Q_HEADS // Q_PER_KV
    chains = min(BACKWARD_CHAINS, n_blocks)

    def body(sink_ref, q_ref, k_ref, v_ref, do_ref, cos_ref, sin_ref, token_ref, dq_ref, dk_ref, dv_ref, dsink_ref,
             s_all, dp_all, dsc_all, dsp_all, pc_all, pp_all, dk_acc, dv_acc):
        low = lax.broadcasted_iota(jnp.int32, (BLOCK, LANES), 1) < HEAD_DIM

        @pl.when(pl.program_id(0) == 0)
        def _():
            dsink_ref[...] = jnp.zeros_like(dsink_ref)

        def fold(x):
            return x + pltpu.roll(x, HEAD_DIM, 1)

        def onto_keys(ref, kv, other):
            even, odd = _dot_tn(ref[2 * kv], other), _dot_tn(ref[2 * kv + 1], other)
            return fold(jnp.where(low, even, odd))

        def step(i, dsink):
            blocks = []
            for u in range(chains):
                n = i * chains + u
                r0 = pl.multiple_of(n * BLOCK, BLOCK)
                p0 = pl.multiple_of(jnp.maximum(n - 1, 0) * BLOCK, BLOCK)
                cur_mask, prev_mask = _window_masks(n)
                keys = [_window_operand(k_ref, r0, p0, kv) for kv in range(n_kv)]
                values = [_window_operand(v_ref, r0, p0, kv) for kv in range(n_kv)]
                q_rows = [_pair_rows(q_ref, r0, kv) for kv in range(n_kv)]
                do_rows = [_pair_rows(do_ref, r0, kv) for kv in range(n_kv)]
                for kv in range(n_kv):
                    both_s, both_dp = _dot_nt(q_rows[kv], keys[kv]), _dot_nt(do_rows[kv], values[kv])
                    for j in range(2):
                        wide_s, wide_dp = both_s[j * BLOCK:(j + 1) * BLOCK], both_dp[j * BLOCK:(j + 1) * BLOCK]
                        for parity in range(2):
                            head = u * N_Q_HEADS + 4 * kv + 2 * j + parity
                            s_all[head] = _merged_window(wide_s, parity, cur_mask, prev_mask, NEG)
                            dp_all[head] = _merged_window(wide_dp, parity, cur_mask, prev_mask, 0.0)
                blocks.append((n, r0, p0, keys, q_rows, do_rows))

            probs, p_sink = _softmax_with_sink(s_all[...], jnp.tile(sink_ref[:, :, 0:1], (chains, 1, 1)), True)
            dprobs = dp_all[...]
            delta = _lane_sums(probs * dprobs, True)
            dscores = (probs * (dprobs - delta)).astype(BF16)
            sink_terms = jnp.sum((p_sink * delta)[:, :, 0:1], axis=1, keepdims=True)
            probs = probs.astype(BF16)
            zero = jnp.zeros((BLOCK, BLOCK), BF16)
            for u in range(chains):
                dsink = dsink - sink_terms[u * N_Q_HEADS:(u + 1) * N_Q_HEADS]
                for head in range(N_Q_HEADS):
                    group, rows = 2 * (head // Q_PER_KV) + head % 2, pl.ds(((head % Q_PER_KV) // 2) * BLOCK, BLOCK)
                    ds_h, p_h = dscores[u * N_Q_HEADS + head], probs[u * N_Q_HEADS + head]
                    dsc_all[u, group, rows, :] = jnp.where(cur_mask, ds_h, zero)
                    dsp_all[u, group, rows, :] = jnp.where(cur_mask, zero, ds_h)
                    pc_all[u, group, rows, :] = jnp.where(cur_mask, p_h, zero)
                    pp_all[u, group, rows, :] = jnp.where(cur_mask, zero, p_h)

            for u, (n, r0, p0, keys, q_rows, do_rows) in enumerate(blocks):
                dsc_ref, dsp_ref, pc_ref, pp_ref = dsc_all.at[u], dsp_all.at[u], pc_all.at[u], pp_all.at[u]
                for kv in range(n_kv):
                    wide = jnp.concatenate([dsc_ref[2 * kv], dsc_ref[2 * kv + 1], dsp_ref[2 * kv], dsp_ref[2 * kv + 1]], axis=1)
                    both = _dot(wide, keys[kv]) * (HEAD_DIM ** -0.5)
                    for j in range(2):
                        dq = _rope(both[j * BLOCK:(j + 1) * BLOCK], cos_ref[pl.ds(r0, BLOCK), :], -sin_ref[pl.ds(r0, BLOCK), :])
                        dq_ref[pl.ds(r0, BLOCK), (2 * kv + j) * LANES:(2 * kv + j + 1) * LANES] = dq.astype(BF16)

                parts = []
                for kv in range(n_kv):
                    parts.append((onto_keys(dsc_ref, kv, q_rows[kv]), onto_keys(dsp_ref, kv, q_rows[kv]),
                                  onto_keys(pc_ref, kv, do_rows[kv]), onto_keys(pp_ref, kv, do_rows[kv])))
                dk_acc[pl.ds(r0, BLOCK), :] = jnp.where(low, parts[0][0], parts[1][0])
                dv_acc[pl.ds(r0, BLOCK), :] = jnp.where(low, parts[0][2], parts[1][2])

                @pl.when(n > 0)
                def _():
                    dk_acc[pl.ds(p0, BLOCK), :] += jnp.where(low, parts[0][1], parts[1][1])
                    dv_acc[pl.ds(p0, BLOCK), :] += jnp.where(low, parts[0][3], parts[1][3])

            return dsink

        dsink = lax.fori_loop(0, n_blocks // chains, step, jnp.zeros((N_Q_HEADS, 1, 1), F32))
        dsink_ref[...] += jnp.broadcast_to(dsink, dsink_ref.shape)
        dk_ref[...] = _rope(dk_acc[...], cos_ref[...], -sin_ref[...]).astype(BF16)
        dv_ref[...] = dv_acc[...].astype(BF16)

    seq_block = lambda width: pl.BlockSpec((seq, width), lambda b: (b, 0))
    table = pl.BlockSpec((seq, LANES), lambda b: (0, 0))
    per_head = pl.BlockSpec((N_Q_HEADS, 1, LANES), lambda b: (0, 0, 0))
    per_block = pltpu.VMEM((chains * N_Q_HEADS, BLOCK, BLOCK), F32)
    grouped = pltpu.VMEM((chains, 2 * n_kv, 2 * BLOCK, BLOCK), BF16)
    return pl.pallas_call(
        body, name="attention_backward", grid=(n_seq,),
        in_specs=[per_head, seq_block(ATTN_WIDTH), seq_block(2 * KV_WIDTH), seq_block(2 * KV_WIDTH),
                  seq_block(ATTN_WIDTH), table, table, pl.BlockSpec((8, LANES), lambda b: (0, 0))],
        out_specs=[seq_block(ATTN_WIDTH), seq_block(KV_WIDTH), seq_block(KV_WIDTH), per_head],
        out_shape=[jax.ShapeDtypeStruct((n_seq * seq, ATTN_WIDTH), BF16), jax.ShapeDtypeStruct((n_seq * seq, KV_WIDTH), BF16),
                   jax.ShapeDtypeStruct((n_seq * seq, KV_WIDTH), BF16), jax.ShapeDtypeStruct((N_Q_HEADS, 1, LANES), F32)],
        scratch_shapes=[per_block, per_block, grouped, grouped, grouped, grouped, pltpu.VMEM((seq, KV_WIDTH), F32), pltpu.VMEM((seq, KV_WIDTH), F32)],
        compiler_params=_cparams(("arbitrary",), 40),
    )(sinks, q, kd, vd, dout, cos, sin, token)


def _pool_backward(u, dpool, w_pool, pool_scale, n_seq, seq):
    groups = len(POOL_WINDOWS)

    def body(u_ref, dp_ref, w_ref, s_ref, du_ref, dw_ref, ds_ref):
        @pl.when(pl.program_id(0) == 0)
        def _():
            dw_ref[...] = jnp.zeros_like(dw_ref)
            ds_ref[...] = jnp.zeros_like(ds_ref)

        t = lax.broadcasted_iota(jnp.int32, (seq, 1), 0)
        for g, window in enumerate(POOL_WINDOWS):
            cols = slice(g * POOL_GROUP_DIM, (g + 1) * POOL_GROUP_DIM)
            d, count = _pool_features(u_ref[:, cols], window, t, seq)
            dpool_g = dp_ref[:, cols]
            ds_ref[:, cols] += jnp.sum(dpool_g * _dot(d, w_ref[g]), axis=0, keepdims=True)
            dy = (dpool_g * s_ref[:, cols]).astype(BF16)
            dw_ref[g] += _dot_tn(d, dy)
            dd = _dot_nt(dy, w_ref[g])
            du_ref[:, cols] = (_leading(dd / count, window, t, seq) - dd).astype(BF16)

    seq_block = pl.BlockSpec((seq, POOL_WIDTH), lambda b: (b, 0))
    weights = pl.BlockSpec((groups, POOL_GROUP_DIM, POOL_GROUP_DIM), lambda b: (0, 0, 0))
    scale = pl.BlockSpec((1, POOL_WIDTH), lambda b: (0, 0))
    return pl.pallas_call(
        body, name="pool_backward", grid=(n_seq,),
        in_specs=[seq_block, seq_block, weights, scale],
        out_specs=[seq_block, weights, scale],
        out_shape=[jax.ShapeDtypeStruct((n_seq * seq, POOL_WIDTH), BF16),
                   jax.ShapeDtypeStruct((groups, POOL_GROUP_DIM, POOL_GROUP_DIM), F32), jax.ShapeDtypeStruct((1, POOL_WIDTH), F32)],
        compiler_params=_cparams(("arbitrary",), 40),
    )(u, dpool, w_pool, pool_scale)


def _inproj_gradient(dq, dk, dv, du, h1, tm):
    rows = h1.shape[0]

    def body(dq_ref, dk_ref, dv_ref, du_ref, h_ref, dw_ref):
        @pl.when(pl.program_id(0) == 0)
        def _():
            dw_ref[...] = jnp.zeros_like(dw_ref)

        dproj = jnp.concatenate([dq_ref[...], dk_ref[...], dv_ref[...], du_ref[...]], axis=1)
        dw_ref[...] += _dot_tn(h_ref[...], dproj)

    row = lambda width: pl.BlockSpec((tm, width), lambda i: (i, 0))
    return pl.pallas_call(
        body, name="inproj_gradient", grid=(rows // tm,),
        in_specs=[row(ATTN_WIDTH), row(KV_WIDTH), row(KV_WIDTH), row(POOL_WIDTH), row(D_MODEL)],
        out_specs=pl.BlockSpec((D_MODEL, IN_WIDTH), lambda i: (0, 0)),
        out_shape=jax.ShapeDtypeStruct((D_MODEL, IN_WIDTH), F32),
        compiler_params=_cparams(("arbitrary",), 48),
    )(dq, dk, dv, du, h1)


def _inproj_backward(dq, dk, dv, du, w_in, x2d, dx2, gain, token, tm):
    rows = x2d.shape[0]

    def body(dq_ref, dk_ref, dv_ref, du_ref, w_ref, x_ref, dx2_ref, g_ref, token_ref, dx_ref, dg_ref):
        @pl.when(pl.program_id(0) == 0)
        def _():
            dg_ref[...] = jnp.zeros_like(dg_ref)

        dproj = jnp.concatenate([dq_ref[...], dk_ref[...], dv_ref[...], du_ref[...]], axis=1)
        dnorm, dg = _rms_backward(_dot_nt(dproj, w_ref[...]), x_ref[...], g_ref[...])
        dg_ref[...] += dg
        dx_ref[...] = dx2_ref[...] + dnorm

    row = lambda width: pl.BlockSpec((tm, width), lambda i: (i, 0))
    vec = pl.BlockSpec((1, D_MODEL), lambda i: (0, 0))
    return pl.pallas_call(
        body, name="inproj_backward", grid=(rows // tm,),
        in_specs=[row(ATTN_WIDTH), row(KV_WIDTH), row(KV_WIDTH), row(POOL_WIDTH), pl.BlockSpec((D_MODEL, IN_WIDTH), lambda i: (0, 0)),
                  row(D_MODEL), row(D_MODEL), vec, pl.BlockSpec((8, LANES), lambda i: (0, 0))],
        out_specs=[row(D_MODEL), vec],
        out_shape=[jax.ShapeDtypeStruct((rows, D_MODEL), F32), jax.ShapeDtypeStruct((1, D_MODEL), F32)],
        compiler_params=_cparams(("arbitrary",), 48),
    )(dq, dk, dv, du, w_in, x2d, dx2, gain, token)


def _place():
    return lax.axis_index("x"), lax.axis_index("y"), lax.axis_index("c")


def _peer(x, y, c, rel):
    return (1 - x if rel & 4 else x, 1 - y if rel & 2 else y, 1 - c if rel & 1 else c)


def _index(px, py, pc):
    return 4 * px + 2 * py + pc


def _row_slot(ref, d):
    return ref.at[d]


def _column_slot(ref, d):
    return ref.at[:, pl.ds(pl.multiple_of(d * FF_BLOCK, FF_BLOCK), FF_BLOCK)]


def _stage_weights(shards, slots, shapes):
    n = len(shards)

    def body(*refs):
        ins, outs, stage, sems = refs[:n], refs[n:2 * n], refs[2 * n:3 * n], refs[3 * n]
        me = _index(*_place())
        mine = []
        for a in range(n):
            stage[a][...] = ins[a][...].astype(BF16)
            mine.append(pltpu.make_async_copy(stage[a], slots[a](outs[a], me), sems.at[a]))
            mine[-1].start()
        for cp in mine:
            cp.wait()

    return pl.pallas_call(
        body, name="stage_weights",
        in_specs=[pl.BlockSpec(memory_space=pltpu.VMEM)] * n,
        out_specs=[pl.BlockSpec(memory_space=pl.ANY)] * n,
        out_shape=[jax.ShapeDtypeStruct(shape, BF16) for shape in shapes],
        scratch_shapes=[pltpu.VMEM(s.shape, BF16) for s in shards] + [pltpu.SemaphoreType.DMA((n,))],
        compiler_params=pltpu.CompilerParams(vmem_limit_bytes=32 * MIB),
    )(*shards)


ALL_PEERS = tuple(range(1, N_DEV))
FIRST_HOP = (1, 2, 4, 6)
OTHER_CHIPS = (2, 4, 6)


class _Plan:
    def __init__(self, per_array, copies):
        self.per_array, self.copies = per_array, copies


def _own_slot_first_hop(slots):
    def copies(x, y, c, ins, lands):
        me = _index(x, y, c)
        return [(slots[a](lands[a], me), slots[a](lands[a], me), _peer(x, y, c, rel)) for rel in FIRST_HOP for a in range(len(lands))]
    return _Plan(len(FIRST_HOP), copies)


def _landed_to_sibling(slots):
    def copies(x, y, c, ins, lands):
        blocks = [_index(*_peer(x, y, c, rel)) for rel in OTHER_CHIPS]
        return [(slots[a](lands[a], b), slots[a](lands[a], b), (x, y, 1 - c)) for b in blocks for a in range(len(lands))]
    return _Plan(len(OTHER_CHIPS), copies)


def _whole_to_all_copies(x, y, c, ins, lands):
    return [(ins[a], lands[a].at[rel - 1], _peer(x, y, c, rel)) for rel in ALL_PEERS for a in range(len(lands))]


def _block_to_owner_copies(x, y, c, ins, lands):
    return [(ins[a].at[_index(*_peer(x, y, c, rel))], lands[a].at[rel - 1], _peer(x, y, c, rel))
            for rel in ALL_PEERS for a in range(len(lands))]


_whole_to_all = _Plan(len(ALL_PEERS), _whole_to_all_copies)
_block_to_owner = _Plan(len(ALL_PEERS), _block_to_owner_copies)


def _split_copies(plan, ins, lands, send_sems, recv_sems):
    return [pltpu.make_async_remote_copy(src_ref=src, dst_ref=dst, send_sem=send_sems.at[k], recv_sem=recv_sems.at[k],
                                         device_id=to, device_id_type=MESH)
            for k, (src, dst, to) in enumerate(plan.copies(*_place(), ins, lands))]


HBM_SPEC = pl.BlockSpec(memory_space=pltpu.HBM)
SEM_SPEC = pl.BlockSpec(memory_space=pltpu.SEMAPHORE)
EFFECT = pltpu.SideEffectType.DATAFLOW_SIDE_EFFECTING


def _start_copies(name, plan, ins, lands, after):
    n_in, n = len(ins), len(ins) + len(lands)

    def body(*refs):
        send_sems, recv_sems = refs[n + 1], refs[n + 2]
        for cp in _split_copies(plan, refs[:n_in], refs[n_in:n], send_sems, recv_sems):
            cp.start()
        refs[-1][...] = jnp.zeros_like(refs[-1])

    arrays = [pltpu.with_memory_space_constraint(v, pltpu.HBM) for v in (*ins, *lands)]
    sems = pltpu.SemaphoreType.DMA((len(lands) * plan.per_array,))
    send_sems, recv_sems, *flying, token = pl.pallas_call(
        body, name=name,
        out_shape=(sems, sems, *[pltpu.HBM(v.shape, v.dtype) for v in arrays], jax.ShapeDtypeStruct((8, LANES), F32)),
        in_specs=[HBM_SPEC] * n + [pl.BlockSpec(memory_space=pl.ANY)],
        out_specs=(SEM_SPEC, SEM_SPEC, *[HBM_SPEC] * n, pl.BlockSpec(memory_space=pltpu.VMEM)),
        input_output_aliases={i: 2 + i for i in range(n)},
        compiler_params=pltpu.CompilerParams(has_side_effects=EFFECT),
    )(*arrays, after)
    return send_sems, recv_sems, flying, token


def _wait_copies(name, plan, n_in, send_sems, recv_sems, flying, after):
    n = len(flying)

    def body(*refs):
        for cp in _split_copies(plan, refs[:n_in], refs[n_in:n], refs[n], refs[n + 1]):
            cp.wait_send()
            cp.wait_recv()

    landed = pl.pallas_call(
        body, name=name, out_shape=tuple(pltpu.HBM(v.shape, v.dtype) for v in flying),
        in_specs=[HBM_SPEC] * n + [SEM_SPEC, SEM_SPEC, pl.BlockSpec(memory_space=pl.ANY)], out_specs=tuple([HBM_SPEC] * n),
        input_output_aliases={i: i for i in range(n)},
        compiler_params=pltpu.CompilerParams(has_side_effects=EFFECT),
    )(*flying, send_sems, recv_sems, after)
    return landed[:n_in], landed[n_in:]


def _adamw_math(w, g, m, v):
    m = ADAM_B1 * m + (1.0 - ADAM_B1) * g
    v = ADAM_B2 * v + (1.0 - ADAM_B2) * (g * g)
    m_hat = m / (1.0 - ADAM_B1 ** ADAM_STEP)
    v_hat = v / (1.0 - ADAM_B2 ** ADAM_STEP)
    return -ADAM_LR * (m_hat / (jnp.sqrt(v_hat) + ADAM_EPS) + ADAM_WD * w), m, v


def _adamw_sharded(me, own, received, w, m, v, tr):
    rows, cols = w.shape

    def body(me_ref, own_ref, rec_ref, w_ref, m_ref, v_ref, g_ref, d_ref, nm_ref, nv_ref):
        g = own_ref[...]
        for r in range(N_DEV - 1):
            g = g + rec_ref[r].astype(F32)
        g_ref[...] = g
        d_ref[...], nm_ref[...], nv_ref[...] = _adamw_math(w_ref[...], g, m_ref[...], v_ref[...])

    tile = pl.BlockSpec((tr, cols), lambda i, me_ref: (i, 0))
    shape = jax.ShapeDtypeStruct((rows, cols), F32)
    return pl.pallas_call(
        body, name="adamw_sharded",
        grid_spec=pltpu.PrefetchScalarGridSpec(
            num_scalar_prefetch=1, grid=(rows // tr,),
            in_specs=[pl.BlockSpec((None, tr, cols), lambda i, me_ref: (me_ref[0], i, 0)),
                      pl.BlockSpec((N_DEV - 1, tr, cols), lambda i, me_ref: (0, i, 0)), tile, tile, tile],
            out_specs=[tile, tile, tile, tile]),
        out_shape=[shape, shape, shape, shape],
        compiler_params=_cparams(("parallel",), 40),
    )(me, own, received, w, m, v)


VECTOR_ROWS = D_MODEL // LANES
POOL_ROWS = len(POOL_WINDOWS) * POOL_GROUP_DIM


def _pack_small(dg1, dg2, dg3, dps, dsink, loss_cols, dwp):
    def body(g1_ref, g2_ref, g3_ref, ps_ref, sink_ref, loss_ref, wp_ref, o_ref):
        o_ref[...] = jnp.zeros_like(o_ref)
        for base, ref, n in ((ROW_G1, g1_ref, VECTOR_ROWS), (ROW_G2, g2_ref, VECTOR_ROWS), (ROW_G3, g3_ref, VECTOR_ROWS),
                             (ROW_LOSS, loss_ref, VECTOR_ROWS), (ROW_PS, ps_ref, POOL_WIDTH // LANES)):
            for r in range(n):
                o_ref[base + r:base + r + 1, :] = ref[:, r * LANES:(r + 1) * LANES]
        heads = sink_ref[:, 0, :]
        on_diagonal = lax.broadcasted_iota(jnp.int32, heads.shape, 0) == lax.broadcasted_iota(jnp.int32, heads.shape, 1)
        o_ref[ROW_SINK:ROW_SINK + 1, :] = jnp.sum(jnp.where(on_diagonal, heads, 0.0), axis=0, keepdims=True)
        o_ref[ROW_WP:ROW_WP + POOL_ROWS, :] = wp_ref[...].reshape(POOL_ROWS, LANES)

    return pl.pallas_call(body, name="pack_small", out_shape=jax.ShapeDtypeStruct((SMALL_ROWS, LANES), F32))(
        dg1, dg2, dg3, dps, dsink, loss_cols, dwp)


def _finish_small(me, own, landed, params):
    flat = [a for group in params for a in group]

    def body(me_ref, own_ref, landed_ref, *refs):
        ins, outs = refs[:len(flat)], refs[len(flat):]
        total = None
        for source in range(N_DEV):
            rel = jnp.bitwise_xor(me_ref[0], source)
            piece = jnp.where(rel == 0, own_ref[...], landed_ref[jnp.maximum(rel, 1) - 1])
            total = piece if total is None else total + piece
        outs[0][...] = (0.5 / D_MODEL) * jnp.sum(jnp.sum(total[ROW_LOSS:ROW_LOSS + VECTOR_ROWS], axis=1, keepdims=True), axis=0, keepdims=True)
        row = lambda base, n: jnp.concatenate([total[base + r:base + r + 1, :] for r in range(n)], axis=1)
        grads = [row(ROW_G1, VECTOR_ROWS), row(ROW_G2, VECTOR_ROWS), row(ROW_G3, VECTOR_ROWS), row(ROW_PS, POOL_WIDTH // LANES),
                 total[ROW_SINK:ROW_SINK + 1, :N_Q_HEADS], total[ROW_WP:ROW_WP + POOL_ROWS].reshape(params[5][0].shape)]
        for k, g in enumerate(grads):
            w_ref, m_ref, v_ref = ins[3 * k:3 * k + 3]
            g_out, d_out, m_out, v_out = outs[1 + 4 * k:5 + 4 * k]
            g_out[...] = g
            d_out[...], m_out[...], v_out[...] = _adamw_math(w_ref[...], g, m_ref[...], v_ref[...])

    shapes = [jax.ShapeDtypeStruct((1, 1), F32)] + [jax.ShapeDtypeStruct(w.shape, F32) for w, _, _ in params for _ in range(4)]
    vmem = pl.BlockSpec(memory_space=pltpu.VMEM)
    res = pl.pallas_call(body, name="finish_small", in_specs=[pl.BlockSpec(memory_space=pltpu.SMEM)] + [vmem] * (2 + len(flat)),
                         out_shape=shapes)(me, own, landed, *flat)
    return res[0], [res[1 + 4 * k:5 + 4 * k] for k in range(len(params))]


def _local_step(x, target, attn_norm_g, attn_sinks, w_pool, pool_scale, mlp_norm_g, final_norm_g,
                front_token, first_weight, second_hop, later_weights, ship_down, ship_up, ship_in):
    n_seq, seq, _ = x.shape
    rows = n_seq * seq
    tm, tm_mlp, tm_grad = min(512, seq), min(256, seq), min(1024, seq)
    x2d, t2d = x.reshape(rows, D_MODEL), target.reshape(rows, D_MODEL)
    g3 = final_norm_g.reshape(1, D_MODEL)
    cos, sin = _rope_tables(seq)
    wp_b = w_pool[0].astype(BF16)

    sink_rows = jnp.broadcast_to(attn_sinks.reshape(N_Q_HEADS, 1, 1), (N_Q_HEADS, 1, LANES))
    h1 = _first_norm(x2d, attn_norm_g, front_token, tm)
    w_in_full = first_weight(h1)
    q, kd, vd, u = _inproj(h1, w_in_full, cos, sin, seq, tm)
    attn = _attention_forward(sink_rows, q, kd, vd, n_seq, seq)
    pool = _pool_forward(u, wp_b, pool_scale, second_hop(attn), n_seq, seq)
    w_out_full, w_up_full, w_down_full = later_weights(pool)
    x2, h2 = _outproj_norm(x2d, attn, pool, w_out_full, mlp_norm_g, tm)
    slope, f, dx3, dx3b, loss_cols, dg3 = _mlp_forward_loss(h2, x2, w_up_full, w_down_full, g3, t2d, tm_mlp)

    down_token = ship_down(_weight_gradient("down_gradient", f, dx3b, True, tm_grad))
    da, dx2, dattn, dpool, dg2, d_w_out, d_w_out_wire = _mlp_backward_data(
        dx3b, slope, w_down_full, w_up_full, dx3, x2, mlp_norm_g, w_out_full, attn, pool, down_token, tm_mlp)
    up_token = ship_up((d_w_out, d_w_out_wire), _weight_gradient("up_gradient", h2, da, False, tm_grad))
    dq, dk, dv, dsink = _attention_backward(sink_rows, q, kd, vd, dattn, cos, sin, up_token, n_seq, seq)
    du, d_w_pool, d_pool_scale = _pool_backward(u, dpool, wp_b, pool_scale, n_seq, seq)
    in_token = ship_in(_inproj_gradient(dq, dk, dv, du, h1, tm_grad))
    grad_x, dg1 = _inproj_backward(dq, dk, dv, du, w_in_full, x2d, dx2, attn_norm_g, in_token, tm)
    return grad_x.reshape(x.shape), _pack_small(dg1, dg2, dg3, d_pool_scale, dsink, loss_cols, d_w_pool)


def kernel(x, attn_norm_g, w_in, attn_sinks, w_pool, pool_scale, w_out, mlp_norm_g, w_up, w_down, final_norm_g, loss_target, m_attn_norm_g, m_w_in, m_attn_sinks, m_w_pool, m_pool_scale, m_w_out, m_mlp_norm_g, m_w_up, m_w_down, m_final_norm_g, v_attn_norm_g, v_w_in, v_attn_sinks, v_w_pool, v_pool_scale, v_w_out, v_mlp_norm_g, v_w_up, v_w_down, v_final_norm_g):
    me = (4 * lax.axis_index("x") + 2 * lax.axis_index("y") + lax.axis_index("c")).astype(jnp.int32).reshape(1)

    unordered = jnp.zeros((8, LANES), F32)

    win_land, wout_land, wup_land, wdown_land = _stage_weights(
        [w_in[0], w_out[0], w_up[0], w_down[0]], [_row_slot, _row_slot, _column_slot, _row_slot],
        [(N_DEV, D_MODEL, IN_BLOCK), (N_DEV, OUT_BLOCK, D_MODEL), (D_MODEL, D_FF), (N_DEV, FF_BLOCK, D_MODEL)])
    in_slots, later_slots = [_row_slot], [_row_slot, _column_slot, _row_slot]
    in_copies = _start_copies("spread_in_start", _own_slot_first_hop(in_slots), [], [win_land], unordered)
    later_copies = _start_copies("spread_later_start", _own_slot_first_hop(later_slots), [], [wout_land, wup_land, wdown_land], in_copies[3])

    def first_weight(after):
        _, in_landed = _wait_copies("spread_in_wait", _own_slot_first_hop(in_slots), 0, *in_copies[:3], after)
        in_passed = _start_copies("pass_in_start", _landed_to_sibling(in_slots), [], in_landed, unordered)
        _, (win_g,) = _wait_copies("pass_in_wait", _landed_to_sibling(in_slots), 0, *in_passed[:3], in_passed[3])
        return jnp.transpose(win_g, (1, 0, 2)).reshape(D_MODEL, IN_WIDTH)

    passing = []

    def second_hop(after):
        _, later_landed = _wait_copies("spread_later_wait", _own_slot_first_hop(later_slots), 0, *later_copies[:3], after)
        passing.extend(_start_copies("pass_later_start", _landed_to_sibling(later_slots), [], later_landed, unordered))
        return passing[3]

    def later_weights(after):
        _, (wout_g, wup_g, wdown_g) = _wait_copies("pass_later_wait", _landed_to_sibling(later_slots), 0, *passing[:3], after)
        return wout_g.reshape(D_MODEL, D_MODEL), wup_g, wdown_g.reshape(D_FF, D_MODEL)

    deliveries, kept = {}, {}

    def deliver(name, plan, wires):
        lands = [lax.empty((N_DEV - 1,) + (g.shape[1:] if plan is _block_to_owner else g.shape), g.dtype) for g in wires]
        deliveries[name] = _start_copies(name + "_start", plan, wires, lands, unordered)
        return deliveries[name][3]

    def landed(name, plan, after):
        send, recv, flying, _ = deliveries[name]
        return _wait_copies(name + "_wait", plan, len(flying) // 2, send, recv, flying, after)

    def ship_down(d_w_down):
        kept["down"] = d_w_down[0]
        return deliver("deliver_down", _block_to_owner, [d_w_down[1]])

    def ship_up(d_w_out, d_w_up):
        kept["out"], kept["up"] = d_w_out[0].reshape(N_DEV, OUT_BLOCK, D_MODEL), d_w_up[0]
        return deliver("deliver_up", _block_to_owner, [d_w_out[1].reshape(N_DEV, OUT_BLOCK, D_MODEL), d_w_up[1]])

    def ship_in(d_w_in):
        kept["in"] = jnp.transpose(d_w_in.reshape(D_MODEL, N_DEV, IN_BLOCK), (1, 0, 2))
        return deliver("deliver_in", _block_to_owner, [kept["in"].astype(BF16)])

    grad_x, small = _local_step(x, loss_target, attn_norm_g, attn_sinks, w_pool, pool_scale, mlp_norm_g, final_norm_g,
                                later_copies[3], first_weight, second_hop, later_weights, ship_down, ship_up, ship_in)
    small_token = deliver("deliver_small", _whole_to_all, [small])

    _, (got_down,) = landed("deliver_down", _block_to_owner, small_token)
    _, (got_out, got_up) = landed("deliver_up", _block_to_owner, small_token)
    g_down = _adamw_sharded(me, kept["down"], got_down, w_down[0], m_w_down[0], v_w_down[0], 256)
    g_up = _adamw_sharded(me, kept["up"], got_up, w_up[0], m_w_up[0], v_w_up[0], 256)
    g_out = _adamw_sharded(me, kept["out"], got_out, w_out[0], m_w_out[0], v_w_out[0], 128)
    _, (got_in,) = landed("deliver_in", _block_to_owner, g_out[0])
    g_in = _adamw_sharded(me, kept["in"], got_in, w_in[0], m_w_in[0], v_w_in[0], 1024)
    (own_small,), (got_small,) = landed("deliver_small", _whole_to_all, g_in[0])

    row = lambda a: a.reshape(1, D_MODEL)
    params = [(attn_norm_g, m_attn_norm_g, v_attn_norm_g), (mlp_norm_g, m_mlp_norm_g, v_mlp_norm_g),
              (row(final_norm_g), row(m_final_norm_g), row(v_final_norm_g)), (pool_scale, m_pool_scale, v_pool_scale),
              (attn_sinks, m_attn_sinks, v_attn_sinks), (w_pool[0], m_w_pool[0], v_w_pool[0])]
    loss, (s_norm1, s_norm2, s_norm3, s_scale, s_sinks, s_pool) = _finish_small(me, own_small, got_small, params)
    s_norm3 = [a.reshape(D_MODEL) for a in s_norm3]
    s_pool = [a[None] for a in s_pool]

    def ordered(k):
        return [s_norm1[k], g_in[k][None], s_sinks[k], s_pool[k], s_scale[k], g_out[k][None], s_norm2[k], g_up[k][None], g_down[k][None],
                s_norm3[k]]

    return (loss.reshape(()), grad_x, *ordered(0), *ordered(1), *ordered(2), *ordered(3))
```

```python
import jax
import jax.numpy as jnp
from jax import lax
from jax.experimental import pallas as pl
from jax.experimental.pallas import tpu as pltpu

F32 = jnp.float32
BF16 = jnp.bfloat16

D_MODEL = 1024
HEAD_DIM = 64
N_Q_HEADS = 8
Q_PER_KV = 4
ATTN_WIDTH = 512
KV_WIDTH = 128
BLOCK = 128
ROPE_THETA = 10000.0
POOL_WINDOWS = (2, 4, 8, 16)
POOL_WIDTH = 512
POOL_GROUP_DIM = 128
IN_WIDTH = 1280
D_FF = 4096
EPS = 1e-6
N_DEV = 8
FF_BLOCK = D_FF // N_DEV
IN_BLOCK = IN_WIDTH // N_DEV
OUT_BLOCK = D_MODEL // N_DEV
ADAM_LR = 0.001
ADAM_B1 = 0.9
ADAM_B2 = 0.999
ADAM_EPS = 1e-08
ADAM_WD = 0.01
ADAM_STEP = 10
NEG = -1e30
FORWARD_CHAINS = 4
BACKWARD_CHAINS = 2
LANES = 128
MIB = 1024 * 1024
MESH = pl.DeviceIdType.MESH

ROW_G1, ROW_G2, ROW_G3, ROW_PS, ROW_SINK, ROW_LOSS, ROW_WP, SMALL_ROWS = 0, 8, 16, 24, 32, 40, 48, 560


def _cparams(semantics, vmem_mib):
    return pltpu.CompilerParams(dimension_semantics=semantics, vmem_limit_bytes=vmem_mib * MIB)


def _dot(a, b):
    return jnp.dot(a, b, preferred_element_type=F32)


def _dot_nt(a, b):
    return lax.dot_general(a, b, (((1,), (1,)), ((), ())), preferred_element_type=F32)


def _dot_tn(a, b):
    return lax.dot_general(a, b, (((0,), (0,)), ((), ())), preferred_element_type=F32)


def _swap_halves(x):
    width = x.shape[1]
    lane = lax.broadcasted_iota(jnp.int32, x.shape, 1)
    ahead = pltpu.roll(x, width - HEAD_DIM // 2, 1)
    behind = pltpu.roll(x, HEAD_DIM // 2, 1)
    return jnp.where(lane % HEAD_DIM < HEAD_DIM // 2, ahead, behind)


def _rope(x, cos, sin):
    reps = x.shape[1] // LANES
    if reps > 1:
        cos = jnp.tile(cos, (1, reps))
        sin = jnp.tile(sin, (1, reps))
    return x * cos + _swap_halves(x) * sin


def _rope_tables(seq):
    half = HEAD_DIM // 2
    inv_freq = ROPE_THETA ** (-jnp.arange(half, dtype=F32) / half)
    ang = jnp.arange(seq).astype(F32)[:, None] * inv_freq[None, :]
    cos, sin = jnp.cos(ang), jnp.sin(ang)
    cos = jnp.tile(cos, (1, LANES // half))
    sin = jnp.tile(jnp.concatenate([-sin, sin], axis=1), (1, LANES // HEAD_DIM))
    return cos, sin


def _both_halves(x):
    lane = lax.broadcasted_iota(jnp.int32, x.shape, 1)
    other = pltpu.roll(x, HEAD_DIM, 1)
    low = lane < HEAD_DIM
    return jnp.where(low, x, other), jnp.where(low, other, x)


def _rms_backward(dh, xin, gain):
    r = lax.rsqrt(jnp.mean(xin * xin, axis=-1, keepdims=True) + EPS)
    xhat = xin * r
    dxhat = dh * gain
    dx = r * (dxhat - xhat * jnp.mean(dxhat * xhat, axis=-1, keepdims=True))
    return dx, jnp.sum(dh * xhat, axis=0, keepdims=True)


def _first_norm(x2d, gain, token, tm):
    rows = x2d.shape[0]

    def body(x_ref, g_ref, token_ref, h_ref):
        x = x_ref[...]
        r = lax.rsqrt(jnp.mean(x * x, axis=-1, keepdims=True) + EPS)
        h_ref[...] = (x * r * g_ref[...]).astype(BF16)

    row = pl.BlockSpec((tm, D_MODEL), lambda i: (i, 0))
    return pl.pallas_call(
        body, name="first_norm", grid=(rows // tm,),
        in_specs=[row, pl.BlockSpec((1, D_MODEL), lambda i: (0, 0)), pl.BlockSpec((8, LANES), lambda i: (0, 0))],
        out_specs=row, out_shape=jax.ShapeDtypeStruct((rows, D_MODEL), BF16),
        compiler_params=_cparams(("parallel",), 40),
    )(x2d, gain, token)


def _inproj(h1, w_in, cos, sin, seq, tm):
    rows = h1.shape[0]
    tiles_per_seq = seq // tm

    def body(h_ref, w_ref, cos_ref, sin_ref, q_ref, k_ref, v_ref, u_ref):
        proj = _dot(h_ref[...], w_ref[...])
        cos_t, sin_t = cos_ref[...], sin_ref[...]
        q = _rope(proj[:, :ATTN_WIDTH], cos_t, sin_t) * (HEAD_DIM ** -0.5)
        q_ref[...] = q.astype(BF16)
        k = _rope(proj[:, ATTN_WIDTH:ATTN_WIDTH + KV_WIDTH], cos_t, sin_t)
        k0, k1 = _both_halves(k)
        k_ref[...] = jnp.concatenate([k0, k1], axis=1).astype(BF16)
        v0, v1 = _both_halves(proj[:, ATTN_WIDTH + KV_WIDTH:ATTN_WIDTH + 2 * KV_WIDTH])
        v_ref[...] = jnp.concatenate([v0, v1], axis=1).astype(BF16)
        u_ref[...] = proj[:, ATTN_WIDTH + 2 * KV_WIDTH:]

    row = lambda width: pl.BlockSpec((tm, width), lambda i: (i, 0))
    table = pl.BlockSpec((tm, LANES), lambda i: (i % tiles_per_seq, 0))
    return pl.pallas_call(
        body, name="inproj", grid=(rows // tm,),
        in_specs=[row(D_MODEL), pl.BlockSpec((D_MODEL, IN_WIDTH), lambda i: (0, 0)), table, table],
        out_specs=[row(ATTN_WIDTH), row(2 * KV_WIDTH), row(2 * KV_WIDTH), row(POOL_WIDTH)],
        out_shape=[jax.ShapeDtypeStruct((rows, ATTN_WIDTH), BF16), jax.ShapeDtypeStruct((rows, 2 * KV_WIDTH), BF16),
                   jax.ShapeDtypeStruct((rows, 2 * KV_WIDTH), BF16), jax.ShapeDtypeStruct((rows, POOL_WIDTH), F32)],
        compiler_params=_cparams(("parallel",), 40),
    )(h1, w_in, cos, sin)


def _window_masks(n):
    qi = lax.broadcasted_iota(jnp.int32, (BLOCK, BLOCK), 0)
    kj = lax.broadcasted_iota(jnp.int32, (BLOCK, BLOCK), 1)
    return kj <= qi, jnp.logical_and(kj > qi, n > 0)


def _window_operand(ref, r0, p0, kv):
    low = lax.broadcasted_iota(jnp.int32, (BLOCK, LANES), 1) < HEAD_DIM
    cur = ref[pl.ds(r0, BLOCK), kv * LANES:(kv + 1) * LANES]
    prev = ref[pl.ds(p0, BLOCK), kv * LANES:(kv + 1) * LANES]
    zero = jnp.zeros_like(cur)
    return jnp.concatenate([jnp.where(low, cur, zero), jnp.where(low, zero, cur), jnp.where(low, prev, zero), jnp.where(low, zero, prev)], axis=0)


def _pair_rows(ref, r0, kv):
    return jnp.concatenate([ref[pl.ds(r0, BLOCK), (2 * kv + j) * LANES:(2 * kv + j + 1) * LANES] for j in range(2)], axis=0)


def _merged_window(wide, parity, cur_mask, prev_mask, fill):
    cur = wide[:, parity * LANES:(parity + 1) * LANES]
    prev = wide[:, (2 + parity) * LANES:(3 + parity) * LANES]
    return jnp.where(cur_mask, cur, jnp.where(prev_mask, prev, fill))


def _lane_sums(x, one_matmul):
    flat = x.reshape(-1, x.shape[-1])
    high = flat.astype(BF16)
    low = (flat - high.astype(F32)).astype(BF16)
    if one_matmul:
        sums = _dot(jnp.concatenate([high, low], axis=1), jnp.ones((2 * x.shape[-1], LANES), BF16))
    else:
        ones = jnp.ones((x.shape[-1], LANES), BF16)
        sums = _dot(high, ones) + _dot(low, ones)
    return sums.reshape(x.shape[:-1] + (LANES,))


def _softmax_with_sink(scores, sink, one_matmul):
    m = jnp.broadcast_to(jnp.maximum(jnp.max(scores, axis=-1, keepdims=True), sink), scores.shape)
    p, ps = jnp.exp(scores - m), jnp.exp(sink - m)
    inv = 1.0 / (_lane_sums(p, one_matmul) + ps)
    return p * inv, ps * inv


def _attention_forward(sinks, q, kd, vd, n_seq, seq):
    n_blocks = seq // BLOCK
    n_pairs = N_Q_HEADS // 2
    chains = min(FORWARD_CHAINS, n_blocks)

    def body(sink_ref, q_ref, k_ref, v_ref, o_ref, s_ref, p_ref):
        def step(i, carry):
            starts, values = [], []
            for u in range(chains):
                n = i * chains + u
                r0 = pl.multiple_of(n * BLOCK, BLOCK)
                p0 = pl.multiple_of(jnp.maximum(n - 1, 0) * BLOCK, BLOCK)
                cur_mask, prev_mask = _window_masks(n)
                keys = [_window_operand(k_ref, r0, p0, kv) for kv in range(2)]
                starts.append(r0)
                values.append([_window_operand(v_ref, r0, p0, kv) for kv in range(2)])
                for kv in range(2):
                    both = _dot_nt(_pair_rows(q_ref, r0, kv), keys[kv])
                    for j in range(2):
                        wide = both[j * BLOCK:(j + 1) * BLOCK]
                        for parity in range(2):
                            s_ref[u * N_Q_HEADS + 4 * kv + 2 * j + parity] = _merged_window(wide, parity, cur_mask, prev_mask, NEG)
            probs, _ = _softmax_with_sink(s_ref[...], jnp.tile(sink_ref[:, :, 0:1], (chains, 1, 1)), False)
            probs = probs.astype(BF16)
            zero = jnp.zeros((BLOCK, BLOCK), BF16)
            for u in range(chains):
                for pair in range(n_pairs):
                    for parity in range(2):
                        ph = probs[u * N_Q_HEADS + 2 * pair + parity]
                        p_ref[u, pair, :, parity * LANES:(parity + 1) * LANES] = jnp.where(cur_mask, ph, zero)
                        p_ref[u, pair, :, (2 + parity) * LANES:(3 + parity) * LANES] = jnp.where(cur_mask, zero, ph)
            for u in range(chains):
                for kv in range(2):
                    both = _dot(p_ref[u, 2 * kv:2 * kv + 2].reshape(2 * BLOCK, 4 * LANES), values[u][kv])
                    for j in range(2):
                        pair = 2 * kv + j
                        o_ref[pl.ds(starts[u], BLOCK), pair * LANES:(pair + 1) * LANES] = both[j * BLOCK:(j + 1) * BLOCK].astype(BF16)
            return carry

        lax.fori_loop(0, n_blocks // chains, step, 0)

    seq_block = lambda width: pl.BlockSpec((seq, width), lambda b: (b, 0))
    return pl.pallas_call(
        body, name="attention_forward", grid=(n_seq,),
        in_specs=[pl.BlockSpec((N_Q_HEADS, 1, LANES), lambda b: (0, 0, 0)), seq_block(ATTN_WIDTH), seq_block(2 * KV_WIDTH),
                  seq_block(2 * KV_WIDTH)],
        out_specs=seq_block(ATTN_WIDTH),
        out_shape=jax.ShapeDtypeStruct((n_seq * seq, ATTN_WIDTH), BF16),
        scratch_shapes=[pltpu.VMEM((chains * N_Q_HEADS, BLOCK, BLOCK), F32), pltpu.VMEM((chains, n_pairs, BLOCK, 4 * LANES), BF16)],
        compiler_params=_cparams(("parallel",), 40),
    )(sinks, q, kd, vd)


def _trailing(x, window, t, seq):
    k = 1
    while k < window:
        x = x + jnp.where(t >= k, pltpu.roll(x, k, 0), 0.0)
        k *= 2
    return x


def _leading(x, window, t, seq):
    k = 1
    while k < window:
        x = x + jnp.where(t < seq - k, pltpu.roll(x, seq - k, 0), 0.0)
        k *= 2
    return x


def _pool_features(u_g, window, t, seq):
    count = jnp.minimum(t + 1, window).astype(F32)
    return (_trailing(u_g, window, t, seq) / count - u_g).astype(BF16), count


def _pool_forward(u, w_pool, pool_scale, token, n_seq, seq):
    def body(u_ref, w_ref, s_ref, token_ref, o_ref):
        t = lax.broadcasted_iota(jnp.int32, (seq, 1), 0)
        for g, window in enumerate(POOL_WINDOWS):
            cols = slice(g * POOL_GROUP_DIM, (g + 1) * POOL_GROUP_DIM)
            d, _ = _pool_features(u_ref[:, cols], window, t, seq)
            o_ref[:, cols] = (_dot(d, w_ref[g]) * s_ref[:, cols]).astype(BF16)

    seq_block = pl.BlockSpec((seq, POOL_WIDTH), lambda b: (b, 0))
    return pl.pallas_call(
        body, name="pool_forward", grid=(n_seq,),
        in_specs=[seq_block, pl.BlockSpec((len(POOL_WINDOWS), POOL_GROUP_DIM, POOL_GROUP_DIM), lambda b: (0, 0, 0)),
                  pl.BlockSpec((1, POOL_WIDTH), lambda b: (0, 0)), pl.BlockSpec((8, LANES), lambda b: (0, 0))],
        out_specs=seq_block,
        out_shape=jax.ShapeDtypeStruct((n_seq * seq, POOL_WIDTH), BF16),
        compiler_params=_cparams(("parallel",), 40),
    )(u, w_pool, pool_scale, token)


def _outproj_norm(x2d, attn, pool, w_out, gain, tm):
    rows = x2d.shape[0]

    def body(x_ref, a_ref, p_ref, w_ref, g_ref, x2_ref, h_ref):
        x2 = x_ref[...] + _dot(a_ref[...], w_ref[:ATTN_WIDTH, :]) + _dot(p_ref[...], w_ref[ATTN_WIDTH:, :])
        x2_ref[...] = x2
        r = lax.rsqrt(jnp.mean(x2 * x2, axis=-1, keepdims=True) + EPS)
        h_ref[...] = (x2 * r * g_ref[...]).astype(BF16)

    row = lambda width: pl.BlockSpec((tm, width), lambda i: (i, 0))
    return pl.pallas_call(
        body, name="outproj_norm", grid=(rows // tm,),
        in_specs=[row(D_MODEL), row(ATTN_WIDTH), row(POOL_WIDTH), pl.BlockSpec((D_MODEL, D_MODEL), lambda i: (0, 0)),
                  pl.BlockSpec((1, D_MODEL), lambda i: (0, 0))],
        out_specs=[row(D_MODEL), row(D_MODEL)],
        out_shape=[jax.ShapeDtypeStruct((rows, D_MODEL), F32), jax.ShapeDtypeStruct((rows, D_MODEL), BF16)],
        compiler_params=_cparams(("parallel",), 40),
    )(x2d, attn, pool, w_out, gain)


def _resident(shape):
    return pl.BlockSpec(shape, lambda i: (0,) * len(shape), pipeline_mode=pl.Buffered(1))


def _mlp_forward_loss(h2, x2, w_up, w_down, gain, target, tm):
    rows = h2.shape[0]
    chunk = D_MODEL

    def body(h_ref, x_ref, up_ref, down_ref, g_ref, t_ref, slope_ref, f_ref, dx_ref, dxb_ref, loss_ref, dg_ref):
        @pl.when(pl.program_id(0) == 0)
        def _():
            loss_ref[...] = jnp.zeros_like(loss_ref)
            dg_ref[...] = jnp.zeros_like(dg_ref)

        h = h_ref[...]
        for c in range(D_FF // chunk):
            cols = slice(c * chunk, (c + 1) * chunk)
            r = jnp.maximum(_dot(h, up_ref[:, cols]), 0.0)
            slope_ref[:, cols] = (r + r).astype(BF16)
            f_ref[:, cols] = (r * r).astype(BF16)
        x3 = x_ref[...] + _dot(f_ref[...], down_ref[...])
        rn = lax.rsqrt(jnp.mean(x3 * x3, axis=-1, keepdims=True) + EPS)
        xhat = x3 * rn
        err = xhat * g_ref[...] - t_ref[...]
        loss_ref[...] += jnp.sum(err * err, axis=0, keepdims=True)
        dxhat = err * (g_ref[...] * (1.0 / D_MODEL))
        dx = rn * (dxhat - xhat * jnp.mean(dxhat * xhat, axis=-1, keepdims=True))
        dg_ref[...] += jnp.sum(err * xhat, axis=0, keepdims=True) * (1.0 / D_MODEL)
        dx_ref[...] = dx
        dxb_ref[...] = dx.astype(BF16)

    row = lambda width: pl.BlockSpec((tm, width), lambda i: (i, 0))
    vec = pl.BlockSpec((1, D_MODEL), lambda i: (0, 0))
    return pl.pallas_call(
        body, name="mlp_forward_loss", grid=(rows // tm,),
        in_specs=[row(D_MODEL), row(D_MODEL), _resident((D_MODEL, D_FF)), _resident((D_FF, D_MODEL)), vec, row(D_MODEL)],
        out_specs=[row(D_FF), row(D_FF), row(D_MODEL), row(D_MODEL), vec, vec],
        out_shape=[jax.ShapeDtypeStruct((rows, D_FF), BF16), jax.ShapeDtypeStruct((rows, D_FF), BF16),
                   jax.ShapeDtypeStruct((rows, D_MODEL), F32), jax.ShapeDtypeStruct((rows, D_MODEL), BF16),
                   jax.ShapeDtypeStruct((1, D_MODEL), F32), jax.ShapeDtypeStruct((1, D_MODEL), F32)],
        compiler_params=_cparams(("arbitrary",), 56),
    )(h2, x2, w_up, w_down, gain, target)


def _mlp_backward_data(dx3b, slope, w_down, w_up, dx3, x2, gain, w_out, attn, pool, token, tm):
    rows = dx3b.shape[0]
    steps = rows // tm
    chunk = D_MODEL

    def body(dxb_ref, slope_ref, down_ref, up_ref, dx3_ref, x2_ref, g_ref, wo_ref, attn_ref, pool_ref, token_ref,
             da_ref, dx2_ref, dattn_ref, dpool_ref, dg_ref, dwo_hbm, wire_hbm, dwo_acc, wire, sems):
        @pl.when(pl.program_id(0) == 0)
        def _():
            dg_ref[...] = jnp.zeros_like(dg_ref)
            dwo_acc[...] = jnp.zeros_like(dwo_acc)

        dxb = dxb_ref[...]
        for c in range(D_FF // chunk):
            cols = slice(c * chunk, (c + 1) * chunk)
            da_ref[:, cols] = (_dot_nt(dxb, down_ref[cols, :]) * slope_ref[:, cols].astype(F32)).astype(BF16)
        dnorm, dg = _rms_backward(_dot_nt(da_ref[...], up_ref[...]), x2_ref[...], g_ref[...])
        dg_ref[...] += dg
        dx2 = dx3_ref[...] + dnorm
        dx2_ref[...] = dx2
        dx2b = dx2.astype(BF16)
        dmix = _dot_nt(dx2b, wo_ref[...])
        dattn_ref[...] = dmix[:, :ATTN_WIDTH].astype(BF16)
        dpool_ref[...] = dmix[:, ATTN_WIDTH:]
        dwo_acc[:ATTN_WIDTH, :] += _dot_tn(attn_ref[...], dx2b)
        dwo_acc[ATTN_WIDTH:, :] += _dot_tn(pool_ref[...], dx2b)

        @pl.when(pl.program_id(0) == steps - 1)
        def _():
            done = pltpu.make_async_copy(dwo_acc, dwo_hbm, sems.at[0])
            done.start()
            wire[...] = dwo_acc[...].astype(BF16)
            sent = pltpu.make_async_copy(wire, wire_hbm, sems.at[1])
            sent.start()
            done.wait()
            sent.wait()

    row = lambda width: pl.BlockSpec((tm, width), lambda i: (i, 0))
    vec = pl.BlockSpec((1, D_MODEL), lambda i: (0, 0))
    return pl.pallas_call(
        body, name="mlp_backward_data", grid=(steps,),
        in_specs=[row(D_MODEL), row(D_FF), _resident((D_FF, D_MODEL)), _resident((D_MODEL, D_FF)), row(D_MODEL), row(D_MODEL), vec,
                  _resident((D_MODEL, D_MODEL)), row(ATTN_WIDTH), row(POOL_WIDTH), pl.BlockSpec((8, LANES), lambda i: (0, 0))],
        out_specs=[row(D_FF), row(D_MODEL), row(ATTN_WIDTH), row(POOL_WIDTH), vec, pl.BlockSpec(memory_space=pl.ANY),
                   pl.BlockSpec(memory_space=pl.ANY)],
        out_shape=[jax.ShapeDtypeStruct((rows, D_FF), BF16), jax.ShapeDtypeStruct((rows, D_MODEL), F32),
                   jax.ShapeDtypeStruct((rows, ATTN_WIDTH), BF16), jax.ShapeDtypeStruct((rows, POOL_WIDTH), F32),
                   jax.ShapeDtypeStruct((1, D_MODEL), F32), jax.ShapeDtypeStruct((D_MODEL, D_MODEL), F32),
                   jax.ShapeDtypeStruct((D_MODEL, D_MODEL), BF16)],
        scratch_shapes=[pltpu.VMEM((D_MODEL, D_MODEL), F32), pltpu.VMEM((D_MODEL, D_MODEL), BF16), pltpu.SemaphoreType.DMA((2,))],
        compiler_params=_cparams(("arbitrary",), 56),
    )(dx3b, slope, w_down, w_up, dx3, x2, gain, w_out, attn, pool, token)


def _weight_gradient(name, lhs, rhs, block_lhs, tm):
    rows = lhs.shape[0]
    steps = rows // tm
    out = (N_DEV, FF_BLOCK, rhs.shape[1]) if block_lhs else (N_DEV, lhs.shape[1], FF_BLOCK)

    def body(l_ref, r_ref, o_hbm, wire_hbm, acc, wire, sems):
        @pl.when(pl.program_id(0) == 0)
        def _():
            acc[...] = jnp.zeros_like(acc)

        for d in range(N_DEV):
            cols = slice(d * FF_BLOCK, (d + 1) * FF_BLOCK)
            acc[d] += _dot_tn(l_ref[:, cols], r_ref[...]) if block_lhs else _dot_tn(l_ref[...], r_ref[:, cols])

        @pl.when(pl.program_id(0) == steps - 1)
        def _():
            done = pltpu.make_async_copy(acc, o_hbm, sems.at[0])
            done.start()
            wire[...] = acc[...].astype(BF16)
            sent = pltpu.make_async_copy(wire, wire_hbm, sems.at[1])
            sent.start()
            done.wait()
            sent.wait()

    return pl.pallas_call(
        body, name=name, grid=(steps,),
        in_specs=[pl.BlockSpec((tm, lhs.shape[1]), lambda i: (i, 0)), pl.BlockSpec((tm, rhs.shape[1]), lambda i: (i, 0))],
        out_specs=[pl.BlockSpec(memory_space=pl.ANY), pl.BlockSpec(memory_space=pl.ANY)],
        out_shape=[jax.ShapeDtypeStruct(out, F32), jax.ShapeDtypeStruct(out, BF16)],
        scratch_shapes=[pltpu.VMEM(out, F32), pltpu.VMEM(out, BF16), pltpu.SemaphoreType.DMA((2,))],
        compiler_params=_cparams(("arbitrary",), 60),
    )(lhs, rhs)


def _attention_backward(sinks, q, kd, vd, dout, cos, sin, token, n_seq, seq):
    n_blocks = seq // BLOCK
    n_kv = N_Q_HEADS // Q_PER_KV
    chains = min(BACKWARD_CHAINS, n_blocks)

    def body(sink_ref, q_ref, k_ref, v_ref, do_ref, cos_ref, sin_ref, token_ref, dq_ref, dk_ref, dv_ref, dsink_ref,
             s_all, dp_all, dsc_all, dsp_all, pc_all, pp_all, dk_acc, dv_acc):
        low = lax.broadcasted_iota(jnp.int32, (BLOCK, LANES), 1) < HEAD_DIM

        @pl.when(pl.program_id(0) == 0)
        def _():
            dsink_ref[...] = jnp.zeros_like(dsink_ref)

        def fold(x):
            return x + pltpu.roll(x, HEAD_DIM, 1)

        def onto_keys(ref, kv, other):
            even, odd = _dot_tn(ref[2 * kv], other), _dot_tn(ref[2 * kv + 1], other)
            return fold(jnp.where(low, even, odd))

        def step(i, dsink):
            blocks = []
            for u in range(chains):
                n = i * chains + u
                r0 = pl.multiple_of(n * BLOCK, BLOCK)
                p0 = pl.multiple_of(jnp.maximum(n - 1, 0) * BLOCK, BLOCK)
                cur_mask, prev_mask = _window_masks(n)
                keys = [_window_operand(k_ref, r0, p0, kv) for kv in range(n_kv)]
                values = [_window_operand(v_ref, r0, p0, kv) for kv in range(n_kv)]
                q_rows = [_pair_rows(q_ref, r0, kv) for kv in range(n_kv)]
                do_rows = [_pair_rows(do_ref, r0, kv) for kv in range(n_kv)]
                for kv in range(n_kv):
                    both_s, both_dp = _dot_nt(q_rows[kv], keys[kv]), _dot_nt(do_rows[kv], values[kv])
                    for j in range(2):
                        wide_s, wide_dp = both_s[j * BLOCK:(j + 1) * BLOCK], both_dp[j * BLOCK:(j + 1) * BLOCK]
                        for parity in range(2):
                            head = u * N_Q_HEADS + 4 * kv + 2 * j + parity
                            s_all[head] = _merged_window(wide_s, parity, cur_mask, prev_mask, NEG)
                            dp_all[head] = _merged_window(wide_dp, parity, cur_mask, prev_mask, 0.0)
                blocks.append((n, r0, p0, keys, q_rows, do_rows))

            probs, p_sink = _softmax_with_sink(s_all[...], jnp.tile(sink_ref[:, :, 0:1], (chains, 1, 1)), True)
            dprobs = dp_all[...]
            delta = _lane_sums(probs * dprobs, True)
            dscores = (probs * (dprobs - delta)).astype(BF16)
            sink_terms = jnp.sum((p_sink * delta)[:, :, 0:1], axis=1, keepdims=True)
            probs = probs.astype(BF16)
            zero = jnp.zeros((BLOCK, BLOCK), BF16)
            for u in range(chains):
                dsink = dsink - sink_terms[u * N_Q_HEADS:(u + 1) * N_Q_HEADS]
                for head in range(N_Q_HEADS):
                    group, rows = 2 * (head // Q_PER_KV) + head % 2, pl.ds(((head % Q_PER_KV) // 2) * BLOCK, BLOCK)
                    ds_h, p_h = dscores[u * N_Q_HEADS + head], probs[u * N_Q_HEADS + head]
                    dsc_all[u, group, rows, :] = jnp.where(cur_mask, ds_h, zero)
                    dsp_all[u, group, rows, :] = jnp.where(cur_mask, zero, ds_h)
                    pc_all[u, group, rows, :] = jnp.where(cur_mask, p_h, zero)
                    pp_all[u, group, rows, :] = jnp.where(cur_mask, zero, p_h)

            for u, (n, r0, p0, keys, q_rows, do_rows) in enumerate(blocks):
                dsc_ref, dsp_ref, pc_ref, pp_ref = dsc_all.at[u], dsp_all.at[u], pc_all.at[u], pp_all.at[u]
                for kv in range(n_kv):
                    wide = jnp.concatenate([dsc_ref[2 * kv], dsc_ref[2 * kv + 1], dsp_ref[2 * kv], dsp_ref[2 * kv + 1]], axis=1)
                    both = _dot(wide, keys[kv]) * (HEAD_DIM ** -0.5)
                    for j in range(2):
                        dq = _rope(both[j * BLOCK:(j + 1) * BLOCK], cos_ref[pl.ds(r0, BLOCK), :], -sin_ref[pl.ds(r0, BLOCK), :])
                        dq_ref[pl.ds(r0, BLOCK), (2 * kv + j) * LANES:(2 * kv + j + 1) * LANES] = dq.astype(BF16)

                parts = []
                for kv in range(n_kv):
                    parts.append((onto_keys(dsc_ref, kv, q_rows[kv]), onto_keys(dsp_ref, kv, q_rows[kv]),
                                  onto_keys(pc_ref, kv, do_rows[kv]), onto_keys(pp_ref, kv, do_rows[kv])))
                dk_acc[pl.ds(r0, BLOCK), :] = jnp.where(low, parts[0][0], parts[1][0])
                dv_acc[pl.ds(r0, BLOCK), :] = jnp.where(low, parts[0][2], parts[1][2])

                @pl.when(n > 0)
                def _():
                    dk_acc[pl.ds(p0, BLOCK), :] += jnp.where(low, parts[0][1], parts[1][1])
                    dv_acc[pl.ds(p0, BLOCK), :] += jnp.where(low, parts[0][3], parts[1][3])

            return dsink

        dsink = lax.fori_loop(0, n_blocks // chains, step, jnp.zeros((N_Q_HEADS, 1, 1), F32))
        dsink_ref[...] += jnp.broadcast_to(dsink, dsink_ref.shape)
        dk_ref[...] = _rope(dk_acc[...], cos_ref[...], -sin_ref[...]).astype(BF16)
        dv_ref[...] = dv_acc[...].astype(BF16)

    seq_block = lambda width: pl.BlockSpec((seq, width), lambda b: (b, 0))
    table = pl.BlockSpec((seq, LANES), lambda b: (0, 0))
    per_head = pl.BlockSpec((N_Q_HEADS, 1, LANES), lambda b: (0, 0, 0))
    per_block = pltpu.VMEM((chains * N_Q_HEADS, BLOCK, BLOCK), F32)
    grouped = pltpu.VMEM((chains, 2 * n_kv, 2 * BLOCK, BLOCK), BF16)
    return pl.pallas_call(
        body, name="attention_backward", grid=(n_seq,),
        in_specs=[per_head, seq_block(ATTN_WIDTH), seq_block(2 * KV_WIDTH), seq_block(2 * KV_WIDTH),
                  seq_block(ATTN_WIDTH), table, table, pl.BlockSpec((8, LANES), lambda b: (0, 0))],
        out_specs=[seq_block(ATTN_WIDTH), seq_block(KV_WIDTH), seq_block(KV_WIDTH), per_head],
        out_shape=[jax.ShapeDtypeStruct((n_seq * seq, ATTN_WIDTH), BF16), jax.ShapeDtypeStruct((n_seq * seq, KV_WIDTH), BF16),
                   jax.ShapeDtypeStruct((n_seq * seq, KV_WIDTH), BF16), jax.ShapeDtypeStruct((N_Q_HEADS, 1, LANES), F32)],
        scratch_shapes=[per_block, per_block, grouped, grouped, grouped, grouped, pltpu.VMEM((seq, KV_WIDTH), F32), pltpu.VMEM((seq, KV_WIDTH), F32)],
        compiler_params=_cparams(("arbitrary",), 40),
    )(sinks, q, kd, vd, dout, cos, sin, token)


def _pool_backward(u, dpool, w_pool, pool_scale, n_seq, seq):
    groups = len(POOL_WINDOWS)

    def body(u_ref, dp_ref, w_ref, s_ref, du_ref, dw_ref, ds_ref):
        @pl.when(pl.program_id(0) == 0)
        def _():
            dw_ref[...] = jnp.zeros_like(dw_ref)
            ds_ref[...] = jnp.zeros_like(ds_ref)

        t = lax.broadcasted_iota(jnp.int32, (seq, 1), 0)
        for g, window in enumerate(POOL_WINDOWS):
            cols = slice(g * POOL_GROUP_DIM, (g + 1) * POOL_GROUP_DIM)
            d, count = _pool_features(u_ref[:, cols], window, t, seq)
            dpool_g = dp_ref[:, cols]
            ds_ref[:, cols] += jnp.sum(dpool_g * _dot(d, w_ref[g]), axis=0, keepdims=True)
            dy = (dpool_g * s_ref[:, cols]).astype(BF16)
            dw_ref[g] += _dot_tn(d, dy)
            dd = _dot_nt(dy, w_ref[g])
            du_ref[:, cols] = (_leading(dd / count, window, t, seq) - dd).astype(BF16)

    seq_block = pl.BlockSpec((seq, POOL_WIDTH), lambda b: (b, 0))
    weights = pl.BlockSpec((groups, POOL_GROUP_DIM, POOL_GROUP_DIM), lambda b: (0, 0, 0))
    scale = pl.BlockSpec((1, POOL_WIDTH), lambda b: (0, 0))
    return pl.pallas_call(
        body, name="pool_backward", grid=(n_seq,),
        in_specs=[seq_block, seq_block, weights, scale],
        out_specs=[seq_block, weights, scale],
        out_shape=[jax.ShapeDtypeStruct((n_seq * seq, POOL_WIDTH), BF16),
                   jax.ShapeDtypeStruct((groups, POOL_GROUP_DIM, POOL_GROUP_DIM), F32), jax.ShapeDtypeStruct((1, POOL_WIDTH), F32)],
        compiler_params=_cparams(("arbitrary",), 40),
    )(u, dpool, w_pool, pool_scale)


def _inproj_gradient(dq, dk, dv, du, h1, tm):
    rows = h1.shape[0]

    def body(dq_ref, dk_ref, dv_ref, du_ref, h_ref, dw_ref):
        @pl.when(pl.program_id(0) == 0)
        def _():
            dw_ref[...] = jnp.zeros_like(dw_ref)

        dproj = jnp.concatenate([dq_ref[...], dk_ref[...], dv_ref[...], du_ref[...]], axis=1)
        dw_ref[...] += _dot_tn(h_ref[...], dproj)

    row = lambda width: pl.BlockSpec((tm, width), lambda i: (i, 0))
    return pl.pallas_call(
        body, name="inproj_gradient", grid=(rows // tm,),
        in_specs=[row(ATTN_WIDTH), row(KV_WIDTH), row(KV_WIDTH), row(POOL_WIDTH), row(D_MODEL)],
        out_specs=pl.BlockSpec((D_MODEL, IN_WIDTH), lambda i: (0, 0)),
        out_shape=jax.ShapeDtypeStruct((D_MODEL, IN_WIDTH), F32),
        compiler_params=_cparams(("arbitrary",), 48),
    )(dq, dk, dv, du, h1)


def _inproj_backward(dq, dk, dv, du, w_in, x2d, dx2, gain, token, tm):
    rows = x2d.shape[0]

    def body(dq_ref, dk_ref, dv_ref, du_ref, w_ref, x_ref, dx2_ref, g_ref, token_ref, dx_ref, dg_ref):
        @pl.when(pl.program_id(0) == 0)
        def _():
            dg_ref[...] = jnp.zeros_like(dg_ref)

        dproj = jnp.concatenate([dq_ref[...], dk_ref[...], dv_ref[...], du_ref[...]], axis=1)
        dnorm, dg = _rms_backward(_dot_nt(dproj, w_ref[...]), x_ref[...], g_ref[...])
        dg_ref[...] += dg
        dx_ref[...] = dx2_ref[...] + dnorm

    row = lambda width: pl.BlockSpec((tm, width), lambda i: (i, 0))
    vec = pl.BlockSpec((1, D_MODEL), lambda i: (0, 0))
    return pl.pallas_call(
        body, name="inproj_backward", grid=(rows // tm,),
        in_specs=[row(ATTN_WIDTH), row(KV_WIDTH), row(KV_WIDTH), row(POOL_WIDTH), pl.BlockSpec((D_MODEL, IN_WIDTH), lambda i: (0, 0)),
                  row(D_MODEL), row(D_MODEL), vec, pl.BlockSpec((8, LANES), lambda i: (0, 0))],
        out_specs=[row(D_MODEL), vec],
        out_shape=[jax.ShapeDtypeStruct((rows, D_MODEL), F32), jax.ShapeDtypeStruct((1, D_MODEL), F32)],
        compiler_params=_cparams(("arbitrary",), 60),
    )(dq, dk, dv, du, w_in, x2d, dx2, gain, token)


def _place():
    return lax.axis_index("x"), lax.axis_index("y"), lax.axis_index("c")


def _peer(x, y, c, rel):
    return (1 - x if rel & 4 else x, 1 - y if rel & 2 else y, 1 - c if rel & 1 else c)


def _index(px, py, pc):
    return 4 * px + 2 * py + pc


def _row_slot(ref, d):
    return ref.at[d]


def _column_slot(ref, d):
    return ref.at[:, pl.ds(pl.multiple_of(d * FF_BLOCK, FF_BLOCK), FF_BLOCK)]


def _stage_weights(shards, slots, shapes):
    n = len(shards)

    def body(*refs):
        ins, outs, stage, sems = refs[:n], refs[n:2 * n], refs[2 * n:3 * n], refs[3 * n]
        me = _index(*_place())
        mine = []
        for a in range(n):
            stage[a][...] = ins[a][...].astype(BF16)
            mine.append(pltpu.make_async_copy(stage[a], slots[a](outs[a], me), sems.at[a]))
            mine[-1].start()
        for cp in mine:
            cp.wait()

    return pl.pallas_call(
        body, name="stage_weights",
        in_specs=[pl.BlockSpec(memory_space=pltpu.VMEM)] * n,
        out_specs=[pl.BlockSpec(memory_space=pl.ANY)] * n,
        out_shape=[jax.ShapeDtypeStruct(shape, BF16) for shape in shapes],
        scratch_shapes=[pltpu.VMEM(s.shape, BF16) for s in shards] + [pltpu.SemaphoreType.DMA((n,))],
        compiler_params=pltpu.CompilerParams(vmem_limit_bytes=32 * MIB),
    )(*shards)


ALL_PEERS = tuple(range(1, N_DEV))
FIRST_HOP = (1, 2, 4, 6)
OTHER_CHIPS = (2, 4, 6)


class _Plan:
    def __init__(self, per_array, copies):
        self.per_array, self.copies = per_array, copies


def _own_slot_first_hop(slots):
    def copies(x, y, c, ins, lands):
        me = _index(x, y, c)
        return [(slots[a](lands[a], me), slots[a](lands[a], me), _peer(x, y, c, rel)) for rel in FIRST_HOP for a in range(len(lands))]
    return _Plan(len(FIRST_HOP), copies)


def _landed_to_sibling(slots):
    def copies(x, y, c, ins, lands):
        blocks = [_index(*_peer(x, y, c, rel)) for rel in OTHER_CHIPS]
        return [(slots[a](lands[a], b), slots[a](lands[a], b), (x, y, 1 - c)) for b in blocks for a in range(len(lands))]
    return _Plan(len(OTHER_CHIPS), copies)


def _whole_to_all_copies(x, y, c, ins, lands):
    return [(ins[a], lands[a].at[rel - 1], _peer(x, y, c, rel)) for rel in ALL_PEERS for a in range(len(lands))]


def _block_to_owner_copies(x, y, c, ins, lands):
    return [(ins[a].at[_index(*_peer(x, y, c, rel))], lands[a].at[rel - 1], _peer(x, y, c, rel))
            for rel in ALL_PEERS for a in range(len(lands))]


_whole_to_all = _Plan(len(ALL_PEERS), _whole_to_all_copies)
_block_to_owner = _Plan(len(ALL_PEERS), _block_to_owner_copies)


def _split_copies(plan, ins, lands, send_sems, recv_sems):
    return [pltpu.make_async_remote_copy(src_ref=src, dst_ref=dst, send_sem=send_sems.at[k], recv_sem=recv_sems.at[k],
                                         device_id=to, device_id_type=MESH)
            for k, (src, dst, to) in enumerate(plan.copies(*_place(), ins, lands))]


HBM_SPEC = pl.BlockSpec(memory_space=pltpu.HBM)
SEM_SPEC = pl.BlockSpec(memory_space=pltpu.SEMAPHORE)
EFFECT = pltpu.SideEffectType.DATAFLOW_SIDE_EFFECTING


def _start_copies(name, plan, ins, lands, after):
    n_in, n = len(ins), len(ins) + len(lands)

    def body(*refs):
        send_sems, recv_sems = refs[n + 1], refs[n + 2]
        for cp in _split_copies(plan, refs[:n_in], refs[n_in:n], send_sems, recv_sems):
            cp.start()
        refs[-1][...] = jnp.zeros_like(refs[-1])

    arrays = [pltpu.with_memory_space_constraint(v, pltpu.HBM) for v in (*ins, *lands)]
    sems = pltpu.SemaphoreType.DMA((len(lands) * plan.per_array,))
    send_sems, recv_sems, *flying, token = pl.pallas_call(
        body, name=name,
        out_shape=(sems, sems, *[pltpu.HBM(v.shape, v.dtype) for v in arrays], jax.ShapeDtypeStruct((8, LANES), F32)),
        in_specs=[HBM_SPEC] * n + [pl.BlockSpec(memory_space=pl.ANY)],
        out_specs=(SEM_SPEC, SEM_SPEC, *[HBM_SPEC] * n, pl.BlockSpec(memory_space=pltpu.VMEM)),
        input_output_aliases={i: 2 + i for i in range(n)},
        compiler_params=pltpu.CompilerParams(has_side_effects=EFFECT),
    )(*arrays, after)
    return send_sems, recv_sems, flying, token


def _wait_copies(name, plan, n_in, send_sems, recv_sems, flying, after):
    n = len(flying)

    def body(*refs):
        for cp in _split_copies(plan, refs[:n_in], refs[n_in:n], refs[n], refs[n + 1]):
            cp.wait_send()
            cp.wait_recv()

    landed = pl.pallas_call(
        body, name=name, out_shape=tuple(pltpu.HBM(v.shape, v.dtype) for v in flying),
        in_specs=[HBM_SPEC] * n + [SEM_SPEC, SEM_SPEC, pl.BlockSpec(memory_space=pl.ANY)], out_specs=tuple([HBM_SPEC] * n),
        input_output_aliases={i: i for i in range(n)},
        compiler_params=pltpu.CompilerParams(has_side_effects=EFFECT),
    )(*flying, send_sems, recv_sems, after)
    return landed[:n_in], landed[n_in:]


def _adamw_math(w, g, m, v):
    m = ADAM_B1 * m + (1.0 - ADAM_B1) * g
    v = ADAM_B2 * v + (1.0 - ADAM_B2) * (g * g)
    m_hat = m / (1.0 - ADAM_B1 ** ADAM_STEP)
    v_hat = v / (1.0 - ADAM_B2 ** ADAM_STEP)
    return -ADAM_LR * (m_hat / (jnp.sqrt(v_hat) + ADAM_EPS) + ADAM_WD * w), m, v


def _adamw_sharded(me, own, received, w, m, v, tr):
    rows, cols = w.shape

    def body(me_ref, own_ref, rec_ref, w_ref, m_ref, v_ref, g_ref, d_ref, nm_ref, nv_ref):
        g = own_ref[...]
        for r in range(N_DEV - 1):
            g = g + rec_ref[r].astype(F32)
        g_ref[...] = g
        d_ref[...], nm_ref[...], nv_ref[...] = _adamw_math(w_ref[...], g, m_ref[...], v_ref[...])

    tile = pl.BlockSpec((tr, cols), lambda i, me_ref: (i, 0))
    shape = jax.ShapeDtypeStruct((rows, cols), F32)
    return pl.pallas_call(
        body, name="adamw_sharded",
        grid_spec=pltpu.PrefetchScalarGridSpec(
            num_scalar_prefetch=1, grid=(rows // tr,),
            in_specs=[pl.BlockSpec((None, tr, cols), lambda i, me_ref: (me_ref[0], i, 0)),
                      pl.BlockSpec((N_DEV - 1, tr, cols), lambda i, me_ref: (0, i, 0)), tile, tile, tile],
            out_specs=[tile, tile, tile, tile]),
        out_shape=[shape, shape, shape, shape],
        compiler_params=_cparams(("parallel",), 40),
    )(me, own, received, w, m, v)


VECTOR_ROWS = D_MODEL // LANES
POOL_ROWS = len(POOL_WINDOWS) * POOL_GROUP_DIM


def _pack_small(dg1, dg2, dg3, dps, dsink, loss_cols, dwp):
    def body(g1_ref, g2_ref, g3_ref, ps_ref, sink_ref, loss_ref, wp_ref, o_ref):
        o_ref[...] = jnp.zeros_like(o_ref)
        for base, ref, n in ((ROW_G1, g1_ref, VECTOR_ROWS), (ROW_G2, g2_ref, VECTOR_ROWS), (ROW_G3, g3_ref, VECTOR_ROWS),
                             (ROW_LOSS, loss_ref, VECTOR_ROWS), (ROW_PS, ps_ref, POOL_WIDTH // LANES)):
            for r in range(n):
                o_ref[base + r:base + r + 1, :] = ref[:, r * LANES:(r + 1) * LANES]
        heads = sink_ref[:, 0, :]
        on_diagonal = lax.broadcasted_iota(jnp.int32, heads.shape, 0) == lax.broadcasted_iota(jnp.int32, heads.shape, 1)
        o_ref[ROW_SINK:ROW_SINK + 1, :] = jnp.sum(jnp.where(on_diagonal, heads, 0.0), axis=0, keepdims=True)
        o_ref[ROW_WP:ROW_WP + POOL_ROWS, :] = wp_ref[...].reshape(POOL_ROWS, LANES)

    return pl.pallas_call(body, name="pack_small", out_shape=jax.ShapeDtypeStruct((SMALL_ROWS, LANES), F32))(
        dg1, dg2, dg3, dps, dsink, loss_cols, dwp)


def _finish_small(me, own, landed, params):
    flat = [a for group in params for a in group]

    def body(me_ref, own_ref, landed_ref, *refs):
        ins, outs = refs[:len(flat)], refs[len(flat):]
        total = None
        for source in range(N_DEV):
            rel = jnp.bitwise_xor(me_ref[0], source)
            piece = jnp.where(rel == 0, own_ref[...], landed_ref[jnp.maximum(rel, 1) - 1])
            total = piece if total is None else total + piece
        outs[0][...] = (0.5 / D_MODEL) * jnp.sum(jnp.sum(total[ROW_LOSS:ROW_LOSS + VECTOR_ROWS], axis=1, keepdims=True), axis=0, keepdims=True)
        row = lambda base, n: jnp.concatenate([total[base + r:base + r + 1, :] for r in range(n)], axis=1)
        grads = [row(ROW_G1, VECTOR_ROWS), row(ROW_G2, VECTOR_ROWS), row(ROW_G3, VECTOR_ROWS), row(ROW_PS, POOL_WIDTH // LANES),
                 total[ROW_SINK:ROW_SINK + 1, :N_Q_HEADS], total[ROW_WP:ROW_WP + POOL_ROWS].reshape(params[5][0].shape)]
        for k, g in enumerate(grads):
            w_ref, m_ref, v_ref = ins[3 * k:3 * k + 3]
            g_out, d_out, m_out, v_out = outs[1 + 4 * k:5 + 4 * k]
            g_out[...] = g
            d_out[...], m_out[...], v_out[...] = _adamw_math(w_ref[...], g, m_ref[...], v_ref[...])

    shapes = [jax.ShapeDtypeStruct((1, 1), F32)] + [jax.ShapeDtypeStruct(w.shape, F32) for w, _, _ in params for _ in range(4)]
    vmem = pl.BlockSpec(memory_space=pltpu.VMEM)
    res = pl.pallas_call(body, name="finish_small", in_specs=[pl.BlockSpec(memory_space=pltpu.SMEM)] + [vmem] * (2 + len(flat)),
                         out_shape=shapes)(me, own, landed, *flat)
    return res[0], [res[1 + 4 * k:5 + 4 * k] for k in range(len(params))]


def _local_step(x, target, attn_norm_g, attn_sinks, w_pool, pool_scale, mlp_norm_g, final_norm_g,
                front_token, first_weight, second_hop, later_weights, ship_down, ship_up, ship_in):
    n_seq, seq, _ = x.shape
    rows = n_seq * seq
    tm, tm_mlp, tm_norm = min(1024, seq), min(256, seq), min(2048, seq)
    tm_grad = tm
    x2d, t2d = x.reshape(rows, D_MODEL), target.reshape(rows, D_MODEL)
    g3 = final_norm_g.reshape(1, D_MODEL)
    cos, sin = _rope_tables(seq)
    wp_b = w_pool[0].astype(BF16)

    sink_rows = jnp.broadcast_to(attn_sinks.reshape(N_Q_HEADS, 1, 1), (N_Q_HEADS, 1, LANES))
    h1 = _first_norm(x2d, attn_norm_g, front_token, tm_norm)
    w_in_full = first_weight(h1)
    q, kd, vd, u = _inproj(h1, w_in_full, cos, sin, seq, tm)
    attn = _attention_forward(sink_rows, q, kd, vd, n_seq, seq)
    pool = _pool_forward(u, wp_b, pool_scale, second_hop(attn), n_seq, seq)
    w_out_full, w_up_full, w_down_full = later_weights(pool)
    x2, h2 = _outproj_norm(x2d, attn, pool, w_out_full, mlp_norm_g, tm)
    slope, f, dx3, dx3b, loss_cols, dg3 = _mlp_forward_loss(h2, x2, w_up_full, w_down_full, g3, t2d, tm_mlp)

    down_token = ship_down(_weight_gradient("down_gradient", f, dx3b, True, tm_grad))
    da, dx2, dattn, dpool, dg2, d_w_out, d_w_out_wire = _mlp_backward_data(
        dx3b, slope, w_down_full, w_up_full, dx3, x2, mlp_norm_g, w_out_full, attn, pool, down_token, tm_mlp)
    up_token = ship_up((d_w_out, d_w_out_wire), _weight_gradient("up_gradient", h2, da, False, tm_grad))
    dq, dk, dv, dsink = _attention_backward(sink_rows, q, kd, vd, dattn, cos, sin, up_token, n_seq, seq)
    du, d_w_pool, d_pool_scale = _pool_backward(u, dpool, wp_b, pool_scale, n_seq, seq)
    in_token = ship_in(_inproj_gradient(dq, dk, dv, du, h1, tm_grad))
    grad_x, dg1 = _inproj_backward(dq, dk, dv, du, w_in_full, x2d, dx2, attn_norm_g, in_token, tm)
    return grad_x.reshape(x.shape), _pack_small(dg1, dg2, dg3, d_pool_scale, dsink, loss_cols, d_w_pool)


def kernel(x, attn_norm_g, w_in, attn_sinks, w_pool, pool_scale, w_out, mlp_norm_g, w_up, w_down, final_norm_g, loss_target, m_attn_norm_g, m_w_in, m_attn_sinks, m_w_pool, m_pool_scale, m_w_out, m_mlp_norm_g, m_w_up, m_w_down, m_final_norm_g, v_attn_norm_g, v_w_in, v_attn_sinks, v_w_pool, v_pool_scale, v_w_out, v_mlp_norm_g, v_w_up, v_w_down, v_final_norm_g):
    me = (4 * lax.axis_index("x") + 2 * lax.axis_index("y") + lax.axis_index("c")).astype(jnp.int32).reshape(1)

    unordered = jnp.zeros((8, LANES), F32)

    win_land, wout_land, wup_land, wdown_land = _stage_weights(
        [w_in[0], w_out[0], w_up[0], w_down[0]], [_row_slot, _row_slot, _column_slot, _row_slot],
        [(N_DEV, D_MODEL, IN_BLOCK), (N_DEV, OUT_BLOCK, D_MODEL), (D_MODEL, D_FF), (N_DEV, FF_BLOCK, D_MODEL)])
    in_slots, later_slots = [_row_slot], [_row_slot, _column_slot, _row_slot]
    in_copies = _start_copies("spread_in_start", _own_slot_first_hop(in_slots), [], [win_land], unordered)
    later_copies = _start_copies("spread_later_start", _own_slot_first_hop(later_slots), [], [wout_land, wup_land, wdown_land], in_copies[3])

    def first_weight(after):
        _, in_landed = _wait_copies("spread_in_wait", _own_slot_first_hop(in_slots), 0, *in_copies[:3], after)
        in_passed = _start_copies("pass_in_start", _landed_to_sibling(in_slots), [], in_landed, unordered)
        _, (win_g,) = _wait_copies("pass_in_wait", _landed_to_sibling(in_slots), 0, *in_passed[:3], in_passed[3])
        return jnp.transpose(win_g, (1, 0, 2)).reshape(D_MODEL, IN_WIDTH)

    passing = []

    def second_hop(after):
        _, later_landed = _wait_copies("spread_later_wait", _own_slot_first_hop(later_slots), 0, *later_copies[:3], after)
        passing.extend(_start_copies("pass_later_start", _landed_to_sibling(later_slots), [], later_landed, unordered))
        return passing[3]

    def later_weights(after):
        _, (wout_g, wup_g, wdown_g) = _wait_copies("pass_later_wait", _landed_to_sibling(later_slots), 0, *passing[:3], after)
        return wout_g.reshape(D_MODEL, D_MODEL), wup_g, wdown_g.reshape(D_FF, D_MODEL)

    deliveries, kept = {}, {}

    def deliver(name, plan, wires):
        lands = [lax.empty((N_DEV - 1,) + (g.shape[1:] if plan is _block_to_owner else g.shape), g.dtype) for g in wires]
        deliveries[name] = _start_copies(name + "_start", plan, wires, lands, unordered)
        return deliveries[name][3]

    def landed(name, plan, after):
        send, recv, flying, _ = deliveries[name]
        return _wait_copies(name + "_wait", plan, len(flying) // 2, send, recv, flying, after)

    def ship_down(d_w_down):
        kept["down"] = d_w_down[0]
        return deliver("deliver_down", _block_to_owner, [d_w_down[1]])

    def ship_up(d_w_out, d_w_up):
        kept["out"], kept["up"] = d_w_out[0].reshape(N_DEV, OUT_BLOCK, D_MODEL), d_w_up[0]
        return deliver("deliver_up", _block_to_owner, [d_w_out[1].reshape(N_DEV, OUT_BLOCK, D_MODEL), d_w_up[1]])

    def ship_in(d_w_in):
        kept["in"] = jnp.transpose(d_w_in.reshape(D_MODEL, N_DEV, IN_BLOCK), (1, 0, 2))
        return deliver("deliver_in", _block_to_owner, [kept["in"].astype(BF16)])

    grad_x, small = _local_step(x, loss_target, attn_norm_g, attn_sinks, w_pool, pool_scale, mlp_norm_g, final_norm_g,
                                later_copies[3], first_weight, second_hop, later_weights, ship_down, ship_up, ship_in)
    small_token = deliver("deliver_small", _whole_to_all, [small])

    _, (got_down,) = landed("deliver_down", _block_to_owner, small_token)
    _, (got_out, got_up) = landed("deliver_up", _block_to_owner, small_token)
    g_down = _adamw_sharded(me, kept["down"], got_down, w_down[0], m_w_down[0], v_w_down[0], 256)
    g_up = _adamw_sharded(me, kept["up"], got_up, w_up[0], m_w_up[0], v_w_up[0], 256)
    g_out = _adamw_sharded(me, kept["out"], got_out, w_out[0], m_w_out[0], v_w_out[0], 128)
    _, (got_in,) = landed("deliver_in", _block_to_owner, g_out[0])
    g_in = _adamw_sharded(me, kept["in"], got_in, w_in[0], m_w_in[0], v_w_in[0], 1024)
    (own_small,), (got_small,) = landed("deliver_small", _whole_to_all, g_in[0])

    row = lambda a: a.reshape(1, D_MODEL)
    params = [(attn_norm_g, m_attn_norm_g, v_attn_norm_g), (mlp_norm_g, m_mlp_norm_g, v_mlp_norm_g),
              (row(final_norm_g), row(m_final_norm_g), row(v_final_norm_g)), (pool_scale, m_pool_scale, v_pool_scale),
              (attn_sinks, m_attn_sinks, v_attn_sinks), (w_pool[0], m_w_pool[0], v_w_pool[0])]
    loss, (s_norm1, s_norm2, s_norm3, s_scale, s_sinks, s_pool) = _finish_small(me, own_small, got_small, params)
    s_norm3 = [a.reshape(D_MODEL) for a in s_norm3]
    s_pool = [a[None] for a in s_pool]

    def ordered(k):
        return [s_norm1[k], g_in[k][None], s_sinks[k], s_pool[k], s_scale[k], g_out[k][None], s_norm2[k], g_up[k][None], g_down[k][None],
                s_norm3[k]]

    return (loss.reshape(()), grad_x, *ordered(0), *ordered(1), *ordered(2), *ordered(3))
```

```python
import jax
import jax.numpy as jnp
from jax import lax
from jax.experimental import pallas as pl
from jax.experimental.pallas import tpu as pltpu

F32 = jnp.float32
BF16 = jnp.bfloat16

D_MODEL = 1024
HEAD_DIM = 64
N_Q_HEADS = 8
Q_PER_KV = 4
ATTN_WIDTH = 512
KV_WIDTH = 128
BLOCK = 128
ROPE_THETA = 10000.0
POOL_WINDOWS = (2, 4, 8, 16)
POOL_WIDTH = 512
POOL_GROUP_DIM = 128
IN_WIDTH = 1280
D_FF = 4096
EPS = 1e-6
N_DEV = 8
FF_BLOCK = D_FF // N_DEV
IN_BLOCK = IN_WIDTH // N_DEV
OUT_BLOCK = D_MODEL // N_DEV
ADAM_LR = 0.001
ADAM_B1 = 0.9
ADAM_B2 = 0.999
ADAM_EPS = 1e-08
ADAM_WD = 0.01
ADAM_STEP = 10
NEG = -1e30
FORWARD_CHAINS = 4
BACKWARD_CHAINS = 2
LANES = 128
MIB = 1024 * 1024
MESH = pl.DeviceIdType.MESH

ROW_G1, ROW_G2, ROW_G3, ROW_PS, ROW_SINK, ROW_LOSS, ROW_WP, SMALL_ROWS = 0, 8, 16, 24, 32, 40, 48, 560


def _cparams(semantics, vmem_mib):
    return pltpu.CompilerParams(dimension_semantics=semantics, vmem_limit_bytes=vmem_mib * MIB)


def _in_hbm(*arrays):
    pinned = tuple(pltpu.with_memory_space_constraint(a, pltpu.HBM) for a in arrays)
    return pinned[0] if len(pinned) == 1 else pinned


def _dot(a, b):
    return jnp.dot(a, b, preferred_element_type=F32)


def _dot_nt(a, b):
    return lax.dot_general(a, b, (((1,), (1,)), ((), ())), preferred_element_type=F32)


def _dot_tn(a, b):
    return lax.dot_general(a, b, (((0,), (0,)), ((), ())), preferred_element_type=F32)


def _swap_halves(x):
    width = x.shape[1]
    lane = lax.broadcasted_iota(jnp.int32, x.shape, 1)
    ahead = pltpu.roll(x, width - HEAD_DIM // 2, 1)
    behind = pltpu.roll(x, HEAD_DIM // 2, 1)
    return jnp.where(lane % HEAD_DIM < HEAD_DIM // 2, ahead, behind)


def _rope(x, cos, sin):
    reps = x.shape[1] // LANES
    if reps > 1:
        cos = jnp.tile(cos, (1, reps))
        sin = jnp.tile(sin, (1, reps))
    return x * cos + _swap_halves(x) * sin


def _rope_tables(seq):
    half = HEAD_DIM // 2
    inv_freq = ROPE_THETA ** (-jnp.arange(half, dtype=F32) / half)
    ang = jnp.arange(seq).astype(F32)[:, None] * inv_freq[None, :]
    cos, sin = jnp.cos(ang), jnp.sin(ang)
    cos = jnp.tile(cos, (1, LANES // half))
    sin = jnp.tile(jnp.concatenate([-sin, sin], axis=1), (1, LANES // HEAD_DIM))
    return cos, sin


def _both_halves(x):
    lane = lax.broadcasted_iota(jnp.int32, x.shape, 1)
    other = pltpu.roll(x, HEAD_DIM, 1)
    low = lane < HEAD_DIM
    return jnp.where(low, x, other), jnp.where(low, other, x)


def _rms_backward(dh, xin, gain):
    r = lax.rsqrt(jnp.mean(xin * xin, axis=-1, keepdims=True) + EPS)
    xhat = xin * r
    dxhat = dh * gain
    dx = r * (dxhat - xhat * jnp.mean(dxhat * xhat, axis=-1, keepdims=True))
    return dx, jnp.sum(dh * xhat, axis=0, keepdims=True)


def _first_norm(x2d, gain, token, tm):
    rows = x2d.shape[0]

    def body(x_ref, g_ref, token_ref, h_ref):
        x = x_ref[...]
        r = lax.rsqrt(jnp.mean(x * x, axis=-1, keepdims=True) + EPS)
        h_ref[...] = (x * r * g_ref[...]).astype(BF16)

    row = pl.BlockSpec((tm, D_MODEL), lambda i: (i, 0))
    return pl.pallas_call(
        body, name="first_norm", grid=(rows // tm,),
        in_specs=[row, pl.BlockSpec((1, D_MODEL), lambda i: (0, 0)), pl.BlockSpec((8, LANES), lambda i: (0, 0))],
        out_specs=row, out_shape=pltpu.HBM((rows, D_MODEL), BF16),
        compiler_params=_cparams(("parallel",), 40),
    )(x2d, gain, token)


def _inproj(h1, w_in, cos, sin, seq, tm):
    rows = h1.shape[0]
    tiles_per_seq = seq // tm

    def body(h_ref, w_ref, cos_ref, sin_ref, q_ref, k_ref, v_ref, u_ref):
        proj = _dot(h_ref[...], w_ref[...])
        cos_t, sin_t = cos_ref[...], sin_ref[...]
        q = _rope(proj[:, :ATTN_WIDTH], cos_t, sin_t) * (HEAD_DIM ** -0.5)
        q_ref[...] = q.astype(BF16)
        k = _rope(proj[:, ATTN_WIDTH:ATTN_WIDTH + KV_WIDTH], cos_t, sin_t)
        k0, k1 = _both_halves(k)
        k_ref[...] = jnp.concatenate([k0, k1], axis=1).astype(BF16)
        v0, v1 = _both_halves(proj[:, ATTN_WIDTH + KV_WIDTH:ATTN_WIDTH + 2 * KV_WIDTH])
        v_ref[...] = jnp.concatenate([v0, v1], axis=1).astype(BF16)
        u_ref[...] = proj[:, ATTN_WIDTH + 2 * KV_WIDTH:]

    row = lambda width: pl.BlockSpec((tm, width), lambda i: (i, 0))
    table = pl.BlockSpec((tm, LANES), lambda i: (i % tiles_per_seq, 0))
    return pl.pallas_call(
        body, name="inproj", grid=(rows // tm,),
        in_specs=[row(D_MODEL), pl.BlockSpec((D_MODEL, IN_WIDTH), lambda i: (0, 0)), table, table],
        out_specs=[row(ATTN_WIDTH), row(2 * KV_WIDTH), row(2 * KV_WIDTH), row(POOL_WIDTH)],
        out_shape=[pltpu.HBM((rows, ATTN_WIDTH), BF16), pltpu.HBM((rows, 2 * KV_WIDTH), BF16),
                   pltpu.HBM((rows, 2 * KV_WIDTH), BF16), pltpu.HBM((rows, POOL_WIDTH), F32)],
        compiler_params=_cparams(("parallel",), 40),
    )(h1, w_in, cos, sin)


def _window_masks(n):
    qi = lax.broadcasted_iota(jnp.int32, (BLOCK, BLOCK), 0)
    kj = lax.broadcasted_iota(jnp.int32, (BLOCK, BLOCK), 1)
    return kj <= qi, jnp.logical_and(kj > qi, n > 0)


def _window_operand(ref, r0, p0, kv):
    low = lax.broadcasted_iota(jnp.int32, (BLOCK, LANES), 1) < HEAD_DIM
    cur = ref[pl.ds(r0, BLOCK), kv * LANES:(kv + 1) * LANES]
    prev = ref[pl.ds(p0, BLOCK), kv * LANES:(kv + 1) * LANES]
    zero = jnp.zeros_like(cur)
    return jnp.concatenate([jnp.where(low, cur, zero), jnp.where(low, zero, cur), jnp.where(low, prev, zero), jnp.where(low, zero, prev)], axis=0)


def _pair_rows(ref, r0, kv):
    return jnp.concatenate([ref[pl.ds(r0, BLOCK), (2 * kv + j) * LANES:(2 * kv + j + 1) * LANES] for j in range(2)], axis=0)


def _merged_window(wide, parity, cur_mask, prev_mask, fill):
    cur = wide[:, parity * LANES:(parity + 1) * LANES]
    prev = wide[:, (2 + parity) * LANES:(3 + parity) * LANES]
    return jnp.where(cur_mask, cur, jnp.where(prev_mask, prev, fill))


def _lane_sums(x, one_matmul):
    flat = x.reshape(-1, x.shape[-1])
    high = flat.astype(BF16)
    low = (flat - high.astype(F32)).astype(BF16)
    if one_matmul:
        sums = _dot(jnp.concatenate([high, low], axis=1), jnp.ones((2 * x.shape[-1], LANES), BF16))
    else:
        ones = jnp.ones((x.shape[-1], LANES), BF16)
        sums = _dot(high, ones) + _dot(low, ones)
    return sums.reshape(x.shape[:-1] + (LANES,))


def _softmax_with_sink(scores, sink, one_matmul):
    m = jnp.broadcast_to(jnp.maximum(jnp.max(scores, axis=-1, keepdims=True), sink), scores.shape)
    p, ps = jnp.exp(scores - m), jnp.exp(sink - m)
    inv = 1.0 / (_lane_sums(p, one_matmul) + ps)
    return p * inv, ps * inv


def _attention_forward(sinks, q, kd, vd, n_seq, seq):
    n_blocks = seq // BLOCK
    n_pairs = N_Q_HEADS // 2
    chains = min(FORWARD_CHAINS, n_blocks)

    def body(sink_ref, q_ref, k_ref, v_ref, o_ref, s_ref, p_ref):
        def step(i, carry):
            starts, values = [], []
            for u in range(chains):
                n = i * chains + u
                r0 = pl.multiple_of(n * BLOCK, BLOCK)
                p0 = pl.multiple_of(jnp.maximum(n - 1, 0) * BLOCK, BLOCK)
                cur_mask, prev_mask = _window_masks(n)
                keys = [_window_operand(k_ref, r0, p0, kv) for kv in range(2)]
                starts.append(r0)
                values.append([_window_operand(v_ref, r0, p0, kv) for kv in range(2)])
                for kv in range(2):
                    both = _dot_nt(_pair_rows(q_ref, r0, kv), keys[kv])
                    for j in range(2):
                        wide = both[j * BLOCK:(j + 1) * BLOCK]
                        for parity in range(2):
                            s_ref[u * N_Q_HEADS + 4 * kv + 2 * j + parity] = _merged_window(wide, parity, cur_mask, prev_mask, NEG)
            probs, _ = _softmax_with_sink(s_ref[...], jnp.tile(sink_ref[:, :, 0:1], (chains, 1, 1)), False)
            probs = probs.astype(BF16)
            zero = jnp.zeros((BLOCK, BLOCK), BF16)
            for u in range(chains):
                for pair in range(n_pairs):
                    for parity in range(2):
                        ph = probs[u * N_Q_HEADS + 2 * pair + parity]
                        p_ref[u, pair, :, parity * LANES:(parity + 1) * LANES] = jnp.where(cur_mask, ph, zero)
                        p_ref[u, pair, :, (2 + parity) * LANES:(3 + parity) * LANES] = jnp.where(cur_mask, zero, ph)
            for u in range(chains):
                for kv in range(2):
                    both = _dot(p_ref[u, 2 * kv:2 * kv + 2].reshape(2 * BLOCK, 4 * LANES), values[u][kv])
                    for j in range(2):
                        pair = 2 * kv + j
                        o_ref[pl.ds(starts[u], BLOCK), pair * LANES:(pair + 1) * LANES] = both[j * BLOCK:(j + 1) * BLOCK].astype(BF16)
            return carry

        lax.fori_loop(0, n_blocks // chains, step, 0)

    seq_block = lambda width: pl.BlockSpec((seq, width), lambda b: (b, 0))
    return pl.pallas_call(
        body, name="attention_forward", grid=(n_seq,),
        in_specs=[pl.BlockSpec((N_Q_HEADS, 1, LANES), lambda b: (0, 0, 0)), seq_block(ATTN_WIDTH), seq_block(2 * KV_WIDTH),
                  seq_block(2 * KV_WIDTH)],
        out_specs=seq_block(ATTN_WIDTH),
        out_shape=pltpu.HBM((n_seq * seq, ATTN_WIDTH), BF16),
        scratch_shapes=[pltpu.VMEM((chains * N_Q_HEADS, BLOCK, BLOCK), F32), pltpu.VMEM((chains, n_pairs, BLOCK, 4 * LANES), BF16)],
        compiler_params=_cparams(("parallel",), 40),
    )(sinks, q, kd, vd)


def _trailing(x, window, t, seq):
    k = 1
    while k < window:
        x = x + jnp.where(t >= k, pltpu.roll(x, k, 0), 0.0)
        k *= 2
    return x


def _leading(x, window, t, seq):
    k = 1
    while k < window:
        x = x + jnp.where(t < seq - k, pltpu.roll(x, seq - k, 0), 0.0)
        k *= 2
    return x


def _pool_features(u_g, window, t, seq):
    count = jnp.minimum(t + 1, window).astype(F32)
    return (_trailing(u_g, window, t, seq) / count - u_g).astype(BF16), count


def _pool_forward(u, w_pool, pool_scale, token, n_seq, seq):
    def body(u_ref, w_ref, s_ref, token_ref, o_ref):
        t = lax.broadcasted_iota(jnp.int32, (seq, 1), 0)
        for g, window in enumerate(POOL_WINDOWS):
            cols = slice(g * POOL_GROUP_DIM, (g + 1) * POOL_GROUP_DIM)
            d, _ = _pool_features(u_ref[:, cols], window, t, seq)
            o_ref[:, cols] = (_dot(d, w_ref[g]) * s_ref[:, cols]).astype(BF16)

    seq_block = pl.BlockSpec((seq, POOL_WIDTH), lambda b: (b, 0))
    return pl.pallas_call(
        body, name="pool_forward", grid=(n_seq,),
        in_specs=[seq_block, pl.BlockSpec((len(POOL_WINDOWS), POOL_GROUP_DIM, POOL_GROUP_DIM), lambda b: (0, 0, 0)),
                  pl.BlockSpec((1, POOL_WIDTH), lambda b: (0, 0)), pl.BlockSpec((8, LANES), lambda b: (0, 0))],
        out_specs=seq_block,
        out_shape=pltpu.HBM((n_seq * seq, POOL_WIDTH), BF16),
        compiler_params=_cparams(("parallel",), 40),
    )(u, w_pool, pool_scale, token)


def _outproj_norm(x2d, attn, pool, w_out, gain, tm):
    rows = x2d.shape[0]

    def body(x_ref, a_ref, p_ref, w_ref, g_ref, x2_ref, h_ref):
        x2 = x_ref[...] + _dot(a_ref[...], w_ref[:ATTN_WIDTH, :]) + _dot(p_ref[...], w_ref[ATTN_WIDTH:, :])
        x2_ref[...] = x2
        r = lax.rsqrt(jnp.mean(x2 * x2, axis=-1, keepdims=True) + EPS)
        h_ref[...] = (x2 * r * g_ref[...]).astype(BF16)

    row = lambda width: pl.BlockSpec((tm, width), lambda i: (i, 0))
    return pl.pallas_call(
        body, name="outproj_norm", grid=(rows // tm,),
        in_specs=[row(D_MODEL), row(ATTN_WIDTH), row(POOL_WIDTH), pl.BlockSpec((D_MODEL, D_MODEL), lambda i: (0, 0)),
                  pl.BlockSpec((1, D_MODEL), lambda i: (0, 0))],
        out_specs=[row(D_MODEL), row(D_MODEL)],
        out_shape=[pltpu.HBM((rows, D_MODEL), F32), pltpu.HBM((rows, D_MODEL), BF16)],
        compiler_params=_cparams(("parallel",), 40),
    )(x2d, attn, pool, w_out, gain)


def _resident(shape):
    return pl.BlockSpec(shape, lambda i: (0,) * len(shape), pipeline_mode=pl.Buffered(1))


def _mlp_forward_loss(h2, x2, w_up, w_down, gain, target, tm):
    rows = h2.shape[0]
    chunk = D_MODEL

    def body(h_ref, x_ref, up_ref, down_ref, g_ref, t_ref, slope_ref, f_ref, dx_ref, dxb_ref, loss_ref, dg_ref):
        @pl.when(pl.program_id(0) == 0)
        def _():
            loss_ref[...] = jnp.zeros_like(loss_ref)
            dg_ref[...] = jnp.zeros_like(dg_ref)

        h = h_ref[...]
        for c in range(D_FF // chunk):
            cols = slice(c * chunk, (c + 1) * chunk)
            r = jnp.maximum(_dot(h, up_ref[:, cols]), 0.0)
            slope_ref[:, cols] = (r + r).astype(BF16)
            f_ref[:, cols] = (r * r).astype(BF16)
        x3 = x_ref[...] + _dot(f_ref[...], down_ref[...])
        rn = lax.rsqrt(jnp.mean(x3 * x3, axis=-1, keepdims=True) + EPS)
        xhat = x3 * rn
        err = xhat * g_ref[...] - t_ref[...]
        loss_ref[...] += jnp.sum(err * err, axis=0, keepdims=True)
        dxhat = err * (g_ref[...] * (1.0 / D_MODEL))
        dx = rn * (dxhat - xhat * jnp.mean(dxhat * xhat, axis=-1, keepdims=True))
        dg_ref[...] += jnp.sum(err * xhat, axis=0, keepdims=True) * (1.0 / D_MODEL)
        dx_ref[...] = dx
        dxb_ref[...] = dx.astype(BF16)

    row = lambda width: pl.BlockSpec((tm, width), lambda i: (i, 0))
    vec = pl.BlockSpec((1, D_MODEL), lambda i: (0, 0))
    return pl.pallas_call(
        body, name="mlp_forward_loss", grid=(rows // tm,),
        in_specs=[row(D_MODEL), row(D_MODEL), _resident((D_MODEL, D_FF)), _resident((D_FF, D_MODEL)), vec, row(D_MODEL)],
        out_specs=[row(D_FF), row(D_FF), row(D_MODEL), row(D_MODEL), vec, vec],
        out_shape=[pltpu.HBM((rows, D_FF), BF16), pltpu.HBM((rows, D_FF), BF16),
                   pltpu.HBM((rows, D_MODEL), F32), pltpu.HBM((rows, D_MODEL), BF16),
                   jax.ShapeDtypeStruct((1, D_MODEL), F32), jax.ShapeDtypeStruct((1, D_MODEL), F32)],
        compiler_params=_cparams(("arbitrary",), 56),
    )(h2, x2, w_up, w_down, gain, target)


def _mlp_backward_data(dx3b, slope, w_down, w_up, dx3, x2, gain, w_out, attn, pool, token, tm):
    rows = dx3b.shape[0]
    steps = rows // tm
    chunk = D_MODEL

    def body(dxb_ref, slope_ref, down_ref, up_ref, dx3_ref, x2_ref, g_ref, wo_ref, attn_ref, pool_ref, token_ref,
             da_ref, dx2_ref, dattn_ref, dpool_ref, dg_ref, dwo_hbm, wire_hbm, dwo_acc, wire, sems):
        @pl.when(pl.program_id(0) == 0)
        def _():
            dg_ref[...] = jnp.zeros_like(dg_ref)
            dwo_acc[...] = jnp.zeros_like(dwo_acc)

        dxb = dxb_ref[...]
        for c in range(D_FF // chunk):
            cols = slice(c * chunk, (c + 1) * chunk)
            da_ref[:, cols] = (_dot_nt(dxb, down_ref[cols, :]) * slope_ref[:, cols].astype(F32)).astype(BF16)
        dnorm, dg = _rms_backward(_dot_nt(da_ref[...], up_ref[...]), x2_ref[...], g_ref[...])
        dg_ref[...] += dg
        dx2 = dx3_ref[...] + dnorm
        dx2_ref[...] = dx2
        dx2b = dx2.astype(BF16)
        dmix = _dot_nt(dx2b, wo_ref[...])
        dattn_ref[...] = dmix[:, :ATTN_WIDTH].astype(BF16)
        dpool_ref[...] = dmix[:, ATTN_WIDTH:]
        dwo_acc[:ATTN_WIDTH, :] += _dot_tn(attn_ref[...], dx2b)
        dwo_acc[ATTN_WIDTH:, :] += _dot_tn(pool_ref[...], dx2b)

        @pl.when(pl.program_id(0) == steps - 1)
        def _():
            done = pltpu.make_async_copy(dwo_acc, dwo_hbm, sems.at[0])
            done.start()
            wire[...] = dwo_acc[...].astype(BF16)
            sent = pltpu.make_async_copy(wire, wire_hbm, sems.at[1])
            sent.start()
            done.wait()
            sent.wait()

    row = lambda width: pl.BlockSpec((tm, width), lambda i: (i, 0))
    vec = pl.BlockSpec((1, D_MODEL), lambda i: (0, 0))
    return pl.pallas_call(
        body, name="mlp_backward_data", grid=(steps,),
        in_specs=[row(D_MODEL), row(D_FF), _resident((D_FF, D_MODEL)), _resident((D_MODEL, D_FF)), row(D_MODEL), row(D_MODEL), vec,
                  _resident((D_MODEL, D_MODEL)), row(ATTN_WIDTH), row(POOL_WIDTH), pl.BlockSpec((8, LANES), lambda i: (0, 0))],
        out_specs=[row(D_FF), row(D_MODEL), row(ATTN_WIDTH), row(POOL_WIDTH), vec, pl.BlockSpec(memory_space=pl.ANY),
                   pl.BlockSpec(memory_space=pl.ANY)],
        out_shape=[pltpu.HBM((rows, D_FF), BF16), pltpu.HBM((rows, D_MODEL), F32),
                   pltpu.HBM((rows, ATTN_WIDTH), BF16), pltpu.HBM((rows, POOL_WIDTH), F32),
                   jax.ShapeDtypeStruct((1, D_MODEL), F32), jax.ShapeDtypeStruct((D_MODEL, D_MODEL), F32),
                   jax.ShapeDtypeStruct((D_MODEL, D_MODEL), BF16)],
        scratch_shapes=[pltpu.VMEM((D_MODEL, D_MODEL), F32), pltpu.VMEM((D_MODEL, D_MODEL), BF16), pltpu.SemaphoreType.DMA((2,))],
        compiler_params=_cparams(("arbitrary",), 56),
    )(dx3b, slope, w_down, w_up, dx3, x2, gain, w_out, attn, pool, token)


def _weight_gradient(name, lhs, rhs, block_lhs, tm):
    rows = lhs.shape[0]
    steps = rows // tm
    out = (N_DEV, FF_BLOCK, rhs.shape[1]) if block_lhs else (N_DEV, lhs.shape[1], FF_BLOCK)

    def body(l_ref, r_ref, o_hbm, wire_hbm, acc, wire, sems):
        @pl.when(pl.program_id(0) == 0)
        def _():
            acc[...] = jnp.zeros_like(acc)

        for d in range(N_DEV):
            cols = slice(d * FF_BLOCK, (d + 1) * FF_BLOCK)
            acc[d] += _dot_tn(l_ref[:, cols], r_ref[...]) if block_lhs else _dot_tn(l_ref[...], r_ref[:, cols])

        @pl.when(pl.program_id(0) == steps - 1)
        def _():
            done = pltpu.make_async_copy(acc, o_hbm, sems.at[0])
            done.start()
            wire[...] = acc[...].astype(BF16)
            sent = pltpu.make_async_copy(wire, wire_hbm, sems.at[1])
            sent.start()
            done.wait()
            sent.wait()

    return pl.pallas_call(
        body, name=name, grid=(steps,),
        in_specs=[pl.BlockSpec((tm, lhs.shape[1]), lambda i: (i, 0)), pl.BlockSpec((tm, rhs.shape[1]), lambda i: (i, 0))],
        out_specs=[pl.BlockSpec(memory_space=pl.ANY), pl.BlockSpec(memory_space=pl.ANY)],
        out_shape=[jax.ShapeDtypeStruct(out, F32), jax.ShapeDtypeStruct(out, BF16)],
        scratch_shapes=[pltpu.VMEM(out, F32), pltpu.VMEM(out, BF16), pltpu.SemaphoreType.DMA((2,))],
        compiler_params=_cparams(("arbitrary",), 60),
    )(lhs, rhs)


def _attention_backward(sinks, q, kd, vd, dout, cos, sin, token, n_seq, seq):
    n_blocks = seq // BLOCK
    n_kv = N_Q_HEADS // Q_PER_KV
    chains = min(BACKWARD_CHAINS, n_blocks)

    def body(sink_ref, q_ref, k_ref, v_ref, do_ref, cos_ref, sin_ref, token_ref, dq_ref, dk_ref, dv_ref, dsink_ref,
             s_all, dp_all, dsc_all, dsp_all, pc_all, pp_all, dk_acc, dv_acc):
        low = lax.broadcasted_iota(jnp.int32, (BLOCK, LANES), 1) < HEAD_DIM

        @pl.when(pl.program_id(0) == 0)
        def _():
            dsink_ref[...] = jnp.zeros_like(dsink_ref)

        def fold(x):
            return x + pltpu.roll(x, HEAD_DIM, 1)

        def onto_keys(ref, kv, other):
            even, odd = _dot_tn(ref[2 * kv], other), _dot_tn(ref[2 * kv + 1], other)
            return fold(jnp.where(low, even, odd))

        def step(i, dsink):
            blocks = []
            for u in range(chains):
                n = i * chains + u
                r0 = pl.multiple_of(n * BLOCK, BLOCK)
                p0 = pl.multiple_of(jnp.maximum(n - 1, 0) * BLOCK, BLOCK)
                cur_mask, prev_mask = _window_masks(n)
                keys = [_window_operand(k_ref, r0, p0, kv) for kv in range(n_kv)]
                values = [_window_operand(v_ref, r0, p0, kv) for kv in range(n_kv)]
                q_rows = [_pair_rows(q_ref, r0, kv) for kv in range(n_kv)]
                do_rows = [_pair_rows(do_ref, r0, kv) for kv in range(n_kv)]
                for kv in range(n_kv):
                    both_s, both_dp = _dot_nt(q_rows[kv], keys[kv]), _dot_nt(do_rows[kv], values[kv])
                    for j in range(2):
                        wide_s, wide_dp = both_s[j * BLOCK:(j + 1) * BLOCK], both_dp[j * BLOCK:(j + 1) * BLOCK]
                        for parity in range(2):
                            head = u * N_Q_HEADS + 4 * kv + 2 * j + parity
                            s_all[head] = _merged_window(wide_s, parity, cur_mask, prev_mask, NEG)
                            dp_all[head] = _merged_window(wide_dp, parity, cur_mask, prev_mask, 0.0)
                blocks.append((n, r0, p0, keys, q_rows, do_rows))

            probs, p_sink = _softmax_with_sink(s_all[...], jnp.tile(sink_ref[:, :, 0:1], (chains, 1, 1)), True)
            dprobs = dp_all[...]
            delta = _lane_sums(probs * dprobs, True)
            dscores = (probs * (dprobs - delta)).astype(BF16)
            sink_terms = jnp.sum((p_sink * delta)[:, :, 0:1], axis=1, keepdims=True)
            probs = probs.astype(BF16)
            zero = jnp.zeros((BLOCK, BLOCK), BF16)
            for u in range(chains):
                dsink = dsink - sink_terms[u * N_Q_HEADS:(u + 1) * N_Q_HEADS]
                for head in range(N_Q_HEADS):
                    group, rows = 2 * (head // Q_PER_KV) + head % 2, pl.ds(((head % Q_PER_KV) // 2) * BLOCK, BLOCK)
                    ds_h, p_h = dscores[u * N_Q_HEADS + head], probs[u * N_Q_HEADS + head]
                    dsc_all[u, group, rows, :] = jnp.where(cur_mask, ds_h, zero)
                    dsp_all[u, group, rows, :] = jnp.where(cur_mask, zero, ds_h)
                    pc_all[u, group, rows, :] = jnp.where(cur_mask, p_h, zero)
                    pp_all[u, group, rows, :] = jnp.where(cur_mask, zero, p_h)

            for u, (n, r0, p0, keys, q_rows, do_rows) in enumerate(blocks):
                dsc_ref, dsp_ref, pc_ref, pp_ref = dsc_all.at[u], dsp_all.at[u], pc_all.at[u], pp_all.at[u]
                for kv in range(n_kv):
                    wide = jnp.concatenate([dsc_ref[2 * kv], dsc_ref[2 * kv + 1], dsp_ref[2 * kv], dsp_ref[2 * kv + 1]], axis=1)
                    both = _dot(wide, keys[kv]) * (HEAD_DIM ** -0.5)
                    for j in range(2):
                        dq = _rope(both[j * BLOCK:(j + 1) * BLOCK], cos_ref[pl.ds(r0, BLOCK), :], -sin_ref[pl.ds(r0, BLOCK), :])
                        dq_ref[pl.ds(r0, BLOCK), (2 * kv + j) * LANES:(2 * kv + j + 1) * LANES] = dq.astype(BF16)

                parts = []
                for kv in range(n_kv):
                    parts.append((onto_keys(dsc_ref, kv, q_rows[kv]), onto_keys(dsp_ref, kv, q_rows[kv]),
                                  onto_keys(pc_ref, kv, do_rows[kv]), onto_keys(pp_ref, kv, do_rows[kv])))
                dk_acc[pl.ds(r0, BLOCK), :] = jnp.where(low, parts[0][0], parts[1][0])
                dv_acc[pl.ds(r0, BLOCK), :] = jnp.where(low, parts[0][2], parts[1][2])

                @pl.when(n > 0)
                def _():
                    dk_acc[pl.ds(p0, BLOCK), :] += jnp.where(low, parts[0][1], parts[1][1])
                    dv_acc[pl.ds(p0, BLOCK), :] += jnp.where(low, parts[0][3], parts[1][3])

            return dsink

        dsink = lax.fori_loop(0, n_blocks // chains, step, jnp.zeros((N_Q_HEADS, 1, 1), F32))
        dsink_ref[...] += jnp.broadcast_to(dsink, dsink_ref.shape)
        dk_ref[...] = _rope(dk_acc[...], cos_ref[...], -sin_ref[...]).astype(BF16)
        dv_ref[...] = dv_acc[...].astype(BF16)

    seq_block = lambda width: pl.BlockSpec((seq, width), lambda b: (b, 0))
    table = pl.BlockSpec((seq, LANES), lambda b: (0, 0))
    per_head = pl.BlockSpec((N_Q_HEADS, 1, LANES), lambda b: (0, 0, 0))
    per_block = pltpu.VMEM((chains * N_Q_HEADS, BLOCK, BLOCK), F32)
    grouped = pltpu.VMEM((chains, 2 * n_kv, 2 * BLOCK, BLOCK), BF16)
    return pl.pallas_call(
        body, name="attention_backward", grid=(n_seq,),
        in_specs=[per_head, seq_block(ATTN_WIDTH), seq_block(2 * KV_WIDTH), seq_block(2 * KV_WIDTH),
                  seq_block(ATTN_WIDTH), table, table, pl.BlockSpec((8, LANES), lambda b: (0, 0))],
        out_specs=[seq_block(ATTN_WIDTH), seq_block(KV_WIDTH), seq_block(KV_WIDTH), per_head],
        out_shape=[pltpu.HBM((n_seq * seq, ATTN_WIDTH), BF16), pltpu.HBM((n_seq * seq, KV_WIDTH), BF16),
                   pltpu.HBM((n_seq * seq, KV_WIDTH), BF16), jax.ShapeDtypeStruct((N_Q_HEADS, 1, LANES), F32)],
        scratch_shapes=[per_block, per_block, grouped, grouped, grouped, grouped, pltpu.VMEM((seq, KV_WIDTH), F32), pltpu.VMEM((seq, KV_WIDTH), F32)],
        compiler_params=_cparams(("arbitrary",), 40),
    )(sinks, q, kd, vd, dout, cos, sin, token)


def _pool_backward(u, dpool, w_pool, pool_scale, n_seq, seq):
    groups = len(POOL_WINDOWS)

    def body(u_ref, dp_ref, w_ref, s_ref, du_ref, dw_ref, ds_ref):
        @pl.when(pl.program_id(0) == 0)
        def _():
            dw_ref[...] = jnp.zeros_like(dw_ref)
            ds_ref[...] = jnp.zeros_like(ds_ref)

        t = lax.broadcasted_iota(jnp.int32, (seq, 1), 0)
        for g, window in enumerate(POOL_WINDOWS):
            cols = slice(g * POOL_GROUP_DIM, (g + 1) * POOL_GROUP_DIM)
            d, count = _pool_features(u_ref[:, cols], window, t, seq)
            dpool_g = dp_ref[:, cols]
            ds_ref[:, cols] += jnp.sum(dpool_g * _dot(d, w_ref[g]), axis=0, keepdims=True)
            dy = (dpool_g * s_ref[:, cols]).astype(BF16)
            dw_ref[g] += _dot_tn(d, dy)
            dd = _dot_nt(dy, w_ref[g])
            du_ref[:, cols] = (_leading(dd / count, window, t, seq) - dd).astype(BF16)

    seq_block = pl.BlockSpec((seq, POOL_WIDTH), lambda b: (b, 0))
    weights = pl.BlockSpec((groups, POOL_GROUP_DIM, POOL_GROUP_DIM), lambda b: (0, 0, 0))
    scale = pl.BlockSpec((1, POOL_WIDTH), lambda b: (0, 0))
    return pl.pallas_call(
        body, name="pool_backward", grid=(n_seq,),
        in_specs=[seq_block, seq_block, weights, scale],
        out_specs=[seq_block, weights, scale],
        out_shape=[pltpu.HBM((n_seq * seq, POOL_WIDTH), BF16),
                   jax.ShapeDtypeStruct((groups, POOL_GROUP_DIM, POOL_GROUP_DIM), F32), jax.ShapeDtypeStruct((1, POOL_WIDTH), F32)],
        compiler_params=_cparams(("arbitrary",), 40),
    )(u, dpool, w_pool, pool_scale)


def _inproj_gradient(dq, dk, dv, du, h1, tm):
    rows = h1.shape[0]

    def body(dq_ref, dk_ref, dv_ref, du_ref, h_ref, dw_ref):
        @pl.when(pl.program_id(0) == 0)
        def _():
            dw_ref[...] = jnp.zeros_like(dw_ref)

        dproj = jnp.concatenate([dq_ref[...], dk_ref[...], dv_ref[...], du_ref[...]], axis=1)
        dw_ref[...] += _dot_tn(h_ref[...], dproj)

    row = lambda width: pl.BlockSpec((tm, width), lambda i: (i, 0))
    return pl.pallas_call(
        body, name="inproj_gradient", grid=(rows // tm,),
        in_specs=[row(ATTN_WIDTH), row(KV_WIDTH), row(KV_WIDTH), row(POOL_WIDTH), row(D_MODEL)],
        out_specs=pl.BlockSpec((D_MODEL, IN_WIDTH), lambda i: (0, 0)),
        out_shape=jax.ShapeDtypeStruct((D_MODEL, IN_WIDTH), F32),
        compiler_params=_cparams(("arbitrary",), 48),
    )(dq, dk, dv, du, h1)


def _inproj_backward(dq, dk, dv, du, w_in, x2d, dx2, gain, token, tm):
    rows = x2d.shape[0]

    def body(dq_ref, dk_ref, dv_ref, du_ref, w_ref, x_ref, dx2_ref, g_ref, token_ref, dx_ref, dg_ref):
        @pl.when(pl.program_id(0) == 0)
        def _():
            dg_ref[...] = jnp.zeros_like(dg_ref)

        dproj = jnp.concatenate([dq_ref[...], dk_ref[...], dv_ref[...], du_ref[...]], axis=1)
        dnorm, dg = _rms_backward(_dot_nt(dproj, w_ref[...]), x_ref[...], g_ref[...])
        dg_ref[...] += dg
        dx_ref[...] = dx2_ref[...] + dnorm

    row = lambda width: pl.BlockSpec((tm, width), lambda i: (i, 0))
    vec = pl.BlockSpec((1, D_MODEL), lambda i: (0, 0))
    return pl.pallas_call(
        body, name="inproj_backward", grid=(rows // tm,),
        in_specs=[row(ATTN_WIDTH), row(KV_WIDTH), row(KV_WIDTH), row(POOL_WIDTH), pl.BlockSpec((D_MODEL, IN_WIDTH), lambda i: (0, 0)),
                  row(D_MODEL), row(D_MODEL), vec, pl.BlockSpec((8, LANES), lambda i: (0, 0))],
        out_specs=[row(D_MODEL), vec],
        out_shape=[pltpu.HBM((rows, D_MODEL), F32), jax.ShapeDtypeStruct((1, D_MODEL), F32)],
        compiler_params=_cparams(("arbitrary",), 60),
    )(dq, dk, dv, du, w_in, x2d, dx2, gain, token)


def _place():
    return lax.axis_index("x"), lax.axis_index("y"), lax.axis_index("c")


def _peer(x, y, c, rel):
    return (1 - x if rel & 4 else x, 1 - y if rel & 2 else y, 1 - c if rel & 1 else c)


def _index(px, py, pc):
    return 4 * px + 2 * py + pc


def _row_slot(ref, d):
    return ref.at[d]


def _column_slot(ref, d):
    return ref.at[:, pl.ds(pl.multiple_of(d * FF_BLOCK, FF_BLOCK), FF_BLOCK)]


def _stage_weights(shards, slots, shapes):
    n = len(shards)

    def body(*refs):
        ins, outs, stage, sems = refs[:n], refs[n:2 * n], refs[2 * n:3 * n], refs[3 * n]
        me = _index(*_place())
        mine = []
        for a in range(n):
            stage[a][...] = ins[a][...].astype(BF16)
            mine.append(pltpu.make_async_copy(stage[a], slots[a](outs[a], me), sems.at[a]))
            mine[-1].start()
        for cp in mine:
            cp.wait()

    return pl.pallas_call(
        body, name="stage_weights",
        in_specs=[pl.BlockSpec(memory_space=pltpu.VMEM)] * n,
        out_specs=[pl.BlockSpec(memory_space=pl.ANY)] * n,
        out_shape=[pltpu.HBM(shape, BF16) for shape in shapes],
        scratch_shapes=[pltpu.VMEM(s.shape, BF16) for s in shards] + [pltpu.SemaphoreType.DMA((n,))],
        compiler_params=pltpu.CompilerParams(vmem_limit_bytes=32 * MIB),
    )(*shards)


ALL_PEERS = tuple(range(1, N_DEV))
FIRST_HOP = (1, 2, 4, 6)
OTHER_CHIPS = (2, 4, 6)


class _Plan:
    def __init__(self, per_array, copies):
        self.per_array, self.copies = per_array, copies


def _own_slot_first_hop(slots):
    def copies(x, y, c, ins, lands):
        me = _index(x, y, c)
        return [(slots[a](lands[a], me), slots[a](lands[a], me), _peer(x, y, c, rel)) for rel in FIRST_HOP for a in range(len(lands))]
    return _Plan(len(FIRST_HOP), copies)


def _landed_to_sibling(slots):
    def copies(x, y, c, ins, lands):
        blocks = [_index(*_peer(x, y, c, rel)) for rel in OTHER_CHIPS]
        return [(slots[a](lands[a], b), slots[a](lands[a], b), (x, y, 1 - c)) for b in blocks for a in range(len(lands))]
    return _Plan(len(OTHER_CHIPS), copies)


def _whole_to_all_copies(x, y, c, ins, lands):
    return [(ins[a], lands[a].at[rel - 1], _peer(x, y, c, rel)) for rel in ALL_PEERS for a in range(len(lands))]


def _block_to_owner_copies(x, y, c, ins, lands):
    return [(ins[a].at[_index(*_peer(x, y, c, rel))], lands[a].at[rel - 1], _peer(x, y, c, rel))
            for rel in ALL_PEERS for a in range(len(lands))]


_whole_to_all = _Plan(len(ALL_PEERS), _whole_to_all_copies)
_block_to_owner = _Plan(len(ALL_PEERS), _block_to_owner_copies)


def _split_copies(plan, ins, lands, send_sems, recv_sems):
    return [pltpu.make_async_remote_copy(src_ref=src, dst_ref=dst, send_sem=send_sems.at[k], recv_sem=recv_sems.at[k],
                                         device_id=to, device_id_type=MESH)
            for k, (src, dst, to) in enumerate(plan.copies(*_place(), ins, lands))]


HBM_SPEC = pl.BlockSpec(memory_space=pltpu.HBM)
SEM_SPEC = pl.BlockSpec(memory_space=pltpu.SEMAPHORE)
EFFECT = pltpu.SideEffectType.DATAFLOW_SIDE_EFFECTING


def _start_copies(name, plan, ins, lands, after):
    n_in, n = len(ins), len(ins) + len(lands)

    def body(*refs):
        send_sems, recv_sems = refs[n + 1], refs[n + 2]
        for cp in _split_copies(plan, refs[:n_in], refs[n_in:n], send_sems, recv_sems):
            cp.start()
        refs[-1][...] = jnp.zeros_like(refs[-1])

    arrays = [pltpu.with_memory_space_constraint(v, pltpu.HBM) for v in (*ins, *lands)]
    sems = pltpu.SemaphoreType.DMA((len(lands) * plan.per_array,))
    send_sems, recv_sems, *flying, token = pl.pallas_call(
        body, name=name,
        out_shape=(sems, sems, *[pltpu.HBM(v.shape, v.dtype) for v in arrays], jax.ShapeDtypeStruct((8, LANES), F32)),
        in_specs=[HBM_SPEC] * n + [pl.BlockSpec(memory_space=pl.ANY)],
        out_specs=(SEM_SPEC, SEM_SPEC, *[HBM_SPEC] * n, pl.BlockSpec(memory_space=pltpu.VMEM)),
        input_output_aliases={i: 2 + i for i in range(n)},
        compiler_params=pltpu.CompilerParams(has_side_effects=EFFECT),
    )(*arrays, after)
    return send_sems, recv_sems, flying, token


def _wait_copies(name, plan, n_in, send_sems, recv_sems, flying, after):
    n = len(flying)

    def body(*refs):
        for cp in _split_copies(plan, refs[:n_in], refs[n_in:n], refs[n], refs[n + 1]):
            cp.wait_send()
            cp.wait_recv()

    landed = pl.pallas_call(
        body, name=name, out_shape=tuple(pltpu.HBM(v.shape, v.dtype) for v in flying),
        in_specs=[HBM_SPEC] * n + [SEM_SPEC, SEM_SPEC, pl.BlockSpec(memory_space=pl.ANY)], out_specs=tuple([HBM_SPEC] * n),
        input_output_aliases={i: i for i in range(n)},
        compiler_params=pltpu.CompilerParams(has_side_effects=EFFECT),
    )(*flying, send_sems, recv_sems, after)
    return landed[:n_in], landed[n_in:]


def _adamw_math(w, g, m, v):
    m = ADAM_B1 * m + (1.0 - ADAM_B1) * g
    v = ADAM_B2 * v + (1.0 - ADAM_B2) * (g * g)
    m_hat = m / (1.0 - ADAM_B1 ** ADAM_STEP)
    v_hat = v / (1.0 - ADAM_B2 ** ADAM_STEP)
    return -ADAM_LR * (m_hat / (jnp.sqrt(v_hat) + ADAM_EPS) + ADAM_WD * w), m, v


def _adamw_sharded(me, own, received, w, m, v, tr):
    rows, cols = w.shape
    own, received, w, m, v = _in_hbm(own, received, w, m, v)

    def body(me_ref, own_ref, rec_ref, w_ref, m_ref, v_ref, g_ref, d_ref, nm_ref, nv_ref):
        g = own_ref[...]
        for r in range(N_DEV - 1):
            g = g + rec_ref[r].astype(F32)
        g_ref[...] = g
        d_ref[...], nm_ref[...], nv_ref[...] = _adamw_math(w_ref[...], g, m_ref[...], v_ref[...])

    tile = pl.BlockSpec((tr, cols), lambda i, me_ref: (i, 0))
    shape = pltpu.HBM((rows, cols), F32)
    return pl.pallas_call(
        body, name="adamw_sharded",
        grid_spec=pltpu.PrefetchScalarGridSpec(
            num_scalar_prefetch=1, grid=(rows // tr,),
            in_specs=[pl.BlockSpec((None, tr, cols), lambda i, me_ref: (me_ref[0], i, 0)),
                      pl.BlockSpec((N_DEV - 1, tr, cols), lambda i, me_ref: (0, i, 0)), tile, tile, tile],
            out_specs=[tile, tile, tile, tile]),
        out_shape=[shape, shape, shape, shape],
        compiler_params=_cparams(("parallel",), 40),
    )(me, own, received, w, m, v)


VECTOR_ROWS = D_MODEL // LANES
POOL_ROWS = len(POOL_WINDOWS) * POOL_GROUP_DIM


def _pack_small(dg1, dg2, dg3, dps, dsink, loss_cols, dwp):
    def body(g1_ref, g2_ref, g3_ref, ps_ref, sink_ref, loss_ref, wp_ref, o_ref):
        o_ref[...] = jnp.zeros_like(o_ref)
        for base, ref, n in ((ROW_G1, g1_ref, VECTOR_ROWS), (ROW_G2, g2_ref, VECTOR_ROWS), (ROW_G3, g3_ref, VECTOR_ROWS),
                             (ROW_LOSS, loss_ref, VECTOR_ROWS), (ROW_PS, ps_ref, POOL_WIDTH // LANES)):
            for r in range(n):
                o_ref[base + r:base + r + 1, :] = ref[:, r * LANES:(r + 1) * LANES]
        heads = sink_ref[:, 0, :]
        on_diagonal = lax.broadcasted_iota(jnp.int32, heads.shape, 0) == lax.broadcasted_iota(jnp.int32, heads.shape, 1)
        o_ref[ROW_SINK:ROW_SINK + 1, :] = jnp.sum(jnp.where(on_diagonal, heads, 0.0), axis=0, keepdims=True)
        o_ref[ROW_WP:ROW_WP + POOL_ROWS, :] = wp_ref[...].reshape(POOL_ROWS, LANES)

    return pl.pallas_call(body, name="pack_small", out_shape=jax.ShapeDtypeStruct((SMALL_ROWS, LANES), F32))(
        dg1, dg2, dg3, dps, dsink, loss_cols, dwp)


def _finish_small(me, own, landed, params):
    flat = [a for group in params for a in group]

    def body(me_ref, own_ref, landed_ref, *refs):
        ins, outs = refs[:len(flat)], refs[len(flat):]
        total = None
        for source in range(N_DEV):
            rel = jnp.bitwise_xor(me_ref[0], source)
            piece = jnp.where(rel == 0, own_ref[...], landed_ref[jnp.maximum(rel, 1) - 1])
            total = piece if total is None else total + piece
        outs[0][...] = (0.5 / D_MODEL) * jnp.sum(jnp.sum(total[ROW_LOSS:ROW_LOSS + VECTOR_ROWS], axis=1, keepdims=True), axis=0, keepdims=True)
        row = lambda base, n: jnp.concatenate([total[base + r:base + r + 1, :] for r in range(n)], axis=1)
        grads = [row(ROW_G1, VECTOR_ROWS), row(ROW_G2, VECTOR_ROWS), row(ROW_G3, VECTOR_ROWS), row(ROW_PS, POOL_WIDTH // LANES),
                 total[ROW_SINK:ROW_SINK + 1, :N_Q_HEADS], total[ROW_WP:ROW_WP + POOL_ROWS].reshape(params[5][0].shape)]
        for k, g in enumerate(grads):
            w_ref, m_ref, v_ref = ins[3 * k:3 * k + 3]
            g_out, d_out, m_out, v_out = outs[1 + 4 * k:5 + 4 * k]
            g_out[...] = g
            d_out[...], m_out[...], v_out[...] = _adamw_math(w_ref[...], g, m_ref[...], v_ref[...])

    shapes = [jax.ShapeDtypeStruct((1, 1), F32)] + [jax.ShapeDtypeStruct(w.shape, F32) for w, _, _ in params for _ in range(4)]
    vmem = pl.BlockSpec(memory_space=pltpu.VMEM)
    res = pl.pallas_call(body, name="finish_small", in_specs=[pl.BlockSpec(memory_space=pltpu.SMEM)] + [vmem] * (2 + len(flat)),
                         out_shape=shapes)(me, own, landed, *flat)
    return res[0], [res[1 + 4 * k:5 + 4 * k] for k in range(len(params))]


def _local_step(x, target, attn_norm_g, attn_sinks, w_pool, pool_scale, mlp_norm_g, final_norm_g,
                front_token, first_weight, second_hop, later_weights, ship_down, ship_up, ship_in):
    n_seq, seq, _ = x.shape
    rows = n_seq * seq
    tm, tm_mlp, tm_norm = min(1024, seq), min(256, seq), min(2048, seq)
    tm_grad = tm
    x2d, t2d = x.reshape(rows, D_MODEL), target.reshape(rows, D_MODEL)
    g3 = final_norm_g.reshape(1, D_MODEL)
    cos, sin = _rope_tables(seq)
    wp_b = w_pool[0].astype(BF16)

    sink_rows = jnp.broadcast_to(attn_sinks.reshape(N_Q_HEADS, 1, 1), (N_Q_HEADS, 1, LANES))
    cos, sin = _in_hbm(cos, sin)
    h1 = _in_hbm(_first_norm(x2d, attn_norm_g, front_token, tm_norm))
    w_in_full = _in_hbm(first_weight(h1))
    q, kd, vd, u = _in_hbm(*_inproj(h1, w_in_full, cos, sin, seq, tm))
    attn = _in_hbm(_attention_forward(sink_rows, q, kd, vd, n_seq, seq))
    pool = _in_hbm(_pool_forward(u, wp_b, pool_scale, second_hop(attn), n_seq, seq))
    w_out_full, w_up_full, w_down_full = _in_hbm(*later_weights(pool))
    x2, h2 = _in_hbm(*_outproj_norm(x2d, attn, pool, w_out_full, mlp_norm_g, tm))
    slope, f, dx3, dx3b, loss_cols, dg3 = _mlp_forward_loss(h2, x2, w_up_full, w_down_full, g3, t2d, tm_mlp)
    slope, f, dx3, dx3b = _in_hbm(slope, f, dx3, dx3b)

    down_token = ship_down(_weight_gradient("down_gradient", f, dx3b, True, tm_grad))
    da, dx2, dattn, dpool, dg2, d_w_out, d_w_out_wire = _mlp_backward_data(
        dx3b, slope, w_down_full, w_up_full, dx3, x2, mlp_norm_g, w_out_full, attn, pool, down_token, tm_mlp)
    da, dx2, dattn, dpool = _in_hbm(da, dx2, dattn, dpool)
    up_token = ship_up((d_w_out, d_w_out_wire), _weight_gradient("up_gradient", h2, da, False, tm_grad))
    dq, dk, dv, dsink = _attention_backward(sink_rows, q, kd, vd, dattn, cos, sin, up_token, n_seq, seq)
    dq, dk, dv = _in_hbm(dq, dk, dv)
    du, d_w_pool, d_pool_scale = _pool_backward(u, dpool, wp_b, pool_scale, n_seq, seq)
    du = _in_hbm(du)
    in_token = ship_in(_inproj_gradient(dq, dk, dv, du, h1, tm_grad))
    grad_x, dg1 = _inproj_backward(dq, dk, dv, du, w_in_full, x2d, dx2, attn_norm_g, in_token, tm)
    return grad_x.reshape(x.shape), _pack_small(dg1, dg2, dg3, d_pool_scale, dsink, loss_cols, d_w_pool)


def kernel(x, attn_norm_g, w_in, attn_sinks, w_pool, pool_scale, w_out, mlp_norm_g, w_up, w_down, final_norm_g, loss_target, m_attn_norm_g, m_w_in, m_attn_sinks, m_w_pool, m_pool_scale, m_w_out, m_mlp_norm_g, m_w_up, m_w_down, m_final_norm_g, v_attn_norm_g, v_w_in, v_attn_sinks, v_w_pool, v_pool_scale, v_w_out, v_mlp_norm_g, v_w_up, v_w_down, v_final_norm_g):
    me = (4 * lax.axis_index("x") + 2 * lax.axis_index("y") + lax.axis_index("c")).astype(jnp.int32).reshape(1)

    unordered = jnp.zeros((8, LANES), F32)

    win_land, wout_land, wup_land, wdown_land = _stage_weights(
        [w_in[0], w_out[0], w_up[0], w_down[0]], [_row_slot, _row_slot, _column_slot, _row_slot],
        [(N_DEV, D_MODEL, IN_BLOCK), (N_DEV, OUT_BLOCK, D_MODEL), (D_MODEL, D_FF), (N_DEV, FF_BLOCK, D_MODEL)])
    in_slots, later_slots = [_row_slot], [_row_slot, _column_slot, _row_slot]
    in_copies = _start_copies("spread_in_start", _own_slot_first_hop(in_slots), [], [win_land], unordered)
    later_copies = _start_copies("spread_later_start", _own_slot_first_hop(later_slots), [], [wout_land, wup_land, wdown_land], in_copies[3])

    def first_weight(after):
        _, in_landed = _wait_copies("spread_in_wait", _own_slot_first_hop(in_slots), 0, *in_copies[:3], after)
        in_passed = _start_copies("pass_in_start", _landed_to_sibling(in_slots), [], in_landed, unordered)
        _, (win_g,) = _wait_copies("pass_in_wait", _landed_to_sibling(in_slots), 0, *in_passed[:3], in_passed[3])
        return jnp.transpose(win_g, (1, 0, 2)).reshape(D_MODEL, IN_WIDTH)

    passing = []

    def second_hop(after):
        _, later_landed = _wait_copies("spread_later_wait", _own_slot_first_hop(later_slots), 0, *later_copies[:3], after)
        passing.extend(_start_copies("pass_later_start", _landed_to_sibling(later_slots), [], later_landed, unordered))
        return passing[3]

    def later_weights(after):
        _, (wout_g, wup_g, wdown_g) = _wait_copies("pass_later_wait", _landed_to_sibling(later_slots), 0, *passing[:3], after)
        return wout_g.reshape(D_MODEL, D_MODEL), wup_g, wdown_g.reshape(D_FF, D_MODEL)

    deliveries, kept = {}, {}

    def deliver(name, plan, wires):
        lands = [lax.empty((N_DEV - 1,) + (g.shape[1:] if plan is _block_to_owner else g.shape), g.dtype) for g in wires]
        deliveries[name] = _start_copies(name + "_start", plan, wires, lands, unordered)
        return deliveries[name][3]

    def landed(name, plan, after):
        send, recv, flying, _ = deliveries[name]
        return _wait_copies(name + "_wait", plan, len(flying) // 2, send, recv, flying, after)

    def ship_down(d_w_down):
        kept["down"] = d_w_down[0]
        return deliver("deliver_down", _block_to_owner, [d_w_down[1]])

    def ship_up(d_w_out, d_w_up):
        kept["out"], kept["up"] = d_w_out[0].reshape(N_DEV, OUT_BLOCK, D_MODEL), d_w_up[0]
        return deliver("deliver_up", _block_to_owner, [d_w_out[1].reshape(N_DEV, OUT_BLOCK, D_MODEL), d_w_up[1]])

    def ship_in(d_w_in):
        kept["in"] = jnp.transpose(d_w_in.reshape(D_MODEL, N_DEV, IN_BLOCK), (1, 0, 2))
        return deliver("deliver_in", _block_to_owner, [kept["in"].astype(BF16)])

    grad_x, small = _local_step(x, loss_target, attn_norm_g, attn_sinks, w_pool, pool_scale, mlp_norm_g, final_norm_g,
                                later_copies[3], first_weight, second_hop, later_weights, ship_down, ship_up, ship_in)
    small_token = deliver("deliver_small", _whole_to_all, [small])

    _, (got_down,) = landed("deliver_down", _block_to_owner, small_token)
    _, (got_out, got_up) = landed("deliver_up", _block_to_owner, small_token)
    g_down = _adamw_sharded(me, kept["down"], got_down, w_down[0], m_w_down[0], v_w_down[0], 256)
    g_up = _adamw_sharded(me, kept["up"], got_up, w_up[0], m_w_up[0], v_w_up[0], 256)
    g_out = _adamw_sharded(me, kept["out"], got_out, w_out[0], m_w_out[0], v_w_out[0], 128)
    _, (got_in,) = landed("deliver_in", _block_to_owner, g_out[0])
    g_in = _adamw_sharded(me, kept["in"], got_in, w_in[0], m_w_in[0], v_w_in[0], 1024)
    (own_small,), (got_small,) = landed("deliver_small", _whole_to_all, g_in[0])

    row = lambda a: a.reshape(1, D_MODEL)
    params = [(attn_norm_g, m_attn_norm_g, v_attn_norm_g), (mlp_norm_g, m_mlp_norm_g, v_mlp_norm_g),
              (row(final_norm_g), row(m_final_norm_g), row(v_final_norm_g)), (pool_scale, m_pool_scale, v_pool_scale),
              (attn_sinks, m_attn_sinks, v_attn_sinks), (w_pool[0], m_w_pool[0], v_w_pool[0])]
    loss, (s_norm1, s_norm2, s_norm3, s_scale, s_sinks, s_pool) = _finish_small(me, own_small, got_small, params)
    s_norm3 = [a.reshape(D_MODEL) for a in s_norm3]
    s_pool = [a[None] for a in s_pool]

    def ordered(k):
        return [s_norm1[k], g_in[k][None], s_sinks[k], s_pool[k], s_scale[k], g_out[k][None], s_norm2[k], g_up[k][None], g_down[k][None],
                s_norm3[k]]

    return (loss.reshape(()), grad_x, *ordered(0), *ordered(1), *ordered(2), *ordered(3))
```

```python
import jax
import jax.numpy as jnp
from jax import lax
from jax.experimental import pallas as pl
from jax.experimental.pallas import tpu as pltpu

F32 = jnp.float32
BF16 = jnp.bfloat16

D_MODEL = 1024
HEAD_DIM = 64
N_Q_HEADS = 8
Q_PER_KV = 4
ATTN_WIDTH = 512
KV_WIDTH = 128
BLOCK = 128
ROPE_THETA = 10000.0
POOL_WINDOWS = (2, 4, 8, 16)
POOL_WIDTH = 512
POOL_GROUP_DIM = 128
IN_WIDTH = 1280
D_FF = 4096
EPS = 1e-6
N_DEV = 8
FF_BLOCK = D_FF // N_DEV
IN_BLOCK = IN_WIDTH // N_DEV
OUT_BLOCK = D_MODEL // N_DEV
ADAM_LR = 0.001
ADAM_B1 = 0.9
ADAM_B2 = 0.999
ADAM_EPS = 1e-08
ADAM_WD = 0.01
ADAM_STEP = 10
NEG = -1e30
FORWARD_CHAINS = 4
BACKWARD_CHAINS = 2
LANES = 128
MIB = 1024 * 1024
MESH = pl.DeviceIdType.MESH

ROW_G1, ROW_G2, ROW_G3, ROW_PS, ROW_SINK, ROW_LOSS, ROW_WP, SMALL_ROWS = 0, 8, 16, 24, 32, 40, 48, 560


def _cparams(semantics, vmem_mib):
    return pltpu.CompilerParams(dimension_semantics=semantics, vmem_limit_bytes=vmem_mib * MIB)


def _in_hbm(*arrays):
    pinned = tuple(pltpu.with_memory_space_constraint(a, pltpu.HBM) for a in arrays)
    return pinned[0] if len(pinned) == 1 else pinned


def _dot(a, b):
    return jnp.dot(a, b, preferred_element_type=F32)


def _dot_nt(a, b):
    return lax.dot_general(a, b, (((1,), (1,)), ((), ())), preferred_element_type=F32)


def _dot_tn(a, b):
    return lax.dot_general(a, b, (((0,), (0,)), ((), ())), preferred_element_type=F32)


def _swap_halves(x):
    width = x.shape[1]
    lane = lax.broadcasted_iota(jnp.int32, x.shape, 1)
    ahead = pltpu.roll(x, width - HEAD_DIM // 2, 1)
    behind = pltpu.roll(x, HEAD_DIM // 2, 1)
    return jnp.where(lane % HEAD_DIM < HEAD_DIM // 2, ahead, behind)


def _rope(x, cos, sin):
    reps = x.shape[1] // LANES
    if reps > 1:
        cos = jnp.tile(cos, (1, reps))
        sin = jnp.tile(sin, (1, reps))
    return x * cos + _swap_halves(x) * sin


def _rope_tables(seq):
    half = HEAD_DIM // 2
    inv_freq = ROPE_THETA ** (-jnp.arange(half, dtype=F32) / half)
    ang = jnp.arange(seq).astype(F32)[:, None] * inv_freq[None, :]
    cos, sin = jnp.cos(ang), jnp.sin(ang)
    cos = jnp.tile(cos, (1, LANES // half))
    sin = jnp.tile(jnp.concatenate([-sin, sin], axis=1), (1, LANES // HEAD_DIM))
    return cos, sin


def _both_halves(x):
    lane = lax.broadcasted_iota(jnp.int32, x.shape, 1)
    other = pltpu.roll(x, HEAD_DIM, 1)
    low = lane < HEAD_DIM
    return jnp.where(low, x, other), jnp.where(low, other, x)


def _rms_backward(dh, xin, gain):
    r = lax.rsqrt(jnp.mean(xin * xin, axis=-1, keepdims=True) + EPS)
    xhat = xin * r
    dxhat = dh * gain
    dx = r * (dxhat - xhat * jnp.mean(dxhat * xhat, axis=-1, keepdims=True))
    return dx, jnp.sum(dh * xhat, axis=0, keepdims=True)


def _first_norm(x2d, gain, token, tm):
    rows = x2d.shape[0]

    def body(x_ref, g_ref, token_ref, h_ref):
        x = x_ref[...]
        r = lax.rsqrt(jnp.mean(x * x, axis=-1, keepdims=True) + EPS)
        h_ref[...] = (x * r * g_ref[...]).astype(BF16)

    row = pl.BlockSpec((tm, D_MODEL), lambda i: (i, 0))
    return pl.pallas_call(
        body, name="first_norm", grid=(rows // tm,),
        in_specs=[row, pl.BlockSpec((1, D_MODEL), lambda i: (0, 0)), pl.BlockSpec((8, LANES), lambda i: (0, 0))],
        out_specs=row, out_shape=pltpu.HBM((rows, D_MODEL), BF16),
        compiler_params=_cparams(("parallel",), 40),
    )(x2d, gain, token)


def _inproj(h1, w_in, cos, sin, seq, tm):
    rows = h1.shape[0]
    tiles_per_seq = seq // tm

    def body(h_ref, w_ref, cos_ref, sin_ref, q_ref, k_ref, v_ref, u_ref):
        proj = _dot_nt(h_ref[...], w_ref[...])
        cos_t, sin_t = cos_ref[...], sin_ref[...]
        q = _rope(proj[:, :ATTN_WIDTH], cos_t, sin_t) * (HEAD_DIM ** -0.5)
        q_ref[...] = q.astype(BF16)
        k = _rope(proj[:, ATTN_WIDTH:ATTN_WIDTH + KV_WIDTH], cos_t, sin_t)
        k0, k1 = _both_halves(k)
        k_ref[...] = jnp.concatenate([k0, k1], axis=1).astype(BF16)
        v0, v1 = _both_halves(proj[:, ATTN_WIDTH + KV_WIDTH:ATTN_WIDTH + 2 * KV_WIDTH])
        v_ref[...] = jnp.concatenate([v0, v1], axis=1).astype(BF16)
        u_ref[...] = proj[:, ATTN_WIDTH + 2 * KV_WIDTH:]

    row = lambda width: pl.BlockSpec((tm, width), lambda i: (i, 0))
    table = pl.BlockSpec((tm, LANES), lambda i: (i % tiles_per_seq, 0))
    return pl.pallas_call(
        body, name="inproj", grid=(rows // tm,),
        in_specs=[row(D_MODEL), pl.BlockSpec((IN_WIDTH, D_MODEL), lambda i: (0, 0)), table, table],
        out_specs=[row(ATTN_WIDTH), row(2 * KV_WIDTH), row(2 * KV_WIDTH), row(POOL_WIDTH)],
        out_shape=[pltpu.HBM((rows, ATTN_WIDTH), BF16), pltpu.HBM((rows, 2 * KV_WIDTH), BF16),
                   pltpu.HBM((rows, 2 * KV_WIDTH), BF16), pltpu.HBM((rows, POOL_WIDTH), F32)],
        compiler_params=_cparams(("parallel",), 40),
    )(h1, w_in, cos, sin)


def _window_masks(n):
    qi = lax.broadcasted_iota(jnp.int32, (BLOCK, BLOCK), 0)
    kj = lax.broadcasted_iota(jnp.int32, (BLOCK, BLOCK), 1)
    return kj <= qi, jnp.logical_and(kj > qi, n > 0)


def _window_operand(ref, r0, p0, kv):
    low = lax.broadcasted_iota(jnp.int32, (BLOCK, LANES), 1) < HEAD_DIM
    cur = ref[pl.ds(r0, BLOCK), kv * LANES:(kv + 1) * LANES]
    prev = ref[pl.ds(p0, BLOCK), kv * LANES:(kv + 1) * LANES]
    zero = jnp.zeros_like(cur)
    return jnp.concatenate([jnp.where(low, cur, zero), jnp.where(low, zero, cur), jnp.where(low, prev, zero), jnp.where(low, zero, prev)], axis=0)


def _pair_rows(ref, r0, kv):
    return jnp.concatenate([ref[pl.ds(r0, BLOCK), (2 * kv + j) * LANES:(2 * kv + j + 1) * LANES] for j in range(2)], axis=0)


def _merged_window(wide, parity, cur_mask, prev_mask, fill):
    cur = wide[:, parity * LANES:(parity + 1) * LANES]
    prev = wide[:, (2 + parity) * LANES:(3 + parity) * LANES]
    return jnp.where(cur_mask, cur, jnp.where(prev_mask, prev, fill))


def _lane_sums(x, one_matmul):
    flat = x.reshape(-1, x.shape[-1])
    high = flat.astype(BF16)
    low = (flat - high.astype(F32)).astype(BF16)
    if one_matmul:
        sums = _dot(jnp.concatenate([high, low], axis=1), jnp.ones((2 * x.shape[-1], LANES), BF16))
    else:
        ones = jnp.ones((x.shape[-1], LANES), BF16)
        sums = _dot(high, ones) + _dot(low, ones)
    return sums.reshape(x.shape[:-1] + (LANES,))


def _softmax_with_sink(scores, sink, one_matmul):
    m = jnp.broadcast_to(jnp.maximum(jnp.max(scores, axis=-1, keepdims=True), sink), scores.shape)
    p, ps = jnp.exp(scores - m), jnp.exp(sink - m)
    inv = 1.0 / (_lane_sums(p, one_matmul) + ps)
    return p * inv, ps * inv


def _attention_forward(sinks, q, kd, vd, n_seq, seq):
    n_blocks = seq // BLOCK
    n_pairs = N_Q_HEADS // 2
    chains = min(FORWARD_CHAINS, n_blocks)

    def body(sink_ref, q_ref, k_ref, v_ref, o_ref, s_ref, p_ref):
        def step(i, carry):
            starts, values = [], []
            for u in range(chains):
                n = i * chains + u
                r0 = pl.multiple_of(n * BLOCK, BLOCK)
                p0 = pl.multiple_of(jnp.maximum(n - 1, 0) * BLOCK, BLOCK)
                cur_mask, prev_mask = _window_masks(n)
                keys = [_window_operand(k_ref, r0, p0, kv) for kv in range(2)]
                starts.append(r0)
                values.append([_window_operand(v_ref, r0, p0, kv) for kv in range(2)])
                for kv in range(2):
                    both = _dot_nt(_pair_rows(q_ref, r0, kv), keys[kv])
                    for j in range(2):
                        wide = both[j * BLOCK:(j + 1) * BLOCK]
                        for parity in range(2):
                            s_ref[u * N_Q_HEADS + 4 * kv + 2 * j + parity] = _merged_window(wide, parity, cur_mask, prev_mask, NEG)
            probs, _ = _softmax_with_sink(s_ref[...], jnp.tile(sink_ref[:, :, 0:1], (chains, 1, 1)), False)
            probs = probs.astype(BF16)
            zero = jnp.zeros((BLOCK, BLOCK), BF16)
            for u in range(chains):
                for pair in range(n_pairs):
                    for parity in range(2):
                        ph = probs[u * N_Q_HEADS + 2 * pair + parity]
                        p_ref[u, pair, :, parity * LANES:(parity + 1) * LANES] = jnp.where(cur_mask, ph, zero)
                        p_ref[u, pair, :, (2 + parity) * LANES:(3 + parity) * LANES] = jnp.where(cur_mask, zero, ph)
            for u in range(chains):
                for kv in range(2):
                    both = _dot(p_ref[u, 2 * kv:2 * kv + 2].reshape(2 * BLOCK, 4 * LANES), values[u][kv])
                    for j in range(2):
                        pair = 2 * kv + j
                        o_ref[pl.ds(starts[u], BLOCK), pair * LANES:(pair + 1) * LANES] = both[j * BLOCK:(j + 1) * BLOCK].astype(BF16)
            return carry

        lax.fori_loop(0, n_blocks // chains, step, 0)

    seq_block = lambda width: pl.BlockSpec((seq, width), lambda b: (b, 0))
    return pl.pallas_call(
        body, name="attention_forward", grid=(n_seq,),
        in_specs=[pl.BlockSpec((N_Q_HEADS, 1, LANES), lambda b: (0, 0, 0)), seq_block(ATTN_WIDTH), seq_block(2 * KV_WIDTH),
                  seq_block(2 * KV_WIDTH)],
        out_specs=seq_block(ATTN_WIDTH),
        out_shape=pltpu.HBM((n_seq * seq, ATTN_WIDTH), BF16),
        scratch_shapes=[pltpu.VMEM((chains * N_Q_HEADS, BLOCK, BLOCK), F32), pltpu.VMEM((chains, n_pairs, BLOCK, 4 * LANES), BF16)],
        compiler_params=_cparams(("parallel",), 40),
    )(sinks, q, kd, vd)


def _trailing(x, window, t, seq):
    k = 1
    while k < window:
        x = x + jnp.where(t >= k, pltpu.roll(x, k, 0), 0.0)
        k *= 2
    return x


def _leading(x, window, t, seq):
    k = 1
    while k < window:
        x = x + jnp.where(t < seq - k, pltpu.roll(x, seq - k, 0), 0.0)
        k *= 2
    return x


def _pool_features(u_g, window, t, seq):
    count = jnp.minimum(t + 1, window).astype(F32)
    return (_trailing(u_g, window, t, seq) / count - u_g).astype(BF16), count


def _pool_forward(u, w_pool, pool_scale, token, n_seq, seq):
    def body(u_ref, w_ref, s_ref, token_ref, o_ref):
        t = lax.broadcasted_iota(jnp.int32, (seq, 1), 0)
        for g, window in enumerate(POOL_WINDOWS):
            cols = slice(g * POOL_GROUP_DIM, (g + 1) * POOL_GROUP_DIM)
            d, _ = _pool_features(u_ref[:, cols], window, t, seq)
            o_ref[:, cols] = (_dot(d, w_ref[g]) * s_ref[:, cols]).astype(BF16)

    seq_block = pl.BlockSpec((seq, POOL_WIDTH), lambda b: (b, 0))
    return pl.pallas_call(
        body, name="pool_forward", grid=(n_seq,),
        in_specs=[seq_block, pl.BlockSpec((len(POOL_WINDOWS), POOL_GROUP_DIM, POOL_GROUP_DIM), lambda b: (0, 0, 0)),
                  pl.BlockSpec((1, POOL_WIDTH), lambda b: (0, 0)), pl.BlockSpec((8, LANES), lambda b: (0, 0))],
        out_specs=seq_block,
        out_shape=pltpu.HBM((n_seq * seq, POOL_WIDTH), BF16),
        compiler_params=_cparams(("parallel",), 40),
    )(u, w_pool, pool_scale, token)


def _outproj_norm(x2d, attn, pool, w_out, gain, tm):
    rows = x2d.shape[0]

    def body(x_ref, a_ref, p_ref, w_ref, g_ref, x2_ref, h_ref):
        x2 = x_ref[...] + _dot(a_ref[...], w_ref[:ATTN_WIDTH, :]) + _dot(p_ref[...], w_ref[ATTN_WIDTH:, :])
        x2_ref[...] = x2
        r = lax.rsqrt(jnp.mean(x2 * x2, axis=-1, keepdims=True) + EPS)
        h_ref[...] = (x2 * r * g_ref[...]).astype(BF16)

    row = lambda width: pl.BlockSpec((tm, width), lambda i: (i, 0))
    return pl.pallas_call(
        body, name="outproj_norm", grid=(rows // tm,),
        in_specs=[row(D_MODEL), row(ATTN_WIDTH), row(POOL_WIDTH), pl.BlockSpec((D_MODEL, D_MODEL), lambda i: (0, 0)),
                  pl.BlockSpec((1, D_MODEL), lambda i: (0, 0))],
        out_specs=[row(D_MODEL), row(D_MODEL)],
        out_shape=[pltpu.HBM((rows, D_MODEL), F32), pltpu.HBM((rows, D_MODEL), BF16)],
        compiler_params=_cparams(("parallel",), 40),
    )(x2d, attn, pool, w_out, gain)


def _resident(shape):
    return pl.BlockSpec(shape, lambda i: (0,) * len(shape), pipeline_mode=pl.Buffered(1))


def _mlp_forward_loss(h2, x2, w_up, w_down, gain, target, tm):
    rows = h2.shape[0]
    chunk = D_MODEL

    def body(h_ref, x_ref, up_ref, down_ref, g_ref, t_ref, slope_ref, f_ref, dx_ref, dxb_ref, loss_ref, dg_ref):
        @pl.when(pl.program_id(0) == 0)
        def _():
            loss_ref[...] = jnp.zeros_like(loss_ref)
            dg_ref[...] = jnp.zeros_like(dg_ref)

        h = h_ref[...]
        for c in range(D_FF // chunk):
            cols = slice(c * chunk, (c + 1) * chunk)
            r = jnp.maximum(_dot(h, up_ref[:, cols]), 0.0)
            slope_ref[:, cols] = (r + r).astype(BF16)
            f_ref[:, cols] = (r * r).astype(BF16)
        x3 = x_ref[...] + _dot(f_ref[...], down_ref[...])
        rn = lax.rsqrt(jnp.mean(x3 * x3, axis=-1, keepdims=True) + EPS)
        xhat = x3 * rn
        err = xhat * g_ref[...] - t_ref[...]
        loss_ref[...] += jnp.sum(err * err, axis=0, keepdims=True)
        dxhat = err * (g_ref[...] * (1.0 / D_MODEL))
        dx = rn * (dxhat - xhat * jnp.mean(dxhat * xhat, axis=-1, keepdims=True))
        dg_ref[...] += jnp.sum(err * xhat, axis=0, keepdims=True) * (1.0 / D_MODEL)
        dx_ref[...] = dx
        dxb_ref[...] = dx.astype(BF16)

    row = lambda width: pl.BlockSpec((tm, width), lambda i: (i, 0))
    vec = pl.BlockSpec((1, D_MODEL), lambda i: (0, 0))
    return pl.pallas_call(
        body, name="mlp_forward_loss", grid=(rows // tm,),
        in_specs=[row(D_MODEL), row(D_MODEL), _resident((D_MODEL, D_FF)), _resident((D_FF, D_MODEL)), vec, row(D_MODEL)],
        out_specs=[row(D_FF), row(D_FF), row(D_MODEL), row(D_MODEL), vec, vec],
        out_shape=[pltpu.HBM((rows, D_FF), BF16), pltpu.HBM((rows, D_FF), BF16),
                   pltpu.HBM((rows, D_MODEL), F32), pltpu.HBM((rows, D_MODEL), BF16),
                   jax.ShapeDtypeStruct((1, D_MODEL), F32), jax.ShapeDtypeStruct((1, D_MODEL), F32)],
        compiler_params=_cparams(("arbitrary",), 56),
    )(h2, x2, w_up, w_down, gain, target)


def _mlp_backward_data(dx3b, slope, w_down, w_up, dx3, x2, gain, w_out, attn, pool, token, tm):
    rows = dx3b.shape[0]
    steps = rows // tm
    chunk = D_MODEL

    def body(dxb_ref, slope_ref, down_ref, up_ref, dx3_ref, x2_ref, g_ref, wo_ref, attn_ref, pool_ref, token_ref,
             da_ref, dx2_ref, dattn_ref, dpool_ref, dg_ref, dwo_hbm, wire_hbm, dwo_acc, wire, sems):
        @pl.when(pl.program_id(0) == 0)
        def _():
            dg_ref[...] = jnp.zeros_like(dg_ref)
            dwo_acc[...] = jnp.zeros_like(dwo_acc)

        dxb = dxb_ref[...]
        for c in range(D_FF // chunk):
            cols = slice(c * chunk, (c + 1) * chunk)
            da_ref[:, cols] = (_dot_nt(dxb, down_ref[cols, :]) * slope_ref[:, cols].astype(F32)).astype(BF16)
        dnorm, dg = _rms_backward(_dot_nt(da_ref[...], up_ref[...]), x2_ref[...], g_ref[...])
        dg_ref[...] += dg
        dx2 = dx3_ref[...] + dnorm
        dx2_ref[...] = dx2
        dx2b = dx2.astype(BF16)
        dmix = _dot_nt(dx2b, wo_ref[...])
        dattn_ref[...] = dmix[:, :ATTN_WIDTH].astype(BF16)
        dpool_ref[...] = dmix[:, ATTN_WIDTH:]
        dwo_acc[:ATTN_WIDTH, :] += _dot_tn(attn_ref[...], dx2b)
        dwo_acc[ATTN_WIDTH:, :] += _dot_tn(pool_ref[...], dx2b)

        @pl.when(pl.program_id(0) == steps - 1)
        def _():
            done = pltpu.make_async_copy(dwo_acc, dwo_hbm, sems.at[0])
            done.start()
            wire[...] = dwo_acc[...].astype(BF16)
            sent = pltpu.make_async_copy(wire, wire_hbm, sems.at[1])
            sent.start()
            done.wait()
            sent.wait()

    row = lambda width: pl.BlockSpec((tm, width), lambda i: (i, 0))
    vec = pl.BlockSpec((1, D_MODEL), lambda i: (0, 0))
    return pl.pallas_call(
        body, name="mlp_backward_data", grid=(steps,),
        in_specs=[row(D_MODEL), row(D_FF), _resident((D_FF, D_MODEL)), _resident((D_MODEL, D_FF)), row(D_MODEL), row(D_MODEL), vec,
                  _resident((D_MODEL, D_MODEL)), row(ATTN_WIDTH), row(POOL_WIDTH), pl.BlockSpec((8, LANES), lambda i: (0, 0))],
        out_specs=[row(D_FF), row(D_MODEL), row(ATTN_WIDTH), row(POOL_WIDTH), vec, pl.BlockSpec(memory_space=pl.ANY),
                   pl.BlockSpec(memory_space=pl.ANY)],
        out_shape=[pltpu.HBM((rows, D_FF), BF16), pltpu.HBM((rows, D_MODEL), F32),
                   pltpu.HBM((rows, ATTN_WIDTH), BF16), pltpu.HBM((rows, POOL_WIDTH), F32),
                   jax.ShapeDtypeStruct((1, D_MODEL), F32), jax.ShapeDtypeStruct((D_MODEL, D_MODEL), F32),
                   jax.ShapeDtypeStruct((D_MODEL, D_MODEL), BF16)],
        scratch_shapes=[pltpu.VMEM((D_MODEL, D_MODEL), F32), pltpu.VMEM((D_MODEL, D_MODEL), BF16), pltpu.SemaphoreType.DMA((2,))],
        compiler_params=_cparams(("arbitrary",), 56),
    )(dx3b, slope, w_down, w_up, dx3, x2, gain, w_out, attn, pool, token)


def _weight_gradient(name, lhs, rhs, block_lhs, tm):
    rows = lhs.shape[0]
    steps = rows // tm
    out = (N_DEV, FF_BLOCK, rhs.shape[1]) if block_lhs else (N_DEV, lhs.shape[1], FF_BLOCK)

    def body(l_ref, r_ref, o_hbm, wire_hbm, acc, wire, sems):
        @pl.when(pl.program_id(0) == 0)
        def _():
            acc[...] = jnp.zeros_like(acc)

        for d in range(N_DEV):
            cols = slice(d * FF_BLOCK, (d + 1) * FF_BLOCK)
            acc[d] += _dot_tn(l_ref[:, cols], r_ref[...]) if block_lhs else _dot_tn(l_ref[...], r_ref[:, cols])

        @pl.when(pl.program_id(0) == steps - 1)
        def _():
            done = pltpu.make_async_copy(acc, o_hbm, sems.at[0])
            done.start()
            wire[...] = acc[...].astype(BF16)
            sent = pltpu.make_async_copy(wire, wire_hbm, sems.at[1])
            sent.start()
            done.wait()
            sent.wait()

    return pl.pallas_call(
        body, name=name, grid=(steps,),
        in_specs=[pl.BlockSpec((tm, lhs.shape[1]), lambda i: (i, 0)), pl.BlockSpec((tm, rhs.shape[1]), lambda i: (i, 0))],
        out_specs=[pl.BlockSpec(memory_space=pl.ANY), pl.BlockSpec(memory_space=pl.ANY)],
        out_shape=[jax.ShapeDtypeStruct(out, F32), jax.ShapeDtypeStruct(out, BF16)],
        scratch_shapes=[pltpu.VMEM(out, F32), pltpu.VMEM(out, BF16), pltpu.SemaphoreType.DMA((2,))],
        compiler_params=_cparams(("arbitrary",), 60),
    )(lhs, rhs)


def _attention_backward(sinks, q, kd, vd, dout, cos, sin, token, n_seq, seq):
    n_blocks = seq // BLOCK
    n_kv = N_Q_HEADS // Q_PER_KV
    chains = min(BACKWARD_CHAINS, n_blocks)

    def body(sink_ref, q_ref, k_ref, v_ref, do_ref, cos_ref, sin_ref, token_ref, dq_ref, dk_ref, dv_ref, dsink_ref,
             s_all, dp_all, dsc_all, dsp_all, pc_all, pp_all, dk_acc, dv_acc):
        low = lax.broadcasted_iota(jnp.int32, (BLOCK, LANES), 1) < HEAD_DIM

        @pl.when(pl.program_id(0) == 0)
        def _():
            dsink_ref[...] = jnp.zeros_like(dsink_ref)

        def fold(x):
            return x + pltpu.roll(x, HEAD_DIM, 1)

        def onto_keys(cur_ref, prev_ref, kv, other):
            even = _dot_tn(jnp.concatenate([cur_ref[2 * kv], prev_ref[2 * kv]], axis=1), other)
            odd = _dot_tn(jnp.concatenate([cur_ref[2 * kv + 1], prev_ref[2 * kv + 1]], axis=1), other)
            both = fold(jnp.where(jnp.concatenate([low, low], axis=0), even, odd))
            return both[:BLOCK], both[BLOCK:]

        def step(i, dsink):
            blocks = []
            for u in range(chains):
                n = i * chains + u
                r0 = pl.multiple_of(n * BLOCK, BLOCK)
                p0 = pl.multiple_of(jnp.maximum(n - 1, 0) * BLOCK, BLOCK)
                cur_mask, prev_mask = _window_masks(n)
                keys = [_window_operand(k_ref, r0, p0, kv) for kv in range(n_kv)]
                values = [_window_operand(v_ref, r0, p0, kv) for kv in range(n_kv)]
                q_rows = [_pair_rows(q_ref, r0, kv) for kv in range(n_kv)]
                do_rows = [_pair_rows(do_ref, r0, kv) for kv in range(n_kv)]
                for kv in range(n_kv):
                    both_s, both_dp = _dot_nt(q_rows[kv], keys[kv]), _dot_nt(do_rows[kv], values[kv])
                    for j in range(2):
                        wide_s, wide_dp = both_s[j * BLOCK:(j + 1) * BLOCK], both_dp[j * BLOCK:(j + 1) * BLOCK]
                        for parity in range(2):
                            head = u * N_Q_HEADS + 4 * kv + 2 * j + parity
                            s_all[head] = _merged_window(wide_s, parity, cur_mask, prev_mask, NEG)
                            dp_all[head] = _merged_window(wide_dp, parity, cur_mask, prev_mask, 0.0)
                blocks.append((n, r0, p0, keys, q_rows, do_rows))

            probs, p_sink = _softmax_with_sink(s_all[...], jnp.tile(sink_ref[:, :, 0:1], (chains, 1, 1)), True)
            dprobs = dp_all[...]
            delta = _lane_sums(probs * dprobs, True)
            dscores = (probs * (dprobs - delta)).astype(BF16)
            sink_terms = jnp.sum((p_sink * delta)[:, :, 0:1], axis=1, keepdims=True)
            probs = probs.astype(BF16)
            zero = jnp.zeros((BLOCK, BLOCK), BF16)
            for u in range(chains):
                dsink = dsink - sink_terms[u * N_Q_HEADS:(u + 1) * N_Q_HEADS]
                for head in range(N_Q_HEADS):
                    group, rows = 2 * (head // Q_PER_KV) + head % 2, pl.ds(((head % Q_PER_KV) // 2) * BLOCK, BLOCK)
                    ds_h, p_h = dscores[u * N_Q_HEADS + head], probs[u * N_Q_HEADS + head]
                    dsc_all[u, group, rows, :] = jnp.where(cur_mask, ds_h, zero)
                    dsp_all[u, group, rows, :] = jnp.where(cur_mask, zero, ds_h)
                    pc_all[u, group, rows, :] = jnp.where(cur_mask, p_h, zero)
                    pp_all[u, group, rows, :] = jnp.where(cur_mask, zero, p_h)

            for u, (n, r0, p0, keys, q_rows, do_rows) in enumerate(blocks):
                dsc_ref, dsp_ref, pc_ref, pp_ref = dsc_all.at[u], dsp_all.at[u], pc_all.at[u], pp_all.at[u]
                for kv in range(n_kv):
                    wide = jnp.concatenate([dsc_ref[2 * kv], dsc_ref[2 * kv + 1], dsp_ref[2 * kv], dsp_ref[2 * kv + 1]], axis=1)
                    both = _dot(wide, keys[kv]) * (HEAD_DIM ** -0.5)
                    for j in range(2):
                        dq = _rope(both[j * BLOCK:(j + 1) * BLOCK], cos_ref[pl.ds(r0, BLOCK), :], -sin_ref[pl.ds(r0, BLOCK), :])
                        dq_ref[pl.ds(r0, BLOCK), (2 * kv + j) * LANES:(2 * kv + j + 1) * LANES] = dq.astype(BF16)

                parts = []
                for kv in range(n_kv):
                    parts.append(onto_keys(dsc_ref, dsp_ref, kv, q_rows[kv]) + onto_keys(pc_ref, pp_ref, kv, do_rows[kv]))
                dk_acc[pl.ds(r0, BLOCK), :] = jnp.where(low, parts[0][0], parts[1][0])
                dv_acc[pl.ds(r0, BLOCK), :] = jnp.where(low, parts[0][2], parts[1][2])

                @pl.when(n > 0)
                def _():
                    dk_acc[pl.ds(p0, BLOCK), :] += jnp.where(low, parts[0][1], parts[1][1])
                    dv_acc[pl.ds(p0, BLOCK), :] += jnp.where(low, parts[0][3], parts[1][3])

            return dsink

        dsink = lax.fori_loop(0, n_blocks // chains, step, jnp.zeros((N_Q_HEADS, 1, 1), F32))
        dsink_ref[...] += jnp.broadcast_to(dsink, dsink_ref.shape)
        dk_ref[...] = _rope(dk_acc[...], cos_ref[...], -sin_ref[...]).astype(BF16)
        dv_ref[...] = dv_acc[...].astype(BF16)

    seq_block = lambda width: pl.BlockSpec((seq, width), lambda b: (b, 0))
    table = pl.BlockSpec((seq, LANES), lambda b: (0, 0))
    per_head = pl.BlockSpec((N_Q_HEADS, 1, LANES), lambda b: (0, 0, 0))
    per_block = pltpu.VMEM((chains * N_Q_HEADS, BLOCK, BLOCK), F32)
    grouped = pltpu.VMEM((chains, 2 * n_kv, 2 * BLOCK, BLOCK), BF16)
    return pl.pallas_call(
        body, name="attention_backward", grid=(n_seq,),
        in_specs=[per_head, seq_block(ATTN_WIDTH), seq_block(2 * KV_WIDTH), seq_block(2 * KV_WIDTH),
                  seq_block(ATTN_WIDTH), table, table, pl.BlockSpec((8, LANES), lambda b: (0, 0))],
        out_specs=[seq_block(ATTN_WIDTH), seq_block(KV_WIDTH), seq_block(KV_WIDTH), per_head],
        out_shape=[pltpu.HBM((n_seq * seq, ATTN_WIDTH), BF16), pltpu.HBM((n_seq * seq, KV_WIDTH), BF16),
                   pltpu.HBM((n_seq * seq, KV_WIDTH), BF16), jax.ShapeDtypeStruct((N_Q_HEADS, 1, LANES), F32)],
        scratch_shapes=[per_block, per_block, grouped, grouped, grouped, grouped, pltpu.VMEM((seq, KV_WIDTH), F32), pltpu.VMEM((seq, KV_WIDTH), F32)],
        compiler_params=_cparams(("arbitrary",), 40),
    )(sinks, q, kd, vd, dout, cos, sin, token)


def _pool_backward(u, dpool, w_pool, pool_scale, n_seq, seq):
    groups = len(POOL_WINDOWS)

    def body(u_ref, dp_ref, w_ref, s_ref, du_ref, dw_ref, ds_ref):
        @pl.when(pl.program_id(0) == 0)
        def _():
            dw_ref[...] = jnp.zeros_like(dw_ref)
            ds_ref[...] = jnp.zeros_like(ds_ref)

        t = lax.broadcasted_iota(jnp.int32, (seq, 1), 0)
        for g, window in enumerate(POOL_WINDOWS):
            cols = slice(g * POOL_GROUP_DIM, (g + 1) * POOL_GROUP_DIM)
            d, count = _pool_features(u_ref[:, cols], window, t, seq)
            dpool_g = dp_ref[:, cols]
            ds_ref[:, cols] += jnp.sum(dpool_g * _dot(d, w_ref[g]), axis=0, keepdims=True)
            dy = (dpool_g * s_ref[:, cols]).astype(BF16)
            dw_ref[g] += _dot_tn(d, dy)
            dd = _dot_nt(dy, w_ref[g])
            du_ref[:, cols] = (_leading(dd / count, window, t, seq) - dd).astype(BF16)

    seq_block = pl.BlockSpec((seq, POOL_WIDTH), lambda b: (b, 0))
    weights = pl.BlockSpec((groups, POOL_GROUP_DIM, POOL_GROUP_DIM), lambda b: (0, 0, 0))
    scale = pl.BlockSpec((1, POOL_WIDTH), lambda b: (0, 0))
    return pl.pallas_call(
        body, name="pool_backward", grid=(n_seq,),
        in_specs=[seq_block, seq_block, weights, scale],
        out_specs=[seq_block, weights, scale],
        out_shape=[pltpu.HBM((n_seq * seq, POOL_WIDTH), BF16),
                   jax.ShapeDtypeStruct((groups, POOL_GROUP_DIM, POOL_GROUP_DIM), F32), jax.ShapeDtypeStruct((1, POOL_WIDTH), F32)],
        compiler_params=_cparams(("arbitrary",), 40),
    )(u, dpool, w_pool, pool_scale)


def _inproj_gradient(dq, dk, dv, du, h1, tm):
    rows = h1.shape[0]
    steps = rows // tm

    def body(dq_ref, dk_ref, dv_ref, du_ref, h_ref, dw_ref, wire_ref):
        @pl.when(pl.program_id(0) == 0)
        def _():
            dw_ref[...] = jnp.zeros_like(dw_ref)

        dproj = jnp.concatenate([dq_ref[...], dk_ref[...], dv_ref[...], du_ref[...]], axis=1)
        dw_ref[...] += _dot_tn(dproj, h_ref[...])

        @pl.when(pl.program_id(0) == steps - 1)
        def _():
            wire_ref[...] = dw_ref[...].astype(BF16)

    row = lambda width: pl.BlockSpec((tm, width), lambda i: (i, 0))
    whole = pl.BlockSpec((IN_WIDTH, D_MODEL), lambda i: (0, 0))
    return pl.pallas_call(
        body, name="inproj_gradient", grid=(steps,),
        in_specs=[row(ATTN_WIDTH), row(KV_WIDTH), row(KV_WIDTH), row(POOL_WIDTH), row(D_MODEL)],
        out_specs=[whole, whole],
        out_shape=[pltpu.HBM((IN_WIDTH, D_MODEL), F32), pltpu.HBM((IN_WIDTH, D_MODEL), BF16)],
        compiler_params=_cparams(("arbitrary",), 48),
    )(dq, dk, dv, du, h1)


def _inproj_backward(dq, dk, dv, du, w_in, x2d, dx2, gain, token, tm):
    rows = x2d.shape[0]

    def body(dq_ref, dk_ref, dv_ref, du_ref, w_ref, x_ref, dx2_ref, g_ref, token_ref, dx_ref, dg_ref):
        @pl.when(pl.program_id(0) == 0)
        def _():
            dg_ref[...] = jnp.zeros_like(dg_ref)

        dproj = jnp.concatenate([dq_ref[...], dk_ref[...], dv_ref[...], du_ref[...]], axis=1)
        dnorm, dg = _rms_backward(_dot(dproj, w_ref[...]), x_ref[...], g_ref[...])
        dg_ref[...] += dg
        dx_ref[...] = dx2_ref[...] + dnorm

    row = lambda width: pl.BlockSpec((tm, width), lambda i: (i, 0))
    vec = pl.BlockSpec((1, D_MODEL), lambda i: (0, 0))
    return pl.pallas_call(
        body, name="inproj_backward", grid=(rows // tm,),
        in_specs=[row(ATTN_WIDTH), row(KV_WIDTH), row(KV_WIDTH), row(POOL_WIDTH), pl.BlockSpec((IN_WIDTH, D_MODEL), lambda i: (0, 0)),
                  row(D_MODEL), row(D_MODEL), vec, pl.BlockSpec((8, LANES), lambda i: (0, 0))],
        out_specs=[row(D_MODEL), vec],
        out_shape=[pltpu.HBM((rows, D_MODEL), F32), jax.ShapeDtypeStruct((1, D_MODEL), F32)],
        compiler_params=_cparams(("arbitrary",), 60),
    )(dq, dk, dv, du, w_in, x2d, dx2, gain, token)


def _place():
    return lax.axis_index("x"), lax.axis_index("y"), lax.axis_index("c")


def _peer(x, y, c, rel):
    return (1 - x if rel & 4 else x, 1 - y if rel & 2 else y, 1 - c if rel & 1 else c)


def _index(px, py, pc):
    return 4 * px + 2 * py + pc


def _row_slot(ref, d):
    return ref.at[d]


def _column_slot(ref, d):
    return ref.at[:, pl.ds(pl.multiple_of(d * FF_BLOCK, FF_BLOCK), FF_BLOCK)]


def _stage_weights(shards, slots, shapes):
    n = len(shards)

    def body(*refs):
        ins, outs, stage, sems = refs[:n], refs[n:2 * n], refs[2 * n:3 * n], refs[3 * n]
        me = _index(*_place())
        mine = []
        for a in range(n):
            stage[a][...] = ins[a][...].astype(BF16)
            mine.append(pltpu.make_async_copy(stage[a], slots[a](outs[a], me), sems.at[a]))
            mine[-1].start()
        for cp in mine:
            cp.wait()

    return pl.pallas_call(
        body, name="stage_weights",
        in_specs=[pl.BlockSpec(memory_space=pltpu.VMEM)] * n,
        out_specs=[pl.BlockSpec(memory_space=pl.ANY)] * n,
        out_shape=[pltpu.HBM(shape, BF16) for shape in shapes],
        scratch_shapes=[pltpu.VMEM(s.shape, BF16) for s in shards] + [pltpu.SemaphoreType.DMA((n,))],
        compiler_params=pltpu.CompilerParams(vmem_limit_bytes=32 * MIB),
    )(*shards)


ALL_PEERS = tuple(range(1, N_DEV))
FIRST_HOP = (1, 2, 4, 6)
OTHER_CHIPS = (2, 4, 6)


class _Plan:
    def __init__(self, per_array, copies):
        self.per_array, self.copies = per_array, copies


def _own_slot_first_hop(slots):
    def copies(x, y, c, ins, lands):
        me = _index(x, y, c)
        return [(slots[a](lands[a], me), slots[a](lands[a], me), _peer(x, y, c, rel)) for rel in FIRST_HOP for a in range(len(lands))]
    return _Plan(len(FIRST_HOP), copies)


def _landed_to_sibling(slots):
    def copies(x, y, c, ins, lands):
        blocks = [_index(*_peer(x, y, c, rel)) for rel in OTHER_CHIPS]
        return [(slots[a](lands[a], b), slots[a](lands[a], b), (x, y, 1 - c)) for b in blocks for a in range(len(lands))]
    return _Plan(len(OTHER_CHIPS), copies)


def _whole_to_all_copies(x, y, c, ins, lands):
    return [(ins[a], lands[a].at[rel - 1], _peer(x, y, c, rel)) for rel in ALL_PEERS for a in range(len(lands))]


def _block_to_owner_copies(x, y, c, ins, lands):
    return [(ins[a].at[_index(*_peer(x, y, c, rel))], lands[a].at[rel - 1], _peer(x, y, c, rel))
            for rel in ALL_PEERS for a in range(len(lands))]


_whole_to_all = _Plan(len(ALL_PEERS), _whole_to_all_copies)
_block_to_owner = _Plan(len(ALL_PEERS), _block_to_owner_copies)


def _split_copies(plan, ins, lands, send_sems, recv_sems):
    return [pltpu.make_async_remote_copy(src_ref=src, dst_ref=dst, send_sem=send_sems.at[k], recv_sem=recv_sems.at[k],
                                         device_id=to, device_id_type=MESH)
            for k, (src, dst, to) in enumerate(plan.copies(*_place(), ins, lands))]


HBM_SPEC = pl.BlockSpec(memory_space=pltpu.HBM)
SEM_SPEC = pl.BlockSpec(memory_space=pltpu.SEMAPHORE)
EFFECT = pltpu.SideEffectType.DATAFLOW_SIDE_EFFECTING


def _start_copies(name, plan, ins, lands, after):
    n_in, n = len(ins), len(ins) + len(lands)

    def body(*refs):
        send_sems, recv_sems = refs[n + 1], refs[n + 2]
        for cp in _split_copies(plan, refs[:n_in], refs[n_in:n], send_sems, recv_sems):
            cp.start()
        refs[-1][...] = jnp.zeros_like(refs[-1])

    arrays = [pltpu.with_memory_space_constraint(v, pltpu.HBM) for v in (*ins, *lands)]
    sems = pltpu.SemaphoreType.DMA((len(lands) * plan.per_array,))
    send_sems, recv_sems, *flying, token = pl.pallas_call(
        body, name=name,
        out_shape=(sems, sems, *[pltpu.HBM(v.shape, v.dtype) for v in arrays], jax.ShapeDtypeStruct((8, LANES), F32)),
        in_specs=[HBM_SPEC] * n + [pl.BlockSpec(memory_space=pl.ANY)],
        out_specs=(SEM_SPEC, SEM_SPEC, *[HBM_SPEC] * n, pl.BlockSpec(memory_space=pltpu.VMEM)),
        input_output_aliases={i: 2 + i for i in range(n)},
        compiler_params=pltpu.CompilerParams(has_side_effects=EFFECT),
    )(*arrays, after)
    return send_sems, recv_sems, flying, token


def _wait_copies(name, plan, n_in, send_sems, recv_sems, flying, after):
    n = len(flying)

    def body(*refs):
        for cp in _split_copies(plan, refs[:n_in], refs[n_in:n], refs[n], refs[n + 1]):
            cp.wait_send()
            cp.wait_recv()

    landed = pl.pallas_call(
        body, name=name, out_shape=tuple(pltpu.HBM(v.shape, v.dtype) for v in flying),
        in_specs=[HBM_SPEC] * n + [SEM_SPEC, SEM_SPEC, pl.BlockSpec(memory_space=pl.ANY)], out_specs=tuple([HBM_SPEC] * n),
        input_output_aliases={i: i for i in range(n)},
        compiler_params=pltpu.CompilerParams(has_side_effects=EFFECT),
    )(*flying, send_sems, recv_sems, after)
    return landed[:n_in], landed[n_in:]


def _adamw_math(w, g, m, v):
    m = ADAM_B1 * m + (1.0 - ADAM_B1) * g
    v = ADAM_B2 * v + (1.0 - ADAM_B2) * (g * g)
    m_hat = m / (1.0 - ADAM_B1 ** ADAM_STEP)
    v_hat = v / (1.0 - ADAM_B2 ** ADAM_STEP)
    return -ADAM_LR * (m_hat / (jnp.sqrt(v_hat) + ADAM_EPS) + ADAM_WD * w), m, v


def _adamw_sharded(me, own, received, w, m, v, tr):
    rows, cols = w.shape
    own, received, w, m, v = _in_hbm(own, received, w, m, v)

    def body(me_ref, own_ref, rec_ref, w_ref, m_ref, v_ref, g_ref, d_ref, nm_ref, nv_ref):
        g = own_ref[...]
        for r in range(N_DEV - 1):
            g = g + rec_ref[r].astype(F32)
        g_ref[...] = g
        d_ref[...], nm_ref[...], nv_ref[...] = _adamw_math(w_ref[...], g, m_ref[...], v_ref[...])

    tile = pl.BlockSpec((tr, cols), lambda i, me_ref: (i, 0))
    shape = pltpu.HBM((rows, cols), F32)
    return pl.pallas_call(
        body, name="adamw_sharded",
        grid_spec=pltpu.PrefetchScalarGridSpec(
            num_scalar_prefetch=1, grid=(rows // tr,),
            in_specs=[pl.BlockSpec((None, tr, cols), lambda i, me_ref: (me_ref[0], i, 0)),
                      pl.BlockSpec((N_DEV - 1, tr, cols), lambda i, me_ref: (0, i, 0)), tile, tile, tile],
            out_specs=[tile, tile, tile, tile]),
        out_shape=[shape, shape, shape, shape],
        compiler_params=_cparams(("parallel",), 40),
    )(me, own, received, w, m, v)


def _sum_blocks(me, own, received):
    shape = own.shape[1:]
    own, received = _in_hbm(own, received)

    def body(me_ref, own_ref, rec_ref, g_ref):
        g = own_ref[...]
        for r in range(N_DEV - 1):
            g = g + rec_ref[r].astype(F32)
        g_ref[...] = g

    return pl.pallas_call(
        body, name="sum_blocks",
        grid_spec=pltpu.PrefetchScalarGridSpec(
            num_scalar_prefetch=1, grid=(1,),
            in_specs=[pl.BlockSpec((None,) + shape, lambda i, me_ref: (me_ref[0], 0, 0)),
                      pl.BlockSpec((N_DEV - 1,) + shape, lambda i, me_ref: (0, 0, 0))],
            out_specs=pl.BlockSpec(shape, lambda i, me_ref: (0, 0))),
        out_shape=pltpu.HBM(shape, F32),
        compiler_params=_cparams(("arbitrary",), 40),
    )(me, own, received)


def _adamw_whole(g, w, m, v):
    def body(g_ref, w_ref, m_ref, v_ref, d_ref, nm_ref, nv_ref):
        d_ref[...], nm_ref[...], nv_ref[...] = _adamw_math(w_ref[...], g_ref[...], m_ref[...], v_ref[...])

    shape = jax.ShapeDtypeStruct(w.shape, F32)
    return pl.pallas_call(body, name="adamw_whole", out_shape=[shape, shape, shape])(g, w, m, v)


VECTOR_ROWS = D_MODEL // LANES
POOL_ROWS = len(POOL_WINDOWS) * POOL_GROUP_DIM


def _pack_small(dg1, dg2, dg3, dps, dsink, loss_cols, dwp):
    def body(g1_ref, g2_ref, g3_ref, ps_ref, sink_ref, loss_ref, wp_ref, o_ref):
        o_ref[...] = jnp.zeros_like(o_ref)
        for base, ref, n in ((ROW_G1, g1_ref, VECTOR_ROWS), (ROW_G2, g2_ref, VECTOR_ROWS), (ROW_G3, g3_ref, VECTOR_ROWS),
                             (ROW_LOSS, loss_ref, VECTOR_ROWS), (ROW_PS, ps_ref, POOL_WIDTH // LANES)):
            for r in range(n):
                o_ref[base + r:base + r + 1, :] = ref[:, r * LANES:(r + 1) * LANES]
        heads = sink_ref[:, 0, :]
        on_diagonal = lax.broadcasted_iota(jnp.int32, heads.shape, 0) == lax.broadcasted_iota(jnp.int32, heads.shape, 1)
        o_ref[ROW_SINK:ROW_SINK + 1, :] = jnp.sum(jnp.where(on_diagonal, heads, 0.0), axis=0, keepdims=True)
        o_ref[ROW_WP:ROW_WP + POOL_ROWS, :] = wp_ref[...].reshape(POOL_ROWS, LANES)

    return pl.pallas_call(body, name="pack_small", out_shape=jax.ShapeDtypeStruct((SMALL_ROWS, LANES), F32))(
        dg1, dg2, dg3, dps, dsink, loss_cols, dwp)


def _finish_small(me, own, landed, params):
    flat = [a for group in params for a in group]

    def body(me_ref, own_ref, landed_ref, *refs):
        ins, outs = refs[:len(flat)], refs[len(flat):]
        total = None
        for source in range(N_DEV):
            rel = jnp.bitwise_xor(me_ref[0], source)
            piece = jnp.where(rel == 0, own_ref[...], landed_ref[jnp.maximum(rel, 1) - 1])
            total = piece if total is None else total + piece
        outs[0][...] = (0.5 / D_MODEL) * jnp.sum(jnp.sum(total[ROW_LOSS:ROW_LOSS + VECTOR_ROWS], axis=1, keepdims=True), axis=0, keepdims=True)
        row = lambda base, n: jnp.concatenate([total[base + r:base + r + 1, :] for r in range(n)], axis=1)
        grads = [row(ROW_G1, VECTOR_ROWS), row(ROW_G2, VECTOR_ROWS), row(ROW_G3, VECTOR_ROWS), row(ROW_PS, POOL_WIDTH // LANES),
                 total[ROW_SINK:ROW_SINK + 1, :N_Q_HEADS], total[ROW_WP:ROW_WP + POOL_ROWS].reshape(params[5][0].shape)]
        for k, g in enumerate(grads):
            w_ref, m_ref, v_ref = ins[3 * k:3 * k + 3]
            g_out, d_out, m_out, v_out = outs[1 + 4 * k:5 + 4 * k]
            g_out[...] = g
            d_out[...], m_out[...], v_out[...] = _adamw_math(w_ref[...], g, m_ref[...], v_ref[...])

    shapes = [jax.ShapeDtypeStruct((1, 1), F32)] + [jax.ShapeDtypeStruct(w.shape, F32) for w, _, _ in params for _ in range(4)]
    vmem = pl.BlockSpec(memory_space=pltpu.VMEM)
    res = pl.pallas_call(body, name="finish_small", in_specs=[pl.BlockSpec(memory_space=pltpu.SMEM)] + [vmem] * (2 + len(flat)),
                         out_shape=shapes)(me, own, landed, *flat)
    return res[0], [res[1 + 4 * k:5 + 4 * k] for k in range(len(params))]


def _local_step(x, target, attn_norm_g, attn_sinks, w_pool, pool_scale, mlp_norm_g, final_norm_g,
                front_token, first_weight, second_hop, later_weights, ship_down, ship_up, ship_in):
    n_seq, seq, _ = x.shape
    rows = n_seq * seq
    tm, tm_mlp, tm_norm = min(1024, seq), min(256, seq), min(2048, seq)
    tm_grad = tm
    x2d, t2d = x.reshape(rows, D_MODEL), target.reshape(rows, D_MODEL)
    g3 = final_norm_g.reshape(1, D_MODEL)
    cos, sin = _rope_tables(seq)
    wp_b = w_pool[0].astype(BF16)

    sink_rows = jnp.broadcast_to(attn_sinks.reshape(N_Q_HEADS, 1, 1), (N_Q_HEADS, 1, LANES))
    cos, sin = _in_hbm(cos, sin)
    h1 = _in_hbm(_first_norm(x2d, attn_norm_g, front_token, tm_norm))
    w_in_full = _in_hbm(first_weight(h1))
    q, kd, vd, u = _in_hbm(*_inproj(h1, w_in_full, cos, sin, seq, tm))
    attn = _in_hbm(_attention_forward(sink_rows, q, kd, vd, n_seq, seq))
    pool = _in_hbm(_pool_forward(u, wp_b, pool_scale, second_hop(attn), n_seq, seq))
    w_out_full, w_up_full, w_down_full = _in_hbm(*later_weights(pool))
    x2, h2 = _in_hbm(*_outproj_norm(x2d, attn, pool, w_out_full, mlp_norm_g, tm))
    slope, f, dx3, dx3b, loss_cols, dg3 = _mlp_forward_loss(h2, x2, w_up_full, w_down_full, g3, t2d, tm_mlp)
    slope, f, dx3, dx3b = _in_hbm(slope, f, dx3, dx3b)

    down_token = ship_down(_weight_gradient("down_gradient", f, dx3b, True, tm_grad))
    da, dx2, dattn, dpool, dg2, d_w_out, d_w_out_wire = _mlp_backward_data(
        dx3b, slope, w_down_full, w_up_full, dx3, x2, mlp_norm_g, w_out_full, attn, pool, down_token, tm_mlp)
    da, dx2, dattn, dpool = _in_hbm(da, dx2, dattn, dpool)
    up_token = ship_up((d_w_out, d_w_out_wire), _weight_gradient("up_gradient", h2, da, False, tm_grad))
    dq, dk, dv, dsink = _attention_backward(sink_rows, q, kd, vd, dattn, cos, sin, up_token, n_seq, seq)
    dq, dk, dv = _in_hbm(dq, dk, dv)
    du, d_w_pool, d_pool_scale = _pool_backward(u, dpool, wp_b, pool_scale, n_seq, seq)
    du = _in_hbm(du)
    in_token = ship_in(_inproj_gradient(dq, dk, dv, du, h1, tm_grad))
    grad_x, dg1 = _inproj_backward(dq, dk, dv, du, w_in_full, x2d, dx2, attn_norm_g, in_token, tm)
    return grad_x.reshape(x.shape), _pack_small(dg1, dg2, dg3, d_pool_scale, dsink, loss_cols, d_w_pool)


def kernel(x, attn_norm_g, w_in, attn_sinks, w_pool, pool_scale, w_out, mlp_norm_g, w_up, w_down, final_norm_g, loss_target, m_attn_norm_g, m_w_in, m_attn_sinks, m_w_pool, m_pool_scale, m_w_out, m_mlp_norm_g, m_w_up, m_w_down, m_final_norm_g, v_attn_norm_g, v_w_in, v_attn_sinks, v_w_pool, v_pool_scale, v_w_out, v_mlp_norm_g, v_w_up, v_w_down, v_final_norm_g):
    me = (4 * lax.axis_index("x") + 2 * lax.axis_index("y") + lax.axis_index("c")).astype(jnp.int32).reshape(1)

    unordered = jnp.zeros((8, LANES), F32)

    win_land, wout_land, wup_land, wdown_land = _stage_weights(
        [w_in[0].T, w_out[0], w_up[0], w_down[0]], [_row_slot, _row_slot, _column_slot, _row_slot],
        [(N_DEV, IN_BLOCK, D_MODEL), (N_DEV, OUT_BLOCK, D_MODEL), (D_MODEL, D_FF), (N_DEV, FF_BLOCK, D_MODEL)])
    in_slots, later_slots = [_row_slot], [_row_slot, _column_slot, _row_slot]
    in_copies = _start_copies("spread_in_start", _own_slot_first_hop(in_slots), [], [win_land], unordered)
    later_copies = _start_copies("spread_later_start", _own_slot_first_hop(later_slots), [], [wout_land, wup_land, wdown_land], in_copies[3])

    def first_weight(after):
        _, in_landed = _wait_copies("spread_in_wait", _own_slot_first_hop(in_slots), 0, *in_copies[:3], after)
        in_passed = _start_copies("pass_in_start", _landed_to_sibling(in_slots), [], in_landed, unordered)
        _, (win_g,) = _wait_copies("pass_in_wait", _landed_to_sibling(in_slots), 0, *in_passed[:3], in_passed[3])
        return win_g.reshape(IN_WIDTH, D_MODEL)

    passing = []

    def second_hop(after):
        _, later_landed = _wait_copies("spread_later_wait", _own_slot_first_hop(later_slots), 0, *later_copies[:3], after)
        passing.extend(_start_copies("pass_later_start", _landed_to_sibling(later_slots), [], later_landed, unordered))
        return passing[3]

    def later_weights(after):
        _, (wout_g, wup_g, wdown_g) = _wait_copies("pass_later_wait", _landed_to_sibling(later_slots), 0, *passing[:3], after)
        return wout_g.reshape(D_MODEL, D_MODEL), wup_g, wdown_g.reshape(D_FF, D_MODEL)

    deliveries, kept = {}, {}

    def deliver(name, plan, wires):
        lands = [lax.empty((N_DEV - 1,) + (g.shape[1:] if plan is _block_to_owner else g.shape), g.dtype) for g in wires]
        deliveries[name] = _start_copies(name + "_start", plan, wires, lands, unordered)
        return deliveries[name][3]

    def landed(name, plan, after):
        send, recv, flying, _ = deliveries[name]
        return _wait_copies(name + "_wait", plan, len(flying) // 2, send, recv, flying, after)

    def ship_down(d_w_down):
        kept["down"] = d_w_down[0]
        return deliver("deliver_down", _block_to_owner, [d_w_down[1]])

    def ship_up(d_w_out, d_w_up):
        kept["out"], kept["up"] = d_w_out[0].reshape(N_DEV, OUT_BLOCK, D_MODEL), d_w_up[0]
        return deliver("deliver_up", _block_to_owner, [d_w_out[1].reshape(N_DEV, OUT_BLOCK, D_MODEL), d_w_up[1]])

    def ship_in(d_w_in):
        kept["in"] = d_w_in[0].reshape(N_DEV, IN_BLOCK, D_MODEL)
        return deliver("deliver_in", _block_to_owner, [d_w_in[1].reshape(N_DEV, IN_BLOCK, D_MODEL)])

    grad_x, small = _local_step(x, loss_target, attn_norm_g, attn_sinks, w_pool, pool_scale, mlp_norm_g, final_norm_g,
                                later_copies[3], first_weight, second_hop, later_weights, ship_down, ship_up, ship_in)
    small_token = deliver("deliver_small", _whole_to_all, [small])

    _, (got_down,) = landed("deliver_down", _block_to_owner, small_token)
    _, (got_out, got_up) = landed("deliver_up", _block_to_owner, small_token)
    g_down = _adamw_sharded(me, kept["down"], got_down, w_down[0], m_w_down[0], v_w_down[0], 256)
    g_up = _adamw_sharded(me, kept["up"], got_up, w_up[0], m_w_up[0], v_w_up[0], 256)
    g_out = _adamw_sharded(me, kept["out"], got_out, w_out[0], m_w_out[0], v_w_out[0], 128)
    _, (got_in,) = landed("deliver_in", _block_to_owner, g_out[0])
    grad_in = _sum_blocks(me, kept["in"], got_in).T
    g_in = [grad_in, *_adamw_whole(grad_in, w_in[0], m_w_in[0], v_w_in[0])]
    (own_small,), (got_small,) = landed("deliver_small", _whole_to_all, g_in[1])

    row = lambda a: a.reshape(1, D_MODEL)
    params = [(attn_norm_g, m_attn_norm_g, v_attn_norm_g), (mlp_norm_g, m_mlp_norm_g, v_mlp_norm_g),
              (row(final_norm_g), row(m_final_norm_g), row(v_final_norm_g)), (pool_scale, m_pool_scale, v_pool_scale),
              (attn_sinks, m_attn_sinks, v_attn_sinks), (w_pool[0], m_w_pool[0], v_w_pool[0])]
    loss, (s_norm1, s_norm2, s_norm3, s_scale, s_sinks, s_pool) = _finish_small(me, own_small, got_small, params)
    s_norm3 = [a.reshape(D_MODEL) for a in s_norm3]
    s_pool = [a[None] for a in s_pool]

    def ordered(k):
        return [s_norm1[k], g_in[k][None], s_sinks[k], s_pool[k], s_scale[k], g_out[k][None], s_norm2[k], g_up[k][None], g_down[k][None],
                s_norm3[k]]

    return (loss.reshape(()), grad_x, *ordered(0), *ordered(1), *ordered(2), *ordered(3))
```

```python
import jax
import jax.numpy as jnp
from jax import lax
from jax.experimental import pallas as pl
from jax.experimental.pallas import tpu as pltpu

F32 = jnp.float32
BF16 = jnp.bfloat16

D_MODEL = 1024
HEAD_DIM = 64
N_Q_HEADS = 8
Q_PER_KV = 4
ATTN_WIDTH = 512
KV_WIDTH = 128
BLOCK = 128
ROPE_THETA = 10000.0
POOL_WINDOWS = (2, 4, 8, 16)
POOL_WIDTH = 512
POOL_GROUP_DIM = 128
IN_WIDTH = 1280
D_FF = 4096
EPS = 1e-6
N_DEV = 8
FF_BLOCK = D_FF // N_DEV
IN_BLOCK = IN_WIDTH // N_DEV
OUT_BLOCK = D_MODEL // N_DEV
ADAM_LR = 0.001
ADAM_B1 = 0.9
ADAM_B2 = 0.999
ADAM_EPS = 1e-08
ADAM_WD = 0.01
ADAM_STEP = 10
NEG = -1e30
FORWARD_CHAINS = 8
BACKWARD_CHAINS = 2
LANES = 128
MIB = 1024 * 1024
MESH = pl.DeviceIdType.MESH

ROW_G1, ROW_G2, ROW_G3, ROW_PS, ROW_SINK, ROW_LOSS, ROW_WP, SMALL_ROWS = 0, 8, 16, 24, 32, 40, 48, 560


def _cparams(semantics, vmem_mib):
    return pltpu.CompilerParams(dimension_semantics=semantics, vmem_limit_bytes=vmem_mib * MIB)


def _in_hbm(*arrays):
    pinned = tuple(pltpu.with_memory_space_constraint(a, pltpu.HBM) for a in arrays)
    return pinned[0] if len(pinned) == 1 else pinned


def _dot(a, b):
    return jnp.dot(a, b, preferred_element_type=F32)


def _dot_nt(a, b):
    return lax.dot_general(a, b, (((1,), (1,)), ((), ())), preferred_element_type=F32)


def _dot_tn(a, b):
    return lax.dot_general(a, b, (((0,), (0,)), ((), ())), preferred_element_type=F32)


def _swap_halves(x):
    width = x.shape[1]
    lane = lax.broadcasted_iota(jnp.int32, x.shape, 1)
    ahead = pltpu.roll(x, width - HEAD_DIM // 2, 1)
    behind = pltpu.roll(x, HEAD_DIM // 2, 1)
    return jnp.where(lane % HEAD_DIM < HEAD_DIM // 2, ahead, behind)


def _rope(x, cos, sin):
    reps = x.shape[1] // LANES
    if reps > 1:
        cos = jnp.tile(cos, (1, reps))
        sin = jnp.tile(sin, (1, reps))
    return x * cos + _swap_halves(x) * sin


def _rope_tables(seq):
    half = HEAD_DIM // 2
    inv_freq = ROPE_THETA ** (-jnp.arange(half, dtype=F32) / half)
    ang = jnp.arange(seq).astype(F32)[:, None] * inv_freq[None, :]
    cos, sin = jnp.cos(ang), jnp.sin(ang)
    cos = jnp.tile(cos, (1, LANES // half))
    sin = jnp.tile(jnp.concatenate([-sin, sin], axis=1), (1, LANES // HEAD_DIM))
    return cos, sin


def _both_halves(x):
    lane = lax.broadcasted_iota(jnp.int32, x.shape, 1)
    other = pltpu.roll(x, HEAD_DIM, 1)
    low = lane < HEAD_DIM
    return jnp.where(low, x, other), jnp.where(low, other, x)


def _rms_backward(dh, xin, gain):
    r = lax.rsqrt(jnp.mean(xin * xin, axis=-1, keepdims=True) + EPS)
    xhat = xin * r
    dxhat = dh * gain
    dx = r * (dxhat - xhat * jnp.mean(dxhat * xhat, axis=-1, keepdims=True))
    return dx, jnp.sum(dh * xhat, axis=0, keepdims=True)


def _first_norm(x2d, gain, token, tm):
    rows = x2d.shape[0]

    def body(x_ref, g_ref, token_ref, h_ref):
        x = x_ref[...]
        r = lax.rsqrt(jnp.mean(x * x, axis=-1, keepdims=True) + EPS)
        h_ref[...] = (x * r * g_ref[...]).astype(BF16)

    row = pl.BlockSpec((tm, D_MODEL), lambda i: (i, 0))
    return pl.pallas_call(
        body, name="first_norm", grid=(rows // tm,),
        in_specs=[row, pl.BlockSpec((1, D_MODEL), lambda i: (0, 0)), pl.BlockSpec((8, LANES), lambda i: (0, 0))],
        out_specs=row, out_shape=pltpu.HBM((rows, D_MODEL), BF16),
        compiler_params=_cparams(("parallel",), 40),
    )(x2d, gain, token)


def _inproj(h1, w_in, cos, sin, seq, tm):
    rows = h1.shape[0]
    tiles_per_seq = seq // tm

    def body(h_ref, w_ref, cos_ref, sin_ref, q_ref, k_ref, v_ref, u_ref):
        proj = _dot_nt(h_ref[...], w_ref[...])
        cos_t, sin_t = cos_ref[...], sin_ref[...]
        q = _rope(proj[:, :ATTN_WIDTH], cos_t, sin_t) * (HEAD_DIM ** -0.5)
        q_ref[...] = q.astype(BF16)
        k = _rope(proj[:, ATTN_WIDTH:ATTN_WIDTH + KV_WIDTH], cos_t, sin_t)
        k0, k1 = _both_halves(k)
        k_ref[...] = jnp.concatenate([k0, k1], axis=1).astype(BF16)
        v0, v1 = _both_halves(proj[:, ATTN_WIDTH + KV_WIDTH:ATTN_WIDTH + 2 * KV_WIDTH])
        v_ref[...] = jnp.concatenate([v0, v1], axis=1).astype(BF16)
        u_ref[...] = proj[:, ATTN_WIDTH + 2 * KV_WIDTH:]

    row = lambda width: pl.BlockSpec((tm, width), lambda i: (i, 0))
    table = pl.BlockSpec((tm, LANES), lambda i: (i % tiles_per_seq, 0))
    return pl.pallas_call(
        body, name="inproj", grid=(rows // tm,),
        in_specs=[row(D_MODEL), pl.BlockSpec((IN_WIDTH, D_MODEL), lambda i: (0, 0)), table, table],
        out_specs=[row(ATTN_WIDTH), row(2 * KV_WIDTH), row(2 * KV_WIDTH), row(POOL_WIDTH)],
        out_shape=[pltpu.HBM((rows, ATTN_WIDTH), BF16), pltpu.HBM((rows, 2 * KV_WIDTH), BF16),
                   pltpu.HBM((rows, 2 * KV_WIDTH), BF16), pltpu.HBM((rows, POOL_WIDTH), F32)],
        compiler_params=_cparams(("parallel",), 40),
    )(h1, w_in, cos, sin)


def _window_masks(n):
    qi = lax.broadcasted_iota(jnp.int32, (BLOCK, BLOCK), 0)
    kj = lax.broadcasted_iota(jnp.int32, (BLOCK, BLOCK), 1)
    return kj <= qi, jnp.logical_and(kj > qi, n > 0)


def _window_operand(ref, r0, p0, kv):
    low = lax.broadcasted_iota(jnp.int32, (BLOCK, LANES), 1) < HEAD_DIM
    cur = ref[pl.ds(r0, BLOCK), kv * LANES:(kv + 1) * LANES]
    prev = ref[pl.ds(p0, BLOCK), kv * LANES:(kv + 1) * LANES]
    zero = jnp.zeros_like(cur)
    return jnp.concatenate([jnp.where(low, cur, zero), jnp.where(low, zero, cur), jnp.where(low, prev, zero), jnp.where(low, zero, prev)], axis=0)


def _pair_rows(ref, r0, kv):
    return jnp.concatenate([ref[pl.ds(r0, BLOCK), (2 * kv + j) * LANES:(2 * kv + j + 1) * LANES] for j in range(2)], axis=0)


def _merged_window(wide, parity, cur_mask, prev_mask, fill):
    cur = wide[:, parity * LANES:(parity + 1) * LANES]
    prev = wide[:, (2 + parity) * LANES:(3 + parity) * LANES]
    return jnp.where(cur_mask, cur, jnp.where(prev_mask, prev, fill))


def _lane_sums(x, one_matmul):
    flat = x.reshape(-1, x.shape[-1])
    high = flat.astype(BF16)
    low = (flat - high.astype(F32)).astype(BF16)
    if one_matmul:
        sums = _dot(jnp.concatenate([high, low], axis=1), jnp.ones((2 * x.shape[-1], LANES), BF16))
    else:
        ones = jnp.ones((x.shape[-1], LANES), BF16)
        sums = _dot(high, ones) + _dot(low, ones)
    return sums.reshape(x.shape[:-1] + (LANES,))


def _softmax_with_sink(scores, sink, one_matmul):
    m = jnp.broadcast_to(jnp.maximum(jnp.max(scores, axis=-1, keepdims=True), sink), scores.shape)
    p, ps = jnp.exp(scores - m), jnp.exp(sink - m)
    inv = 1.0 / (_lane_sums(p, one_matmul) + ps)
    return p * inv, ps * inv


def _attention_forward(sinks, q, kd, vd, n_seq, seq):
    n_blocks = seq // BLOCK
    n_pairs = N_Q_HEADS // 2
    chains = min(FORWARD_CHAINS, n_blocks)

    def body(sink_ref, q_ref, k_ref, v_ref, o_ref, s_ref, p_ref):
        def step(i, carry):
            starts, values = [], []
            for u in range(chains):
                n = i * chains + u
                r0 = pl.multiple_of(n * BLOCK, BLOCK)
                p0 = pl.multiple_of(jnp.maximum(n - 1, 0) * BLOCK, BLOCK)
                cur_mask, prev_mask = _window_masks(n)
                keys = [_window_operand(k_ref, r0, p0, kv) for kv in range(2)]
                starts.append(r0)
                values.append([_window_operand(v_ref, r0, p0, kv) for kv in range(2)])
                for kv in range(2):
                    both = _dot_nt(_pair_rows(q_ref, r0, kv), keys[kv])
                    for j in range(2):
                        wide = both[j * BLOCK:(j + 1) * BLOCK]
                        for parity in range(2):
                            s_ref[u * N_Q_HEADS + 4 * kv + 2 * j + parity] = _merged_window(wide, parity, cur_mask, prev_mask, NEG)
            probs, _ = _softmax_with_sink(s_ref[...], jnp.tile(sink_ref[:, :, 0:1], (chains, 1, 1)), False)
            probs = probs.astype(BF16)
            zero = jnp.zeros((BLOCK, BLOCK), BF16)
            for u in range(chains):
                for pair in range(n_pairs):
                    for parity in range(2):
                        ph = probs[u * N_Q_HEADS + 2 * pair + parity]
                        p_ref[u, pair, :, parity * LANES:(parity + 1) * LANES] = jnp.where(cur_mask, ph, zero)
                        p_ref[u, pair, :, (2 + parity) * LANES:(3 + parity) * LANES] = jnp.where(cur_mask, zero, ph)
            for u in range(chains):
                for kv in range(2):
                    both = _dot(p_ref[u, 2 * kv:2 * kv + 2].reshape(2 * BLOCK, 4 * LANES), values[u][kv])
                    for j in range(2):
                        pair = 2 * kv + j
                        o_ref[pl.ds(starts[u], BLOCK), pair * LANES:(pair + 1) * LANES] = both[j * BLOCK:(j + 1) * BLOCK].astype(BF16)
            return carry

        lax.fori_loop(0, n_blocks // chains, step, 0)

    seq_block = lambda width: pl.BlockSpec((seq, width), lambda b: (b, 0))
    return pl.pallas_call(
        body, name="attention_forward", grid=(n_seq,),
        in_specs=[pl.BlockSpec((N_Q_HEADS, 1, LANES), lambda b: (0, 0, 0)), seq_block(ATTN_WIDTH), seq_block(2 * KV_WIDTH),
                  seq_block(2 * KV_WIDTH)],
        out_specs=seq_block(ATTN_WIDTH),
        out_shape=pltpu.HBM((n_seq * seq, ATTN_WIDTH), BF16),
        scratch_shapes=[pltpu.VMEM((chains * N_Q_HEADS, BLOCK, BLOCK), F32), pltpu.VMEM((chains, n_pairs, BLOCK, 4 * LANES), BF16)],
        compiler_params=_cparams(("parallel",), 40),
    )(sinks, q, kd, vd)


def _trailing(x, window, t, seq):
    k = 1
    while k < window:
        x = x + jnp.where(t >= k, pltpu.roll(x, k, 0), 0.0)
        k *= 2
    return x


def _leading(x, window, t, seq):
    k = 1
    while k < window:
        x = x + jnp.where(t < seq - k, pltpu.roll(x, seq - k, 0), 0.0)
        k *= 2
    return x


def _pool_features(u_g, window, t, seq):
    count = jnp.minimum(t + 1, window).astype(F32)
    return (_trailing(u_g, window, t, seq) / count - u_g).astype(BF16), count


def _pool_forward(u, w_pool, pool_scale, token, n_seq, seq):
    def body(u_ref, w_ref, s_ref, token_ref, o_ref):
        t = lax.broadcasted_iota(jnp.int32, (seq, 1), 0)
        for g, window in enumerate(POOL_WINDOWS):
            cols = slice(g * POOL_GROUP_DIM, (g + 1) * POOL_GROUP_DIM)
            d, _ = _pool_features(u_ref[:, cols], window, t, seq)
            o_ref[:, cols] = (_dot(d, w_ref[g]) * s_ref[:, cols]).astype(BF16)

    seq_block = pl.BlockSpec((seq, POOL_WIDTH), lambda b: (b, 0))
    return pl.pallas_call(
        body, name="pool_forward", grid=(n_seq,),
        in_specs=[seq_block, pl.BlockSpec((len(POOL_WINDOWS), POOL_GROUP_DIM, POOL_GROUP_DIM), lambda b: (0, 0, 0)),
                  pl.BlockSpec((1, POOL_WIDTH), lambda b: (0, 0)), pl.BlockSpec((8, LANES), lambda b: (0, 0))],
        out_specs=seq_block,
        out_shape=pltpu.HBM((n_seq * seq, POOL_WIDTH), BF16),
        compiler_params=_cparams(("parallel",), 40),
    )(u, w_pool, pool_scale, token)


def _outproj_norm(x2d, attn, pool, w_out, gain, tm):
    rows = x2d.shape[0]

    def body(x_ref, a_ref, p_ref, w_ref, g_ref, x2_ref, h_ref):
        x2 = x_ref[...] + _dot(a_ref[...], w_ref[:ATTN_WIDTH, :]) + _dot(p_ref[...], w_ref[ATTN_WIDTH:, :])
        x2_ref[...] = x2
        r = lax.rsqrt(jnp.mean(x2 * x2, axis=-1, keepdims=True) + EPS)
        h_ref[...] = (x2 * r * g_ref[...]).astype(BF16)

    row = lambda width: pl.BlockSpec((tm, width), lambda i: (i, 0))
    return pl.pallas_call(
        body, name="outproj_norm", grid=(rows // tm,),
        in_specs=[row(D_MODEL), row(ATTN_WIDTH), row(POOL_WIDTH), pl.BlockSpec((D_MODEL, D_MODEL), lambda i: (0, 0)),
                  pl.BlockSpec((1, D_MODEL), lambda i: (0, 0))],
        out_specs=[row(D_MODEL), row(D_MODEL)],
        out_shape=[pltpu.HBM((rows, D_MODEL), F32), pltpu.HBM((rows, D_MODEL), BF16)],
        compiler_params=_cparams(("parallel",), 40),
    )(x2d, attn, pool, w_out, gain)


def _resident(shape):
    return pl.BlockSpec(shape, lambda i: (0,) * len(shape), pipeline_mode=pl.Buffered(1))


def _mlp_forward_loss(h2, x2, w_up, w_down, gain, target, tm):
    rows = h2.shape[0]
    chunk = D_MODEL

    def body(h_ref, x_ref, up_ref, down_ref, g_ref, t_ref, slope_ref, f_ref, dx_ref, dxb_ref, loss_ref, dg_ref):
        @pl.when(pl.program_id(0) == 0)
        def _():
            loss_ref[...] = jnp.zeros_like(loss_ref)
            dg_ref[...] = jnp.zeros_like(dg_ref)

        h = h_ref[...]
        for c in range(D_FF // chunk):
            cols = slice(c * chunk, (c + 1) * chunk)
            r = jnp.maximum(_dot(h, up_ref[:, cols]), 0.0)
            slope_ref[:, cols] = (r + r).astype(BF16)
            f_ref[:, cols] = (r * r).astype(BF16)
        x3 = x_ref[...] + _dot(f_ref[...], down_ref[...])
        rn = lax.rsqrt(jnp.mean(x3 * x3, axis=-1, keepdims=True) + EPS)
        xhat = x3 * rn
        err = xhat * g_ref[...] - t_ref[...]
        loss_ref[...] += jnp.sum(err * err, axis=0, keepdims=True)
        dxhat = err * (g_ref[...] * (1.0 / D_MODEL))
        dx = rn * (dxhat - xhat * jnp.mean(dxhat * xhat, axis=-1, keepdims=True))
        dg_ref[...] += jnp.sum(err * xhat, axis=0, keepdims=True) * (1.0 / D_MODEL)
        dx_ref[...] = dx
        dxb_ref[...] = dx.astype(BF16)

    row = lambda width: pl.BlockSpec((tm, width), lambda i: (i, 0))
    vec = pl.BlockSpec((1, D_MODEL), lambda i: (0, 0))
    return pl.pallas_call(
        body, name="mlp_forward_loss", grid=(rows // tm,),
        in_specs=[row(D_MODEL), row(D_MODEL), _resident((D_MODEL, D_FF)), _resident((D_FF, D_MODEL)), vec, row(D_MODEL)],
        out_specs=[row(D_FF), row(D_FF), row(D_MODEL), row(D_MODEL), vec, vec],
        out_shape=[pltpu.HBM((rows, D_FF), BF16), pltpu.HBM((rows, D_FF), BF16),
                   pltpu.HBM((rows, D_MODEL), F32), pltpu.HBM((rows, D_MODEL), BF16),
                   jax.ShapeDtypeStruct((1, D_MODEL), F32), jax.ShapeDtypeStruct((1, D_MODEL), F32)],
        compiler_params=_cparams(("arbitrary",), 56),
    )(h2, x2, w_up, w_down, gain, target)


def _mlp_backward_data(dx3b, slope, w_down, w_up, dx3, x2, gain, w_out, attn, pool, token, tm):
    rows = dx3b.shape[0]
    steps = rows // tm
    chunk = D_MODEL

    def body(dxb_ref, slope_ref, down_ref, up_ref, dx3_ref, x2_ref, g_ref, wo_ref, attn_ref, pool_ref, token_ref,
             da_ref, dx2_ref, dattn_ref, dpool_ref, dg_ref, dwo_hbm, wire_hbm, dwo_acc, wire, sems):
        @pl.when(pl.program_id(0) == 0)
        def _():
            dg_ref[...] = jnp.zeros_like(dg_ref)
            dwo_acc[...] = jnp.zeros_like(dwo_acc)

        dxb = dxb_ref[...]
        for c in range(D_FF // chunk):
            cols = slice(c * chunk, (c + 1) * chunk)
            da_ref[:, cols] = (_dot_nt(dxb, down_ref[cols, :]) * slope_ref[:, cols].astype(F32)).astype(BF16)
        dnorm, dg = _rms_backward(_dot_nt(da_ref[...], up_ref[...]), x2_ref[...], g_ref[...])
        dg_ref[...] += dg
        dx2 = dx3_ref[...] + dnorm
        dx2_ref[...] = dx2
        dx2b = dx2.astype(BF16)
        dmix = _dot_nt(dx2b, wo_ref[...])
        dattn_ref[...] = dmix[:, :ATTN_WIDTH].astype(BF16)
        dpool_ref[...] = dmix[:, ATTN_WIDTH:]
        dwo_acc[:ATTN_WIDTH, :] += _dot_tn(attn_ref[...], dx2b)
        dwo_acc[ATTN_WIDTH:, :] += _dot_tn(pool_ref[...], dx2b)

        @pl.when(pl.program_id(0) == steps - 1)
        def _():
            done = pltpu.make_async_copy(dwo_acc, dwo_hbm, sems.at[0])
            done.start()
            wire[...] = dwo_acc[...].astype(BF16)
            sent = pltpu.make_async_copy(wire, wire_hbm, sems.at[1])
            sent.start()
            done.wait()
            sent.wait()

    row = lambda width: pl.BlockSpec((tm, width), lambda i: (i, 0))
    vec = pl.BlockSpec((1, D_MODEL), lambda i: (0, 0))
    return pl.pallas_call(
        body, name="mlp_backward_data", grid=(steps,),
        in_specs=[row(D_MODEL), row(D_FF), _resident((D_FF, D_MODEL)), _resident((D_MODEL, D_FF)), row(D_MODEL), row(D_MODEL), vec,
                  _resident((D_MODEL, D_MODEL)), row(ATTN_WIDTH), row(POOL_WIDTH), pl.BlockSpec((8, LANES), lambda i: (0, 0))],
        out_specs=[row(D_FF), row(D_MODEL), row(ATTN_WIDTH), row(POOL_WIDTH), vec, pl.BlockSpec(memory_space=pl.ANY),
                   pl.BlockSpec(memory_space=pl.ANY)],
        out_shape=[pltpu.HBM((rows, D_FF), BF16), pltpu.HBM((rows, D_MODEL), F32),
                   pltpu.HBM((rows, ATTN_WIDTH), BF16), pltpu.HBM((rows, POOL_WIDTH), F32),
                   jax.ShapeDtypeStruct((1, D_MODEL), F32), jax.ShapeDtypeStruct((D_MODEL, D_MODEL), F32),
                   jax.ShapeDtypeStruct((D_MODEL, D_MODEL), BF16)],
        scratch_shapes=[pltpu.VMEM((D_MODEL, D_MODEL), F32), pltpu.VMEM((D_MODEL, D_MODEL), BF16), pltpu.SemaphoreType.DMA((2,))],
        compiler_params=_cparams(("arbitrary",), 56),
    )(dx3b, slope, w_down, w_up, dx3, x2, gain, w_out, attn, pool, token)


def _weight_gradient(name, lhs, rhs, block_lhs, tm):
    rows = lhs.shape[0]
    steps = rows // tm
    out = (N_DEV, FF_BLOCK, rhs.shape[1]) if block_lhs else (N_DEV, lhs.shape[1], FF_BLOCK)

    def body(l_ref, r_ref, o_hbm, wire_hbm, acc, wire, sems):
        @pl.when(pl.program_id(0) == 0)
        def _():
            acc[...] = jnp.zeros_like(acc)

        for d in range(N_DEV):
            cols = slice(d * FF_BLOCK, (d + 1) * FF_BLOCK)
            acc[d] += _dot_tn(l_ref[:, cols], r_ref[...]) if block_lhs else _dot_tn(l_ref[...], r_ref[:, cols])

        @pl.when(pl.program_id(0) == steps - 1)
        def _():
            done = pltpu.make_async_copy(acc, o_hbm, sems.at[0])
            done.start()
            wire[...] = acc[...].astype(BF16)
            sent = pltpu.make_async_copy(wire, wire_hbm, sems.at[1])
            sent.start()
            done.wait()
            sent.wait()

    return pl.pallas_call(
        body, name=name, grid=(steps,),
        in_specs=[pl.BlockSpec((tm, lhs.shape[1]), lambda i: (i, 0)), pl.BlockSpec((tm, rhs.shape[1]), lambda i: (i, 0))],
        out_specs=[pl.BlockSpec(memory_space=pl.ANY), pl.BlockSpec(memory_space=pl.ANY)],
        out_shape=[jax.ShapeDtypeStruct(out, F32), jax.ShapeDtypeStruct(out, BF16)],
        scratch_shapes=[pltpu.VMEM(out, F32), pltpu.VMEM(out, BF16), pltpu.SemaphoreType.DMA((2,))],
        compiler_params=_cparams(("arbitrary",), 60),
    )(lhs, rhs)


def _attention_backward(sinks, q, kd, vd, dout, cos, sin, token, n_seq, seq):
    n_blocks = seq // BLOCK
    n_kv = N_Q_HEADS // Q_PER_KV
    chains = min(BACKWARD_CHAINS, n_blocks)

    def body(sink_ref, q_ref, k_ref, v_ref, do_ref, cos_ref, sin_ref, token_ref, dq_ref, dk_ref, dv_ref, dsink_ref,
             s_all, dp_all, dsc_all, dsp_all, pc_all, pp_all, dk_acc, dv_acc):
        low = lax.broadcasted_iota(jnp.int32, (BLOCK, LANES), 1) < HEAD_DIM

        @pl.when(pl.program_id(0) == 0)
        def _():
            dsink_ref[...] = jnp.zeros_like(dsink_ref)

        def fold(x):
            return x + pltpu.roll(x, HEAD_DIM, 1)

        def onto_keys(cur_ref, prev_ref, kv, other):
            even = _dot_tn(jnp.concatenate([cur_ref[2 * kv], prev_ref[2 * kv]], axis=1), other)
            odd = _dot_tn(jnp.concatenate([cur_ref[2 * kv + 1], prev_ref[2 * kv + 1]], axis=1), other)
            both = fold(jnp.where(jnp.concatenate([low, low], axis=0), even, odd))
            return both[:BLOCK], both[BLOCK:]

        def step(i, dsink):
            blocks = []
            for u in range(chains):
                n = i * chains + u
                r0 = pl.multiple_of(n * BLOCK, BLOCK)
                p0 = pl.multiple_of(jnp.maximum(n - 1, 0) * BLOCK, BLOCK)
                cur_mask, prev_mask = _window_masks(n)
                keys = [_window_operand(k_ref, r0, p0, kv) for kv in range(n_kv)]
                values = [_window_operand(v_ref, r0, p0, kv) for kv in range(n_kv)]
                q_rows = [_pair_rows(q_ref, r0, kv) for kv in range(n_kv)]
                do_rows = [_pair_rows(do_ref, r0, kv) for kv in range(n_kv)]
                for kv in range(n_kv):
                    both_s, both_dp = _dot_nt(q_rows[kv], keys[kv]), _dot_nt(do_rows[kv], values[kv])
                    for j in range(2):
                        wide_s, wide_dp = both_s[j * BLOCK:(j + 1) * BLOCK], both_dp[j * BLOCK:(j + 1) * BLOCK]
                        for parity in range(2):
                            head = u * N_Q_HEADS + 4 * kv + 2 * j + parity
                            s_all[head] = _merged_window(wide_s, parity, cur_mask, prev_mask, NEG)
                            dp_all[head] = _merged_window(wide_dp, parity, cur_mask, prev_mask, 0.0)
                blocks.append((n, r0, p0, keys, q_rows, do_rows))

            probs, p_sink = _softmax_with_sink(s_all[...], jnp.tile(sink_ref[:, :, 0:1], (chains, 1, 1)), True)
            dprobs = dp_all[...]
            delta = _lane_sums(probs * dprobs, True)
            dscores = (probs * (dprobs - delta)).astype(BF16)
            sink_terms = jnp.sum((p_sink * delta)[:, :, 0:1], axis=1, keepdims=True)
            probs = probs.astype(BF16)
            zero = jnp.zeros((BLOCK, BLOCK), BF16)
            for u in range(chains):
                dsink = dsink - sink_terms[u * N_Q_HEADS:(u + 1) * N_Q_HEADS]
                for head in range(N_Q_HEADS):
                    group, rows = 2 * (head // Q_PER_KV) + head % 2, pl.ds(((head % Q_PER_KV) // 2) * BLOCK, BLOCK)
                    ds_h, p_h = dscores[u * N_Q_HEADS + head], probs[u * N_Q_HEADS + head]
                    dsc_all[u, group, rows, :] = jnp.where(cur_mask, ds_h, zero)
                    dsp_all[u, group, rows, :] = jnp.where(cur_mask, zero, ds_h)
                    pc_all[u, group, rows, :] = jnp.where(cur_mask, p_h, zero)
                    pp_all[u, group, rows, :] = jnp.where(cur_mask, zero, p_h)

            for u, (n, r0, p0, keys, q_rows, do_rows) in enumerate(blocks):
                dsc_ref, dsp_ref, pc_ref, pp_ref = dsc_all.at[u], dsp_all.at[u], pc_all.at[u], pp_all.at[u]
                for kv in range(n_kv):
                    wide = jnp.concatenate([dsc_ref[2 * kv], dsc_ref[2 * kv + 1], dsp_ref[2 * kv], dsp_ref[2 * kv + 1]], axis=1)
                    both = _dot(wide, keys[kv]) * (HEAD_DIM ** -0.5)
                    for j in range(2):
                        dq = _rope(both[j * BLOCK:(j + 1) * BLOCK], cos_ref[pl.ds(r0, BLOCK), :], -sin_ref[pl.ds(r0, BLOCK), :])
                        dq_ref[pl.ds(r0, BLOCK), (2 * kv + j) * LANES:(2 * kv + j + 1) * LANES] = dq.astype(BF16)

                parts = []
                for kv in range(n_kv):
                    parts.append(onto_keys(dsc_ref, dsp_ref, kv, q_rows[kv]) + onto_keys(pc_ref, pp_ref, kv, do_rows[kv]))
                dk_acc[pl.ds(r0, BLOCK), :] = jnp.where(low, parts[0][0], parts[1][0])
                dv_acc[pl.ds(r0, BLOCK), :] = jnp.where(low, parts[0][2], parts[1][2])

                @pl.when(n > 0)
                def _():
                    dk_acc[pl.ds(p0, BLOCK), :] += jnp.where(low, parts[0][1], parts[1][1])
                    dv_acc[pl.ds(p0, BLOCK), :] += jnp.where(low, parts[0][3], parts[1][3])

            return dsink

        dsink = lax.fori_loop(0, n_blocks // chains, step, jnp.zeros((N_Q_HEADS, 1, 1), F32))
        dsink_ref[...] += jnp.broadcast_to(dsink, dsink_ref.shape)
        dk_ref[...] = _rope(dk_acc[...], cos_ref[...], -sin_ref[...]).astype(BF16)
        dv_ref[...] = dv_acc[...].astype(BF16)

    seq_block = lambda width: pl.BlockSpec((seq, width), lambda b: (b, 0))
    table = pl.BlockSpec((seq, LANES), lambda b: (0, 0))
    per_head = pl.BlockSpec((N_Q_HEADS, 1, LANES), lambda b: (0, 0, 0))
    per_block = pltpu.VMEM((chains * N_Q_HEADS, BLOCK, BLOCK), F32)
    grouped = pltpu.VMEM((chains, 2 * n_kv, 2 * BLOCK, BLOCK), BF16)
    return pl.pallas_call(
        body, name="attention_backward", grid=(n_seq,),
        in_specs=[per_head, seq_block(ATTN_WIDTH), seq_block(2 * KV_WIDTH), seq_block(2 * KV_WIDTH),
                  seq_block(ATTN_WIDTH), table, table, pl.BlockSpec((8, LANES), lambda b: (0, 0))],
        out_specs=[seq_block(ATTN_WIDTH), seq_block(KV_WIDTH), seq_block(KV_WIDTH), per_head],
        out_shape=[pltpu.HBM((n_seq * seq, ATTN_WIDTH), BF16), pltpu.HBM((n_seq * seq, KV_WIDTH), BF16),
                   pltpu.HBM((n_seq * seq, KV_WIDTH), BF16), jax.ShapeDtypeStruct((N_Q_HEADS, 1, LANES), F32)],
        scratch_shapes=[per_block, per_block, grouped, grouped, grouped, grouped, pltpu.VMEM((seq, KV_WIDTH), F32), pltpu.VMEM((seq, KV_WIDTH), F32)],
        compiler_params=_cparams(("arbitrary",), 40),
    )(sinks, q, kd, vd, dout, cos, sin, token)


def _pool_backward(u, dpool, w_pool, pool_scale, n_seq, seq):
    groups = len(POOL_WINDOWS)

    def body(u_ref, dp_ref, w_ref, s_ref, du_ref, dw_ref, ds_ref):
        @pl.when(pl.program_id(0) == 0)
        def _():
            dw_ref[...] = jnp.zeros_like(dw_ref)
            ds_ref[...] = jnp.zeros_like(ds_ref)

        t = lax.broadcasted_iota(jnp.int32, (seq, 1), 0)
        for g, window in enumerate(POOL_WINDOWS):
            cols = slice(g * POOL_GROUP_DIM, (g + 1) * POOL_GROUP_DIM)
            d, count = _pool_features(u_ref[:, cols], window, t, seq)
            dpool_g = dp_ref[:, cols]
            ds_ref[:, cols] += jnp.sum(dpool_g * _dot(d, w_ref[g]), axis=0, keepdims=True)
            dy = (dpool_g * s_ref[:, cols]).astype(BF16)
            dw_ref[g] += _dot_tn(d, dy)
            dd = _dot_nt(dy, w_ref[g])
            du_ref[:, cols] = (_leading(dd / count, window, t, seq) - dd).astype(BF16)

    seq_block = pl.BlockSpec((seq, POOL_WIDTH), lambda b: (b, 0))
    weights = pl.BlockSpec((groups, POOL_GROUP_DIM, POOL_GROUP_DIM), lambda b: (0, 0, 0))
    scale = pl.BlockSpec((1, POOL_WIDTH), lambda b: (0, 0))
    return pl.pallas_call(
        body, name="pool_backward", grid=(n_seq,),
        in_specs=[seq_block, seq_block, weights, scale],
        out_specs=[seq_block, weights, scale],
        out_shape=[pltpu.HBM((n_seq * seq, POOL_WIDTH), BF16),
                   jax.ShapeDtypeStruct((groups, POOL_GROUP_DIM, POOL_GROUP_DIM), F32), jax.ShapeDtypeStruct((1, POOL_WIDTH), F32)],
        compiler_params=_cparams(("arbitrary",), 40),
    )(u, dpool, w_pool, pool_scale)


def _inproj_gradient(dq, dk, dv, du, h1, tm):
    rows = h1.shape[0]
    steps = rows // tm

    def body(dq_ref, dk_ref, dv_ref, du_ref, h_ref, dw_ref, wire_ref):
        @pl.when(pl.program_id(0) == 0)
        def _():
            dw_ref[...] = jnp.zeros_like(dw_ref)

        dproj = jnp.concatenate([dq_ref[...], dk_ref[...], dv_ref[...], du_ref[...]], axis=1)
        dw_ref[...] += _dot_tn(dproj, h_ref[...])

        @pl.when(pl.program_id(0) == steps - 1)
        def _():
            wire_ref[...] = dw_ref[...].astype(BF16)

    row = lambda width: pl.BlockSpec((tm, width), lambda i: (i, 0))
    whole = pl.BlockSpec((IN_WIDTH, D_MODEL), lambda i: (0, 0))
    return pl.pallas_call(
        body, name="inproj_gradient", grid=(steps,),
        in_specs=[row(ATTN_WIDTH), row(KV_WIDTH), row(KV_WIDTH), row(POOL_WIDTH), row(D_MODEL)],
        out_specs=[whole, whole],
        out_shape=[pltpu.HBM((IN_WIDTH, D_MODEL), F32), pltpu.HBM((IN_WIDTH, D_MODEL), BF16)],
        compiler_params=_cparams(("arbitrary",), 48),
    )(dq, dk, dv, du, h1)


def _inproj_backward(dq, dk, dv, du, w_in, x2d, dx2, gain, token, tm):
    rows = x2d.shape[0]

    def body(dq_ref, dk_ref, dv_ref, du_ref, w_ref, x_ref, dx2_ref, g_ref, token_ref, dx_ref, dg_ref):
        @pl.when(pl.program_id(0) == 0)
        def _():
            dg_ref[...] = jnp.zeros_like(dg_ref)

        dproj = jnp.concatenate([dq_ref[...], dk_ref[...], dv_ref[...], du_ref[...]], axis=1)
        dnorm, dg = _rms_backward(_dot(dproj, w_ref[...]), x_ref[...], g_ref[...])
        dg_ref[...] += dg
        dx_ref[...] = dx2_ref[...] + dnorm

    row = lambda width: pl.BlockSpec((tm, width), lambda i: (i, 0))
    vec = pl.BlockSpec((1, D_MODEL), lambda i: (0, 0))
    return pl.pallas_call(
        body, name="inproj_backward", grid=(rows // tm,),
        in_specs=[row(ATTN_WIDTH), row(KV_WIDTH), row(KV_WIDTH), row(POOL_WIDTH), pl.BlockSpec((IN_WIDTH, D_MODEL), lambda i: (0, 0)),
                  row(D_MODEL), row(D_MODEL), vec, pl.BlockSpec((8, LANES), lambda i: (0, 0))],
        out_specs=[row(D_MODEL), vec],
        out_shape=[pltpu.HBM((rows, D_MODEL), F32), jax.ShapeDtypeStruct((1, D_MODEL), F32)],
        compiler_params=_cparams(("arbitrary",), 60),
    )(dq, dk, dv, du, w_in, x2d, dx2, gain, token)


def _place():
    return lax.axis_index("x"), lax.axis_index("y"), lax.axis_index("c")


def _peer(x, y, c, rel):
    return (1 - x if rel & 4 else x, 1 - y if rel & 2 else y, 1 - c if rel & 1 else c)


def _index(px, py, pc):
    return 4 * px + 2 * py + pc


def _row_slot(ref, d):
    return ref.at[d]


def _column_slot(ref, d):
    return ref.at[:, pl.ds(pl.multiple_of(d * FF_BLOCK, FF_BLOCK), FF_BLOCK)]


def _stage_weights(shards, slots, shapes):
    n = len(shards)

    def body(*refs):
        ins, outs, stage, sems = refs[:n], refs[n:2 * n], refs[2 * n:3 * n], refs[3 * n]
        me = _index(*_place())
        mine = []
        for a in range(n):
            stage[a][...] = ins[a][...].astype(BF16)
            mine.append(pltpu.make_async_copy(stage[a], slots[a](outs[a], me), sems.at[a]))
            mine[-1].start()
        for cp in mine:
            cp.wait()

    return pl.pallas_call(
        body, name="stage_weights",
        in_specs=[pl.BlockSpec(memory_space=pltpu.VMEM)] * n,
        out_specs=[pl.BlockSpec(memory_space=pl.ANY)] * n,
        out_shape=[pltpu.HBM(shape, BF16) for shape in shapes],
        scratch_shapes=[pltpu.VMEM(s.shape, BF16) for s in shards] + [pltpu.SemaphoreType.DMA((n,))],
        compiler_params=pltpu.CompilerParams(vmem_limit_bytes=32 * MIB),
    )(*shards)


ALL_PEERS = tuple(range(1, N_DEV))
FIRST_HOP = (1, 2, 4, 6)
OTHER_CHIPS = (2, 4, 6)


class _Plan:
    def __init__(self, per_array, copies):
        self.per_array, self.copies = per_array, copies


def _own_slot_first_hop(slots):
    def copies(x, y, c, ins, lands):
        me = _index(x, y, c)
        return [(slots[a](lands[a], me), slots[a](lands[a], me), _peer(x, y, c, rel)) for rel in FIRST_HOP for a in range(len(lands))]
    return _Plan(len(FIRST_HOP), copies)


def _landed_to_sibling(slots):
    def copies(x, y, c, ins, lands):
        blocks = [_index(*_peer(x, y, c, rel)) for rel in OTHER_CHIPS]
        return [(slots[a](lands[a], b), slots[a](lands[a], b), (x, y, 1 - c)) for b in blocks for a in range(len(lands))]
    return _Plan(len(OTHER_CHIPS), copies)


def _whole_to_all_copies(x, y, c, ins, lands):
    return [(ins[a], lands[a].at[rel - 1], _peer(x, y, c, rel)) for rel in ALL_PEERS for a in range(len(lands))]


def _block_to_owner_copies(x, y, c, ins, lands):
    return [(ins[a].at[_index(*_peer(x, y, c, rel))], lands[a].at[rel - 1], _peer(x, y, c, rel))
            for rel in ALL_PEERS for a in range(len(lands))]


_whole_to_all = _Plan(len(ALL_PEERS), _whole_to_all_copies)
_block_to_owner = _Plan(len(ALL_PEERS), _block_to_owner_copies)


def _split_copies(plan, ins, lands, send_sems, recv_sems):
    return [pltpu.make_async_remote_copy(src_ref=src, dst_ref=dst, send_sem=send_sems.at[k], recv_sem=recv_sems.at[k],
                                         device_id=to, device_id_type=MESH)
            for k, (src, dst, to) in enumerate(plan.copies(*_place(), ins, lands))]


HBM_SPEC = pl.BlockSpec(memory_space=pltpu.HBM)
SEM_SPEC = pl.BlockSpec(memory_space=pltpu.SEMAPHORE)
EFFECT = pltpu.SideEffectType.DATAFLOW_SIDE_EFFECTING


def _start_copies(name, plan, ins, lands, after):
    n_in, n = len(ins), len(ins) + len(lands)

    def body(*refs):
        send_sems, recv_sems = refs[n + 1], refs[n + 2]
        for cp in _split_copies(plan, refs[:n_in], refs[n_in:n], send_sems, recv_sems):
            cp.start()
        refs[-1][...] = jnp.zeros_like(refs[-1])

    arrays = [pltpu.with_memory_space_constraint(v, pltpu.HBM) for v in (*ins, *lands)]
    sems = pltpu.SemaphoreType.DMA((len(lands) * plan.per_array,))
    send_sems, recv_sems, *flying, token = pl.pallas_call(
        body, name=name,
        out_shape=(sems, sems, *[pltpu.HBM(v.shape, v.dtype) for v in arrays], jax.ShapeDtypeStruct((8, LANES), F32)),
        in_specs=[HBM_SPEC] * n + [pl.BlockSpec(memory_space=pl.ANY)],
        out_specs=(SEM_SPEC, SEM_SPEC, *[HBM_SPEC] * n, pl.BlockSpec(memory_space=pltpu.VMEM)),
        input_output_aliases={i: 2 + i for i in range(n)},
        compiler_params=pltpu.CompilerParams(has_side_effects=EFFECT),
    )(*arrays, after)
    return send_sems, recv_sems, flying, token


def _wait_copies(name, plan, n_in, send_sems, recv_sems, flying, after):
    n = len(flying)

    def body(*refs):
        for cp in _split_copies(plan, refs[:n_in], refs[n_in:n], refs[n], refs[n + 1]):
            cp.wait_send()
            cp.wait_recv()

    landed = pl.pallas_call(
        body, name=name, out_shape=tuple(pltpu.HBM(v.shape, v.dtype) for v in flying),
        in_specs=[HBM_SPEC] * n + [SEM_SPEC, SEM_SPEC, pl.BlockSpec(memory_space=pl.ANY)], out_specs=tuple([HBM_SPEC] * n),
        input_output_aliases={i: i for i in range(n)},
        compiler_params=pltpu.CompilerParams(has_side_effects=EFFECT),
    )(*flying, send_sems, recv_sems, after)
    return landed[:n_in], landed[n_in:]


def _adamw_math(w, g, m, v):
    m = ADAM_B1 * m + (1.0 - ADAM_B1) * g
    v = ADAM_B2 * v + (1.0 - ADAM_B2) * (g * g)
    m_hat = m / (1.0 - ADAM_B1 ** ADAM_STEP)
    v_hat = v / (1.0 - ADAM_B2 ** ADAM_STEP)
    return -ADAM_LR * (m_hat / (jnp.sqrt(v_hat) + ADAM_EPS) + ADAM_WD * w), m, v


def _adamw_sharded(me, own, received, w, m, v, tr):
    rows, cols = w.shape
    own, received, w, m, v = _in_hbm(own, received, w, m, v)

    def body(me_ref, own_ref, rec_ref, w_ref, m_ref, v_ref, g_ref, d_ref, nm_ref, nv_ref):
        g = own_ref[...]
        for r in range(N_DEV - 1):
            g = g + rec_ref[r].astype(F32)
        g_ref[...] = g
        d_ref[...], nm_ref[...], nv_ref[...] = _adamw_math(w_ref[...], g, m_ref[...], v_ref[...])

    tile = pl.BlockSpec((tr, cols), lambda i, me_ref: (i, 0))
    shape = pltpu.HBM((rows, cols), F32)
    return pl.pallas_call(
        body, name="adamw_sharded",
        grid_spec=pltpu.PrefetchScalarGridSpec(
            num_scalar_prefetch=1, grid=(rows // tr,),
            in_specs=[pl.BlockSpec((None, tr, cols), lambda i, me_ref: (me_ref[0], i, 0)),
                      pl.BlockSpec((N_DEV - 1, tr, cols), lambda i, me_ref: (0, i, 0)), tile, tile, tile],
            out_specs=[tile, tile, tile, tile]),
        out_shape=[shape, shape, shape, shape],
        compiler_params=_cparams(("parallel",), 40),
    )(me, own, received, w, m, v)


def _sum_blocks(me, own, received):
    shape = own.shape[1:]
    own, received = _in_hbm(own, received)

    def body(me_ref, own_ref, rec_ref, g_ref):
        g = own_ref[...]
        for r in range(N_DEV - 1):
            g = g + rec_ref[r].astype(F32)
        g_ref[...] = g

    return pl.pallas_call(
        body, name="sum_blocks",
        grid_spec=pltpu.PrefetchScalarGridSpec(
            num_scalar_prefetch=1, grid=(1,),
            in_specs=[pl.BlockSpec((None,) + shape, lambda i, me_ref: (me_ref[0], 0, 0)),
                      pl.BlockSpec((N_DEV - 1,) + shape, lambda i, me_ref: (0, 0, 0))],
            out_specs=pl.BlockSpec(shape, lambda i, me_ref: (0, 0))),
        out_shape=pltpu.HBM(shape, F32),
        compiler_params=_cparams(("arbitrary",), 40),
    )(me, own, received)


def _adamw_whole(g, w, m, v):
    def body(g_ref, w_ref, m_ref, v_ref, d_ref, nm_ref, nv_ref):
        d_ref[...], nm_ref[...], nv_ref[...] = _adamw_math(w_ref[...], g_ref[...], m_ref[...], v_ref[...])

    shape = jax.ShapeDtypeStruct(w.shape, F32)
    return pl.pallas_call(body, name="adamw_whole", out_shape=[shape, shape, shape])(g, w, m, v)


VECTOR_ROWS = D_MODEL // LANES
POOL_ROWS = len(POOL_WINDOWS) * POOL_GROUP_DIM


def _pack_small(dg1, dg2, dg3, dps, dsink, loss_cols, dwp):
    def body(g1_ref, g2_ref, g3_ref, ps_ref, sink_ref, loss_ref, wp_ref, o_ref):
        o_ref[...] = jnp.zeros_like(o_ref)
        for base, ref, n in ((ROW_G1, g1_ref, VECTOR_ROWS), (ROW_G2, g2_ref, VECTOR_ROWS), (ROW_G3, g3_ref, VECTOR_ROWS),
                             (ROW_LOSS, loss_ref, VECTOR_ROWS), (ROW_PS, ps_ref, POOL_WIDTH // LANES)):
            for r in range(n):
                o_ref[base + r:base + r + 1, :] = ref[:, r * LANES:(r + 1) * LANES]
        heads = sink_ref[:, 0, :]
        on_diagonal = lax.broadcasted_iota(jnp.int32, heads.shape, 0) == lax.broadcasted_iota(jnp.int32, heads.shape, 1)
        o_ref[ROW_SINK:ROW_SINK + 1, :] = jnp.sum(jnp.where(on_diagonal, heads, 0.0), axis=0, keepdims=True)
        o_ref[ROW_WP:ROW_WP + POOL_ROWS, :] = wp_ref[...].reshape(POOL_ROWS, LANES)

    return pl.pallas_call(body, name="pack_small", out_shape=jax.ShapeDtypeStruct((SMALL_ROWS, LANES), F32))(
        dg1, dg2, dg3, dps, dsink, loss_cols, dwp)


def _finish_small(me, own, landed, params):
    flat = [a for group in params for a in group]

    def body(me_ref, own_ref, landed_ref, *refs):
        ins, outs = refs[:len(flat)], refs[len(flat):]
        total = None
        for source in range(N_DEV):
            rel = jnp.bitwise_xor(me_ref[0], source)
            piece = jnp.where(rel == 0, own_ref[...], landed_ref[jnp.maximum(rel, 1) - 1])
            total = piece if total is None else total + piece
        outs[0][...] = (0.5 / D_MODEL) * jnp.sum(jnp.sum(total[ROW_LOSS:ROW_LOSS + VECTOR_ROWS], axis=1, keepdims=True), axis=0, keepdims=True)
        row = lambda base, n: jnp.concatenate([total[base + r:base + r + 1, :] for r in range(n)], axis=1)
        grads = [row(ROW_G1, VECTOR_ROWS), row(ROW_G2, VECTOR_ROWS), row(ROW_G3, VECTOR_ROWS), row(ROW_PS, POOL_WIDTH // LANES),
                 total[ROW_SINK:ROW_SINK + 1, :N_Q_HEADS], total[ROW_WP:ROW_WP + POOL_ROWS].reshape(params[5][0].shape)]
        for k, g in enumerate(grads):
            w_ref, m_ref, v_ref = ins[3 * k:3 * k + 3]
            g_out, d_out, m_out, v_out = outs[1 + 4 * k:5 + 4 * k]
            g_out[...] = g
            d_out[...], m_out[...], v_out[...] = _adamw_math(w_ref[...], g, m_ref[...], v_ref[...])

    shapes = [jax.ShapeDtypeStruct((1, 1), F32)] + [jax.ShapeDtypeStruct(w.shape, F32) for w, _, _ in params for _ in range(4)]
    vmem = pl.BlockSpec(memory_space=pltpu.VMEM)
    res = pl.pallas_call(body, name="finish_small", in_specs=[pl.BlockSpec(memory_space=pltpu.SMEM)] + [vmem] * (2 + len(flat)),
                         out_shape=shapes)(me, own, landed, *flat)
    return res[0], [res[1 + 4 * k:5 + 4 * k] for k in range(len(params))]


def _local_step(x, target, attn_norm_g, attn_sinks, w_pool, pool_scale, mlp_norm_g, final_norm_g,
                front_token, first_weight, second_hop, later_weights, ship_down, ship_up, ship_in):
    n_seq, seq, _ = x.shape
    rows = n_seq * seq
    tm, tm_mlp, tm_norm = min(1024, seq), min(256, seq), min(2048, seq)
    tm_grad = tm
    x2d, t2d = x.reshape(rows, D_MODEL), target.reshape(rows, D_MODEL)
    g3 = final_norm_g.reshape(1, D_MODEL)
    cos, sin = _rope_tables(seq)
    wp_b = w_pool[0].astype(BF16)

    sink_rows = jnp.broadcast_to(attn_sinks.reshape(N_Q_HEADS, 1, 1), (N_Q_HEADS, 1, LANES))
    cos, sin = _in_hbm(cos, sin)
    h1 = _in_hbm(_first_norm(x2d, attn_norm_g, front_token, tm_norm))
    w_in_full = _in_hbm(first_weight(h1))
    q, kd, vd, u = _in_hbm(*_inproj(h1, w_in_full, cos, sin, seq, tm))
    attn = _in_hbm(_attention_forward(sink_rows, q, kd, vd, n_seq, seq))
    pool = _in_hbm(_pool_forward(u, wp_b, pool_scale, second_hop(attn), n_seq, seq))
    w_out_full, w_up_full, w_down_full = _in_hbm(*later_weights(pool))
    x2, h2 = _in_hbm(*_outproj_norm(x2d, attn, pool, w_out_full, mlp_norm_g, tm))
    slope, f, dx3, dx3b, loss_cols, dg3 = _mlp_forward_loss(h2, x2, w_up_full, w_down_full, g3, t2d, tm_mlp)
    slope, f, dx3, dx3b = _in_hbm(slope, f, dx3, dx3b)

    down_token = ship_down(_weight_gradient("down_gradient", f, dx3b, True, tm_grad))
    da, dx2, dattn, dpool, dg2, d_w_out, d_w_out_wire = _mlp_backward_data(
        dx3b, slope, w_down_full, w_up_full, dx3, x2, mlp_norm_g, w_out_full, attn, pool, down_token, tm_mlp)
    da, dx2, dattn, dpool = _in_hbm(da, dx2, dattn, dpool)
    up_token = ship_up((d_w_out, d_w_out_wire), _weight_gradient("up_gradient", h2, da, False, tm_grad))
    dq, dk, dv, dsink = _attention_backward(sink_rows, q, kd, vd, dattn, cos, sin, up_token, n_seq, seq)
    dq, dk, dv = _in_hbm(dq, dk, dv)
    du, d_w_pool, d_pool_scale = _pool_backward(u, dpool, wp_b, pool_scale, n_seq, seq)
    du = _in_hbm(du)
    in_token = ship_in(_inproj_gradient(dq, dk, dv, du, h1, tm_grad))
    grad_x, dg1 = _inproj_backward(dq, dk, dv, du, w_in_full, x2d, dx2, attn_norm_g, in_token, tm)
    return grad_x.reshape(x.shape), _pack_small(dg1, dg2, dg3, d_pool_scale, dsink, loss_cols, d_w_pool)


def kernel(x, attn_norm_g, w_in, attn_sinks, w_pool, pool_scale, w_out, mlp_norm_g, w_up, w_down, final_norm_g, loss_target, m_attn_norm_g, m_w_in, m_attn_sinks, m_w_pool, m_pool_scale, m_w_out, m_mlp_norm_g, m_w_up, m_w_down, m_final_norm_g, v_attn_norm_g, v_w_in, v_attn_sinks, v_w_pool, v_pool_scale, v_w_out, v_mlp_norm_g, v_w_up, v_w_down, v_final_norm_g):
    me = (4 * lax.axis_index("x") + 2 * lax.axis_index("y") + lax.axis_index("c")).astype(jnp.int32).reshape(1)

    unordered = jnp.zeros((8, LANES), F32)

    win_land, wout_land, wup_land, wdown_land = _stage_weights(
        [w_in[0].T, w_out[0], w_up[0], w_down[0]], [_row_slot, _row_slot, _column_slot, _row_slot],
        [(N_DEV, IN_BLOCK, D_MODEL), (N_DEV, OUT_BLOCK, D_MODEL), (D_MODEL, D_FF), (N_DEV, FF_BLOCK, D_MODEL)])
    in_slots, later_slots = [_row_slot], [_row_slot, _column_slot, _row_slot]
    in_copies = _start_copies("spread_in_start", _own_slot_first_hop(in_slots), [], [win_land], unordered)
    later_copies = _start_copies("spread_later_start", _own_slot_first_hop(later_slots), [], [wout_land, wup_land, wdown_land], in_copies[3])

    def first_weight(after):
        _, in_landed = _wait_copies("spread_in_wait", _own_slot_first_hop(in_slots), 0, *in_copies[:3], after)
        in_passed = _start_copies("pass_in_start", _landed_to_sibling(in_slots), [], in_landed, unordered)
        _, (win_g,) = _wait_copies("pass_in_wait", _landed_to_sibling(in_slots), 0, *in_passed[:3], in_passed[3])
        return win_g.reshape(IN_WIDTH, D_MODEL)

    passing = []

    def second_hop(after):
        _, later_landed = _wait_copies("spread_later_wait", _own_slot_first_hop(later_slots), 0, *later_copies[:3], after)
        passing.extend(_start_copies("pass_later_start", _landed_to_sibling(later_slots), [], later_landed, unordered))
        return passing[3]

    def later_weights(after):
        _, (wout_g, wup_g, wdown_g) = _wait_copies("pass_later_wait", _landed_to_sibling(later_slots), 0, *passing[:3], after)
        return wout_g.reshape(D_MODEL, D_MODEL), wup_g, wdown_g.reshape(D_FF, D_MODEL)

    deliveries, kept = {}, {}

    def deliver(name, plan, wires):
        lands = [lax.empty((N_DEV - 1,) + (g.shape[1:] if plan is _block_to_owner else g.shape), g.dtype) for g in wires]
        deliveries[name] = _start_copies(name + "_start", plan, wires, lands, unordered)
        return deliveries[name][3]

    def landed(name, plan, after):
        send, recv, flying, _ = deliveries[name]
        return _wait_copies(name + "_wait", plan, len(flying) // 2, send, recv, flying, after)

    def ship_down(d_w_down):
        kept["down"] = d_w_down[0]
        return deliver("deliver_down", _block_to_owner, [d_w_down[1]])

    def ship_up(d_w_out, d_w_up):
        kept["out"], kept["up"] = d_w_out[0].reshape(N_DEV, OUT_BLOCK, D_MODEL), d_w_up[0]
        return deliver("deliver_up", _block_to_owner, [d_w_out[1].reshape(N_DEV, OUT_BLOCK, D_MODEL), d_w_up[1]])

    def ship_in(d_w_in):
        kept["in"] = d_w_in[0].reshape(N_DEV, IN_BLOCK, D_MODEL)
        return deliver("deliver_in", _block_to_owner, [d_w_in[1].reshape(N_DEV, IN_BLOCK, D_MODEL)])

    grad_x, small = _local_step(x, loss_target, attn_norm_g, attn_sinks, w_pool, pool_scale, mlp_norm_g, final_norm_g,
                                later_copies[3], first_weight, second_hop, later_weights, ship_down, ship_up, ship_in)
    small_token = deliver("deliver_small", _whole_to_all, [small])

    _, (got_down,) = landed("deliver_down", _block_to_owner, small_token)
    _, (got_out, got_up) = landed("deliver_up", _block_to_owner, small_token)
    g_down = _adamw_sharded(me, kept["down"], got_down, w_down[0], m_w_down[0], v_w_down[0], 256)
    g_up = _adamw_sharded(me, kept["up"], got_up, w_up[0], m_w_up[0], v_w_up[0], 256)
    g_out = _adamw_sharded(me, kept["out"], got_out, w_out[0], m_w_out[0], v_w_out[0], 128)
    _, (got_in,) = landed("deliver_in", _block_to_owner, g_out[0])
    grad_in = _sum_blocks(me, kept["in"], got_in).T
    g_in = [grad_in, *_adamw_whole(grad_in, w_in[0], m_w_in[0], v_w_in[0])]
    (own_small,), (got_small,) = landed("deliver_small", _whole_to_all, g_in[1])

    row = lambda a: a.reshape(1, D_MODEL)
    params = [(attn_norm_g, m_attn_norm_g, v_attn_norm_g), (mlp_norm_g, m_mlp_norm_g, v_mlp_norm_g),
              (row(final_norm_g), row(m_final_norm_g), row(v_final_norm_g)), (pool_scale, m_pool_scale, v_pool_scale),
              (attn_sinks, m_attn_sinks, v_attn_sinks), (w_pool[0], m_w_pool[0], v_w_pool[0])]
    loss, (s_norm1, s_norm2, s_norm3, s_scale, s_sinks, s_pool) = _finish_small(me, own_small, got_small, params)
    s_norm3 = [a.reshape(D_MODEL) for a in s_norm3]
    s_pool = [a[None] for a in s_pool]

    def ordered(k):
        return [s_norm1[k], g_in[k][None], s_sinks[k], s_pool[k], s_scale[k], g_out[k][None], s_norm2[k], g_up[k][None], g_down[k][None],
                s_norm3[k]]

    return (loss.reshape(()), grad_x, *ordered(0), *ordered(1), *ordered(2), *ordered(3))
```

```python
import jax
import jax.numpy as jnp
from jax import lax
from jax.experimental import pallas as pl
from jax.experimental.pallas import tpu as pltpu

F32 = jnp.float32
BF16 = jnp.bfloat16

D_MODEL = 1024
HEAD_DIM = 64
N_Q_HEADS = 8
Q_PER_KV = 4
ATTN_WIDTH = 512
KV_WIDTH = 128
BLOCK = 128
ROPE_THETA = 10000.0
POOL_WINDOWS = (2, 4, 8, 16)
POOL_WIDTH = 512
POOL_GROUP_DIM = 128
IN_WIDTH = 1280
D_FF = 4096
EPS = 1e-6
N_DEV = 8
FF_BLOCK = D_FF // N_DEV
IN_BLOCK = IN_WIDTH // N_DEV
OUT_BLOCK = D_MODEL // N_DEV
ADAM_LR = 0.001
ADAM_B1 = 0.9
ADAM_B2 = 0.999
ADAM_EPS = 1e-08
ADAM_WD = 0.01
ADAM_STEP = 10
NEG = -1e30
FORWARD_CHAINS = 8
BACKWARD_CHAINS = 2
LANES = 128
MIB = 1024 * 1024
MESH = pl.DeviceIdType.MESH

ROW_G1, ROW_G2, ROW_G3, ROW_PS, ROW_SINK, ROW_LOSS, ROW_WP, SMALL_ROWS = 0, 8, 16, 24, 32, 40, 48, 560


def _cparams(semantics, vmem_mib):
    return pltpu.CompilerParams(dimension_semantics=semantics, vmem_limit_bytes=vmem_mib * MIB)


def _in_hbm(*arrays):
    pinned = tuple(pltpu.with_memory_space_constraint(a, pltpu.HBM) for a in arrays)
    return pinned[0] if len(pinned) == 1 else pinned


def _dot(a, b):
    return jnp.dot(a, b, preferred_element_type=F32)


def _dot_nt(a, b):
    return lax.dot_general(a, b, (((1,), (1,)), ((), ())), preferred_element_type=F32)


def _dot_tn(a, b):
    return lax.dot_general(a, b, (((0,), (0,)), ((), ())), preferred_element_type=F32)


def _swap_halves(x):
    width = x.shape[1]
    lane = lax.broadcasted_iota(jnp.int32, x.shape, 1)
    ahead = pltpu.roll(x, width - HEAD_DIM // 2, 1)
    behind = pltpu.roll(x, HEAD_DIM // 2, 1)
    return jnp.where(lane % HEAD_DIM < HEAD_DIM // 2, ahead, behind)


def _rope(x, cos, sin):
    reps = x.shape[1] // LANES
    if reps > 1:
        cos = jnp.tile(cos, (1, reps))
        sin = jnp.tile(sin, (1, reps))
    return x * cos + _swap_halves(x) * sin


def _rope_tables(seq):
    half = HEAD_DIM // 2
    inv_freq = ROPE_THETA ** (-jnp.arange(half, dtype=F32) / half)
    ang = jnp.arange(seq).astype(F32)[:, None] * inv_freq[None, :]
    cos, sin = jnp.cos(ang), jnp.sin(ang)
    cos = jnp.tile(cos, (1, LANES // half))
    sin = jnp.tile(jnp.concatenate([-sin, sin], axis=1), (1, LANES // HEAD_DIM))
    return cos, sin


def _both_halves(x):
    lane = lax.broadcasted_iota(jnp.int32, x.shape, 1)
    other = pltpu.roll(x, HEAD_DIM, 1)
    low = lane < HEAD_DIM
    return jnp.where(low, x, other), jnp.where(low, other, x)


def _rms_backward(dh, xin, gain):
    r = lax.rsqrt(jnp.mean(xin * xin, axis=-1, keepdims=True) + EPS)
    xhat = xin * r
    dxhat = dh * gain
    dx = r * (dxhat - xhat * jnp.mean(dxhat * xhat, axis=-1, keepdims=True))
    return dx, jnp.sum(dh * xhat, axis=0, keepdims=True)


def _first_norm(x2d, gain, token, tm):
    rows = x2d.shape[0]

    def body(x_ref, g_ref, token_ref, h_ref):
        x = x_ref[...]
        r = lax.rsqrt(jnp.mean(x * x, axis=-1, keepdims=True) + EPS)
        h_ref[...] = (x * r * g_ref[...]).astype(BF16)

    row = pl.BlockSpec((tm, D_MODEL), lambda i: (i, 0))
    return pl.pallas_call(
        body, name="first_norm", grid=(rows // tm,),
        in_specs=[row, pl.BlockSpec((1, D_MODEL), lambda i: (0, 0)), pl.BlockSpec((8, LANES), lambda i: (0, 0))],
        out_specs=row, out_shape=pltpu.HBM((rows, D_MODEL), BF16),
        compiler_params=_cparams(("parallel",), 40),
    )(x2d, gain, token)


def _inproj(h1, w_in, cos, sin, seq, tm):
    rows = h1.shape[0]
    tiles_per_seq = seq // tm

    def body(h_ref, w_ref, cos_ref, sin_ref, q_ref, k_ref, v_ref, u_ref):
        proj = _dot_nt(h_ref[...], w_ref[...])
        cos_t, sin_t = cos_ref[...], sin_ref[...]
        q = _rope(proj[:, :ATTN_WIDTH], cos_t, sin_t) * (HEAD_DIM ** -0.5)
        q_ref[...] = q.astype(BF16)
        k = _rope(proj[:, ATTN_WIDTH:ATTN_WIDTH + KV_WIDTH], cos_t, sin_t)
        k0, k1 = _both_halves(k)
        k_ref[...] = jnp.concatenate([k0, k1], axis=1).astype(BF16)
        v0, v1 = _both_halves(proj[:, ATTN_WIDTH + KV_WIDTH:ATTN_WIDTH + 2 * KV_WIDTH])
        v_ref[...] = jnp.concatenate([v0, v1], axis=1).astype(BF16)
        u_ref[...] = proj[:, ATTN_WIDTH + 2 * KV_WIDTH:]

    row = lambda width: pl.BlockSpec((tm, width), lambda i: (i, 0))
    table = pl.BlockSpec((tm, LANES), lambda i: (i % tiles_per_seq, 0))
    return pl.pallas_call(
        body, name="inproj", grid=(rows // tm,),
        in_specs=[row(D_MODEL), pl.BlockSpec((IN_WIDTH, D_MODEL), lambda i: (0, 0)), table, table],
        out_specs=[row(ATTN_WIDTH), row(2 * KV_WIDTH), row(2 * KV_WIDTH), row(POOL_WIDTH)],
        out_shape=[pltpu.HBM((rows, ATTN_WIDTH), BF16), pltpu.HBM((rows, 2 * KV_WIDTH), BF16),
                   pltpu.HBM((rows, 2 * KV_WIDTH), BF16), pltpu.HBM((rows, POOL_WIDTH), F32)],
        compiler_params=_cparams(("parallel",), 40),
    )(h1, w_in, cos, sin)


def _window_masks(n):
    qi = lax.broadcasted_iota(jnp.int32, (BLOCK, BLOCK), 0)
    kj = lax.broadcasted_iota(jnp.int32, (BLOCK, BLOCK), 1)
    return kj <= qi, jnp.logical_and(kj > qi, n > 0)


def _window_operand(ref, r0, p0, kv):
    low = lax.broadcasted_iota(jnp.int32, (BLOCK, LANES), 1) < HEAD_DIM
    cur = ref[pl.ds(r0, BLOCK), kv * LANES:(kv + 1) * LANES]
    prev = ref[pl.ds(p0, BLOCK), kv * LANES:(kv + 1) * LANES]
    zero = jnp.zeros_like(cur)
    return jnp.concatenate([jnp.where(low, cur, zero), jnp.where(low, zero, cur), jnp.where(low, prev, zero), jnp.where(low, zero, prev)], axis=0)


def _pair_rows(ref, r0, kv):
    return jnp.concatenate([ref[pl.ds(r0, BLOCK), (2 * kv + j) * LANES:(2 * kv + j + 1) * LANES] for j in range(2)], axis=0)


def _merged_window(wide, parity, cur_mask, prev_mask, fill):
    cur = wide[:, parity * LANES:(parity + 1) * LANES]
    prev = wide[:, (2 + parity) * LANES:(3 + parity) * LANES]
    return jnp.where(cur_mask, cur, jnp.where(prev_mask, prev, fill))


def _lane_sums(x, one_matmul):
    flat = x.reshape(-1, x.shape[-1])
    high = flat.astype(BF16)
    low = (flat - high.astype(F32)).astype(BF16)
    if one_matmul:
        sums = _dot(jnp.concatenate([high, low], axis=1), jnp.ones((2 * x.shape[-1], LANES), BF16))
    else:
        ones = jnp.ones((x.shape[-1], LANES), BF16)
        sums = _dot(high, ones) + _dot(low, ones)
    return sums.reshape(x.shape[:-1] + (LANES,))


def _softmax_with_sink(scores, sink, one_matmul):
    m = jnp.broadcast_to(jnp.maximum(jnp.max(scores, axis=-1, keepdims=True), sink), scores.shape)
    p, ps = jnp.exp(scores - m), jnp.exp(sink - m)
    inv = 1.0 / (_lane_sums(p, one_matmul) + ps)
    return p * inv, ps * inv


def _attention_forward(sinks, q, kd, vd, n_seq, seq):
    n_blocks = seq // BLOCK
    n_pairs = N_Q_HEADS // 2
    chains = min(FORWARD_CHAINS, n_blocks)

    def body(sink_ref, q_ref, k_ref, v_ref, o_ref, s_ref, p_ref):
        def step(i, carry):
            starts, values = [], []
            for u in range(chains):
                n = i * chains + u
                r0 = pl.multiple_of(n * BLOCK, BLOCK)
                p0 = pl.multiple_of(jnp.maximum(n - 1, 0) * BLOCK, BLOCK)
                cur_mask, prev_mask = _window_masks(n)
                keys = [_window_operand(k_ref, r0, p0, kv) for kv in range(2)]
                starts.append(r0)
                values.append([_window_operand(v_ref, r0, p0, kv) for kv in range(2)])
                for kv in range(2):
                    both = _dot_nt(_pair_rows(q_ref, r0, kv), keys[kv])
                    for j in range(2):
                        wide = both[j * BLOCK:(j + 1) * BLOCK]
                        for parity in range(2):
                            s_ref[u * N_Q_HEADS + 4 * kv + 2 * j + parity] = _merged_window(wide, parity, cur_mask, prev_mask, NEG)
            probs, _ = _softmax_with_sink(s_ref[...], jnp.tile(sink_ref[:, :, 0:1], (chains, 1, 1)), False)
            probs = probs.astype(BF16)
            zero = jnp.zeros((BLOCK, BLOCK), BF16)
            for u in range(chains):
                for pair in range(n_pairs):
                    for parity in range(2):
                        ph = probs[u * N_Q_HEADS + 2 * pair + parity]
                        p_ref[u, pair, :, parity * LANES:(parity + 1) * LANES] = jnp.where(cur_mask, ph, zero)
                        p_ref[u, pair, :, (2 + parity) * LANES:(3 + parity) * LANES] = jnp.where(cur_mask, zero, ph)
            for u in range(chains):
                for kv in range(2):
                    both = _dot(p_ref[u, 2 * kv:2 * kv + 2].reshape(2 * BLOCK, 4 * LANES), values[u][kv])
                    for j in range(2):
                        pair = 2 * kv + j
                        o_ref[pl.ds(starts[u], BLOCK), pair * LANES:(pair + 1) * LANES] = both[j * BLOCK:(j + 1) * BLOCK].astype(BF16)
            return carry

        lax.fori_loop(0, n_blocks // chains, step, 0)

    seq_block = lambda width: pl.BlockSpec((seq, width), lambda b: (b, 0))
    return pl.pallas_call(
        body, name="attention_forward", grid=(n_seq,),
        in_specs=[pl.BlockSpec((N_Q_HEADS, 1, LANES), lambda b: (0, 0, 0)), seq_block(ATTN_WIDTH), seq_block(2 * KV_WIDTH),
                  seq_block(2 * KV_WIDTH)],
        out_specs=seq_block(ATTN_WIDTH),
        out_shape=pltpu.HBM((n_seq * seq, ATTN_WIDTH), BF16),
        scratch_shapes=[pltpu.VMEM((chains * N_Q_HEADS, BLOCK, BLOCK), F32), pltpu.VMEM((chains, n_pairs, BLOCK, 4 * LANES), BF16)],
        compiler_params=_cparams(("parallel",), 40),
    )(sinks, q, kd, vd)


def _trailing(x, window, t, seq):
    k = 1
    while k < window:
        x = x + jnp.where(t >= k, pltpu.roll(x, k, 0), 0.0)
        k *= 2
    return x


def _leading(x, window, t, seq):
    k = 1
    while k < window:
        x = x + jnp.where(t < seq - k, pltpu.roll(x, seq - k, 0), 0.0)
        k *= 2
    return x


def _pool_features(u_g, window, t, seq):
    count = jnp.minimum(t + 1, window).astype(F32)
    return (_trailing(u_g, window, t, seq) / count - u_g).astype(BF16), count


def _pool_forward(u, w_pool, pool_scale, token, n_seq, seq):
    def body(u_ref, w_ref, s_ref, token_ref, o_ref):
        t = lax.broadcasted_iota(jnp.int32, (seq, 1), 0)
        for g, window in enumerate(POOL_WINDOWS):
            cols = slice(g * POOL_GROUP_DIM, (g + 1) * POOL_GROUP_DIM)
            d, _ = _pool_features(u_ref[:, cols], window, t, seq)
            o_ref[:, cols] = (_dot(d, w_ref[g]) * s_ref[:, cols]).astype(BF16)

    seq_block = pl.BlockSpec((seq, POOL_WIDTH), lambda b: (b, 0))
    return pl.pallas_call(
        body, name="pool_forward", grid=(n_seq,),
        in_specs=[seq_block, pl.BlockSpec((len(POOL_WINDOWS), POOL_GROUP_DIM, POOL_GROUP_DIM), lambda b: (0, 0, 0)),
                  pl.BlockSpec((1, POOL_WIDTH), lambda b: (0, 0)), pl.BlockSpec((8, LANES), lambda b: (0, 0))],
        out_specs=seq_block,
        out_shape=pltpu.HBM((n_seq * seq, POOL_WIDTH), BF16),
        compiler_params=_cparams(("parallel",), 40),
    )(u, w_pool, pool_scale, token)


def _outproj_norm(x2d, attn, pool, w_out, gain, tm):
    rows = x2d.shape[0]

    def body(x_ref, a_ref, p_ref, w_ref, g_ref, x2_ref, h_ref):
        x2 = x_ref[...] + _dot(a_ref[...], w_ref[:ATTN_WIDTH, :]) + _dot(p_ref[...], w_ref[ATTN_WIDTH:, :])
        x2_ref[...] = x2
        r = lax.rsqrt(jnp.mean(x2 * x2, axis=-1, keepdims=True) + EPS)
        h_ref[...] = (x2 * r * g_ref[...]).astype(BF16)

    row = lambda width: pl.BlockSpec((tm, width), lambda i: (i, 0))
    return pl.pallas_call(
        body, name="outproj_norm", grid=(rows // tm,),
        in_specs=[row(D_MODEL), row(ATTN_WIDTH), row(POOL_WIDTH), pl.BlockSpec((D_MODEL, D_MODEL), lambda i: (0, 0)),
                  pl.BlockSpec((1, D_MODEL), lambda i: (0, 0))],
        out_specs=[row(D_MODEL), row(D_MODEL)],
        out_shape=[pltpu.HBM((rows, D_MODEL), F32), pltpu.HBM((rows, D_MODEL), BF16)],
        compiler_params=_cparams(("parallel",), 40),
    )(x2d, attn, pool, w_out, gain)


def _resident(shape):
    return pl.BlockSpec(shape, lambda i: (0,) * len(shape), pipeline_mode=pl.Buffered(1))


def _mlp_forward_loss(h2, x2, w_up, w_down, gain, target, tm):
    rows = h2.shape[0]
    chunk = D_MODEL

    def body(h_ref, x_ref, up_ref, down_ref, g_ref, t_ref, slope_ref, f_ref, dx_ref, dxb_ref, loss_ref, dg_ref):
        @pl.when(pl.program_id(0) == 0)
        def _():
            loss_ref[...] = jnp.zeros_like(loss_ref)
            dg_ref[...] = jnp.zeros_like(dg_ref)

        h = h_ref[...]
        for c in range(D_FF // chunk):
            cols = slice(c * chunk, (c + 1) * chunk)
            r = jnp.maximum(_dot(h, up_ref[:, cols]), 0.0)
            slope_ref[:, cols] = (r + r).astype(BF16)
            f_ref[:, cols] = (r * r).astype(BF16)
        x3 = x_ref[...] + _dot(f_ref[...], down_ref[...])
        rn = lax.rsqrt(jnp.mean(x3 * x3, axis=-1, keepdims=True) + EPS)
        xhat = x3 * rn
        err = xhat * g_ref[...] - t_ref[...]
        loss_ref[...] += jnp.sum(err * err, axis=0, keepdims=True)
        dxhat = err * (g_ref[...] * (1.0 / D_MODEL))
        dx = rn * (dxhat - xhat * jnp.mean(dxhat * xhat, axis=-1, keepdims=True))
        dg_ref[...] += jnp.sum(err * xhat, axis=0, keepdims=True) * (1.0 / D_MODEL)
        dx_ref[...] = dx
        dxb_ref[...] = dx.astype(BF16)

    row = lambda width: pl.BlockSpec((tm, width), lambda i: (i, 0))
    vec = pl.BlockSpec((1, D_MODEL), lambda i: (0, 0))
    return pl.pallas_call(
        body, name="mlp_forward_loss", grid=(rows // tm,),
        in_specs=[row(D_MODEL), row(D_MODEL), _resident((D_MODEL, D_FF)), _resident((D_FF, D_MODEL)), vec, row(D_MODEL)],
        out_specs=[row(D_FF), row(D_FF), row(D_MODEL), row(D_MODEL), vec, vec],
        out_shape=[pltpu.HBM((rows, D_FF), BF16), pltpu.HBM((rows, D_FF), BF16),
                   pltpu.HBM((rows, D_MODEL), F32), pltpu.HBM((rows, D_MODEL), BF16),
                   jax.ShapeDtypeStruct((1, D_MODEL), F32), jax.ShapeDtypeStruct((1, D_MODEL), F32)],
        compiler_params=_cparams(("arbitrary",), 56),
    )(h2, x2, w_up, w_down, gain, target)


def _mlp_backward_data(dx3b, slope, w_down, w_up, dx3, x2, gain, w_out, attn, pool, token, tm):
    rows = dx3b.shape[0]
    steps = rows // tm
    chunk = D_MODEL

    def body(dxb_ref, slope_ref, down_ref, up_ref, dx3_ref, x2_ref, g_ref, wo_ref, attn_ref, pool_ref, token_ref,
             da_ref, dx2_ref, dattn_ref, dpool_ref, dg_ref, dwo_hbm, wire_hbm, dwo_acc, wire, sems):
        @pl.when(pl.program_id(0) == 0)
        def _():
            dg_ref[...] = jnp.zeros_like(dg_ref)
            dwo_acc[...] = jnp.zeros_like(dwo_acc)

        dxb = dxb_ref[...]
        for c in range(D_FF // chunk):
            cols = slice(c * chunk, (c + 1) * chunk)
            da_ref[:, cols] = (_dot_nt(dxb, down_ref[cols, :]) * slope_ref[:, cols].astype(F32)).astype(BF16)
        dnorm, dg = _rms_backward(_dot_nt(da_ref[...], up_ref[...]), x2_ref[...], g_ref[...])
        dg_ref[...] += dg
        dx2 = dx3_ref[...] + dnorm
        dx2_ref[...] = dx2
        dx2b = dx2.astype(BF16)
        dmix = _dot_nt(dx2b, wo_ref[...])
        dattn_ref[...] = dmix[:, :ATTN_WIDTH].astype(BF16)
        dpool_ref[...] = dmix[:, ATTN_WIDTH:]
        dwo_acc[:ATTN_WIDTH, :] += _dot_tn(attn_ref[...], dx2b)
        dwo_acc[ATTN_WIDTH:, :] += _dot_tn(pool_ref[...], dx2b)

        @pl.when(pl.program_id(0) == steps - 1)
        def _():
            done = pltpu.make_async_copy(dwo_acc, dwo_hbm, sems.at[0])
            done.start()
            wire[...] = dwo_acc[...].astype(BF16)
            sent = pltpu.make_async_copy(wire, wire_hbm, sems.at[1])
            sent.start()
            done.wait()
            sent.wait()

    row = lambda width: pl.BlockSpec((tm, width), lambda i: (i, 0))
    vec = pl.BlockSpec((1, D_MODEL), lambda i: (0, 0))
    return pl.pallas_call(
        body, name="mlp_backward_data", grid=(steps,),
        in_specs=[row(D_MODEL), row(D_FF), _resident((D_FF, D_MODEL)), _resident((D_MODEL, D_FF)), row(D_MODEL), row(D_MODEL), vec,
                  _resident((D_MODEL, D_MODEL)), row(ATTN_WIDTH), row(POOL_WIDTH), pl.BlockSpec((8, LANES), lambda i: (0, 0))],
        out_specs=[row(D_FF), row(D_MODEL), row(ATTN_WIDTH), row(POOL_WIDTH), vec, pl.BlockSpec(memory_space=pl.ANY),
                   pl.BlockSpec(memory_space=pl.ANY)],
        out_shape=[pltpu.HBM((rows, D_FF), BF16), pltpu.HBM((rows, D_MODEL), F32),
                   pltpu.HBM((rows, ATTN_WIDTH), BF16), pltpu.HBM((rows, POOL_WIDTH), F32),
                   jax.ShapeDtypeStruct((1, D_MODEL), F32), jax.ShapeDtypeStruct((D_MODEL, D_MODEL), F32),
                   jax.ShapeDtypeStruct((D_MODEL, D_MODEL), BF16)],
        scratch_shapes=[pltpu.VMEM((D_MODEL, D_MODEL), F32), pltpu.VMEM((D_MODEL, D_MODEL), BF16), pltpu.SemaphoreType.DMA((2,))],
        compiler_params=_cparams(("arbitrary",), 56),
    )(dx3b, slope, w_down, w_up, dx3, x2, gain, w_out, attn, pool, token)


def _weight_gradient(name, lhs, rhs, block_lhs, tm):
    rows = lhs.shape[0]
    steps = rows // tm
    out = (N_DEV, FF_BLOCK, rhs.shape[1]) if block_lhs else (N_DEV, lhs.shape[1], FF_BLOCK)

    def body(l_ref, r_ref, o_hbm, wire_hbm, acc, wire, sems):
        @pl.when(pl.program_id(0) == 0)
        def _():
            acc[...] = jnp.zeros_like(acc)

        for d in range(N_DEV):
            cols = slice(d * FF_BLOCK, (d + 1) * FF_BLOCK)
            acc[d] += _dot_tn(l_ref[:, cols], r_ref[...]) if block_lhs else _dot_tn(l_ref[...], r_ref[:, cols])

        @pl.when(pl.program_id(0) == steps - 1)
        def _():
            done = pltpu.make_async_copy(acc, o_hbm, sems.at[0])
            done.start()
            wire[...] = acc[...].astype(BF16)
            sent = pltpu.make_async_copy(wire, wire_hbm, sems.at[1])
            sent.start()
            done.wait()
            sent.wait()

    return pl.pallas_call(
        body, name=name, grid=(steps,),
        in_specs=[pl.BlockSpec((tm, lhs.shape[1]), lambda i: (i, 0)), pl.BlockSpec((tm, rhs.shape[1]), lambda i: (i, 0))],
        out_specs=[pl.BlockSpec(memory_space=pl.ANY), pl.BlockSpec(memory_space=pl.ANY)],
        out_shape=[jax.ShapeDtypeStruct(out, F32), jax.ShapeDtypeStruct(out, BF16)],
        scratch_shapes=[pltpu.VMEM(out, F32), pltpu.VMEM(out, BF16), pltpu.SemaphoreType.DMA((2,))],
        compiler_params=_cparams(("arbitrary",), 60),
    )(lhs, rhs)


def _attention_backward(sinks, q, kd, vd, dout, cos, sin, token, n_seq, seq):
    n_blocks = seq // BLOCK
    n_kv = N_Q_HEADS // Q_PER_KV
    chains = min(BACKWARD_CHAINS, n_blocks)

    def body(sink_ref, q_ref, k_ref, v_ref, do_ref, cos_ref, sin_ref, token_ref, dq_ref, dk_ref, dv_ref, dsink_ref,
             s_all, dp_all, dsc_all, dsp_all, pc_all, pp_all, dk_acc, dv_acc):
        low = lax.broadcasted_iota(jnp.int32, (BLOCK, LANES), 1) < HEAD_DIM

        @pl.when(pl.program_id(0) == 0)
        def _():
            dsink_ref[...] = jnp.zeros_like(dsink_ref)

        def fold(x):
            return x + pltpu.roll(x, HEAD_DIM, 1)

        def onto_keys(cur_ref, prev_ref, kv, other):
            even = _dot_tn(jnp.concatenate([cur_ref[2 * kv], prev_ref[2 * kv]], axis=1), other)
            odd = _dot_tn(jnp.concatenate([cur_ref[2 * kv + 1], prev_ref[2 * kv + 1]], axis=1), other)
            both = fold(jnp.where(jnp.concatenate([low, low], axis=0), even, odd))
            return both[:BLOCK], both[BLOCK:]

        def step(i, dsink):
            blocks = []
            for u in range(chains):
                n = i * chains + u
                r0 = pl.multiple_of(n * BLOCK, BLOCK)
                p0 = pl.multiple_of(jnp.maximum(n - 1, 0) * BLOCK, BLOCK)
                cur_mask, prev_mask = _window_masks(n)
                keys = [_window_operand(k_ref, r0, p0, kv) for kv in range(n_kv)]
                values = [_window_operand(v_ref, r0, p0, kv) for kv in range(n_kv)]
                q_rows = [_pair_rows(q_ref, r0, kv) for kv in range(n_kv)]
                do_rows = [_pair_rows(do_ref, r0, kv) for kv in range(n_kv)]
                for kv in range(n_kv):
                    both_s, both_dp = _dot_nt(q_rows[kv], keys[kv]), _dot_nt(do_rows[kv], values[kv])
                    for j in range(2):
                        wide_s, wide_dp = both_s[j * BLOCK:(j + 1) * BLOCK], both_dp[j * BLOCK:(j + 1) * BLOCK]
                        for parity in range(2):
                            head = u * N_Q_HEADS + 4 * kv + 2 * j + parity
                            s_all[head] = _merged_window(wide_s, parity, cur_mask, prev_mask, NEG)
                            dp_all[head] = _merged_window(wide_dp, parity, cur_mask, prev_mask, 0.0)
                blocks.append((n, r0, p0, keys, q_rows, do_rows))

            probs, p_sink = _softmax_with_sink(s_all[...], jnp.tile(sink_ref[:, :, 0:1], (chains, 1, 1)), True)
            dprobs = dp_all[...]
            delta = _lane_sums(probs * dprobs, True)
            dscores = (probs * (dprobs - delta)).astype(BF16)
            sink_terms = jnp.sum((p_sink * delta)[:, :, 0:1], axis=1, keepdims=True)
            probs = probs.astype(BF16)
            zero = jnp.zeros((BLOCK, BLOCK), BF16)
            for u in range(chains):
                dsink = dsink - sink_terms[u * N_Q_HEADS:(u + 1) * N_Q_HEADS]
                for head in range(N_Q_HEADS):
                    group, rows = 2 * (head // Q_PER_KV) + head % 2, pl.ds(((head % Q_PER_KV) // 2) * BLOCK, BLOCK)
                    ds_h, p_h = dscores[u * N_Q_HEADS + head], probs[u * N_Q_HEADS + head]
                    dsc_all[u, group, rows, :] = jnp.where(cur_mask, ds_h, zero)
                    dsp_all[u, group, rows, :] = jnp.where(cur_mask, zero, ds_h)
                    pc_all[u, group, rows, :] = jnp.where(cur_mask, p_h, zero)
                    pp_all[u, group, rows, :] = jnp.where(cur_mask, zero, p_h)

            for u, (n, r0, p0, keys, q_rows, do_rows) in enumerate(blocks):
                dsc_ref, dsp_ref, pc_ref, pp_ref = dsc_all.at[u], dsp_all.at[u], pc_all.at[u], pp_all.at[u]
                for kv in range(n_kv):
                    wide = jnp.concatenate([dsc_ref[2 * kv], dsc_ref[2 * kv + 1], dsp_ref[2 * kv], dsp_ref[2 * kv + 1]], axis=1)
                    both = _dot(wide, keys[kv]) * (HEAD_DIM ** -0.5)
                    for j in range(2):
                        dq = _rope(both[j * BLOCK:(j + 1) * BLOCK], cos_ref[pl.ds(r0, BLOCK), :], -sin_ref[pl.ds(r0, BLOCK), :])
                        dq_ref[pl.ds(r0, BLOCK), (2 * kv + j) * LANES:(2 * kv + j + 1) * LANES] = dq.astype(BF16)

                parts = []
                for kv in range(n_kv):
                    parts.append(onto_keys(dsc_ref, dsp_ref, kv, q_rows[kv]) + onto_keys(pc_ref, pp_ref, kv, do_rows[kv]))
                dk_acc[pl.ds(r0, BLOCK), :] = jnp.where(low, parts[0][0], parts[1][0])
                dv_acc[pl.ds(r0, BLOCK), :] = jnp.where(low, parts[0][2], parts[1][2])

                @pl.when(n > 0)
                def _():
                    dk_acc[pl.ds(p0, BLOCK), :] += jnp.where(low, parts[0][1], parts[1][1])
                    dv_acc[pl.ds(p0, BLOCK), :] += jnp.where(low, parts[0][3], parts[1][3])

            return dsink

        dsink = lax.fori_loop(0, n_blocks // chains, step, jnp.zeros((N_Q_HEADS, 1, 1), F32))
        dsink_ref[...] += jnp.broadcast_to(dsink, dsink_ref.shape)
        dk_ref[...] = _rope(dk_acc[...], cos_ref[...], -sin_ref[...]).astype(BF16)
        dv_ref[...] = dv_acc[...].astype(BF16)

    seq_block = lambda width: pl.BlockSpec((seq, width), lambda b: (b, 0))
    table = pl.BlockSpec((seq, LANES), lambda b: (0, 0))
    per_head = pl.BlockSpec((N_Q_HEADS, 1, LANES), lambda b: (0, 0, 0))
    per_block = pltpu.VMEM((chains * N_Q_HEADS, BLOCK, BLOCK), F32)
    grouped = pltpu.VMEM((chains, 2 * n_kv, 2 * BLOCK, BLOCK), BF16)
    return pl.pallas_call(
        body, name="attention_backward", grid=(n_seq,),
        in_specs=[per_head, seq_block(ATTN_WIDTH), seq_block(2 * KV_WIDTH), seq_block(2 * KV_WIDTH),
                  seq_block(ATTN_WIDTH), table, table, pl.BlockSpec((8, LANES), lambda b: (0, 0))],
        out_specs=[seq_block(ATTN_WIDTH), seq_block(KV_WIDTH), seq_block(KV_WIDTH), per_head],
        out_shape=[pltpu.HBM((n_seq * seq, ATTN_WIDTH), BF16), pltpu.HBM((n_seq * seq, KV_WIDTH), BF16),
                   pltpu.HBM((n_seq * seq, KV_WIDTH), BF16), jax.ShapeDtypeStruct((N_Q_HEADS, 1, LANES), F32)],
        scratch_shapes=[per_block, per_block, grouped, grouped, grouped, grouped, pltpu.VMEM((seq, KV_WIDTH), F32), pltpu.VMEM((seq, KV_WIDTH), F32)],
        compiler_params=_cparams(("arbitrary",), 40),
    )(sinks, q, kd, vd, dout, cos, sin, token)


def _pool_backward(u, dpool, w_pool, pool_scale, n_seq, seq):
    groups = len(POOL_WINDOWS)

    def body(u_ref, dp_ref, w_ref, s_ref, du_ref, dw_ref, ds_ref):
        @pl.when(pl.program_id(0) == 0)
        def _():
            dw_ref[...] = jnp.zeros_like(dw_ref)
            ds_ref[...] = jnp.zeros_like(ds_ref)

        t = lax.broadcasted_iota(jnp.int32, (seq, 1), 0)
        for g, window in enumerate(POOL_WINDOWS):
            cols = slice(g * POOL_GROUP_DIM, (g + 1) * POOL_GROUP_DIM)
            d, count = _pool_features(u_ref[:, cols], window, t, seq)
            dpool_g = dp_ref[:, cols]
            ds_ref[:, cols] += jnp.sum(dpool_g * _dot(d, w_ref[g]), axis=0, keepdims=True)
            dy = (dpool_g * s_ref[:, cols]).astype(BF16)
            dw_ref[g] += _dot_tn(d, dy)
            dd = _dot_nt(dy, w_ref[g])
            du_ref[:, cols] = (_leading(dd / count, window, t, seq) - dd).astype(BF16)

    seq_block = pl.BlockSpec((seq, POOL_WIDTH), lambda b: (b, 0))
    weights = pl.BlockSpec((groups, POOL_GROUP_DIM, POOL_GROUP_DIM), lambda b: (0, 0, 0))
    scale = pl.BlockSpec((1, POOL_WIDTH), lambda b: (0, 0))
    return pl.pallas_call(
        body, name="pool_backward", grid=(n_seq,),
        in_specs=[seq_block, seq_block, weights, scale],
        out_specs=[seq_block, weights, scale],
        out_shape=[pltpu.HBM((n_seq * seq, POOL_WIDTH), BF16),
                   jax.ShapeDtypeStruct((groups, POOL_GROUP_DIM, POOL_GROUP_DIM), F32), jax.ShapeDtypeStruct((1, POOL_WIDTH), F32)],
        compiler_params=_cparams(("arbitrary",), 40),
    )(u, dpool, w_pool, pool_scale)


def _inproj_gradient(dq, dk, dv, du, h1, tm):
    rows = h1.shape[0]
    steps = rows // tm

    def body(dq_ref, dk_ref, dv_ref, du_ref, h_ref, dw_ref, wire_ref):
        @pl.when(pl.program_id(0) == 0)
        def _():
            dw_ref[...] = jnp.zeros_like(dw_ref)

        dproj = jnp.concatenate([dq_ref[...], dk_ref[...], dv_ref[...], du_ref[...]], axis=1)
        dw_ref[...] += _dot_tn(dproj, h_ref[...])

        @pl.when(pl.program_id(0) == steps - 1)
        def _():
            wire_ref[...] = dw_ref[...].astype(BF16)

    row = lambda width: pl.BlockSpec((tm, width), lambda i: (i, 0))
    whole = pl.BlockSpec((IN_WIDTH, D_MODEL), lambda i: (0, 0))
    return pl.pallas_call(
        body, name="inproj_gradient", grid=(steps,),
        in_specs=[row(ATTN_WIDTH), row(KV_WIDTH), row(KV_WIDTH), row(POOL_WIDTH), row(D_MODEL)],
        out_specs=[whole, whole],
        out_shape=[pltpu.HBM((IN_WIDTH, D_MODEL), F32), pltpu.HBM((IN_WIDTH, D_MODEL), BF16)],
        compiler_params=_cparams(("arbitrary",), 48),
    )(dq, dk, dv, du, h1)


def _inproj_backward(dq, dk, dv, du, w_in, x2d, dx2, gain, token, tm):
    rows = x2d.shape[0]

    def body(dq_ref, dk_ref, dv_ref, du_ref, w_ref, x_ref, dx2_ref, g_ref, token_ref, dx_ref, dg_ref):
        @pl.when(pl.program_id(0) == 0)
        def _():
            dg_ref[...] = jnp.zeros_like(dg_ref)

        dproj = jnp.concatenate([dq_ref[...], dk_ref[...], dv_ref[...], du_ref[...]], axis=1)
        dnorm, dg = _rms_backward(_dot(dproj, w_ref[...]), x_ref[...], g_ref[...])
        dg_ref[...] += dg
        dx_ref[...] = dx2_ref[...] + dnorm

    row = lambda width: pl.BlockSpec((tm, width), lambda i: (i, 0))
    vec = pl.BlockSpec((1, D_MODEL), lambda i: (0, 0))
    return pl.pallas_call(
        body, name="inproj_backward", grid=(rows // tm,),
        in_specs=[row(ATTN_WIDTH), row(KV_WIDTH), row(KV_WIDTH), row(POOL_WIDTH), pl.BlockSpec((IN_WIDTH, D_MODEL), lambda i: (0, 0)),
                  row(D_MODEL), row(D_MODEL), vec, pl.BlockSpec((8, LANES), lambda i: (0, 0))],
        out_specs=[row(D_MODEL), vec],
        out_shape=[pltpu.HBM((rows, D_MODEL), F32), jax.ShapeDtypeStruct((1, D_MODEL), F32)],
        compiler_params=_cparams(("arbitrary",), 60),
    )(dq, dk, dv, du, w_in, x2d, dx2, gain, token)


def _place():
    return lax.axis_index("x"), lax.axis_index("y"), lax.axis_index("c")


def _peer(x, y, c, rel):
    return (1 - x if rel & 4 else x, 1 - y if rel & 2 else y, 1 - c if rel & 1 else c)


def _index(px, py, pc):
    return 4 * px + 2 * py + pc


def _row_slot(ref, d):
    return ref.at[d]


def _column_slot(ref, d):
    return ref.at[:, pl.ds(pl.multiple_of(d * FF_BLOCK, FF_BLOCK), FF_BLOCK)]


def _stage_weights(shards, slots, shapes):
    n = len(shards)

    def body(*refs):
        ins, outs, stage, sems = refs[:n], refs[n:2 * n], refs[2 * n:3 * n], refs[3 * n]
        me = _index(*_place())
        mine = []
        for a in range(n):
            stage[a][...] = ins[a][...].astype(BF16)
            mine.append(pltpu.make_async_copy(stage[a], slots[a](outs[a], me), sems.at[a]))
            mine[-1].start()
        for cp in mine:
            cp.wait()

    return pl.pallas_call(
        body, name="stage_weights",
        in_specs=[pl.BlockSpec(memory_space=pltpu.VMEM)] * n,
        out_specs=[pl.BlockSpec(memory_space=pl.ANY)] * n,
        out_shape=[pltpu.HBM(shape, BF16) for shape in shapes],
        scratch_shapes=[pltpu.VMEM(s.shape, BF16) for s in shards] + [pltpu.SemaphoreType.DMA((n,))],
        compiler_params=pltpu.CompilerParams(vmem_limit_bytes=32 * MIB),
    )(*shards)


ALL_PEERS = tuple(range(1, N_DEV))
FIRST_HOP = (1, 2, 4, 6)
OTHER_CHIPS = (2, 4, 6)


class _Plan:
    def __init__(self, per_array, copies):
        self.per_array, self.copies = per_array, copies


def _own_slot_first_hop(slots):
    def copies(x, y, c, ins, lands):
        me = _index(x, y, c)
        return [(slots[a](lands[a], me), slots[a](lands[a], me), _peer(x, y, c, rel)) for rel in FIRST_HOP for a in range(len(lands))]
    return _Plan(len(FIRST_HOP), copies)


def _landed_to_sibling(slots):
    def copies(x, y, c, ins, lands):
        blocks = [_index(*_peer(x, y, c, rel)) for rel in OTHER_CHIPS]
        return [(slots[a](lands[a], b), slots[a](lands[a], b), (x, y, 1 - c)) for b in blocks for a in range(len(lands))]
    return _Plan(len(OTHER_CHIPS), copies)


def _whole_to_all_copies(x, y, c, ins, lands):
    return [(ins[a], lands[a].at[rel - 1], _peer(x, y, c, rel)) for rel in ALL_PEERS for a in range(len(lands))]


def _block_to_owner_copies(x, y, c, ins, lands):
    return [(ins[a].at[_index(*_peer(x, y, c, rel))], lands[a].at[rel - 1], _peer(x, y, c, rel))
            for rel in ALL_PEERS for a in range(len(lands))]


_whole_to_all = _Plan(len(ALL_PEERS), _whole_to_all_copies)
_block_to_owner = _Plan(len(ALL_PEERS), _block_to_owner_copies)


def _split_copies(plan, ins, lands, send_sems, recv_sems):
    return [pltpu.make_async_remote_copy(src_ref=src, dst_ref=dst, send_sem=send_sems.at[k], recv_sem=recv_sems.at[k],
                                         device_id=to, device_id_type=MESH)
            for k, (src, dst, to) in enumerate(plan.copies(*_place(), ins, lands))]


HBM_SPEC = pl.BlockSpec(memory_space=pltpu.HBM)
SEM_SPEC = pl.BlockSpec(memory_space=pltpu.SEMAPHORE)
EFFECT = pltpu.SideEffectType.DATAFLOW_SIDE_EFFECTING


def _start_copies(name, plan, ins, lands, after):
    n_in, n = len(ins), len(ins) + len(lands)

    def body(*refs):
        send_sems, recv_sems = refs[n + 1], refs[n + 2]
        for cp in _split_copies(plan, refs[:n_in], refs[n_in:n], send_sems, recv_sems):
            cp.start()
        refs[-1][...] = jnp.zeros_like(refs[-1])

    arrays = [pltpu.with_memory_space_constraint(v, pltpu.HBM) for v in (*ins, *lands)]
    sems = pltpu.SemaphoreType.DMA((len(lands) * plan.per_array,))
    send_sems, recv_sems, *flying, token = pl.pallas_call(
        body, name=name,
        out_shape=(sems, sems, *[pltpu.HBM(v.shape, v.dtype) for v in arrays], jax.ShapeDtypeStruct((8, LANES), F32)),
        in_specs=[HBM_SPEC] * n + [pl.BlockSpec(memory_space=pl.ANY)],
        out_specs=(SEM_SPEC, SEM_SPEC, *[HBM_SPEC] * n, pl.BlockSpec(memory_space=pltpu.VMEM)),
        input_output_aliases={i: 2 + i for i in range(n)},
        compiler_params=pltpu.CompilerParams(has_side_effects=EFFECT),
    )(*arrays, after)
    return send_sems, recv_sems, flying, token


def _wait_copies(name, plan, n_in, send_sems, recv_sems, flying, after):
    n = len(flying)

    def body(*refs):
        for cp in _split_copies(plan, refs[:n_in], refs[n_in:n], refs[n], refs[n + 1]):
            cp.wait_send()
            cp.wait_recv()

    landed = pl.pallas_call(
        body, name=name, out_shape=tuple(pltpu.HBM(v.shape, v.dtype) for v in flying),
        in_specs=[HBM_SPEC] * n + [SEM_SPEC, SEM_SPEC, pl.BlockSpec(memory_space=pl.ANY)], out_specs=tuple([HBM_SPEC] * n),
        input_output_aliases={i: i for i in range(n)},
        compiler_params=pltpu.CompilerParams(has_side_effects=EFFECT),
    )(*flying, send_sems, recv_sems, after)
    return landed[:n_in], landed[n_in:]


def _adamw_math(w, g, m, v):
    m = ADAM_B1 * m + (1.0 - ADAM_B1) * g
    v = ADAM_B2 * v + (1.0 - ADAM_B2) * (g * g)
    m_hat = m / (1.0 - ADAM_B1 ** ADAM_STEP)
    v_hat = v / (1.0 - ADAM_B2 ** ADAM_STEP)
    return -ADAM_LR * (m_hat / (jnp.sqrt(v_hat) + ADAM_EPS) + ADAM_WD * w), m, v


def _adamw_sharded(me, own, received, w, m, v, tr):
    rows, cols = w.shape
    own, received, w, m, v = _in_hbm(own, received, w, m, v)

    def body(me_ref, own_ref, rec_ref, w_ref, m_ref, v_ref, g_ref, d_ref, nm_ref, nv_ref):
        g = own_ref[...]
        for r in range(N_DEV - 1):
            g = g + rec_ref[r].astype(F32)
        g_ref[...] = g
        d_ref[...], nm_ref[...], nv_ref[...] = _adamw_math(w_ref[...], g, m_ref[...], v_ref[...])

    tile = pl.BlockSpec((tr, cols), lambda i, me_ref: (i, 0))
    shape = pltpu.HBM((rows, cols), F32)
    return pl.pallas_call(
        body, name="adamw_sharded",
        grid_spec=pltpu.PrefetchScalarGridSpec(
            num_scalar_prefetch=1, grid=(rows // tr,),
            in_specs=[pl.BlockSpec((None, tr, cols), lambda i, me_ref: (me_ref[0], i, 0)),
                      pl.BlockSpec((N_DEV - 1, tr, cols), lambda i, me_ref: (0, i, 0)), tile, tile, tile],
            out_specs=[tile, tile, tile, tile]),
        out_shape=[shape, shape, shape, shape],
        compiler_params=_cparams(("parallel",), 40),
    )(me, own, received, w, m, v)


VECTOR_ROWS = D_MODEL // LANES
POOL_ROWS = len(POOL_WINDOWS) * POOL_GROUP_DIM


def _pack_small(dg1, dg2, dg3, dps, dsink, loss_cols, dwp):
    def body(g1_ref, g2_ref, g3_ref, ps_ref, sink_ref, loss_ref, wp_ref, o_ref):
        o_ref[...] = jnp.zeros_like(o_ref)
        for base, ref, n in ((ROW_G1, g1_ref, VECTOR_ROWS), (ROW_G2, g2_ref, VECTOR_ROWS), (ROW_G3, g3_ref, VECTOR_ROWS),
                             (ROW_LOSS, loss_ref, VECTOR_ROWS), (ROW_PS, ps_ref, POOL_WIDTH // LANES)):
            for r in range(n):
                o_ref[base + r:base + r + 1, :] = ref[:, r * LANES:(r + 1) * LANES]
        heads = sink_ref[:, 0, :]
        on_diagonal = lax.broadcasted_iota(jnp.int32, heads.shape, 0) == lax.broadcasted_iota(jnp.int32, heads.shape, 1)
        o_ref[ROW_SINK:ROW_SINK + 1, :] = jnp.sum(jnp.where(on_diagonal, heads, 0.0), axis=0, keepdims=True)
        o_ref[ROW_WP:ROW_WP + POOL_ROWS, :] = wp_ref[...].reshape(POOL_ROWS, LANES)

    return pl.pallas_call(body, name="pack_small", out_shape=jax.ShapeDtypeStruct((SMALL_ROWS, LANES), F32))(
        dg1, dg2, dg3, dps, dsink, loss_cols, dwp)


def _finish_small(me, own, landed, params):
    flat = [a for group in params for a in group]

    def body(me_ref, own_ref, landed_ref, *refs):
        ins, outs = refs[:len(flat)], refs[len(flat):]
        total = None
        for source in range(N_DEV):
            rel = jnp.bitwise_xor(me_ref[0], source)
            piece = jnp.where(rel == 0, own_ref[...], landed_ref[jnp.maximum(rel, 1) - 1])
            total = piece if total is None else total + piece
        outs[0][...] = (0.5 / D_MODEL) * jnp.sum(jnp.sum(total[ROW_LOSS:ROW_LOSS + VECTOR_ROWS], axis=1, keepdims=True), axis=0, keepdims=True)
        row = lambda base, n: jnp.concatenate([total[base + r:base + r + 1, :] for r in range(n)], axis=1)
        grads = [row(ROW_G1, VECTOR_ROWS), row(ROW_G2, VECTOR_ROWS), row(ROW_G3, VECTOR_ROWS), row(ROW_PS, POOL_WIDTH // LANES),
                 total[ROW_SINK:ROW_SINK + 1, :N_Q_HEADS], total[ROW_WP:ROW_WP + POOL_ROWS].reshape(params[5][0].shape)]
        for k, g in enumerate(grads):
            w_ref, m_ref, v_ref = ins[3 * k:3 * k + 3]
            g_out, d_out, m_out, v_out = outs[1 + 4 * k:5 + 4 * k]
            g_out[...] = g
            d_out[...], m_out[...], v_out[...] = _adamw_math(w_ref[...], g, m_ref[...], v_ref[...])

    shapes = [jax.ShapeDtypeStruct((1, 1), F32)] + [jax.ShapeDtypeStruct(w.shape, F32) for w, _, _ in params for _ in range(4)]
    vmem = pl.BlockSpec(memory_space=pltpu.VMEM)
    res = pl.pallas_call(body, name="finish_small", in_specs=[pl.BlockSpec(memory_space=pltpu.SMEM)] + [vmem] * (2 + len(flat)),
                         out_shape=shapes)(me, own, landed, *flat)
    return res[0], [res[1 + 4 * k:5 + 4 * k] for k in range(len(params))]


def _local_step(x, target, attn_norm_g, attn_sinks, w_pool, pool_scale, mlp_norm_g, final_norm_g,
                front_token, first_weight, second_hop, later_weights, ship_down, ship_up, ship_in):
    n_seq, seq, _ = x.shape
    rows = n_seq * seq
    tm, tm_mlp, tm_norm = min(1024, seq), min(256, seq), min(2048, seq)
    tm_grad = tm
    x2d, t2d = x.reshape(rows, D_MODEL), target.reshape(rows, D_MODEL)
    g3 = final_norm_g.reshape(1, D_MODEL)
    cos, sin = _rope_tables(seq)
    wp_b = w_pool[0].astype(BF16)

    sink_rows = jnp.broadcast_to(attn_sinks.reshape(N_Q_HEADS, 1, 1), (N_Q_HEADS, 1, LANES))
    cos, sin = _in_hbm(cos, sin)
    h1 = _in_hbm(_first_norm(x2d, attn_norm_g, front_token, tm_norm))
    w_in_full = _in_hbm(first_weight(h1))
    q, kd, vd, u = _in_hbm(*_inproj(h1, w_in_full, cos, sin, seq, tm))
    attn = _in_hbm(_attention_forward(sink_rows, q, kd, vd, n_seq, seq))
    pool = _in_hbm(_pool_forward(u, wp_b, pool_scale, second_hop(attn), n_seq, seq))
    w_out_full, w_up_full, w_down_full = _in_hbm(*later_weights(pool))
    x2, h2 = _in_hbm(*_outproj_norm(x2d, attn, pool, w_out_full, mlp_norm_g, tm))
    slope, f, dx3, dx3b, loss_cols, dg3 = _mlp_forward_loss(h2, x2, w_up_full, w_down_full, g3, t2d, tm_mlp)
    slope, f, dx3, dx3b = _in_hbm(slope, f, dx3, dx3b)

    down_token = ship_down(_weight_gradient("down_gradient", f, dx3b, True, tm_grad))
    da, dx2, dattn, dpool, dg2, d_w_out, d_w_out_wire = _mlp_backward_data(
        dx3b, slope, w_down_full, w_up_full, dx3, x2, mlp_norm_g, w_out_full, attn, pool, down_token, tm_mlp)
    da, dx2, dattn, dpool = _in_hbm(da, dx2, dattn, dpool)
    up_token = ship_up((d_w_out, d_w_out_wire), _weight_gradient("up_gradient", h2, da, False, tm_grad))
    dq, dk, dv, dsink = _attention_backward(sink_rows, q, kd, vd, dattn, cos, sin, up_token, n_seq, seq)
    dq, dk, dv = _in_hbm(dq, dk, dv)
    du, d_w_pool, d_pool_scale = _pool_backward(u, dpool, wp_b, pool_scale, n_seq, seq)
    du = _in_hbm(du)
    in_token = ship_in(_inproj_gradient(dq, dk, dv, du, h1, tm_grad))
    grad_x, dg1 = _inproj_backward(dq, dk, dv, du, w_in_full, x2d, dx2, attn_norm_g, in_token, tm)
    return grad_x.reshape(x.shape), _pack_small(dg1, dg2, dg3, d_pool_scale, dsink, loss_cols, d_w_pool)


def kernel(x, attn_norm_g, w_in, attn_sinks, w_pool, pool_scale, w_out, mlp_norm_g, w_up, w_down, final_norm_g, loss_target, m_attn_norm_g, m_w_in, m_attn_sinks, m_w_pool, m_pool_scale, m_w_out, m_mlp_norm_g, m_w_up, m_w_down, m_final_norm_g, v_attn_norm_g, v_w_in, v_attn_sinks, v_w_pool, v_pool_scale, v_w_out, v_mlp_norm_g, v_w_up, v_w_down, v_final_norm_g):
    me = (4 * lax.axis_index("x") + 2 * lax.axis_index("y") + lax.axis_index("c")).astype(jnp.int32).reshape(1)

    unordered = jnp.zeros((8, LANES), F32)

    win_land, wout_land, wup_land, wdown_land = _stage_weights(
        [w_in[0].T, w_out[0], w_up[0], w_down[0]], [_row_slot, _row_slot, _column_slot, _row_slot],
        [(N_DEV, IN_BLOCK, D_MODEL), (N_DEV, OUT_BLOCK, D_MODEL), (D_MODEL, D_FF), (N_DEV, FF_BLOCK, D_MODEL)])
    in_slots, later_slots = [_row_slot], [_row_slot, _column_slot, _row_slot]
    in_copies = _start_copies("spread_in_start", _own_slot_first_hop(in_slots), [], [win_land], unordered)
    later_copies = _start_copies("spread_later_start", _own_slot_first_hop(later_slots), [], [wout_land, wup_land, wdown_land], in_copies[3])

    def first_weight(after):
        _, in_landed = _wait_copies("spread_in_wait", _own_slot_first_hop(in_slots), 0, *in_copies[:3], after)
        in_passed = _start_copies("pass_in_start", _landed_to_sibling(in_slots), [], in_landed, unordered)
        _, (win_g,) = _wait_copies("pass_in_wait", _landed_to_sibling(in_slots), 0, *in_passed[:3], in_passed[3])
        return win_g.reshape(IN_WIDTH, D_MODEL)

    passing = []

    def second_hop(after):
        _, later_landed = _wait_copies("spread_later_wait", _own_slot_first_hop(later_slots), 0, *later_copies[:3], after)
        passing.extend(_start_copies("pass_later_start", _landed_to_sibling(later_slots), [], later_landed, unordered))
        return passing[3]

    def later_weights(after):
        _, (wout_g, wup_g, wdown_g) = _wait_copies("pass_later_wait", _landed_to_sibling(later_slots), 0, *passing[:3], after)
        return wout_g.reshape(D_MODEL, D_MODEL), wup_g, wdown_g.reshape(D_FF, D_MODEL)

    deliveries, kept = {}, {}

    def deliver(name, plan, wires):
        lands = [lax.empty((N_DEV - 1,) + (g.shape[1:] if plan is _block_to_owner else g.shape), g.dtype) for g in wires]
        deliveries[name] = _start_copies(name + "_start", plan, wires, lands, unordered)
        return deliveries[name][3]

    def landed(name, plan, after):
        send, recv, flying, _ = deliveries[name]
        return _wait_copies(name + "_wait", plan, len(flying) // 2, send, recv, flying, after)

    def ship_down(d_w_down):
        kept["down"] = d_w_down[0]
        return deliver("deliver_down", _block_to_owner, [d_w_down[1]])

    def ship_up(d_w_out, d_w_up):
        kept["out"], kept["up"] = d_w_out[0].reshape(N_DEV, OUT_BLOCK, D_MODEL), d_w_up[0]
        return deliver("deliver_up", _block_to_owner, [d_w_out[1].reshape(N_DEV, OUT_BLOCK, D_MODEL), d_w_up[1]])

    def ship_in(d_w_in):
        kept["in"] = d_w_in[0].reshape(N_DEV, IN_BLOCK, D_MODEL)
        return deliver("deliver_in", _block_to_owner, [d_w_in[1].reshape(N_DEV, IN_BLOCK, D_MODEL)])

    grad_x, small = _local_step(x, loss_target, attn_norm_g, attn_sinks, w_pool, pool_scale, mlp_norm_g, final_norm_g,
                                later_copies[3], first_weight, second_hop, later_weights, ship_down, ship_up, ship_in)
    small_token = deliver("deliver_small", _whole_to_all, [small])

    _, (got_down,) = landed("deliver_down", _block_to_owner, small_token)
    _, (got_out, got_up) = landed("deliver_up", _block_to_owner, small_token)
    g_down = _adamw_sharded(me, kept["down"], got_down, w_down[0], m_w_down[0], v_w_down[0], 256)
    g_up = _adamw_sharded(me, kept["up"], got_up, w_up[0], m_w_up[0], v_w_up[0], 256)
    g_out = _adamw_sharded(me, kept["out"], got_out, w_out[0], m_w_out[0], v_w_out[0], 128)
    _, (got_in,) = landed("deliver_in", _block_to_owner, g_out[0])
    g_in = [a.T for a in _adamw_sharded(me, kept["in"], got_in, w_in[0].T, m_w_in[0].T, v_w_in[0].T, IN_BLOCK)]
    (own_small,), (got_small,) = landed("deliver_small", _whole_to_all, g_in[1])

    row = lambda a: a.reshape(1, D_MODEL)
    params = [(attn_norm_g, m_attn_norm_g, v_attn_norm_g), (mlp_norm_g, m_mlp_norm_g, v_mlp_norm_g),
              (row(final_norm_g), row(m_final_norm_g), row(v_final_norm_g)), (pool_scale, m_pool_scale, v_pool_scale),
              (attn_sinks, m_attn_sinks, v_attn_sinks), (w_pool[0], m_w_pool[0], v_w_pool[0])]
    loss, (s_norm1, s_norm2, s_norm3, s_scale, s_sinks, s_pool) = _finish_small(me, own_small, got_small, params)
    s_norm3 = [a.reshape(D_MODEL) for a in s_norm3]
    s_pool = [a[None] for a in s_pool]

    def ordered(k):
        return [s_norm1[k], g_in[k][None], s_sinks[k], s_pool[k], s_scale[k], g_out[k][None], s_norm2[k], g_up[k][None], g_down[k][None],
                s_norm3[k]]

    return (loss.reshape(()), grad_x, *ordered(0), *ordered(1), *ordered(2), *ordered(3))
```

```python
import jax
import jax.numpy as jnp
from jax import lax
from jax.experimental import pallas as pl
from jax.experimental.pallas import tpu as pltpu

F32 = jnp.float32
BF16 = jnp.bfloat16

D_MODEL = 1024
HEAD_DIM = 64
N_Q_HEADS = 8
Q_PER_KV = 4
ATTN_WIDTH = 512
KV_WIDTH = 128
BLOCK = 128
ROPE_THETA = 10000.0
POOL_WINDOWS = (2, 4, 8, 16)
POOL_WIDTH = 512
POOL_GROUP_DIM = 128
IN_WIDTH = 1280
D_FF = 4096
EPS = 1e-6
N_DEV = 8
FF_BLOCK = D_FF // N_DEV
IN_BLOCK = IN_WIDTH // N_DEV
OUT_BLOCK = D_MODEL // N_DEV
ADAM_LR = 0.001
ADAM_B1 = 0.9
ADAM_B2 = 0.999
ADAM_EPS = 1e-08
ADAM_WD = 0.01
ADAM_STEP = 10
NEG = -1e30
FORWARD_CHAINS = 8
BACKWARD_CHAINS = 2
LANES = 128
MIB = 1024 * 1024
MESH = pl.DeviceIdType.MESH

ROW_G1, ROW_G2, ROW_G3, ROW_PS, ROW_SINK, ROW_LOSS, ROW_WP, SMALL_ROWS = 0, 8, 16, 24, 32, 40, 48, 560


def _cparams(semantics, vmem_mib):
    return pltpu.CompilerParams(dimension_semantics=semantics, vmem_limit_bytes=vmem_mib * MIB)


def _in_hbm(*arrays):
    pinned = tuple(pltpu.with_memory_space_constraint(a, pltpu.HBM) for a in arrays)
    return pinned[0] if len(pinned) == 1 else pinned


def _streamed(tm, width):
    return pl.BlockSpec((tm, width), lambda i: (i, 0))


def _dot(a, b):
    return jnp.dot(a, b, preferred_element_type=F32)


def _dot_nt(a, b):
    return lax.dot_general(a, b, (((1,), (1,)), ((), ())), preferred_element_type=F32)


def _dot_tn(a, b):
    return lax.dot_general(a, b, (((0,), (0,)), ((), ())), preferred_element_type=F32)


def _swap_halves(x):
    width = x.shape[1]
    lane = lax.broadcasted_iota(jnp.int32, x.shape, 1)
    ahead = pltpu.roll(x, width - HEAD_DIM // 2, 1)
    behind = pltpu.roll(x, HEAD_DIM // 2, 1)
    return jnp.where(lane % HEAD_DIM < HEAD_DIM // 2, ahead, behind)


def _rope(x, cos, sin):
    reps = x.shape[1] // LANES
    if reps > 1:
        cos = jnp.tile(cos, (1, reps))
        sin = jnp.tile(sin, (1, reps))
    return x * cos + _swap_halves(x) * sin


def _rope_tables(seq):
    half = HEAD_DIM // 2
    inv_freq = ROPE_THETA ** (-jnp.arange(half, dtype=F32) / half)
    ang = jnp.arange(seq).astype(F32)[:, None] * inv_freq[None, :]
    cos, sin = jnp.cos(ang), jnp.sin(ang)
    cos = jnp.tile(cos, (1, LANES // half))
    sin = jnp.tile(jnp.concatenate([-sin, sin], axis=1), (1, LANES // HEAD_DIM))
    return cos, sin


def _both_halves(x):
    lane = lax.broadcasted_iota(jnp.int32, x.shape, 1)
    other = pltpu.roll(x, HEAD_DIM, 1)
    low = lane < HEAD_DIM
    return jnp.where(low, x, other), jnp.where(low, other, x)


def _rms_backward(dh, xin, gain):
    r = lax.rsqrt(jnp.mean(xin * xin, axis=-1, keepdims=True) + EPS)
    xhat = xin * r
    dxhat = dh * gain
    dx = r * (dxhat - xhat * jnp.mean(dxhat * xhat, axis=-1, keepdims=True))
    return dx, jnp.sum(dh * xhat, axis=0, keepdims=True)


def _first_norm(x2d, gain, token, tm):
    rows = x2d.shape[0]
    steps = rows // tm
    reads, writes = min(3, steps), min(2, steps)

    def body(x_hbm, g_ref, token_ref, h_hbm, x_buf, h_buf, in_sems, out_sems):
        def fetch(i):
            return pltpu.make_async_copy(x_hbm.at[pl.ds(i * tm, tm)], x_buf.at[i % reads], in_sems.at[i % reads])

        def store(i):
            return pltpu.make_async_copy(h_buf.at[i % writes], h_hbm.at[pl.ds(i * tm, tm)], out_sems.at[i % writes])

        for i in range(reads):
            fetch(i).start()
        for i in range(steps):
            fetch(i).wait()
            if i >= writes:
                store(i - writes).wait()
            x = x_buf[i % reads]
            r = lax.rsqrt(jnp.mean(x * x, axis=-1, keepdims=True) + EPS)
            h_buf[i % writes] = (x * r * g_ref[...]).astype(BF16)
            store(i).start()
            if i + reads < steps:
                fetch(i + reads).start()
        for i in range(steps - writes, steps):
            store(i).wait()

    vmem = pl.BlockSpec(memory_space=pltpu.VMEM)
    return pl.pallas_call(
        body, name="first_norm",
        in_specs=[pl.BlockSpec(memory_space=pl.ANY), vmem, vmem],
        out_specs=pl.BlockSpec(memory_space=pl.ANY), out_shape=pltpu.HBM((rows, D_MODEL), BF16),
        scratch_shapes=[pltpu.VMEM((reads, tm, D_MODEL), F32), pltpu.VMEM((writes, tm, D_MODEL), BF16),
                        pltpu.SemaphoreType.DMA((reads,)), pltpu.SemaphoreType.DMA((writes,))],
        compiler_params=pltpu.CompilerParams(vmem_limit_bytes=40 * MIB),
    )(x2d, gain, token)


def _inproj(h1, w_in, cos, sin, seq, tm):
    rows = h1.shape[0]
    tiles_per_seq = seq // tm

    def body(h_ref, w_ref, cos_ref, sin_ref, q_ref, k_ref, v_ref, u_ref):
        proj = _dot_nt(h_ref[...], w_ref[...])
        cos_t, sin_t = cos_ref[...], sin_ref[...]
        q = _rope(proj[:, :ATTN_WIDTH], cos_t, sin_t) * (HEAD_DIM ** -0.5)
        q_ref[...] = q.astype(BF16)
        k = _rope(proj[:, ATTN_WIDTH:ATTN_WIDTH + KV_WIDTH], cos_t, sin_t)
        k0, k1 = _both_halves(k)
        k_ref[...] = jnp.concatenate([k0, k1], axis=1).astype(BF16)
        v0, v1 = _both_halves(proj[:, ATTN_WIDTH + KV_WIDTH:ATTN_WIDTH + 2 * KV_WIDTH])
        v_ref[...] = jnp.concatenate([v0, v1], axis=1).astype(BF16)
        u_ref[...] = proj[:, ATTN_WIDTH + 2 * KV_WIDTH:]

    row = lambda width: pl.BlockSpec((tm, width), lambda i: (i, 0))
    table = pl.BlockSpec((tm, LANES), lambda i: (i % tiles_per_seq, 0))
    return pl.pallas_call(
        body, name="inproj", grid=(rows // tm,),
        in_specs=[row(D_MODEL), pl.BlockSpec((IN_WIDTH, D_MODEL), lambda i: (0, 0)), table, table],
        out_specs=[row(ATTN_WIDTH), row(2 * KV_WIDTH), row(2 * KV_WIDTH), row(POOL_WIDTH)],
        out_shape=[pltpu.HBM((rows, ATTN_WIDTH), BF16), pltpu.HBM((rows, 2 * KV_WIDTH), BF16),
                   pltpu.HBM((rows, 2 * KV_WIDTH), BF16), pltpu.HBM((rows, POOL_WIDTH), F32)],
        compiler_params=_cparams(("parallel",), 40),
    )(h1, w_in, cos, sin)


def _window_masks(n):
    qi = lax.broadcasted_iota(jnp.int32, (BLOCK, BLOCK), 0)
    kj = lax.broadcasted_iota(jnp.int32, (BLOCK, BLOCK), 1)
    return kj <= qi, jnp.logical_and(kj > qi, n > 0)


def _window_operand(ref, r0, p0, kv):
    low = lax.broadcasted_iota(jnp.int32, (BLOCK, LANES), 1) < HEAD_DIM
    cur = ref[pl.ds(r0, BLOCK), kv * LANES:(kv + 1) * LANES]
    prev = ref[pl.ds(p0, BLOCK), kv * LANES:(kv + 1) * LANES]
    zero = jnp.zeros_like(cur)
    return jnp.concatenate([jnp.where(low, cur, zero), jnp.where(low, zero, cur), jnp.where(low, prev, zero), jnp.where(low, zero, prev)], axis=0)


def _pair_rows(ref, r0, kv):
    return jnp.concatenate([ref[pl.ds(r0, BLOCK), (2 * kv + j) * LANES:(2 * kv + j + 1) * LANES] for j in range(2)], axis=0)


def _merged_window(wide, parity, cur_mask, prev_mask, fill):
    cur = wide[:, parity * LANES:(parity + 1) * LANES]
    prev = wide[:, (2 + parity) * LANES:(3 + parity) * LANES]
    return jnp.where(cur_mask, cur, jnp.where(prev_mask, prev, fill))


def _lane_sums(x, one_matmul):
    flat = x.reshape(-1, x.shape[-1])
    high = flat.astype(BF16)
    low = (flat - high.astype(F32)).astype(BF16)
    if one_matmul:
        sums = _dot(jnp.concatenate([high, low], axis=1), jnp.ones((2 * x.shape[-1], LANES), BF16))
    else:
        ones = jnp.ones((x.shape[-1], LANES), BF16)
        sums = _dot(high, ones) + _dot(low, ones)
    return sums.reshape(x.shape[:-1] + (LANES,))


def _softmax_with_sink(scores, sink, one_matmul):
    m = jnp.broadcast_to(jnp.maximum(jnp.max(scores, axis=-1, keepdims=True), sink), scores.shape)
    p, ps = jnp.exp(scores - m), jnp.exp(sink - m)
    inv = 1.0 / (_lane_sums(p, one_matmul) + ps)
    return p * inv, ps * inv


def _attention_forward(sinks, q, kd, vd, n_seq, seq):
    n_blocks = seq // BLOCK
    n_pairs = N_Q_HEADS // 2
    chains = min(FORWARD_CHAINS, n_blocks)

    def body(sink_ref, q_ref, k_ref, v_ref, o_ref, s_ref, p_ref):
        def step(i, carry):
            starts, values = [], []
            for u in range(chains):
                n = i * chains + u
                r0 = pl.multiple_of(n * BLOCK, BLOCK)
                p0 = pl.multiple_of(jnp.maximum(n - 1, 0) * BLOCK, BLOCK)
                cur_mask, prev_mask = _window_masks(n)
                keys = [_window_operand(k_ref, r0, p0, kv) for kv in range(2)]
                starts.append(r0)
                values.append([_window_operand(v_ref, r0, p0, kv) for kv in range(2)])
                for kv in range(2):
                    both = _dot_nt(_pair_rows(q_ref, r0, kv), keys[kv])
                    for j in range(2):
                        wide = both[j * BLOCK:(j + 1) * BLOCK]
                        for parity in range(2):
                            s_ref[u * N_Q_HEADS + 4 * kv + 2 * j + parity] = _merged_window(wide, parity, cur_mask, prev_mask, NEG)
            probs, _ = _softmax_with_sink(s_ref[...], jnp.tile(sink_ref[:, :, 0:1], (chains, 1, 1)), False)
            probs = probs.astype(BF16)
            zero = jnp.zeros((BLOCK, BLOCK), BF16)
            for u in range(chains):
                for pair in range(n_pairs):
                    for parity in range(2):
                        ph = probs[u * N_Q_HEADS + 2 * pair + parity]
                        p_ref[u, pair, :, parity * LANES:(parity + 1) * LANES] = jnp.where(cur_mask, ph, zero)
                        p_ref[u, pair, :, (2 + parity) * LANES:(3 + parity) * LANES] = jnp.where(cur_mask, zero, ph)
            for u in range(chains):
                for kv in range(2):
                    both = _dot(p_ref[u, 2 * kv:2 * kv + 2].reshape(2 * BLOCK, 4 * LANES), values[u][kv])
                    for j in range(2):
                        pair = 2 * kv + j
                        o_ref[pl.ds(starts[u], BLOCK), pair * LANES:(pair + 1) * LANES] = both[j * BLOCK:(j + 1) * BLOCK].astype(BF16)
            return carry

        lax.fori_loop(0, n_blocks // chains, step, 0)

    seq_block = lambda width: pl.BlockSpec((seq, width), lambda b: (b, 0))
    return pl.pallas_call(
        body, name="attention_forward", grid=(n_seq,),
        in_specs=[pl.BlockSpec((N_Q_HEADS, 1, LANES), lambda b: (0, 0, 0)), seq_block(ATTN_WIDTH), seq_block(2 * KV_WIDTH),
                  seq_block(2 * KV_WIDTH)],
        out_specs=seq_block(ATTN_WIDTH),
        out_shape=pltpu.HBM((n_seq * seq, ATTN_WIDTH), BF16),
        scratch_shapes=[pltpu.VMEM((chains * N_Q_HEADS, BLOCK, BLOCK), F32), pltpu.VMEM((chains, n_pairs, BLOCK, 4 * LANES), BF16)],
        compiler_params=_cparams(("parallel",), 40),
    )(sinks, q, kd, vd)


def _trailing(x, window, t, seq):
    k = 1
    while k < window:
        x = x + jnp.where(t >= k, pltpu.roll(x, k, 0), 0.0)
        k *= 2
    return x


def _leading(x, window, t, seq):
    k = 1
    while k < window:
        x = x + jnp.where(t < seq - k, pltpu.roll(x, seq - k, 0), 0.0)
        k *= 2
    return x


def _pool_features(u_g, window, t, seq):
    count = jnp.minimum(t + 1, window).astype(F32)
    return (_trailing(u_g, window, t, seq) / count - u_g).astype(BF16), count


def _pool_forward(u, w_pool, pool_scale, token, n_seq, seq):
    def body(u_ref, w_ref, s_ref, token_ref, o_ref):
        t = lax.broadcasted_iota(jnp.int32, (seq, 1), 0)
        for g, window in enumerate(POOL_WINDOWS):
            cols = slice(g * POOL_GROUP_DIM, (g + 1) * POOL_GROUP_DIM)
            d, _ = _pool_features(u_ref[:, cols], window, t, seq)
            o_ref[:, cols] = (_dot(d, w_ref[g]) * s_ref[:, cols]).astype(BF16)

    seq_block = pl.BlockSpec((seq, POOL_WIDTH), lambda b: (b, 0))
    return pl.pallas_call(
        body, name="pool_forward", grid=(n_seq,),
        in_specs=[seq_block, pl.BlockSpec((len(POOL_WINDOWS), POOL_GROUP_DIM, POOL_GROUP_DIM), lambda b: (0, 0, 0)),
                  pl.BlockSpec((1, POOL_WIDTH), lambda b: (0, 0)), pl.BlockSpec((8, LANES), lambda b: (0, 0))],
        out_specs=seq_block,
        out_shape=pltpu.HBM((n_seq * seq, POOL_WIDTH), BF16),
        compiler_params=_cparams(("parallel",), 40),
    )(u, w_pool, pool_scale, token)


def _outproj_norm(x2d, attn, pool, w_out, gain, tm):
    rows = x2d.shape[0]

    def body(x_ref, a_ref, p_ref, w_ref, g_ref, x2_ref, h_ref):
        x2 = x_ref[...] + _dot(a_ref[...], w_ref[:ATTN_WIDTH, :]) + _dot(p_ref[...], w_ref[ATTN_WIDTH:, :])
        x2_ref[...] = x2
        r = lax.rsqrt(jnp.mean(x2 * x2, axis=-1, keepdims=True) + EPS)
        h_ref[...] = (x2 * r * g_ref[...]).astype(BF16)

    row = lambda width: pl.BlockSpec((tm, width), lambda i: (i, 0))
    return pl.pallas_call(
        body, name="outproj_norm", grid=(rows // tm,),
        in_specs=[_streamed(tm, D_MODEL), _streamed(tm, ATTN_WIDTH), _streamed(tm, POOL_WIDTH), pl.BlockSpec((D_MODEL, D_MODEL), lambda i: (0, 0)),
                  pl.BlockSpec((1, D_MODEL), lambda i: (0, 0))],
        out_specs=[row(D_MODEL), row(D_MODEL)],
        out_shape=[pltpu.HBM((rows, D_MODEL), F32), pltpu.HBM((rows, D_MODEL), BF16)],
        compiler_params=_cparams(("parallel",), 40),
    )(x2d, attn, pool, w_out, gain)


def _resident(shape):
    return pl.BlockSpec(shape, lambda i: (0,) * len(shape), pipeline_mode=pl.Buffered(1))


def _mlp_forward_loss(h2, x2, w_up, w_down, gain, target, tm):
    rows = h2.shape[0]
    chunk = D_MODEL

    def body(h_ref, x_ref, up_ref, down_ref, g_ref, t_ref, slope_ref, f_ref, dx_ref, dxb_ref, loss_ref, dg_ref):
        @pl.when(pl.program_id(0) == 0)
        def _():
            loss_ref[...] = jnp.zeros_like(loss_ref)
            dg_ref[...] = jnp.zeros_like(dg_ref)

        h = h_ref[...]
        for c in range(D_FF // chunk):
            cols = slice(c * chunk, (c + 1) * chunk)
            r = jnp.maximum(_dot(h, up_ref[:, cols]), 0.0)
            slope_ref[:, cols] = (r + r).astype(BF16)
            f_ref[:, cols] = (r * r).astype(BF16)
        x3 = x_ref[...] + _dot(f_ref[...], down_ref[...])
        rn = lax.rsqrt(jnp.mean(x3 * x3, axis=-1, keepdims=True) + EPS)
        xhat = x3 * rn
        err = xhat * g_ref[...] - t_ref[...]
        loss_ref[...] += jnp.sum(err * err, axis=0, keepdims=True)
        dxhat = err * (g_ref[...] * (1.0 / D_MODEL))
        dx = rn * (dxhat - xhat * jnp.mean(dxhat * xhat, axis=-1, keepdims=True))
        dg_ref[...] += jnp.sum(err * xhat, axis=0, keepdims=True) * (1.0 / D_MODEL)
        dx_ref[...] = dx
        dxb_ref[...] = dx.astype(BF16)

    row = lambda width: pl.BlockSpec((tm, width), lambda i: (i, 0))
    vec = pl.BlockSpec((1, D_MODEL), lambda i: (0, 0))
    return pl.pallas_call(
        body, name="mlp_forward_loss", grid=(rows // tm,),
        in_specs=[row(D_MODEL), row(D_MODEL), _resident((D_MODEL, D_FF)), _resident((D_FF, D_MODEL)), vec, row(D_MODEL)],
        out_specs=[row(D_FF), row(D_FF), row(D_MODEL), row(D_MODEL), vec, vec],
        out_shape=[pltpu.HBM((rows, D_FF), BF16), pltpu.HBM((rows, D_FF), BF16),
                   pltpu.HBM((rows, D_MODEL), F32), pltpu.HBM((rows, D_MODEL), BF16),
                   jax.ShapeDtypeStruct((1, D_MODEL), F32), jax.ShapeDtypeStruct((1, D_MODEL), F32)],
        compiler_params=_cparams(("arbitrary",), 56),
    )(h2, x2, w_up, w_down, gain, target)


def _mlp_backward_data(dx3b, slope, w_down, w_up, dx3, x2, gain, w_out, attn, pool, token, tm):
    rows = dx3b.shape[0]
    steps = rows // tm
    chunk = D_MODEL

    def body(dxb_ref, slope_ref, down_ref, up_ref, dx3_ref, x2_ref, g_ref, wo_ref, attn_ref, pool_ref, token_ref,
             da_ref, dx2_ref, dattn_ref, dpool_ref, dg_ref, dwo_hbm, wire_hbm, dwo_acc, wire, sems):
        @pl.when(pl.program_id(0) == 0)
        def _():
            dg_ref[...] = jnp.zeros_like(dg_ref)
            dwo_acc[...] = jnp.zeros_like(dwo_acc)

        dxb = dxb_ref[...]
        for c in range(D_FF // chunk):
            cols = slice(c * chunk, (c + 1) * chunk)
            da_ref[:, cols] = (_dot_nt(dxb, down_ref[cols, :]) * slope_ref[:, cols].astype(F32)).astype(BF16)
        dnorm, dg = _rms_backward(_dot_nt(da_ref[...], up_ref[...]), x2_ref[...], g_ref[...])
        dg_ref[...] += dg
        dx2 = dx3_ref[...] + dnorm
        dx2_ref[...] = dx2
        dx2b = dx2.astype(BF16)
        dmix = _dot_nt(dx2b, wo_ref[...])
        dattn_ref[...] = dmix[:, :ATTN_WIDTH].astype(BF16)
        dpool_ref[...] = dmix[:, ATTN_WIDTH:]
        dwo_acc[:ATTN_WIDTH, :] += _dot_tn(attn_ref[...], dx2b)
        dwo_acc[ATTN_WIDTH:, :] += _dot_tn(pool_ref[...], dx2b)

        @pl.when(pl.program_id(0) == steps - 1)
        def _():
            done = pltpu.make_async_copy(dwo_acc, dwo_hbm, sems.at[0])
            done.start()
            wire[...] = dwo_acc[...].astype(BF16)
            sent = pltpu.make_async_copy(wire, wire_hbm, sems.at[1])
            sent.start()
            done.wait()
            sent.wait()

    row = lambda width: pl.BlockSpec((tm, width), lambda i: (i, 0))
    vec = pl.BlockSpec((1, D_MODEL), lambda i: (0, 0))
    return pl.pallas_call(
        body, name="mlp_backward_data", grid=(steps,),
        in_specs=[row(D_MODEL), row(D_FF), _resident((D_FF, D_MODEL)), _resident((D_MODEL, D_FF)), row(D_MODEL), row(D_MODEL), vec,
                  _resident((D_MODEL, D_MODEL)), row(ATTN_WIDTH), row(POOL_WIDTH), pl.BlockSpec((8, LANES), lambda i: (0, 0))],
        out_specs=[row(D_FF), row(D_MODEL), row(ATTN_WIDTH), row(POOL_WIDTH), vec, pl.BlockSpec(memory_space=pl.ANY),
                   pl.BlockSpec(memory_space=pl.ANY)],
        out_shape=[pltpu.HBM((rows, D_FF), BF16), pltpu.HBM((rows, D_MODEL), F32),
                   pltpu.HBM((rows, ATTN_WIDTH), BF16), pltpu.HBM((rows, POOL_WIDTH), F32),
                   jax.ShapeDtypeStruct((1, D_MODEL), F32), jax.ShapeDtypeStruct((D_MODEL, D_MODEL), F32),
                   jax.ShapeDtypeStruct((D_MODEL, D_MODEL), BF16)],
        scratch_shapes=[pltpu.VMEM((D_MODEL, D_MODEL), F32), pltpu.VMEM((D_MODEL, D_MODEL), BF16), pltpu.SemaphoreType.DMA((2,))],
        compiler_params=_cparams(("arbitrary",), 56),
    )(dx3b, slope, w_down, w_up, dx3, x2, gain, w_out, attn, pool, token)


def _weight_gradient(name, lhs, rhs, block_lhs, tm):
    rows = lhs.shape[0]
    steps = rows // tm
    out = (N_DEV, FF_BLOCK, rhs.shape[1]) if block_lhs else (N_DEV, lhs.shape[1], FF_BLOCK)

    def body(l_ref, r_ref, o_hbm, wire_hbm, acc, wire, sems):
        @pl.when(pl.program_id(0) == 0)
        def _():
            acc[...] = jnp.zeros_like(acc)

        for d in range(N_DEV):
            cols = slice(d * FF_BLOCK, (d + 1) * FF_BLOCK)
            acc[d] += _dot_tn(l_ref[:, cols], r_ref[...]) if block_lhs else _dot_tn(l_ref[...], r_ref[:, cols])

        @pl.when(pl.program_id(0) == steps - 1)
        def _():
            done = pltpu.make_async_copy(acc, o_hbm, sems.at[0])
            done.start()
            wire[...] = acc[...].astype(BF16)
            sent = pltpu.make_async_copy(wire, wire_hbm, sems.at[1])
            sent.start()
            done.wait()
            sent.wait()

    return pl.pallas_call(
        body, name=name, grid=(steps,),
        in_specs=[pl.BlockSpec((tm, lhs.shape[1]), lambda i: (i, 0)), pl.BlockSpec((tm, rhs.shape[1]), lambda i: (i, 0))],
        out_specs=[pl.BlockSpec(memory_space=pl.ANY), pl.BlockSpec(memory_space=pl.ANY)],
        out_shape=[jax.ShapeDtypeStruct(out, F32), jax.ShapeDtypeStruct(out, BF16)],
        scratch_shapes=[pltpu.VMEM(out, F32), pltpu.VMEM(out, BF16), pltpu.SemaphoreType.DMA((2,))],
        compiler_params=_cparams(("arbitrary",), 60),
    )(lhs, rhs)


def _attention_backward(sinks, q, kd, vd, dout, cos, sin, token, n_seq, seq):
    n_blocks = seq // BLOCK
    n_kv = N_Q_HEADS // Q_PER_KV
    chains = min(BACKWARD_CHAINS, n_blocks)

    def body(sink_ref, q_ref, k_ref, v_ref, do_ref, cos_ref, sin_ref, token_ref, dq_ref, dk_ref, dv_ref, dsink_ref,
             s_all, dp_all, dsc_all, dsp_all, pc_all, pp_all, dk_acc, dv_acc):
        low = lax.broadcasted_iota(jnp.int32, (BLOCK, LANES), 1) < HEAD_DIM

        @pl.when(pl.program_id(0) == 0)
        def _():
            dsink_ref[...] = jnp.zeros_like(dsink_ref)

        def fold(x):
            return x + pltpu.roll(x, HEAD_DIM, 1)

        def onto_keys(cur_ref, prev_ref, kv, other):
            even = _dot_tn(jnp.concatenate([cur_ref[2 * kv], prev_ref[2 * kv]], axis=1), other)
            odd = _dot_tn(jnp.concatenate([cur_ref[2 * kv + 1], prev_ref[2 * kv + 1]], axis=1), other)
            both = fold(jnp.where(jnp.concatenate([low, low], axis=0), even, odd))
            return both[:BLOCK], both[BLOCK:]

        def step(i, dsink):
            blocks = []
            for u in range(chains):
                n = i * chains + u
                r0 = pl.multiple_of(n * BLOCK, BLOCK)
                p0 = pl.multiple_of(jnp.maximum(n - 1, 0) * BLOCK, BLOCK)
                cur_mask, prev_mask = _window_masks(n)
                keys = [_window_operand(k_ref, r0, p0, kv) for kv in range(n_kv)]
                values = [_window_operand(v_ref, r0, p0, kv) for kv in range(n_kv)]
                q_rows = [_pair_rows(q_ref, r0, kv) for kv in range(n_kv)]
                do_rows = [_pair_rows(do_ref, r0, kv) for kv in range(n_kv)]
                for kv in range(n_kv):
                    both_s, both_dp = _dot_nt(q_rows[kv], keys[kv]), _dot_nt(do_rows[kv], values[kv])
                    for j in range(2):
                        wide_s, wide_dp = both_s[j * BLOCK:(j + 1) * BLOCK], both_dp[j * BLOCK:(j + 1) * BLOCK]
                        for parity in range(2):
                            head = u * N_Q_HEADS + 4 * kv + 2 * j + parity
                            s_all[head] = _merged_window(wide_s, parity, cur_mask, prev_mask, NEG)
                            dp_all[head] = _merged_window(wide_dp, parity, cur_mask, prev_mask, 0.0)
                blocks.append((n, r0, p0, keys, q_rows, do_rows))

            probs, p_sink = _softmax_with_sink(s_all[...], jnp.tile(sink_ref[:, :, 0:1], (chains, 1, 1)), True)
            dprobs = dp_all[...]
            delta = _lane_sums(probs * dprobs, True)
            dscores = (probs * (dprobs - delta)).astype(BF16)
            sink_terms = jnp.sum((p_sink * delta)[:, :, 0:1], axis=1, keepdims=True)
            probs = probs.astype(BF16)
            zero = jnp.zeros((BLOCK, BLOCK), BF16)
            for u in range(chains):
                dsink = dsink - sink_terms[u * N_Q_HEADS:(u + 1) * N_Q_HEADS]
                for head in range(N_Q_HEADS):
                    group, rows = 2 * (head // Q_PER_KV) + head % 2, pl.ds(((head % Q_PER_KV) // 2) * BLOCK, BLOCK)
                    ds_h, p_h = dscores[u * N_Q_HEADS + head], probs[u * N_Q_HEADS + head]
                    dsc_all[u, group, rows, :] = jnp.where(cur_mask, ds_h, zero)
                    dsp_all[u, group, rows, :] = jnp.where(cur_mask, zero, ds_h)
                    pc_all[u, group, rows, :] = jnp.where(cur_mask, p_h, zero)
                    pp_all[u, group, rows, :] = jnp.where(cur_mask, zero, p_h)

            for u, (n, r0, p0, keys, q_rows, do_rows) in enumerate(blocks):
                dsc_ref, dsp_ref, pc_ref, pp_ref = dsc_all.at[u], dsp_all.at[u], pc_all.at[u], pp_all.at[u]
                for kv in range(n_kv):
                    wide = jnp.concatenate([dsc_ref[2 * kv], dsc_ref[2 * kv + 1], dsp_ref[2 * kv], dsp_ref[2 * kv + 1]], axis=1)
                    both = _dot(wide, keys[kv]) * (HEAD_DIM ** -0.5)
                    for j in range(2):
                        dq = _rope(both[j * BLOCK:(j + 1) * BLOCK], cos_ref[pl.ds(r0, BLOCK), :], -sin_ref[pl.ds(r0, BLOCK), :])
                        dq_ref[pl.ds(r0, BLOCK), (2 * kv + j) * LANES:(2 * kv + j + 1) * LANES] = dq.astype(BF16)

                parts = []
                for kv in range(n_kv):
                    parts.append(onto_keys(dsc_ref, dsp_ref, kv, q_rows[kv]) + onto_keys(pc_ref, pp_ref, kv, do_rows[kv]))
                dk_acc[pl.ds(r0, BLOCK), :] = jnp.where(low, parts[0][0], parts[1][0])
                dv_acc[pl.ds(r0, BLOCK), :] = jnp.where(low, parts[0][2], parts[1][2])

                @pl.when(n > 0)
                def _():
                    dk_acc[pl.ds(p0, BLOCK), :] += jnp.where(low, parts[0][1], parts[1][1])
                    dv_acc[pl.ds(p0, BLOCK), :] += jnp.where(low, parts[0][3], parts[1][3])

            return dsink

        dsink = lax.fori_loop(0, n_blocks // chains, step, jnp.zeros((N_Q_HEADS, 1, 1), F32))
        dsink_ref[...] += jnp.broadcast_to(dsink, dsink_ref.shape)
        dk_ref[...] = _rope(dk_acc[...], cos_ref[...], -sin_ref[...]).astype(BF16)
        dv_ref[...] = dv_acc[...].astype(BF16)

    seq_block = lambda width: pl.BlockSpec((seq, width), lambda b: (b, 0))
    table = pl.BlockSpec((seq, LANES), lambda b: (0, 0))
    per_head = pl.BlockSpec((N_Q_HEADS, 1, LANES), lambda b: (0, 0, 0))
    per_block = pltpu.VMEM((chains * N_Q_HEADS, BLOCK, BLOCK), F32)
    grouped = pltpu.VMEM((chains, 2 * n_kv, 2 * BLOCK, BLOCK), BF16)
    return pl.pallas_call(
        body, name="attention_backward", grid=(n_seq,),
        in_specs=[per_head, seq_block(ATTN_WIDTH), seq_block(2 * KV_WIDTH), seq_block(2 * KV_WIDTH),
                  seq_block(ATTN_WIDTH), table, table, pl.BlockSpec((8, LANES), lambda b: (0, 0))],
        out_specs=[seq_block(ATTN_WIDTH), seq_block(KV_WIDTH), seq_block(KV_WIDTH), per_head],
        out_shape=[pltpu.HBM((n_seq * seq, ATTN_WIDTH), BF16), pltpu.HBM((n_seq * seq, KV_WIDTH), BF16),
                   pltpu.HBM((n_seq * seq, KV_WIDTH), BF16), jax.ShapeDtypeStruct((N_Q_HEADS, 1, LANES), F32)],
        scratch_shapes=[per_block, per_block, grouped, grouped, grouped, grouped, pltpu.VMEM((seq, KV_WIDTH), F32), pltpu.VMEM((seq, KV_WIDTH), F32)],
        compiler_params=_cparams(("arbitrary",), 40),
    )(sinks, q, kd, vd, dout, cos, sin, token)


def _pool_backward(u, dpool, w_pool, pool_scale, n_seq, seq):
    groups = len(POOL_WINDOWS)

    def body(u_ref, dp_ref, w_ref, s_ref, du_ref, dw_ref, ds_ref):
        @pl.when(pl.program_id(0) == 0)
        def _():
            dw_ref[...] = jnp.zeros_like(dw_ref)
            ds_ref[...] = jnp.zeros_like(ds_ref)

        t = lax.broadcasted_iota(jnp.int32, (seq, 1), 0)
        for g, window in enumerate(POOL_WINDOWS):
            cols = slice(g * POOL_GROUP_DIM, (g + 1) * POOL_GROUP_DIM)
            d, count = _pool_features(u_ref[:, cols], window, t, seq)
            dpool_g = dp_ref[:, cols]
            ds_ref[:, cols] += jnp.sum(dpool_g * _dot(d, w_ref[g]), axis=0, keepdims=True)
            dy = (dpool_g * s_ref[:, cols]).astype(BF16)
            dw_ref[g] += _dot_tn(d, dy)
            dd = _dot_nt(dy, w_ref[g])
            du_ref[:, cols] = (_leading(dd / count, window, t, seq) - dd).astype(BF16)

    seq_block = pl.BlockSpec((seq, POOL_WIDTH), lambda b: (b, 0))
    weights = pl.BlockSpec((groups, POOL_GROUP_DIM, POOL_GROUP_DIM), lambda b: (0, 0, 0))
    scale = pl.BlockSpec((1, POOL_WIDTH), lambda b: (0, 0))
    return pl.pallas_call(
        body, name="pool_backward", grid=(n_seq,),
        in_specs=[seq_block, seq_block, weights, scale],
        out_specs=[seq_block, weights, scale],
        out_shape=[pltpu.HBM((n_seq * seq, POOL_WIDTH), BF16),
                   jax.ShapeDtypeStruct((groups, POOL_GROUP_DIM, POOL_GROUP_DIM), F32), jax.ShapeDtypeStruct((1, POOL_WIDTH), F32)],
        compiler_params=_cparams(("arbitrary",), 40),
    )(u, dpool, w_pool, pool_scale)


def _inproj_gradient(dq, dk, dv, du, h1, tm):
    rows = h1.shape[0]
    steps = rows // tm

    def body(dq_ref, dk_ref, dv_ref, du_ref, h_ref, dw_ref, wire_ref):
        @pl.when(pl.program_id(0) == 0)
        def _():
            dw_ref[...] = jnp.zeros_like(dw_ref)

        dproj = jnp.concatenate([dq_ref[...], dk_ref[...], dv_ref[...], du_ref[...]], axis=1)
        dw_ref[...] += _dot_tn(dproj, h_ref[...])

        @pl.when(pl.program_id(0) == steps - 1)
        def _():
            wire_ref[...] = dw_ref[...].astype(BF16)

    row = lambda width: pl.BlockSpec((tm, width), lambda i: (i, 0))
    whole = pl.BlockSpec((IN_WIDTH, D_MODEL), lambda i: (0, 0))
    return pl.pallas_call(
        body, name="inproj_gradient", grid=(steps,),
        in_specs=[row(ATTN_WIDTH), row(KV_WIDTH), row(KV_WIDTH), row(POOL_WIDTH), row(D_MODEL)],
        out_specs=[whole, whole],
        out_shape=[pltpu.HBM((IN_WIDTH, D_MODEL), F32), pltpu.HBM((IN_WIDTH, D_MODEL), BF16)],
        compiler_params=_cparams(("arbitrary",), 48),
    )(dq, dk, dv, du, h1)


def _inproj_backward(dq, dk, dv, du, w_in, x2d, dx2, gain, token, tm):
    rows = x2d.shape[0]

    def body(dq_ref, dk_ref, dv_ref, du_ref, w_ref, x_ref, dx2_ref, g_ref, token_ref, dx_ref, dg_ref):
        @pl.when(pl.program_id(0) == 0)
        def _():
            dg_ref[...] = jnp.zeros_like(dg_ref)

        dproj = jnp.concatenate([dq_ref[...], dk_ref[...], dv_ref[...], du_ref[...]], axis=1)
        dnorm, dg = _rms_backward(_dot(dproj, w_ref[...]), x_ref[...], g_ref[...])
        dg_ref[...] += dg
        dx_ref[...] = dx2_ref[...] + dnorm

    row = lambda width: pl.BlockSpec((tm, width), lambda i: (i, 0))
    vec = pl.BlockSpec((1, D_MODEL), lambda i: (0, 0))
    return pl.pallas_call(
        body, name="inproj_backward", grid=(rows // tm,),
        in_specs=[row(ATTN_WIDTH), row(KV_WIDTH), row(KV_WIDTH), row(POOL_WIDTH), pl.BlockSpec((IN_WIDTH, D_MODEL), lambda i: (0, 0)),
                  _streamed(tm, D_MODEL), _streamed(tm, D_MODEL), vec, pl.BlockSpec((8, LANES), lambda i: (0, 0))],
        out_specs=[row(D_MODEL), vec],
        out_shape=[pltpu.HBM((rows, D_MODEL), F32), jax.ShapeDtypeStruct((1, D_MODEL), F32)],
        compiler_params=_cparams(("arbitrary",), 60),
    )(dq, dk, dv, du, w_in, x2d, dx2, gain, token)


def _place():
    return lax.axis_index("x"), lax.axis_index("y"), lax.axis_index("c")


def _peer(x, y, c, rel):
    return (1 - x if rel & 4 else x, 1 - y if rel & 2 else y, 1 - c if rel & 1 else c)


def _index(px, py, pc):
    return 4 * px + 2 * py + pc


def _row_slot(ref, d):
    return ref.at[d]


def _column_slot(ref, d):
    return ref.at[:, pl.ds(pl.multiple_of(d * FF_BLOCK, FF_BLOCK), FF_BLOCK)]


def _stage_weights(shards, slots, shapes):
    n = len(shards)

    def body(*refs):
        ins, outs, stage, sems = refs[:n], refs[n:2 * n], refs[2 * n:3 * n], refs[3 * n]
        me = _index(*_place())
        mine = []
        for a in range(n):
            stage[a][...] = ins[a][...].astype(BF16)
            mine.append(pltpu.make_async_copy(stage[a], slots[a](outs[a], me), sems.at[a]))
            mine[-1].start()
        for cp in mine:
            cp.wait()

    return pl.pallas_call(
        body, name="stage_weights",
        in_specs=[pl.BlockSpec(memory_space=pltpu.VMEM)] * n,
        out_specs=[pl.BlockSpec(memory_space=pl.ANY)] * n,
        out_shape=[pltpu.HBM(shape, BF16) for shape in shapes],
        scratch_shapes=[pltpu.VMEM(s.shape, BF16) for s in shards] + [pltpu.SemaphoreType.DMA((n,))],
        compiler_params=pltpu.CompilerParams(vmem_limit_bytes=32 * MIB),
    )(*shards)


ALL_PEERS = tuple(range(1, N_DEV))
FIRST_HOP = (1, 2, 4, 6)
OTHER_CHIPS = (2, 4, 6)


class _Plan:
    def __init__(self, per_array, copies):
        self.per_array, self.copies = per_array, copies


def _own_slot_first_hop(slots):
    def copies(x, y, c, ins, lands):
        me = _index(x, y, c)
        return [(slots[a](lands[a], me), slots[a](lands[a], me), _peer(x, y, c, rel)) for rel in FIRST_HOP for a in range(len(lands))]
    return _Plan(len(FIRST_HOP), copies)


def _landed_to_sibling(slots):
    def copies(x, y, c, ins, lands):
        blocks = [_index(*_peer(x, y, c, rel)) for rel in OTHER_CHIPS]
        return [(slots[a](lands[a], b), slots[a](lands[a], b), (x, y, 1 - c)) for b in blocks for a in range(len(lands))]
    return _Plan(len(OTHER_CHIPS), copies)


def _whole_to_all_copies(x, y, c, ins, lands):
    return [(ins[a], lands[a].at[rel - 1], _peer(x, y, c, rel)) for rel in ALL_PEERS for a in range(len(lands))]


def _block_to_owner_copies(x, y, c, ins, lands):
    return [(ins[a].at[_index(*_peer(x, y, c, rel))], lands[a].at[rel - 1], _peer(x, y, c, rel))
            for rel in ALL_PEERS for a in range(len(lands))]


_whole_to_all = _Plan(len(ALL_PEERS), _whole_to_all_copies)
_block_to_owner = _Plan(len(ALL_PEERS), _block_to_owner_copies)


def _split_copies(plan, ins, lands, send_sems, recv_sems):
    return [pltpu.make_async_remote_copy(src_ref=src, dst_ref=dst, send_sem=send_sems.at[k], recv_sem=recv_sems.at[k],
                                         device_id=to, device_id_type=MESH)
            for k, (src, dst, to) in enumerate(plan.copies(*_place(), ins, lands))]


HBM_SPEC = pl.BlockSpec(memory_space=pltpu.HBM)
SEM_SPEC = pl.BlockSpec(memory_space=pltpu.SEMAPHORE)
EFFECT = pltpu.SideEffectType.DATAFLOW_SIDE_EFFECTING


def _start_copies(name, plan, ins, lands, after):
    n_in, n = len(ins), len(ins) + len(lands)

    def body(*refs):
        send_sems, recv_sems = refs[n + 1], refs[n + 2]
        for cp in _split_copies(plan, refs[:n_in], refs[n_in:n], send_sems, recv_sems):
            cp.start()
        refs[-1][...] = jnp.zeros_like(refs[-1])

    arrays = [pltpu.with_memory_space_constraint(v, pltpu.HBM) for v in (*ins, *lands)]
    sems = pltpu.SemaphoreType.DMA((len(lands) * plan.per_array,))
    send_sems, recv_sems, *flying, token = pl.pallas_call(
        body, name=name,
        out_shape=(sems, sems, *[pltpu.HBM(v.shape, v.dtype) for v in arrays], jax.ShapeDtypeStruct((8, LANES), F32)),
        in_specs=[HBM_SPEC] * n + [pl.BlockSpec(memory_space=pl.ANY)],
        out_specs=(SEM_SPEC, SEM_SPEC, *[HBM_SPEC] * n, pl.BlockSpec(memory_space=pltpu.VMEM)),
        input_output_aliases={i: 2 + i for i in range(n)},
        compiler_params=pltpu.CompilerParams(has_side_effects=EFFECT),
    )(*arrays, after)
    return send_sems, recv_sems, flying, token


def _wait_copies(name, plan, n_in, send_sems, recv_sems, flying, after):
    n = len(flying)

    def body(*refs):
        for cp in _split_copies(plan, refs[:n_in], refs[n_in:n], refs[n], refs[n + 1]):
            cp.wait_send()
            cp.wait_recv()

    landed = pl.pallas_call(
        body, name=name, out_shape=tuple(pltpu.HBM(v.shape, v.dtype) for v in flying),
        in_specs=[HBM_SPEC] * n + [SEM_SPEC, SEM_SPEC, pl.BlockSpec(memory_space=pl.ANY)], out_specs=tuple([HBM_SPEC] * n),
        input_output_aliases={i: i for i in range(n)},
        compiler_params=pltpu.CompilerParams(has_side_effects=EFFECT),
    )(*flying, send_sems, recv_sems, after)
    return landed[:n_in], landed[n_in:]


def _adamw_math(w, g, m, v):
    m = ADAM_B1 * m + (1.0 - ADAM_B1) * g
    v = ADAM_B2 * v + (1.0 - ADAM_B2) * (g * g)
    m_hat = m / (1.0 - ADAM_B1 ** ADAM_STEP)
    v_hat = v / (1.0 - ADAM_B2 ** ADAM_STEP)
    return -ADAM_LR * (m_hat / (jnp.sqrt(v_hat) + ADAM_EPS) + ADAM_WD * w), m, v


def _adamw_sharded(me, own, received, w, m, v, tr):
    rows, cols = w.shape
    own, received, w, m, v = _in_hbm(own, received, w, m, v)

    def body(me_ref, own_ref, rec_ref, w_ref, m_ref, v_ref, g_ref, d_ref, nm_ref, nv_ref):
        g = own_ref[...]
        for r in range(N_DEV - 1):
            g = g + rec_ref[r].astype(F32)
        g_ref[...] = g
        d_ref[...], nm_ref[...], nv_ref[...] = _adamw_math(w_ref[...], g, m_ref[...], v_ref[...])

    tile = pl.BlockSpec((tr, cols), lambda i, me_ref: (i, 0))
    shape = pltpu.HBM((rows, cols), F32)
    return pl.pallas_call(
        body, name="adamw_sharded",
        grid_spec=pltpu.PrefetchScalarGridSpec(
            num_scalar_prefetch=1, grid=(rows // tr,),
            in_specs=[pl.BlockSpec((None, tr, cols), lambda i, me_ref: (me_ref[0], i, 0)),
                      pl.BlockSpec((N_DEV - 1, tr, cols), lambda i, me_ref: (0, i, 0)), tile, tile, tile],
            out_specs=[tile, tile, tile, tile]),
        out_shape=[shape, shape, shape, shape],
        compiler_params=_cparams(("parallel",), 40),
    )(me, own, received, w, m, v)


VECTOR_ROWS = D_MODEL // LANES
POOL_ROWS = len(POOL_WINDOWS) * POOL_GROUP_DIM


def _pack_small(dg1, dg2, dg3, dps, dsink, loss_cols, dwp):
    def body(g1_ref, g2_ref, g3_ref, ps_ref, sink_ref, loss_ref, wp_ref, o_ref):
        o_ref[...] = jnp.zeros_like(o_ref)
        for base, ref, n in ((ROW_G1, g1_ref, VECTOR_ROWS), (ROW_G2, g2_ref, VECTOR_ROWS), (ROW_G3, g3_ref, VECTOR_ROWS),
                             (ROW_LOSS, loss_ref, VECTOR_ROWS), (ROW_PS, ps_ref, POOL_WIDTH // LANES)):
            for r in range(n):
                o_ref[base + r:base + r + 1, :] = ref[:, r * LANES:(r + 1) * LANES]
        heads = sink_ref[:, 0, :]
        on_diagonal = lax.broadcasted_iota(jnp.int32, heads.shape, 0) == lax.broadcasted_iota(jnp.int32, heads.shape, 1)
        o_ref[ROW_SINK:ROW_SINK + 1, :] = jnp.sum(jnp.where(on_diagonal, heads, 0.0), axis=0, keepdims=True)
        o_ref[ROW_WP:ROW_WP + POOL_ROWS, :] = wp_ref[...].reshape(POOL_ROWS, LANES)

    return pl.pallas_call(body, name="pack_small", out_shape=jax.ShapeDtypeStruct((SMALL_ROWS, LANES), F32))(
        dg1, dg2, dg3, dps, dsink, loss_cols, dwp)


def _finish_small(me, own, landed, params):
    flat = [a for group in params for a in group]

    def body(me_ref, own_ref, landed_ref, *refs):
        ins, outs = refs[:len(flat)], refs[len(flat):]
        total = None
        for source in range(N_DEV):
            rel = jnp.bitwise_xor(me_ref[0], source)
            piece = jnp.where(rel == 0, own_ref[...], landed_ref[jnp.maximum(rel, 1) - 1])
            total = piece if total is None else total + piece
        outs[0][...] = (0.5 / D_MODEL) * jnp.sum(jnp.sum(total[ROW_LOSS:ROW_LOSS + VECTOR_ROWS], axis=1, keepdims=True), axis=0, keepdims=True)
        row = lambda base, n: jnp.concatenate([total[base + r:base + r + 1, :] for r in range(n)], axis=1)
        grads = [row(ROW_G1, VECTOR_ROWS), row(ROW_G2, VECTOR_ROWS), row(ROW_G3, VECTOR_ROWS), row(ROW_PS, POOL_WIDTH // LANES),
                 total[ROW_SINK:ROW_SINK + 1, :N_Q_HEADS], total[ROW_WP:ROW_WP + POOL_ROWS].reshape(params[5][0].shape)]
        for k, g in enumerate(grads):
            w_ref, m_ref, v_ref = ins[3 * k:3 * k + 3]
            g_out, d_out, m_out, v_out = outs[1 + 4 * k:5 + 4 * k]
            g_out[...] = g
            d_out[...], m_out[...], v_out[...] = _adamw_math(w_ref[...], g, m_ref[...], v_ref[...])

    shapes = [jax.ShapeDtypeStruct((1, 1), F32)] + [jax.ShapeDtypeStruct(w.shape, F32) for w, _, _ in params for _ in range(4)]
    vmem = pl.BlockSpec(memory_space=pltpu.VMEM)
    res = pl.pallas_call(body, name="finish_small", in_specs=[pl.BlockSpec(memory_space=pltpu.SMEM)] + [vmem] * (2 + len(flat)),
                         out_shape=shapes)(me, own, landed, *flat)
    return res[0], [res[1 + 4 * k:5 + 4 * k] for k in range(len(params))]


def _local_step(x, target, attn_norm_g, attn_sinks, w_pool, pool_scale, mlp_norm_g, final_norm_g,
                front_token, first_weight, second_hop, later_weights, ship_down, ship_up, ship_in):
    n_seq, seq, _ = x.shape
    rows = n_seq * seq
    tm, tm_mlp, tm_norm = min(1024, seq), min(256, seq), min(2048, seq)
    tm_grad = tm
    x2d, t2d = x.reshape(rows, D_MODEL), target.reshape(rows, D_MODEL)
    g3 = final_norm_g.reshape(1, D_MODEL)
    cos, sin = _rope_tables(seq)
    wp_b = w_pool[0].astype(BF16)

    sink_rows = jnp.broadcast_to(attn_sinks.reshape(N_Q_HEADS, 1, 1), (N_Q_HEADS, 1, LANES))
    cos, sin = _in_hbm(cos, sin)
    h1 = _in_hbm(_first_norm(x2d, attn_norm_g, front_token, tm_norm))
    w_in_full = _in_hbm(first_weight(h1))
    q, kd, vd, u = _in_hbm(*_inproj(h1, w_in_full, cos, sin, seq, tm))
    attn = _in_hbm(_attention_forward(sink_rows, q, kd, vd, n_seq, seq))
    pool = _in_hbm(_pool_forward(u, wp_b, pool_scale, second_hop(attn), n_seq, seq))
    w_out_full, w_up_full, w_down_full = _in_hbm(*later_weights(pool))
    x2, h2 = _in_hbm(*_outproj_norm(x2d, attn, pool, w_out_full, mlp_norm_g, tm))
    slope, f, dx3, dx3b, loss_cols, dg3 = _mlp_forward_loss(h2, x2, w_up_full, w_down_full, g3, t2d, tm_mlp)
    slope, f, dx3, dx3b = _in_hbm(slope, f, dx3, dx3b)

    down_token = ship_down(_weight_gradient("down_gradient", f, dx3b, True, tm_grad))
    da, dx2, dattn, dpool, dg2, d_w_out, d_w_out_wire = _mlp_backward_data(
        dx3b, slope, w_down_full, w_up_full, dx3, x2, mlp_norm_g, w_out_full, attn, pool, down_token, tm_mlp)
    da, dx2, dattn, dpool = _in_hbm(da, dx2, dattn, dpool)
    up_token = ship_up((d_w_out, d_w_out_wire), _weight_gradient("up_gradient", h2, da, False, tm_grad))
    dq, dk, dv, dsink = _attention_backward(sink_rows, q, kd, vd, dattn, cos, sin, up_token, n_seq, seq)
    dq, dk, dv = _in_hbm(dq, dk, dv)
    du, d_w_pool, d_pool_scale = _pool_backward(u, dpool, wp_b, pool_scale, n_seq, seq)
    du = _in_hbm(du)
    in_token = ship_in(_inproj_gradient(dq, dk, dv, du, h1, tm_grad))
    grad_x, dg1 = _inproj_backward(dq, dk, dv, du, w_in_full, x2d, dx2, attn_norm_g, in_token, tm)
    return grad_x.reshape(x.shape), _pack_small(dg1, dg2, dg3, d_pool_scale, dsink, loss_cols, d_w_pool)


def kernel(x, attn_norm_g, w_in, attn_sinks, w_pool, pool_scale, w_out, mlp_norm_g, w_up, w_down, final_norm_g, loss_target, m_attn_norm_g, m_w_in, m_attn_sinks, m_w_pool, m_pool_scale, m_w_out, m_mlp_norm_g, m_w_up, m_w_down, m_final_norm_g, v_attn_norm_g, v_w_in, v_attn_sinks, v_w_pool, v_pool_scale, v_w_out, v_mlp_norm_g, v_w_up, v_w_down, v_final_norm_g):
    me = (4 * lax.axis_index("x") + 2 * lax.axis_index("y") + lax.axis_index("c")).astype(jnp.int32).reshape(1)

    unordered = jnp.zeros((8, LANES), F32)

    win_land, wout_land, wup_land, wdown_land = _stage_weights(
        [w_in[0].T, w_out[0], w_up[0], w_down[0]], [_row_slot, _row_slot, _column_slot, _row_slot],
        [(N_DEV, IN_BLOCK, D_MODEL), (N_DEV, OUT_BLOCK, D_MODEL), (D_MODEL, D_FF), (N_DEV, FF_BLOCK, D_MODEL)])
    in_slots, later_slots = [_row_slot], [_row_slot, _column_slot, _row_slot]
    in_copies = _start_copies("spread_in_start", _own_slot_first_hop(in_slots), [], [win_land], unordered)
    later_copies = _start_copies("spread_later_start", _own_slot_first_hop(later_slots), [], [wout_land, wup_land, wdown_land], in_copies[3])

    def first_weight(after):
        _, in_landed = _wait_copies("spread_in_wait", _own_slot_first_hop(in_slots), 0, *in_copies[:3], after)
        in_passed = _start_copies("pass_in_start", _landed_to_sibling(in_slots), [], in_landed, unordered)
        _, (win_g,) = _wait_copies("pass_in_wait", _landed_to_sibling(in_slots), 0, *in_passed[:3], in_passed[3])
        return win_g.reshape(IN_WIDTH, D_MODEL)

    passing = []

    def second_hop(after):
        _, later_landed = _wait_copies("spread_later_wait", _own_slot_first_hop(later_slots), 0, *later_copies[:3], after)
        passing.extend(_start_copies("pass_later_start", _landed_to_sibling(later_slots), [], later_landed, unordered))
        return passing[3]

    def later_weights(after):
        _, (wout_g, wup_g, wdown_g) = _wait_copies("pass_later_wait", _landed_to_sibling(later_slots), 0, *passing[:3], after)
        return wout_g.reshape(D_MODEL, D_MODEL), wup_g, wdown_g.reshape(D_FF, D_MODEL)

    deliveries, kept = {}, {}

    def deliver(name, plan, wires):
        lands = [lax.empty((N_DEV - 1,) + (g.shape[1:] if plan is _block_to_owner else g.shape), g.dtype) for g in wires]
        deliveries[name] = _start_copies(name + "_start", plan, wires, lands, unordered)
        return deliveries[name][3]

    def landed(name, plan, after):
        send, recv, flying, _ = deliveries[name]
        return _wait_copies(name + "_wait", plan, len(flying) // 2, send, recv, flying, after)

    def ship_down(d_w_down):
        kept["down"] = d_w_down[0]
        return deliver("deliver_down", _block_to_owner, [d_w_down[1]])

    def ship_up(d_w_out, d_w_up):
        kept["out"], kept["up"] = d_w_out[0].reshape(N_DEV, OUT_BLOCK, D_MODEL), d_w_up[0]
        return deliver("deliver_up", _block_to_owner, [d_w_out[1].reshape(N_DEV, OUT_BLOCK, D_MODEL), d_w_up[1]])

    def ship_in(d_w_in):
        kept["in"] = d_w_in[0].reshape(N_DEV, IN_BLOCK, D_MODEL)
        return deliver("deliver_in", _block_to_owner, [d_w_in[1].reshape(N_DEV, IN_BLOCK, D_MODEL)])

    grad_x, small = _local_step(x, loss_target, attn_norm_g, attn_sinks, w_pool, pool_scale, mlp_norm_g, final_norm_g,
                                later_copies[3], first_weight, second_hop, later_weights, ship_down, ship_up, ship_in)
    small_token = deliver("deliver_small", _whole_to_all, [small])

    _, (got_down,) = landed("deliver_down", _block_to_owner, small_token)
    _, (got_out, got_up) = landed("deliver_up", _block_to_owner, small_token)
    g_down = _adamw_sharded(me, kept["down"], got_down, w_down[0], m_w_down[0], v_w_down[0], 256)
    g_up = _adamw_sharded(me, kept["up"], got_up, w_up[0], m_w_up[0], v_w_up[0], 256)
    g_out = _adamw_sharded(me, kept["out"], got_out, w_out[0], m_w_out[0], v_w_out[0], 128)
    _, (got_in,) = landed("deliver_in", _block_to_owner, g_out[0])
    g_in = [a.T for a in _adamw_sharded(me, kept["in"], got_in, w_in[0].T, m_w_in[0].T, v_w_in[0].T, IN_BLOCK)]
    (own_small,), (got_small,) = landed("deliver_small", _whole_to_all, g_in[1])

    row = lambda a: a.reshape(1, D_MODEL)
    params = [(attn_norm_g, m_attn_norm_g, v_attn_norm_g), (mlp_norm_g, m_mlp_norm_g, v_mlp_norm_g),
              (row(final_norm_g), row(m_final_norm_g), row(v_final_norm_g)), (pool_scale, m_pool_scale, v_pool_scale),
              (attn_sinks, m_attn_sinks, v_attn_sinks), (w_pool[0], m_w_pool[0], v_w_pool[0])]
    loss, (s_norm1, s_norm2, s_norm3, s_scale, s_sinks, s_pool) = _finish_small(me, own_small, got_small, params)
    s_norm3 = [a.reshape(D_MODEL) for a in s_norm3]
    s_pool = [a[None] for a in s_pool]

    def ordered(k):
        return [s_norm1[k], g_in[k][None], s_sinks[k], s_pool[k], s_scale[k], g_out[k][None], s_norm2[k], g_up[k][None], g_down[k][None],
                s_norm3[k]]

    return (loss.reshape(()), grad_x, *ordered(0), *ordered(1), *ordered(2), *ordered(3))
```
